```python
import math
import jax, jax.numpy as jnp
from jax import lax
import numpy as np

D_MODEL = 2048
BATCH = 8
SEQ = 4096
DEPTH = 4

NORM_EPS = 1e-6
NEG_INF = -1e30
MEM_TOKENS = 256

NSA_HEADS = 8
NSA_KV_GROUPS = 2
NSA_GROUP_HEADS = NSA_HEADS // NSA_KV_GROUPS
NSA_HEAD_DIM = 128
NSA_CMP_LEN = 32
NSA_CMP_STRIDE = 16
NSA_CMP_HIDDEN = 256
NSA_SEL_BLOCK = 64
NSA_N_SELECT = 16
NSA_WINDOW = 512
NSA_Q_BLOCK = 64
NSA_FORCE_SCORE = 1e6
NSA_WIDTH = NSA_HEADS * NSA_HEAD_DIM
NSA_KV_WIDTH = NSA_KV_GROUPS * NSA_HEAD_DIM

RET_HEADS = 4
RET_DK = 256
RET_DV = 256
RET_CHUNK = 128
RET_ROPE_BASE = 10000.0
RET_GN_EPS = 1e-5

GDN_QK_HEADS = 16
GDN_V_HEADS = 32
GDN_HEAD_DIM = 128
GDN_CONV = 4
GDN_CHUNK = 64
GDN_QK_WIDTH = GDN_QK_HEADS * GDN_HEAD_DIM
GDN_V_WIDTH = GDN_V_HEADS * GDN_HEAD_DIM
GDN_CONV_CH = 2 * GDN_QK_WIDTH + GDN_V_WIDTH
GDN_IN = GDN_CONV_CH + GDN_V_WIDTH + 2 * GDN_V_HEADS

REL_BUCKETS = 32
REL_MAX_DIST = 1024

XA_HEADS = 4
XA_HEAD_DIM = 128
XA_WIDTH = XA_HEADS * XA_HEAD_DIM

FFN_HIDDEN = -(-8 * D_MODEL // (3 * 256)) * 256

HYB_IN = NSA_WIDTH + 6 * NSA_KV_WIDTH + 3 * NSA_HEADS + 2 * RET_HEADS * RET_DK + 2 * RET_HEADS * RET_DV
HYB_OUT = NSA_WIDTH + RET_HEADS * RET_DV
N_EVEN = (DEPTH + 1) // 2
N_ODD = DEPTH // 2

kernel_name = 'hybrid_nsa_retention_gdn_trunk'


def rms_norm(x, gain, eps=NORM_EPS):
    x32 = x.astype(jnp.float32)
    y = x32 * lax.rsqrt(jnp.mean(x32 * x32, axis=-1, keepdims=True) + eps)
    return (y * gain.astype(jnp.float32)).astype(x.dtype)


def l2_norm(x, eps=NORM_EPS):
    x32 = x.astype(jnp.float32)
    return x32 * lax.rsqrt(jnp.sum(x32 * x32, axis=-1, keepdims=True) + eps)


def masked_softmax(logits, mask):
    p = jax.nn.softmax(jnp.where(mask, logits.astype(jnp.float32), NEG_INF), axis=-1)
    return jnp.where(mask, p, 0.0)


def rel_bucket(dist):
    dist = jnp.maximum(dist, 0)
    max_exact = REL_BUCKETS // 2
    scaled = (jnp.log(jnp.maximum(dist, max_exact).astype(jnp.float32) / max_exact)
              / math.log(REL_MAX_DIST / max_exact) * (REL_BUCKETS - max_exact))
    large = jnp.minimum(max_exact + scaled.astype(jnp.int32), REL_BUCKETS - 1)
    return jnp.where(dist < max_exact, dist, large)


def rotary(x, pos):
    half = x.shape[-1] // 2
    inv = RET_ROPE_BASE ** (-jnp.arange(half, dtype=jnp.float32) / half)
    ang = pos.astype(jnp.float32)[:, None] * inv[None, :]
    cos, sin = jnp.cos(ang)[None, :, None, :], jnp.sin(ang)[None, :, None, :]
    x1, x2 = x[..., :half], x[..., half:]
    return jnp.concatenate([x1 * cos - x2 * sin, x1 * sin + x2 * cos], axis=-1)


def causal_depthwise_conv(x, w):
    k, ch = w.shape
    return lax.conv_general_dilated(x, w[:, None, :].astype(x.dtype), window_strides=(1,),
                                    padding=[(k - 1, 0)], dimension_numbers=('NWC', 'WIO', 'NWC'),
                                    feature_group_count=ch)


def compress_blocks(t, pos, w1, w2):
    b, s, g, dh = t.shape
    r = NSA_CMP_LEN // NSA_CMP_STRIDE
    n_chunks = s // NSA_CMP_STRIDE
    n_cmp = n_chunks - r + 1
    c = t.reshape(b, n_chunks, NSA_CMP_STRIDE, g, dh)
    blocks = jnp.concatenate([c[:, j:j + n_cmp] for j in range(r)], axis=2)
    blocks = blocks + pos[None, None, :, None, :].astype(t.dtype)
    flat = blocks.transpose(0, 3, 1, 2, 4).reshape(b, g, n_cmp, NSA_CMP_LEN * dh)
    return jax.nn.silu(flat @ w1) @ w2


def nsa_attention(q, k_cmp, v_cmp, k_sel, v_sel, k_win, v_win, gate_logits,
                  q_gain, k_gain, cmp_pos, cmp_w1, cmp_w2, rel_bias):
    f32 = jnp.float32
    B, S = q.shape[0], q.shape[1]
    G, HG, dh = NSA_KV_GROUPS, NSA_GROUP_HEADS, NSA_HEAD_DIM
    QB, W, SB = NSA_Q_BLOCK, NSA_WINDOW, NSA_SEL_BLOCK
    q = rms_norm(q, q_gain) * (dh ** -0.5)
    q = q.reshape(B, S, G, HG, dh).transpose(0, 2, 3, 1, 4)

    kc = rms_norm(compress_blocks(k_cmp, cmp_pos[0], cmp_w1[0], cmp_w2[0]), k_gain[0])
    vc = compress_blocks(v_cmp, cmp_pos[1], cmp_w1[1], cmp_w2[1])
    n_cmp = kc.shape[2]
    n_sb = S // SB
    ks_blocks = rms_norm(k_sel, k_gain[1]).transpose(0, 2, 1, 3).reshape(B, G, n_sb, SB, dh)
    vs_blocks = v_sel.transpose(0, 2, 1, 3).reshape(B, G, n_sb, SB, dh)
    pad = ((0, 0), (0, 0), (W, 0), (0, 0))
    kw_pad = jnp.pad(rms_norm(k_win, k_gain[2]).transpose(0, 2, 1, 3), pad)
    vw_pad = jnp.pad(v_win.transpose(0, 2, 1, 3), pad)
    gates = jax.nn.sigmoid(gate_logits.astype(f32)).reshape(B, S, G, HG, 3).transpose(0, 2, 3, 1, 4)

    cmp_start = jnp.arange(n_cmp) * NSA_CMP_STRIDE
    cmp_end = cmp_start + NSA_CMP_LEN - 1
    sel_start = jnp.arange(n_sb) * SB
    cover = ((cmp_start[:, None] < sel_start[None, :] + SB) & (cmp_end[:, None] >= sel_start[None, :])).astype(f32)
    n_top = min(NSA_N_SELECT, n_sb)
    b_idx = jnp.arange(B)[:, None, None, None]
    g_idx = jnp.arange(G)[None, :, None, None]
    bias_by_group = rel_bias.reshape(REL_BUCKETS, G, HG)
    blk = jnp.arange(n_sb)
    n_qb = S // QB

    def head_bias(dist):
        return jnp.moveaxis(rel_bias[rel_bucket(dist)], -1, 0).reshape((G, HG) + dist.shape).astype(f32)

    def query_block(args):
        qb, gb, iq = args
        t = iq * QB + jnp.arange(QB)
        dist_c = t[:, None] - cmp_end[None, :]
        logit_c = jnp.einsum('bghqd,bgnd->bghqn', qb, kc).astype(f32) + head_bias(dist_c)
        p_c = masked_softmax(logit_c, dist_c >= 0)
        o_c = jnp.einsum('bghqn,bgnd->bghqd', p_c.astype(vc.dtype), vc)
        imp = jnp.einsum('bghqn,nj->bgqj', p_c, cover)
        cur = (t // SB)[:, None]
        forced = (blk[None, :] == 0) | (blk[None, :] == cur) | (blk[None, :] == cur - 1)
        imp = jnp.where(forced, NSA_FORCE_SCORE, imp)
        imp = jnp.where(blk[None, :] > cur, -NSA_FORCE_SCORE, imp)
        _, sel = lax.top_k(imp, n_top)
        k_g = ks_blocks[b_idx, g_idx, sel].reshape(B, G, QB, n_top * SB, dh)
        v_g = vs_blocks[b_idx, g_idx, sel].reshape(B, G, QB, n_top * SB, dh)
        kpos = (sel[..., None] * SB + jnp.arange(SB)).reshape(B, G, QB, n_top * SB)
        dist_s = t[None, None, :, None] - kpos
        bias_s = bias_by_group[rel_bucket(dist_s), g_idx].transpose(0, 1, 4, 2, 3).astype(f32)
        logit_s = jnp.einsum('bghqd,bgqkd->bghqk', qb, k_g).astype(f32) + bias_s
        p_s = masked_softmax(logit_s, (dist_s >= 0)[:, :, None])
        o_s = jnp.einsum('bghqk,bgqkd->bghqd', p_s.astype(v_g.dtype), v_g)
        start = iq * QB
        k_w = lax.dynamic_slice_in_dim(kw_pad, start, QB + W, axis=2)
        v_w = lax.dynamic_slice_in_dim(vw_pad, start, QB + W, axis=2)
        wpos = start - W + jnp.arange(QB + W)
        dist_w = t[:, None] - wpos[None, :]
        mask_w = (dist_w >= 0) & (dist_w < W) & (wpos[None, :] >= 0)
        logit_w = jnp.einsum('bghqd,bgkd->bghqk', qb, k_w).astype(f32) + head_bias(dist_w)
        p_w = masked_softmax(logit_w, mask_w)
        o_w = jnp.einsum('bghqk,bgkd->bghqd', p_w.astype(v_w.dtype), v_w)
        out = gb[..., 0:1] * o_c + gb[..., 1:2] * o_s + gb[..., 2:3] * o_w
        return out.astype(qb.dtype)

    q_blocks = q.reshape(B, G, HG, n_qb, QB, dh).transpose(3, 0, 1, 2, 4, 5)
    g_blocks = gates.reshape(B, G, HG, n_qb, QB, 3).transpose(3, 0, 1, 2, 4, 5)
    outs = lax.map(query_block, (q_blocks, g_blocks, jnp.arange(n_qb)))
    return outs.transpose(1, 0, 4, 2, 3, 5).reshape(B, S, NSA_WIDTH)


def retention(q, k, v, gate, gn_gain):
    f32 = jnp.float32
    B, S, H, dk = q.shape
    dv = v.shape[-1]
    C = RET_CHUNK
    N = S // C
    pos = jnp.arange(S)
    q = rotary(q.astype(f32), pos) * (dk ** -0.5)
    k = rotary(k.astype(f32), pos)
    v = v.astype(f32)
    log_gamma = jnp.log(1.0 - 2.0 ** (-5.0 - jnp.arange(H, dtype=f32)))
    idx = jnp.arange(C, dtype=f32)
    rel = idx[:, None] - idx[None, :]
    decay_in = jnp.where(rel >= 0, jnp.exp(log_gamma[:, None, None] * jnp.maximum(rel, 0.0)), 0.0)
    q_decay = jnp.exp(log_gamma[:, None] * (idx + 1.0))
    k_decay = jnp.exp(log_gamma[:, None] * (C - 1.0 - idx))
    chunk_decay = jnp.exp(log_gamma * C)

    def to_chunks(t):
        return t.reshape(B, N, C, H, t.shape[-1]).transpose(1, 0, 3, 2, 4)

    def step(state, inp):
        qn, kn, vn = inp
        inner = jnp.einsum('bhcd,bhmd->bhcm', qn, kn) * decay_in
        o = (jnp.einsum('bhcm,bhme->bhce', inner, vn)
             + jnp.einsum('bhcd,bhde->bhce', qn * q_decay[..., None], state))
        state = state * chunk_decay[:, None, None] + jnp.einsum('bhmd,bhme->bhde', kn * k_decay[..., None], vn)
        return state, o

    _, o = lax.scan(step, jnp.zeros((B, H, dk, dv), f32), (to_chunks(q), to_chunks(k), to_chunks(v)))
    o = o.transpose(1, 0, 3, 2, 4).reshape(B, S, H, dv)
    mu = jnp.mean(o, axis=-1, keepdims=True)
    var = jnp.mean(jnp.square(o - mu), axis=-1, keepdims=True)
    o = ((o - mu) * lax.rsqrt(var + RET_GN_EPS)).reshape(B, S, H * dv) * gn_gain.astype(f32)
    return (jax.nn.silu(gate.astype(f32)) * o).astype(gate.dtype)


def nsa_retention_mixer(h, w_in, w_out, q_gain, k_gain, cmp_pos, cmp_w1, cmp_w2, gn_gain, rel_bias):
    B, S, _ = h.shape
    widths = ([NSA_WIDTH] + [NSA_KV_WIDTH] * 6 + [3 * NSA_HEADS]
              + [RET_HEADS * RET_DK] * 2 + [RET_HEADS * RET_DV] * 2)
    cuts = [int(c) for c in np.cumsum(widths)[:-1]]
    (nq, kc, vc, ks, vs, kw, vw, gl, rq, rk, rv, rg) = jnp.split(h @ w_in, cuts, axis=-1)
    kv = lambda t: t.reshape(B, S, NSA_KV_GROUPS, NSA_HEAD_DIM)
    a_out = nsa_attention(nq.reshape(B, S, NSA_HEADS, NSA_HEAD_DIM), kv(kc), kv(vc), kv(ks), kv(vs),
                          kv(kw), kv(vw), gl.reshape(B, S, NSA_HEADS, 3),
                          q_gain, k_gain, cmp_pos, cmp_w1, cmp_w2, rel_bias)
    b_out = retention(rq.reshape(B, S, RET_HEADS, RET_DK), rk.reshape(B, S, RET_HEADS, RET_DK),
                      rv.reshape(B, S, RET_HEADS, RET_DV), rg, gn_gain)
    return jnp.concatenate([a_out, b_out], axis=-1) @ w_out


def chunk_gated_delta_rule(q, k, v, beta, g):
    f32 = jnp.float32
    B, S, H, dk = q.shape
    dv = v.shape[-1]
    C = GDN_CHUNK
    N = S // C

    def chunks(t):
        return jnp.moveaxis(t.astype(f32).reshape((B, N, C, H) + t.shape[3:]), 3, 1)

    q = chunks(q) * (dk ** -0.5)
    k, v, beta, g = chunks(k), chunks(v), chunks(beta), chunks(g)
    gc = jnp.cumsum(g, axis=-1)
    i = jnp.arange(C)
    causal = i[:, None] >= i[None, :]
    strict = i[:, None] > i[None, :]
    decay = jnp.exp(jnp.where(causal, gc[..., :, None] - gc[..., None, :], -jnp.inf))
    kb = k * beta[..., None]
    lower = jnp.where(strict, jnp.einsum('bhncd,bhnmd->bhncm', kb, k) * decay, 0.0)
    rhs = jnp.concatenate([v * beta[..., None], kb * jnp.exp(gc)[..., None]], axis=-1)
    sol = lax.linalg.triangular_solve(lower + jnp.eye(C, dtype=f32), rhs, left_side=True,
                                      lower=True, unit_diagonal=True)
    u, w = sol[..., :dv], sol[..., dv:]
    attn = jnp.where(causal, jnp.einsum('bhncd,bhnmd->bhncm', q, k) * decay, 0.0)
    q_dec = q * jnp.exp(gc)[..., None]
    k_dec = k * jnp.exp(gc[..., -1:] - gc)[..., None]
    last = jnp.exp(gc[..., -1])
    xs = tuple(jnp.moveaxis(t, 2, 0) for t in (u, w, attn, q_dec, k_dec, last))

    def step(state, inp):
        u_n, w_n, attn_n, qd_n, kd_n, last_n = inp
        v_new = u_n - w_n @ state
        o = qd_n @ state + attn_n @ v_new
        state = state * last_n[..., None, None] + jnp.einsum('bhcd,bhce->bhde', kd_n, v_new)
        return state, o

    _, o = lax.scan(step, jnp.zeros((B, H, dk, dv), f32), xs)
    return o.transpose(1, 0, 3, 2, 4).reshape(B, S, H, dv)


def gated_deltanet_mixer(h, w_in, conv_w, a_log, dt_bias, norm_gain, w_out):
    B, S, _ = h.shape
    f32 = jnp.float32
    cuts = [GDN_CONV_CH, GDN_CONV_CH + GDN_V_WIDTH, GDN_CONV_CH + GDN_V_WIDTH + GDN_V_HEADS]
    qkv, z, b, a = jnp.split(h @ w_in, cuts, axis=-1)
    qkv = jax.nn.silu(causal_depthwise_conv(qkv, conv_w))
    q, k, v = jnp.split(qkv, [GDN_QK_WIDTH, 2 * GDN_QK_WIDTH], axis=-1)
    rep = GDN_V_HEADS // GDN_QK_HEADS
    q = jnp.repeat(l2_norm(q.reshape(B, S, GDN_QK_HEADS, GDN_HEAD_DIM)), rep, axis=2)
    k = jnp.repeat(l2_norm(k.reshape(B, S, GDN_QK_HEADS, GDN_HEAD_DIM)), rep, axis=2)
    v = v.reshape(B, S, GDN_V_HEADS, GDN_HEAD_DIM)
    beta = jax.nn.sigmoid(b.astype(f32))
    g = -jnp.exp(a_log.astype(f32)) * jax.nn.softplus(a.astype(f32) + dt_bias.astype(f32))
    o = chunk_gated_delta_rule(q, k, v, beta, g)
    o = rms_norm(o, norm_gain) * jax.nn.silu(z.astype(f32).reshape(B, S, GDN_V_HEADS, GDN_HEAD_DIM))
    return o.reshape(B, S, GDN_V_WIDTH).astype(h.dtype) @ w_out


def memory_cross_attention(h, mem, mem_gain, wq, wkv, q_gain, k_gain, wo):
    B, S, _ = h.shape
    M = mem.shape[1]
    q = rms_norm((h @ wq).reshape(B, S, XA_HEADS, XA_HEAD_DIM), q_gain)
    k, v = jnp.split(rms_norm(mem, mem_gain) @ wkv, 2, axis=-1)
    k = rms_norm(k.reshape(B, M, XA_HEADS, XA_HEAD_DIM), k_gain)
    v = v.reshape(B, M, XA_HEADS, XA_HEAD_DIM)
    logits = jnp.einsum('bshd,bmhd->bhsm', q, k).astype(jnp.float32) * (XA_HEAD_DIM ** -0.5)
    p = jax.nn.softmax(logits, axis=-1).astype(v.dtype)
    return jnp.einsum('bhsm,bmhd->bshd', p, v).reshape(B, S, XA_WIDTH) @ wo


def swiglu_ffn(h, w_in, w_out):
    gate, up = jnp.split(h @ w_in, 2, axis=-1)
    return (jax.nn.silu(gate) * up) @ w_out


def setup_inputs(seed: int = 0) -> dict:
    key = jax.random.key(seed)
    keys = iter(jax.random.split(key, 32))
    f32 = jnp.float32
    E, O, L, dh = N_EVEN, N_ODD, DEPTH, NSA_HEAD_DIM

    def dense(shape, fan_in):
        return jax.random.normal(next(keys), shape, f32) * (fan_in ** -0.5)

    def gain(shape):
        return 1.0 + 0.02 * jax.random.normal(next(keys), shape, f32)

    x = jax.random.normal(next(keys), (BATCH, SEQ, D_MODEL), f32)
    mem = jax.random.normal(next(keys), (BATCH, MEM_TOKENS, D_MODEL), f32)
    rel_bias = 0.1 * jax.random.normal(next(keys), (REL_BUCKETS, NSA_HEADS), f32)
    ln_mix = gain((L, D_MODEL))
    ln_mem = gain((L, D_MODEL))
    ln_ffn = gain((L, D_MODEL))
    hyb_w_in = dense((E, D_MODEL, HYB_IN), D_MODEL)
    hyb_w_out = dense((E, HYB_OUT, D_MODEL), HYB_OUT)
    nsa_q_gain = gain((E, dh))
    nsa_k_gain = gain((E, 3, dh))
    nsa_cmp_pos = 0.1 * jax.random.normal(next(keys), (E, 2, NSA_CMP_LEN, dh), f32)
    nsa_cmp_w1 = dense((E, 2, NSA_CMP_LEN * dh, NSA_CMP_HIDDEN), NSA_CMP_LEN * dh)
    nsa_cmp_w2 = dense((E, 2, NSA_CMP_HIDDEN, dh), NSA_CMP_HIDDEN)
    ret_gn_gain = gain((E, RET_HEADS * RET_DV))
    gdn_w_in = dense((O, D_MODEL, GDN_IN), D_MODEL)
    gdn_conv_w = dense((O, GDN_CONV, GDN_CONV_CH), GDN_CONV)
    gdn_a_log = jnp.log(jax.random.uniform(next(keys), (O, GDN_V_HEADS), f32, 1.0, 16.0))
    dt = jnp.exp(jax.random.uniform(next(keys), (O, GDN_V_HEADS), f32, math.log(1e-3), math.log(1e-1)))
    gdn_dt_bias = dt + jnp.log(-jnp.expm1(-dt))
    gdn_norm_gain = gain((O, GDN_HEAD_DIM))
    gdn_w_out = dense((O, GDN_V_WIDTH, D_MODEL), GDN_V_WIDTH)
    xa_wq = dense((L, D_MODEL, XA_WIDTH), D_MODEL)
    xa_wkv = dense((L, D_MODEL, 2 * XA_WIDTH), D_MODEL)
    xa_q_gain = gain((L, XA_HEAD_DIM))
    xa_k_gain = gain((L, XA_HEAD_DIM))
    xa_mem_gain = gain((L, D_MODEL))
    xa_wo = dense((L, XA_WIDTH, D_MODEL), XA_WIDTH)
    ffn_w_in = dense((L, D_MODEL, 2 * FFN_HIDDEN), D_MODEL)
    ffn_w_out = dense((L, FFN_HIDDEN, D_MODEL), FFN_HIDDEN)
    return {'x': x, 'mem': mem, 'rel_bias': rel_bias, 'ln_mix': ln_mix, 'ln_mem': ln_mem, 'ln_ffn': ln_ffn,
            'hyb_w_in': hyb_w_in, 'hyb_w_out': hyb_w_out, 'nsa_q_gain': nsa_q_gain, 'nsa_k_gain': nsa_k_gain,
            'nsa_cmp_pos': nsa_cmp_pos, 'nsa_cmp_w1': nsa_cmp_w1, 'nsa_cmp_w2': nsa_cmp_w2,
            'ret_gn_gain': ret_gn_gain, 'gdn_w_in': gdn_w_in, 'gdn_conv_w': gdn_conv_w, 'gdn_a_log': gdn_a_log,
            'gdn_dt_bias': gdn_dt_bias, 'gdn_norm_gain': gdn_norm_gain, 'gdn_w_out': gdn_w_out,
            'xa_wq': xa_wq, 'xa_wkv': xa_wkv, 'xa_q_gain': xa_q_gain, 'xa_k_gain': xa_k_gain,
            'xa_mem_gain': xa_mem_gain, 'xa_wo': xa_wo, 'ffn_w_in': ffn_w_in, 'ffn_w_out': ffn_w_out}


def reference(x, mem, rel_bias, ln_mix, ln_mem, ln_ffn, hyb_w_in, hyb_w_out, nsa_q_gain, nsa_k_gain,
              nsa_cmp_pos, nsa_cmp_w1, nsa_cmp_w2, ret_gn_gain, gdn_w_in, gdn_conv_w, gdn_a_log,
              gdn_dt_bias, gdn_norm_gain, gdn_w_out, xa_wq, xa_wkv, xa_q_gain, xa_k_gain, xa_mem_gain,
              xa_wo, ffn_w_in, ffn_w_out):
    h = x
    for layer in range(DEPTH):
        hn = rms_norm(h, ln_mix[layer])
        if layer % 2 == 0:
            e = layer // 2
            mix = nsa_retention_mixer(hn, hyb_w_in[e], hyb_w_out[e], nsa_q_gain[e], nsa_k_gain[e],
                                      nsa_cmp_pos[e], nsa_cmp_w1[e], nsa_cmp_w2[e], ret_gn_gain[e], rel_bias)
        else:
            o = layer // 2
            mix = gated_deltanet_mixer(hn, gdn_w_in[o], gdn_conv_w[o], gdn_a_log[o], gdn_dt_bias[o],
                                       gdn_norm_gain[o], gdn_w_out[o])
        h = h + mix
        h = h + memory_cross_attention(rms_norm(h, ln_mem[layer]), mem, xa_mem_gain[layer], xa_wq[layer],
                                       xa_wkv[layer], xa_q_gain[layer], xa_k_gain[layer], xa_wo[layer])
        h = h + swiglu_ffn(rms_norm(h, ln_ffn[layer]), ffn_w_in[layer], ffn_w_out[layer])
    return h
```

```python
import functools
import math

import jax
import jax.numpy as jnp
import numpy as np
from jax import lax
from jax.experimental import pallas as pl
from jax.experimental.pallas import tpu as pltpu

F32 = jnp.float32
BF16 = jnp.bfloat16

D_MODEL = 2048
DEPTH = 4
NORM_EPS = 1e-6
NEG_INF = -1e30

NSA_HEADS = 8
NSA_KV_GROUPS = 2
NSA_GROUP_HEADS = NSA_HEADS // NSA_KV_GROUPS
NSA_HEAD_DIM = 128
NSA_CMP_LEN = 32
NSA_CMP_STRIDE = 16
NSA_SEL_BLOCK = 64
NSA_N_SELECT = 16
NSA_WINDOW = 512
NSA_Q_BLOCK = 64
NSA_FORCE_SCORE = 1e6
NSA_WIDTH = NSA_HEADS * NSA_HEAD_DIM
NSA_KV_WIDTH = NSA_KV_GROUPS * NSA_HEAD_DIM

RET_HEADS = 4
RET_DK = 256
RET_DV = 256
RET_CHUNK = 128
RET_ROPE_BASE = 10000.0
RET_GN_EPS = 1e-5

GDN_QK_HEADS = 16
GDN_V_HEADS = 32
GDN_HEAD_DIM = 128
GDN_CONV = 4
GDN_CHUNK = 64
GDN_QK_WIDTH = GDN_QK_HEADS * GDN_HEAD_DIM
GDN_V_WIDTH = GDN_V_HEADS * GDN_HEAD_DIM
GDN_CONV_CH = 2 * GDN_QK_WIDTH + GDN_V_WIDTH

REL_BUCKETS = 32
REL_MAX_DIST = 1024

XA_HEADS = 4
XA_HEAD_DIM = 128
XA_WIDTH = XA_HEADS * XA_HEAD_DIM

V7X_VMEM_LIMIT_BYTES = 56 * 1024 * 1024


def _cparams(*sem):
    return pltpu.CompilerParams(dimension_semantics=sem, vmem_limit_bytes=V7X_VMEM_LIMIT_BYTES)


def _round_up(n, m):
    return -(-n // m) * m


def _rms_rows(x, gain):
    return x * lax.rsqrt(jnp.mean(x * x, axis=-1, keepdims=True) + NORM_EPS) * gain


def _norm_matmul_kernel(x_ref, g_ref, w_ref, o_ref, xn_ref):
    @pl.when(pl.program_id(1) == 0)
    def _():
        xn_ref[...] = _rms_rows(x_ref[...], g_ref[...]).astype(BF16)

    o_ref[...] = jnp.dot(xn_ref[...], w_ref[...], preferred_element_type=F32).astype(o_ref.dtype)


def norm_matmul(x, gain, w, *, tm=1024, tn=512, out_dtype=F32):
    m, k = x.shape
    n = w.shape[1]
    tm = min(tm, m)
    return pl.pallas_call(
        _norm_matmul_kernel,
        grid=(m // tm, n // tn),
        in_specs=[pl.BlockSpec((tm, k), lambda i, j: (i, 0)),
                  pl.BlockSpec((1, k), lambda i, j: (0, 0)),
                  pl.BlockSpec((k, tn), lambda i, j: (0, j))],
        out_specs=pl.BlockSpec((tm, tn), lambda i, j: (i, j)),
        out_shape=jax.ShapeDtypeStruct((m, n), out_dtype),
        scratch_shapes=[pltpu.VMEM((tm, k), BF16)],
        compiler_params=_cparams("parallel", "arbitrary"),
        name="norm_matmul",
    )(x, gain.reshape(1, k), w)


def _matmul_res_kernel(x_ref, w_ref, r_ref, o_ref):
    o_ref[...] = r_ref[...] + jnp.dot(x_ref[...], w_ref[...], preferred_element_type=F32)


def matmul_residual(x, w, res, *, tm=1024, tn=512):
    m, k = x.shape
    n = w.shape[1]
    return pl.pallas_call(
        _matmul_res_kernel,
        grid=(m // tm, n // tn),
        in_specs=[pl.BlockSpec((tm, k), lambda i, j: (i, 0)),
                  pl.BlockSpec((k, tn), lambda i, j: (0, j)),
                  pl.BlockSpec((tm, tn), lambda i, j: (i, j))],
        out_specs=pl.BlockSpec((tm, tn), lambda i, j: (i, j)),
        out_shape=jax.ShapeDtypeStruct((m, n), F32),
        compiler_params=_cparams("parallel", "arbitrary"),
        name="matmul_residual",
    )(x, w, res)


def _ffn_kernel(x_ref, g_ref, wg_ref, wu_ref, wo_ref, o_ref, xn_ref, acc_ref):
    j = pl.program_id(1)

    @pl.when(j == 0)
    def _():
        x = x_ref[...]
        xn_ref[...] = _rms_rows(x, g_ref[...]).astype(BF16)
        acc_ref[...] = x

    xn = xn_ref[...]
    gate = jnp.dot(xn, wg_ref[...], preferred_element_type=F32)
    up = jnp.dot(xn, wu_ref[...], preferred_element_type=F32)
    act = (gate * jax.nn.sigmoid(gate) * up).astype(BF16)
    acc_ref[...] += jnp.dot(act, wo_ref[...], preferred_element_type=F32)

    @pl.when(j == pl.num_programs(1) - 1)
    def _():
        o_ref[...] = acc_ref[...]


def ffn_residual(h, gain, w_in, w_out, *, tm=512, th=512):
    m, d = h.shape
    hidden = w_out.shape[0]
    nh = hidden // th
    return pl.pallas_call(
        _ffn_kernel,
        grid=(m // tm, nh),
        in_specs=[pl.BlockSpec((tm, d), lambda i, j: (i, 0)),
                  pl.BlockSpec((1, d), lambda i, j: (0, 0)),
                  pl.BlockSpec((d, th), lambda i, j: (0, j)),
                  pl.BlockSpec((d, th), lambda i, j: (0, j + nh)),
                  pl.BlockSpec((th, d), lambda i, j: (j, 0))],
        out_specs=pl.BlockSpec((tm, d), lambda i, j: (i, 0)),
        out_shape=jax.ShapeDtypeStruct((m, d), F32),
        scratch_shapes=[pltpu.VMEM((tm, d), BF16), pltpu.VMEM((tm, d), F32)],
        compiler_params=_cparams("parallel", "arbitrary"),
        name="ffn_residual",
    )(h, gain.reshape(1, d), w_in, w_in, w_out)


def _xattn_kernel(h_ref, g_ref, wq_ref, qg_ref, k_ref, v_ref, wo_ref, o_ref):
    x = h_ref[0]
    xn = _rms_rows(x, g_ref[...]).astype(BF16)
    q = jnp.dot(xn, wq_ref[...], preferred_element_type=F32)
    k = k_ref[0]
    v = v_ref[0]
    outs = []
    for hh in range(XA_HEADS):
        sl = slice(hh * XA_HEAD_DIM, (hh + 1) * XA_HEAD_DIM)
        qh = _rms_rows(q[:, sl], qg_ref[...]).astype(BF16)
        s = lax.dot_general(qh, k[:, sl], (((1,), (1,)), ((), ())), preferred_element_type=F32)
        s = s * (XA_HEAD_DIM ** -0.5)
        s = s - jnp.max(s, axis=-1, keepdims=True)
        p = jnp.exp(s)
        p = p / jnp.sum(p, axis=-1, keepdims=True)
        outs.append(jnp.dot(p.astype(BF16), v[:, sl], preferred_element_type=F32))
    o = jnp.concatenate(outs, axis=-1).astype(BF16)
    o_ref[0] = x + jnp.dot(o, wo_ref[...], preferred_element_type=F32)


def xattn_residual(h, gain, wq, q_gain, k, v, wo, *, tm=512):
    b, s, d = h.shape
    mt = k.shape[1]
    return pl.pallas_call(
        _xattn_kernel,
        grid=(b, s // tm),
        in_specs=[pl.BlockSpec((1, tm, d), lambda bi, i: (bi, i, 0)),
                  pl.BlockSpec((1, d), lambda bi, i: (0, 0)),
                  pl.BlockSpec((d, XA_WIDTH), lambda bi, i: (0, 0)),
                  pl.BlockSpec((1, XA_HEAD_DIM), lambda bi, i: (0, 0)),
                  pl.BlockSpec((1, mt, XA_WIDTH), lambda bi, i: (bi, 0, 0)),
                  pl.BlockSpec((1, mt, XA_WIDTH), lambda bi, i: (bi, 0, 0)),
                  pl.BlockSpec((XA_WIDTH, d), lambda bi, i: (0, 0))],
        out_specs=pl.BlockSpec((1, tm, d), lambda bi, i: (bi, i, 0)),
        out_shape=jax.ShapeDtypeStruct((b, s, d), F32),
        compiler_params=_cparams("parallel", "parallel"),
        name="xattn_residual",
    )(h, gain.reshape(1, d), wq, q_gain.reshape(1, XA_HEAD_DIM), k, v, wo)


def rms_norm(x, gain, eps=NORM_EPS):
    x32 = x.astype(F32)
    y = x32 * lax.rsqrt(jnp.mean(x32 * x32, axis=-1, keepdims=True) + eps)
    return (y * gain.astype(F32)).astype(x.dtype)


def l2_norm(x, eps=NORM_EPS):
    x32 = x.astype(F32)
    return x32 * lax.rsqrt(jnp.sum(x32 * x32, axis=-1, keepdims=True) + eps)


def masked_softmax(logits, mask):
    p = jax.nn.softmax(jnp.where(mask, logits.astype(F32), NEG_INF), axis=-1)
    return jnp.where(mask, p, 0.0)


def rel_bucket(dist):
    dist = jnp.maximum(dist, 0)
    max_exact = REL_BUCKETS // 2
    scaled = (jnp.log(jnp.maximum(dist, max_exact).astype(F32) / max_exact)
              / math.log(REL_MAX_DIST / max_exact) * (REL_BUCKETS - max_exact))
    large = jnp.minimum(max_exact + scaled.astype(jnp.int32), REL_BUCKETS - 1)
    return jnp.where(dist < max_exact, dist, large)


def rotary(x, pos):
    half = x.shape[-1] // 2
    inv = RET_ROPE_BASE ** (-jnp.arange(half, dtype=F32) / half)
    ang = pos.astype(F32)[:, None] * inv[None, :]
    cos, sin = jnp.cos(ang)[None, :, None, :], jnp.sin(ang)[None, :, None, :]
    x1, x2 = x[..., :half], x[..., half:]
    return jnp.concatenate([x1 * cos - x2 * sin, x1 * sin + x2 * cos], axis=-1)


def causal_depthwise_conv(x, w):
    k, ch = w.shape
    return lax.conv_general_dilated(x, w[:, None, :].astype(x.dtype), window_strides=(1,),
                                    padding=[(k - 1, 0)], dimension_numbers=('NWC', 'WIO', 'NWC'),
                                    feature_group_count=ch)


def compress_blocks(t, pos, w1, w2):
    b, s, g, dh = t.shape
    r = NSA_CMP_LEN // NSA_CMP_STRIDE
    n_chunks = s // NSA_CMP_STRIDE
    n_cmp = n_chunks - r + 1
    c = t.reshape(b, n_chunks, NSA_CMP_STRIDE, g, dh)
    blocks = jnp.concatenate([c[:, j:j + n_cmp] for j in range(r)], axis=2)
    blocks = blocks + pos[None, None, :, None, :].astype(t.dtype)
    flat = blocks.transpose(0, 3, 1, 2, 4).reshape(b, g, n_cmp, NSA_CMP_LEN * dh)
    return jax.nn.silu(flat @ w1) @ w2


def nsa_attention(q, k_cmp, v_cmp, k_sel, v_sel, k_win, v_win, gate_logits,
                  q_gain, k_gain, cmp_pos, cmp_w1, cmp_w2, rel_bias):
    B, S = q.shape[0], q.shape[1]
    G, HG, dh = NSA_KV_GROUPS, NSA_GROUP_HEADS, NSA_HEAD_DIM
    QB, W, SB = NSA_Q_BLOCK, NSA_WINDOW, NSA_SEL_BLOCK
    q = rms_norm(q, q_gain) * (dh ** -0.5)
    q = q.reshape(B, S, G, HG, dh).transpose(0, 2, 3, 1, 4)

    kc = rms_norm(compress_blocks(k_cmp, cmp_pos[0], cmp_w1[0], cmp_w2[0]), k_gain[0])
    vc = compress_blocks(v_cmp, cmp_pos[1], cmp_w1[1], cmp_w2[1])
    n_cmp = kc.shape[2]
    n_sb = S // SB
    ks_blocks = rms_norm(k_sel, k_gain[1]).transpose(0, 2, 1, 3).reshape(B, G, n_sb, SB, dh)
    vs_blocks = v_sel.transpose(0, 2, 1, 3).reshape(B, G, n_sb, SB, dh)
    pad = ((0, 0), (0, 0), (W, 0), (0, 0))
    kw_pad = jnp.pad(rms_norm(k_win, k_gain[2]).transpose(0, 2, 1, 3), pad)
    vw_pad = jnp.pad(v_win.transpose(0, 2, 1, 3), pad)
    gates = jax.nn.sigmoid(gate_logits.astype(F32)).reshape(B, S, G, HG, 3).transpose(0, 2, 3, 1, 4)

    cmp_start = jnp.arange(n_cmp) * NSA_CMP_STRIDE
    cmp_end = cmp_start + NSA_CMP_LEN - 1
    sel_start = jnp.arange(n_sb) * SB
    cover = ((cmp_start[:, None] < sel_start[None, :] + SB) & (cmp_end[:, None] >= sel_start[None, :])).astype(F32)
    n_top = min(NSA_N_SELECT, n_sb)
    b_idx = jnp.arange(B)[:, None, None, None]
    g_idx = jnp.arange(G)[None, :, None, None]
    bias_by_group = rel_bias.reshape(REL_BUCKETS, G, HG)
    blk = jnp.arange(n_sb)
    n_qb = S // QB

    def head_bias(dist):
        return jnp.moveaxis(rel_bias[rel_bucket(dist)], -1, 0).reshape((G, HG) + dist.shape).astype(F32)

    def query_block(args):
        qb, gb, iq = args
        t = iq * QB + jnp.arange(QB)
        dist_c = t[:, None] - cmp_end[None, :]
        logit_c = jnp.einsum('bghqd,bgnd->bghqn', qb, kc).astype(F32) + head_bias(dist_c)
        p_c = masked_softmax(logit_c, dist_c >= 0)
        o_c = jnp.einsum('bghqn,bgnd->bghqd', p_c.astype(vc.dtype), vc)
        imp = jnp.einsum('bghqn,nj->bgqj', p_c, cover)
        cur = (t // SB)[:, None]
        forced = (blk[None, :] == 0) | (blk[None, :] == cur) | (blk[None, :] == cur - 1)
        imp = jnp.where(forced, NSA_FORCE_SCORE, imp)
        imp = jnp.where(blk[None, :] > cur, -NSA_FORCE_SCORE, imp)
        _, sel = lax.top_k(imp, n_top)
        k_g = ks_blocks[b_idx, g_idx, sel].reshape(B, G, QB, n_top * SB, dh)
        v_g = vs_blocks[b_idx, g_idx, sel].reshape(B, G, QB, n_top * SB, dh)
        kpos = (sel[..., None] * SB + jnp.arange(SB)).reshape(B, G, QB, n_top * SB)
        dist_s = t[None, None, :, None] - kpos
        bias_s = bias_by_group[rel_bucket(dist_s), g_idx].transpose(0, 1, 4, 2, 3).astype(F32)
        logit_s = jnp.einsum('bghqd,bgqkd->bghqk', qb, k_g).astype(F32) + bias_s
        p_s = masked_softmax(logit_s, (dist_s >= 0)[:, :, None])
        o_s = jnp.einsum('bghqk,bgqkd->bghqd', p_s.astype(v_g.dtype), v_g)
        start = iq * QB
        k_w = lax.dynamic_slice_in_dim(kw_pad, start, QB + W, axis=2)
        v_w = lax.dynamic_slice_in_dim(vw_pad, start, QB + W, axis=2)
        wpos = start - W + jnp.arange(QB + W)
        dist_w = t[:, None] - wpos[None, :]
        mask_w = (dist_w >= 0) & (dist_w < W) & (wpos[None, :] >= 0)
        logit_w = jnp.einsum('bghqd,bgkd->bghqk', qb, k_w).astype(F32) + head_bias(dist_w)
        p_w = masked_softmax(logit_w, mask_w)
        o_w = jnp.einsum('bghqk,bgkd->bghqd', p_w.astype(v_w.dtype), v_w)
        out = gb[..., 0:1] * o_c + gb[..., 1:2] * o_s + gb[..., 2:3] * o_w
        return out.astype(qb.dtype)

    q_blocks = q.reshape(B, G, HG, n_qb, QB, dh).transpose(3, 0, 1, 2, 4, 5)
    g_blocks = gates.reshape(B, G, HG, n_qb, QB, 3).transpose(3, 0, 1, 2, 4, 5)
    outs = lax.map(query_block, (q_blocks, g_blocks, jnp.arange(n_qb)))
    return outs.transpose(1, 0, 4, 2, 3, 5).reshape(B, S, NSA_WIDTH)


def retention(q, k, v, gate, gn_gain):
    B, S, H, dk = q.shape
    dv = v.shape[-1]
    C = RET_CHUNK
    N = S // C
    pos = jnp.arange(S)
    q = rotary(q.astype(F32), pos) * (dk ** -0.5)
    k = rotary(k.astype(F32), pos)
    v = v.astype(F32)
    log_gamma = jnp.log(1.0 - 2.0 ** (-5.0 - jnp.arange(H, dtype=F32)))
    idx = jnp.arange(C, dtype=F32)
    rel = idx[:, None] - idx[None, :]
    decay_in = jnp.where(rel >= 0, jnp.exp(log_gamma[:, None, None] * jnp.maximum(rel, 0.0)), 0.0)
    q_decay = jnp.exp(log_gamma[:, None] * (idx + 1.0))
    k_decay = jnp.exp(log_gamma[:, None] * (C - 1.0 - idx))
    chunk_decay = jnp.exp(log_gamma * C)

    def to_chunks(t):
        return t.reshape(B, N, C, H, t.shape[-1]).transpose(1, 0, 3, 2, 4)

    def step(state, inp):
        qn, kn, vn = inp
        inner = jnp.einsum('bhcd,bhmd->bhcm', qn, kn) * decay_in
        o = (jnp.einsum('bhcm,bhme->bhce', inner, vn)
             + jnp.einsum('bhcd,bhde->bhce', qn * q_decay[..., None], state))
        state = state * chunk_decay[:, None, None] + jnp.einsum('bhmd,bhme->bhde', kn * k_decay[..., None], vn)
        return state, o

    _, o = lax.scan(step, jnp.zeros((B, H, dk, dv), F32), (to_chunks(q), to_chunks(k), to_chunks(v)))
    o = o.transpose(1, 0, 3, 2, 4).reshape(B, S, H, dv)
    mu = jnp.mean(o, axis=-1, keepdims=True)
    var = jnp.mean(jnp.square(o - mu), axis=-1, keepdims=True)
    o = ((o - mu) * lax.rsqrt(var + RET_GN_EPS)).reshape(B, S, H * dv) * gn_gain.astype(F32)
    return (jax.nn.silu(gate.astype(F32)) * o).astype(gate.dtype)


def chunk_gated_delta_rule(q, k, v, beta, g):
    B, S, H, dk = q.shape
    dv = v.shape[-1]
    C = GDN_CHUNK
    N = S // C

    def chunks(t):
        return jnp.moveaxis(t.astype(F32).reshape((B, N, C, H) + t.shape[3:]), 3, 1)

    q = chunks(q) * (dk ** -0.5)
    k, v, beta, g = chunks(k), chunks(v), chunks(beta), chunks(g)
    gc = jnp.cumsum(g, axis=-1)
    i = jnp.arange(C)
    causal = i[:, None] >= i[None, :]
    strict = i[:, None] > i[None, :]
    decay = jnp.exp(jnp.where(causal, gc[..., :, None] - gc[..., None, :], -jnp.inf))
    kb = k * beta[..., None]
    lower = jnp.where(strict, jnp.einsum('bhncd,bhnmd->bhncm', kb, k) * decay, 0.0)
    rhs = jnp.concatenate([v * beta[..., None], kb * jnp.exp(gc)[..., None]], axis=-1)
    sol = lax.linalg.triangular_solve(lower + jnp.eye(C, dtype=F32), rhs, left_side=True,
                                      lower=True, unit_diagonal=True)
    u, w = sol[..., :dv], sol[..., dv:]
    attn = jnp.where(causal, jnp.einsum('bhncd,bhnmd->bhncm', q, k) * decay, 0.0)
    q_dec = q * jnp.exp(gc)[..., None]
    k_dec = k * jnp.exp(gc[..., -1:] - gc)[..., None]
    last = jnp.exp(gc[..., -1])
    xs = tuple(jnp.moveaxis(t, 2, 0) for t in (u, w, attn, q_dec, k_dec, last))

    def step(state, inp):
        u_n, w_n, attn_n, qd_n, kd_n, last_n = inp
        v_new = u_n - w_n @ state
        o = qd_n @ state + attn_n @ v_new
        state = state * last_n[..., None, None] + jnp.einsum('bhcd,bhce->bhde', kd_n, v_new)
        return state, o

    _, o = lax.scan(step, jnp.zeros((B, H, dk, dv), F32), xs)
    return o.transpose(1, 0, 3, 2, 4).reshape(B, S, H, dv)


def _pad_cols(w, mult):
    n = w.shape[1]
    return jnp.pad(w, ((0, 0), (0, _round_up(n, mult) - n)))


def hybrid_mixer(h, ln_gain, w_in, w_out, q_gain, k_gain, cmp_pos, cmp_w1, cmp_w2, gn_gain, rel_bias):
    B, S, D = h.shape
    n_in = w_in.shape[1]
    proj = norm_matmul(h.reshape(B * S, D), ln_gain, _pad_cols(w_in, 512).astype(BF16))
    proj = proj[:, :n_in].reshape(B, S, n_in)
    widths = ([NSA_WIDTH] + [NSA_KV_WIDTH] * 6 + [3 * NSA_HEADS]
              + [RET_HEADS * RET_DK] * 2 + [RET_HEADS * RET_DV] * 2)
    cuts = [int(c) for c in np.cumsum(widths)[:-1]]
    (nq, kc, vc, ks, vs, kw, vw, gl, rq, rk, rv, rg) = jnp.split(proj, cuts, axis=-1)
    kv = lambda t: t.reshape(B, S, NSA_KV_GROUPS, NSA_HEAD_DIM)
    a_out = nsa_attention(nq.reshape(B, S, NSA_HEADS, NSA_HEAD_DIM), kv(kc), kv(vc), kv(ks), kv(vs),
                          kv(kw), kv(vw), gl.reshape(B, S, NSA_HEADS, 3),
                          q_gain, k_gain, cmp_pos, cmp_w1, cmp_w2, rel_bias)
    b_out = retention(rq.reshape(B, S, RET_HEADS, RET_DK), rk.reshape(B, S, RET_HEADS, RET_DK),
                      rv.reshape(B, S, RET_HEADS, RET_DV), rg, gn_gain)
    mix = jnp.concatenate([a_out, b_out], axis=-1).astype(BF16).reshape(B * S, -1)
    return matmul_residual(mix, w_out.astype(BF16), h.reshape(B * S, D)).reshape(B, S, D)


def gdn_mixer(h, ln_gain, w_in, conv_w, a_log, dt_bias, norm_gain, w_out):
    B, S, D = h.shape
    n_in = w_in.shape[1]
    proj = norm_matmul(h.reshape(B * S, D), ln_gain, _pad_cols(w_in, 512).astype(BF16))
    proj = proj[:, :n_in].reshape(B, S, n_in)
    cuts = [GDN_CONV_CH, GDN_CONV_CH + GDN_V_WIDTH, GDN_CONV_CH + GDN_V_WIDTH + GDN_V_HEADS]
    qkv, z, b, a = jnp.split(proj, cuts, axis=-1)
    qkv = jax.nn.silu(causal_depthwise_conv(qkv, conv_w))
    q, k, v = jnp.split(qkv, [GDN_QK_WIDTH, 2 * GDN_QK_WIDTH], axis=-1)
    rep = GDN_V_HEADS // GDN_QK_HEADS
    q = jnp.repeat(l2_norm(q.reshape(B, S, GDN_QK_HEADS, GDN_HEAD_DIM)), rep, axis=2)
    k = jnp.repeat(l2_norm(k.reshape(B, S, GDN_QK_HEADS, GDN_HEAD_DIM)), rep, axis=2)
    v = v.reshape(B, S, GDN_V_HEADS, GDN_HEAD_DIM)
    beta = jax.nn.sigmoid(b.astype(F32))
    g = -jnp.exp(a_log.astype(F32)) * jax.nn.softplus(a.astype(F32) + dt_bias.astype(F32))
    o = chunk_gated_delta_rule(q, k, v, beta, g)
    o = rms_norm(o, norm_gain) * jax.nn.silu(z.astype(F32).reshape(B, S, GDN_V_HEADS, GDN_HEAD_DIM))
    o = o.reshape(B * S, GDN_V_WIDTH).astype(BF16)
    return matmul_residual(o, w_out.astype(BF16), h.reshape(B * S, D)).reshape(B, S, D)


def memory_kv(mem, mem_gain, wkv, k_gain):
    B, M, D = mem.shape
    kv = norm_matmul(mem.reshape(B * M, D), mem_gain, wkv.astype(BF16))
    k, v = jnp.split(kv.reshape(B, M, 2 * XA_WIDTH), 2, axis=-1)
    k = rms_norm(k.reshape(B, M, XA_HEADS, XA_HEAD_DIM), k_gain).reshape(B, M, XA_WIDTH)
    return k.astype(BF16), v.astype(BF16)


def kernel(x, mem, rel_bias, ln_mix, ln_mem, ln_ffn, hyb_w_in, hyb_w_out, nsa_q_gain, nsa_k_gain,
           nsa_cmp_pos, nsa_cmp_w1, nsa_cmp_w2, ret_gn_gain, gdn_w_in, gdn_conv_w, gdn_a_log,
           gdn_dt_bias, gdn_norm_gain, gdn_w_out, xa_wq, xa_wkv, xa_q_gain, xa_k_gain, xa_mem_gain,
           xa_wo, ffn_w_in, ffn_w_out):
    B, S, D = x.shape
    h = x
    for layer in range(DEPTH):
        if layer % 2 == 0:
            e = layer // 2
            h = hybrid_mixer(h, ln_mix[layer], hyb_w_in[e], hyb_w_out[e], nsa_q_gain[e], nsa_k_gain[e],
                             nsa_cmp_pos[e], nsa_cmp_w1[e], nsa_cmp_w2[e], ret_gn_gain[e], rel_bias)
        else:
            o = layer // 2
            h = gdn_mixer(h, ln_mix[layer], gdn_w_in[o], gdn_conv_w[o], gdn_a_log[o], gdn_dt_bias[o],
                          gdn_norm_gain[o], gdn_w_out[o])
        k_mem, v_mem = memory_kv(mem, xa_mem_gain[layer], xa_wkv[layer], xa_k_gain[layer])
        h = xattn_residual(h, ln_mem[layer], xa_wq[layer].astype(BF16), xa_q_gain[layer], k_mem, v_mem,
                           xa_wo[layer].astype(BF16))
        h = ffn_residual(h.reshape(B * S, D), ln_ffn[layer], ffn_w_in[layer].astype(BF16),
                         ffn_w_out[layer].astype(BF16)).reshape(B, S, D)
    return h
```

```python
import functools
import math

import jax
import jax.numpy as jnp
import numpy as np
from jax import lax
from jax.experimental import pallas as pl
from jax.experimental.pallas import tpu as pltpu

F32 = jnp.float32
BF16 = jnp.bfloat16

D_MODEL = 2048
DEPTH = 4
NORM_EPS = 1e-6
NEG_INF = -1e30

NSA_HEADS = 8
NSA_KV_GROUPS = 2
NSA_GROUP_HEADS = NSA_HEADS // NSA_KV_GROUPS
NSA_HEAD_DIM = 128
NSA_CMP_LEN = 32
NSA_CMP_STRIDE = 16
NSA_SEL_BLOCK = 64
NSA_N_SELECT = 16
NSA_WINDOW = 512
NSA_Q_BLOCK = 64
NSA_FORCE_SCORE = 1e6
NSA_WIDTH = NSA_HEADS * NSA_HEAD_DIM
NSA_KV_WIDTH = NSA_KV_GROUPS * NSA_HEAD_DIM

RET_HEADS = 4
RET_DK = 256
RET_DV = 256
RET_CHUNK = 128
RET_ROPE_BASE = 10000.0
RET_GN_EPS = 1e-5

GDN_QK_HEADS = 16
GDN_V_HEADS = 32
GDN_HEAD_DIM = 128
GDN_CONV = 4
GDN_CHUNK = 64
GDN_QK_WIDTH = GDN_QK_HEADS * GDN_HEAD_DIM
GDN_V_WIDTH = GDN_V_HEADS * GDN_HEAD_DIM
GDN_CONV_CH = 2 * GDN_QK_WIDTH + GDN_V_WIDTH

REL_BUCKETS = 32
REL_MAX_DIST = 1024

XA_HEADS = 4
XA_HEAD_DIM = 128
XA_WIDTH = XA_HEADS * XA_HEAD_DIM

V7X_VMEM_LIMIT_BYTES = 56 * 1024 * 1024


def _cparams(*sem):
    return pltpu.CompilerParams(dimension_semantics=sem, vmem_limit_bytes=V7X_VMEM_LIMIT_BYTES)


def _round_up(n, m):
    return -(-n // m) * m


def _rms_rows(x, gain):
    return x * lax.rsqrt(jnp.mean(x * x, axis=-1, keepdims=True) + NORM_EPS) * gain


def _norm_matmul_kernel(x_ref, g_ref, w_ref, o_ref, xn_ref):
    @pl.when(pl.program_id(1) == 0)
    def _():
        xn_ref[...] = _rms_rows(x_ref[...], g_ref[...]).astype(BF16)

    o_ref[...] = jnp.dot(xn_ref[...], w_ref[...], preferred_element_type=F32).astype(o_ref.dtype)


def norm_matmul(x, gain, w, *, tm=1024, tn=512, out_dtype=F32):
    m, k = x.shape
    n = w.shape[1]
    tm = min(tm, m)
    return pl.pallas_call(
        _norm_matmul_kernel,
        grid=(m // tm, n // tn),
        in_specs=[pl.BlockSpec((tm, k), lambda i, j: (i, 0)),
                  pl.BlockSpec((1, k), lambda i, j: (0, 0)),
                  pl.BlockSpec((k, tn), lambda i, j: (0, j))],
        out_specs=pl.BlockSpec((tm, tn), lambda i, j: (i, j)),
        out_shape=jax.ShapeDtypeStruct((m, n), out_dtype),
        scratch_shapes=[pltpu.VMEM((tm, k), BF16)],
        compiler_params=_cparams("parallel", "arbitrary"),
        name="norm_matmul",
    )(x, gain.reshape(1, k), w)


def _matmul_res_kernel(x_ref, w_ref, r_ref, o_ref):
    o_ref[...] = r_ref[...] + jnp.dot(x_ref[...], w_ref[...], preferred_element_type=F32)


def matmul_residual(x, w, res, *, tm=1024, tn=512):
    m, k = x.shape
    n = w.shape[1]
    return pl.pallas_call(
        _matmul_res_kernel,
        grid=(m // tm, n // tn),
        in_specs=[pl.BlockSpec((tm, k), lambda i, j: (i, 0)),
                  pl.BlockSpec((k, tn), lambda i, j: (0, j)),
                  pl.BlockSpec((tm, tn), lambda i, j: (i, j))],
        out_specs=pl.BlockSpec((tm, tn), lambda i, j: (i, j)),
        out_shape=jax.ShapeDtypeStruct((m, n), F32),
        compiler_params=_cparams("parallel", "arbitrary"),
        name="matmul_residual",
    )(x, w, res)


def _ffn_kernel(x_ref, g_ref, wg_ref, wu_ref, wo_ref, o_ref, xn_ref, acc_ref):
    j = pl.program_id(1)

    @pl.when(j == 0)
    def _():
        x = x_ref[...]
        xn_ref[...] = _rms_rows(x, g_ref[...]).astype(BF16)
        acc_ref[...] = x

    xn = xn_ref[...]
    gate = jnp.dot(xn, wg_ref[...], preferred_element_type=F32)
    up = jnp.dot(xn, wu_ref[...], preferred_element_type=F32)
    act = (gate * jax.nn.sigmoid(gate) * up).astype(BF16)
    acc_ref[...] += jnp.dot(act, wo_ref[...], preferred_element_type=F32)

    @pl.when(j == pl.num_programs(1) - 1)
    def _():
        o_ref[...] = acc_ref[...]


def ffn_residual(h, gain, w_in, w_out, *, tm=512, th=512):
    m, d = h.shape
    hidden = w_out.shape[0]
    nh = hidden // th
    return pl.pallas_call(
        _ffn_kernel,
        grid=(m // tm, nh),
        in_specs=[pl.BlockSpec((tm, d), lambda i, j: (i, 0)),
                  pl.BlockSpec((1, d), lambda i, j: (0, 0)),
                  pl.BlockSpec((d, th), lambda i, j: (0, j)),
                  pl.BlockSpec((d, th), lambda i, j: (0, j + nh)),
                  pl.BlockSpec((th, d), lambda i, j: (j, 0))],
        out_specs=pl.BlockSpec((tm, d), lambda i, j: (i, 0)),
        out_shape=jax.ShapeDtypeStruct((m, d), F32),
        scratch_shapes=[pltpu.VMEM((tm, d), BF16), pltpu.VMEM((tm, d), F32)],
        compiler_params=_cparams("parallel", "arbitrary"),
        name="ffn_residual",
    )(h, gain.reshape(1, d), w_in, w_in, w_out)


def _xattn_kernel(h_ref, g_ref, wq_ref, qg_ref, k_ref, v_ref, wo_ref, o_ref):
    x = h_ref[0]
    xn = _rms_rows(x, g_ref[...]).astype(BF16)
    q = jnp.dot(xn, wq_ref[...], preferred_element_type=F32)
    k = k_ref[0]
    v = v_ref[0]
    outs = []
    for hh in range(XA_HEADS):
        sl = slice(hh * XA_HEAD_DIM, (hh + 1) * XA_HEAD_DIM)
        qh = _rms_rows(q[:, sl], qg_ref[...]).astype(BF16)
        s = lax.dot_general(qh, k[:, sl], (((1,), (1,)), ((), ())), preferred_element_type=F32)
        s = s * (XA_HEAD_DIM ** -0.5)
        s = s - jnp.max(s, axis=-1, keepdims=True)
        p = jnp.exp(s)
        p = p / jnp.sum(p, axis=-1, keepdims=True)
        outs.append(jnp.dot(p.astype(BF16), v[:, sl], preferred_element_type=F32))
    o = jnp.concatenate(outs, axis=-1).astype(BF16)
    o_ref[0] = x + jnp.dot(o, wo_ref[...], preferred_element_type=F32)


def xattn_residual(h, gain, wq, q_gain, k, v, wo, *, tm=512):
    b, s, d = h.shape
    mt = k.shape[1]
    return pl.pallas_call(
        _xattn_kernel,
        grid=(b, s // tm),
        in_specs=[pl.BlockSpec((1, tm, d), lambda bi, i: (bi, i, 0)),
                  pl.BlockSpec((1, d), lambda bi, i: (0, 0)),
                  pl.BlockSpec((d, XA_WIDTH), lambda bi, i: (0, 0)),
                  pl.BlockSpec((1, XA_HEAD_DIM), lambda bi, i: (0, 0)),
                  pl.BlockSpec((1, mt, XA_WIDTH), lambda bi, i: (bi, 0, 0)),
                  pl.BlockSpec((1, mt, XA_WIDTH), lambda bi, i: (bi, 0, 0)),
                  pl.BlockSpec((XA_WIDTH, d), lambda bi, i: (0, 0))],
        out_specs=pl.BlockSpec((1, tm, d), lambda bi, i: (bi, i, 0)),
        out_shape=jax.ShapeDtypeStruct((b, s, d), F32),
        compiler_params=_cparams("parallel", "parallel"),
        name="xattn_residual",
    )(h, gain.reshape(1, d), wq, q_gain.reshape(1, XA_HEAD_DIM), k, v, wo)


def rms_norm(x, gain, eps=NORM_EPS):
    x32 = x.astype(F32)
    y = x32 * lax.rsqrt(jnp.mean(x32 * x32, axis=-1, keepdims=True) + eps)
    return (y * gain.astype(F32)).astype(x.dtype)


def l2_norm(x, eps=NORM_EPS):
    x32 = x.astype(F32)
    return x32 * lax.rsqrt(jnp.sum(x32 * x32, axis=-1, keepdims=True) + eps)


def masked_softmax(logits, mask):
    p = jax.nn.softmax(jnp.where(mask, logits.astype(F32), NEG_INF), axis=-1)
    return jnp.where(mask, p, 0.0)


def rel_bucket(dist):
    dist = jnp.maximum(dist, 0)
    max_exact = REL_BUCKETS // 2
    scaled = (jnp.log(jnp.maximum(dist, max_exact).astype(F32) / max_exact)
              / math.log(REL_MAX_DIST / max_exact) * (REL_BUCKETS - max_exact))
    large = jnp.minimum(max_exact + scaled.astype(jnp.int32), REL_BUCKETS - 1)
    return jnp.where(dist < max_exact, dist, large)


def rotary(x, pos):
    half = x.shape[-1] // 2
    inv = RET_ROPE_BASE ** (-jnp.arange(half, dtype=F32) / half)
    ang = pos.astype(F32)[:, None] * inv[None, :]
    cos, sin = jnp.cos(ang)[None, :, None, :], jnp.sin(ang)[None, :, None, :]
    x1, x2 = x[..., :half], x[..., half:]
    return jnp.concatenate([x1 * cos - x2 * sin, x1 * sin + x2 * cos], axis=-1)


def causal_depthwise_conv(x, w):
    k, ch = w.shape
    return lax.conv_general_dilated(x, w[:, None, :].astype(x.dtype), window_strides=(1,),
                                    padding=[(k - 1, 0)], dimension_numbers=('NWC', 'WIO', 'NWC'),
                                    feature_group_count=ch)


def compress_blocks(t, pos, w1, w2):
    b, s, g, dh = t.shape
    r = NSA_CMP_LEN // NSA_CMP_STRIDE
    n_chunks = s // NSA_CMP_STRIDE
    n_cmp = n_chunks - r + 1
    c = t.reshape(b, n_chunks, NSA_CMP_STRIDE, g, dh)
    blocks = jnp.concatenate([c[:, j:j + n_cmp] for j in range(r)], axis=2)
    blocks = blocks + pos[None, None, :, None, :].astype(t.dtype)
    flat = blocks.transpose(0, 3, 1, 2, 4).reshape(b, g, n_cmp, NSA_CMP_LEN * dh)
    return jax.nn.silu(flat @ w1) @ w2


NSA_QT = 128
NSA_KT = 512
NSA_WT = NSA_WINDOW + NSA_QT


def _dot_nt(a, b):
    return lax.dot_general(a, b, (((1,), (1,)), ((), ())), preferred_element_type=F32)


def _toeplitz(seg_row, rows):
    x = jnp.broadcast_to(seg_row, (rows, seg_row.shape[1]))
    return pltpu.roll(x, 0, 1, stride=1, stride_axis=0)[:, rows:]


def _nsa_kernel(q_ref, gl_ref, kc_ref, vc_ref, bc_ref, covt_ref, ks_ref, vs_ref, kw_ref, vw_ref,
                segs_ref, segw_ref, e_ref, qg_ref, o_ref, a_scr, *, n_top):
    QT, KT, WT, HG, dh = NSA_QT, NSA_KT, NSA_WT, NSA_GROUP_HEADS, NSA_HEAD_DIM
    nsb, nc = covt_ref.shape
    q0 = pl.program_id(2) * QT

    x = q_ref[0]
    qs = []
    for h in range(HG):
        xh = _rms_rows(x[:, h * dh:(h + 1) * dh], qg_ref[...]) * (dh ** -0.5)
        qs.append(xh.astype(BF16))
    qn = jnp.concatenate(qs, axis=0)

    sc = _dot_nt(qn, kc_ref[0, 0]) + bc_ref[...].reshape(HG * QT, nc)
    row = lax.broadcasted_iota(jnp.int32, (HG * QT, nc), 0)
    col = lax.broadcasted_iota(jnp.int32, (HG * QT, nc), 1)
    tq = q0 + (row & (QT - 1))
    mask_c = tq >= col * NSA_CMP_STRIDE + (NSA_CMP_LEN - 1)
    sc = jnp.where(mask_c, sc, NEG_INF)
    pc = jnp.where(mask_c, jnp.exp(sc - jnp.max(sc, axis=-1, keepdims=True)), 0.0)
    den = jnp.sum(pc, axis=-1, keepdims=True)
    pcb = (pc / jnp.maximum(den, 1e-30)).astype(BF16)
    o_c = jnp.dot(pcb, vc_ref[0, 0], preferred_element_type=F32)

    imp_all = _dot_nt(covt_ref[...], pcb)
    imp = imp_all[:, 0:QT]
    for h in range(1, HG):
        imp = imp + imp_all[:, h * QT:(h + 1) * QT]
    jj = lax.broadcasted_iota(jnp.int32, (nsb, QT), 0)
    cur = (q0 + lax.broadcasted_iota(jnp.int32, (nsb, QT), 1)) // NSA_SEL_BLOCK
    forced = (jj == 0) | (jj == cur) | (jj == cur - 1)
    a = jnp.where(forced, NSA_FORCE_SCORE, imp)
    a = jnp.where(jj > cur, -NSA_FORCE_SCORE, a)
    a_scr[...] = a

    def rank_body(i, rank):
        r = a_scr[pl.ds(i, 1), :]
        return rank + jnp.where((r > a) | ((r == a) & (jj > i)), 1.0, 0.0)

    rank = lax.fori_loop(0, nsb, rank_body, jnp.zeros((nsb, QT), F32))
    sel_t = jnp.where(rank < n_top, 1.0, 0.0)
    if nsb < 128:
        sel_t = jnp.concatenate([sel_t, jnp.zeros((128 - nsb, QT), F32)], axis=0)
    sel = sel_t.T.astype(BF16)

    cmr = (lax.broadcasted_iota(jnp.int32, (QT, KT), 1) - lax.broadcasted_iota(jnp.int32, (QT, KT), 0))

    def sel_body(kt, carry):
        m_i, l_i, acc = carry
        k0 = pl.multiple_of(kt * KT, KT)
        k = ks_ref[0, pl.ds(k0, KT), :]
        v = vs_ref[0, pl.ds(k0, KT), :]
        s = _dot_nt(qn, k)
        delta = q0 - k0
        visible = (jnp.dot(sel, e_ref[kt], preferred_element_type=F32) > 0.5) & (cmr <= delta)
        seg = delta // QT
        parts = []
        for h in range(HG):
            bias = _toeplitz(segs_ref[h, pl.ds(seg, 1), :], QT)
            parts.append(jnp.where(visible, s[h * QT:(h + 1) * QT] + bias, NEG_INF))
        s = jnp.concatenate(parts, axis=0)
        m_new = jnp.maximum(m_i, jnp.max(s, axis=-1, keepdims=True))
        alpha = jnp.exp(m_i - m_new)
        p = jnp.exp(s - m_new)
        l_new = alpha * l_i + jnp.sum(p, axis=-1, keepdims=True)
        acc = alpha * acc + jnp.dot(p.astype(BF16), v, preferred_element_type=F32)
        return m_new, l_new, acc

    init = (jnp.full((HG * QT, 1), NEG_INF, F32), jnp.zeros((HG * QT, 1), F32), jnp.zeros((HG * QT, dh), F32))
    _, l_s, acc_s = lax.fori_loop(0, q0 // KT + 1, sel_body, init)
    o_s = acc_s / l_s

    w0 = pl.multiple_of(q0, QT)
    kwin = kw_ref[0, pl.ds(w0, WT), :]
    vwin = vw_ref[0, pl.ds(w0, WT), :]
    sw = _dot_nt(qn, kwin)
    cw = lax.broadcasted_iota(jnp.int32, (QT, WT), 1)
    dist = NSA_WINDOW + lax.broadcasted_iota(jnp.int32, (QT, WT), 0) - cw
    vis_w = (dist >= 0) & (dist < NSA_WINDOW) & (cw >= NSA_WINDOW - q0)
    parts = []
    for h in range(HG):
        bias = _toeplitz(segw_ref[h], QT)
        parts.append(jnp.where(vis_w, sw[h * QT:(h + 1) * QT] + bias, NEG_INF))
    sw = jnp.concatenate(parts, axis=0)
    pw = jnp.exp(sw - jnp.max(sw, axis=-1, keepdims=True))
    pw = pw / jnp.sum(pw, axis=-1, keepdims=True)
    o_w = jnp.dot(pw.astype(BF16), vwin, preferred_element_type=F32)

    gates = jax.nn.sigmoid(gl_ref[0, 0])
    outs = []
    for h in range(HG):
        rs = slice(h * QT, (h + 1) * QT)
        outs.append(gates[:, 3 * h:3 * h + 1] * o_c[rs] + gates[:, 3 * h + 1:3 * h + 2] * o_s[rs]
                    + gates[:, 3 * h + 2:3 * h + 3] * o_w[rs])
    o_ref[0] = jnp.concatenate(outs, axis=-1).astype(o_ref.dtype)


def _rel_table(rel_bias, s):
    return rel_bias[rel_bucket(jnp.arange(s))].T.astype(F32)


def nsa_attention_pallas(q, kc, vc, ks, vs, kw, vw, gate_logits, q_gain, rel_bias):
    B, S, _ = q.shape
    G, HG, dh = NSA_KV_GROUPS, NSA_GROUP_HEADS, NSA_HEAD_DIM
    QT, KT, WT, W, SB = NSA_QT, NSA_KT, NSA_WT, NSA_WINDOW, NSA_SEL_BLOCK
    nsb, nc, nkt, nseg = S // SB, S // NSA_CMP_STRIDE, S // KT, S // QT
    n_top = min(NSA_N_SELECT, nsb)
    tab = _rel_table(rel_bias, S)
    dist_c = jnp.arange(S)[:, None] - (jnp.arange(nc) * NSA_CMP_STRIDE + NSA_CMP_LEN - 1)[None, :]
    bc = jnp.where(dist_c >= 0, tab[:, jnp.maximum(dist_c, 0)], 0.0)
    ds = QT * jnp.arange(nseg)[:, None] + QT - jnp.arange(KT + QT)[None, :]
    segs = jnp.where((ds >= 0) & (ds < S), tab[:, jnp.clip(ds, 0, S - 1)], 0.0)
    dw = WT - jnp.arange(WT + QT)
    segw = jnp.where((dw >= 0) & (dw < S), tab[:, jnp.clip(dw, 0, S - 1)], 0.0)[:, None, :]
    cmp_start = np.arange(nc) * NSA_CMP_STRIDE
    cmp_end = cmp_start + NSA_CMP_LEN - 1
    sel_start = np.arange(nsb) * SB
    cover_t = ((cmp_start[None, :] < sel_start[:, None] + SB) & (cmp_end[None, :] >= sel_start[:, None])
               & (np.arange(nc)[None, :] < nc - 1))
    cover_t = jnp.asarray(cover_t, BF16)
    e = (np.arange(128)[None, :, None]
         == (np.arange(nkt)[:, None, None] * (KT // SB) + np.arange(KT)[None, None, :] // SB))
    e = jnp.asarray(e, BF16)
    pad = ((0, 0), (W, 0), (0, 0))
    kw = jnp.pad(kw, pad)
    vw = jnp.pad(vw, pad)
    gl = gate_logits.reshape(B, S, G, 3 * HG).transpose(0, 2, 1, 3)

    kern = functools.partial(_nsa_kernel, n_top=n_top)
    return pl.pallas_call(
        kern,
        grid=(B, G, S // QT),
        in_specs=[
            pl.BlockSpec((1, QT, HG * dh), lambda b, g, i: (b, i, g)),
            pl.BlockSpec((1, 1, QT, 3 * HG), lambda b, g, i: (b, g, i, 0)),
            pl.BlockSpec((1, 1, nc, dh), lambda b, g, i: (b, g, 0, 0)),
            pl.BlockSpec((1, 1, nc, dh), lambda b, g, i: (b, g, 0, 0)),
            pl.BlockSpec((HG, QT, nc), lambda b, g, i: (g, i, 0)),
            pl.BlockSpec((nsb, nc), lambda b, g, i: (0, 0)),
            pl.BlockSpec((1, S, dh), lambda b, g, i: (b, 0, g)),
            pl.BlockSpec((1, S, dh), lambda b, g, i: (b, 0, g)),
            pl.BlockSpec((1, S + W, dh), lambda b, g, i: (b, 0, g)),
            pl.BlockSpec((1, S + W, dh), lambda b, g, i: (b, 0, g)),
            pl.BlockSpec((HG, nseg, KT + QT), lambda b, g, i: (g, 0, 0)),
            pl.BlockSpec((HG, 1, WT + QT), lambda b, g, i: (g, 0, 0)),
            pl.BlockSpec((nkt, 128, KT), lambda b, g, i: (0, 0, 0)),
            pl.BlockSpec((1, dh), lambda b, g, i: (0, 0)),
        ],
        out_specs=pl.BlockSpec((1, QT, HG * dh), lambda b, g, i: (b, i, g)),
        out_shape=jax.ShapeDtypeStruct((B, S, NSA_WIDTH), BF16),
        scratch_shapes=[pltpu.VMEM((nsb, QT), F32)],
        compiler_params=_cparams("parallel", "parallel", "arbitrary"),
        name="nsa_attention",
    )(q, gl, kc, vc, bc, cover_t, ks, vs, kw, vw, segs, segw, e, q_gain.reshape(1, dh))


def nsa_mixer(nq, kc, vc, ks, vs, kw, vw, gl, q_gain, k_gain, cmp_pos, cmp_w1, cmp_w2):
    B, S, _ = nq.shape
    kv = lambda t: t.reshape(B, S, NSA_KV_GROUPS, NSA_HEAD_DIM)
    kcc = rms_norm(compress_blocks(kv(kc), cmp_pos[0], cmp_w1[0], cmp_w2[0]), k_gain[0])
    vcc = compress_blocks(kv(vc), cmp_pos[1], cmp_w1[1], cmp_w2[1])
    padc = ((0, 0), (0, 0), (0, 1), (0, 0))
    kcc = jnp.pad(kcc, padc).astype(BF16)
    vcc = jnp.pad(vcc, padc).astype(BF16)
    ksn = rms_norm(kv(ks), k_gain[1]).reshape(B, S, -1).astype(BF16)
    kwn = rms_norm(kv(kw), k_gain[2]).reshape(B, S, -1).astype(BF16)
    return kcc, vcc, ksn, vs.astype(BF16), kwn, vw.astype(BF16)


def retention(q, k, v, gate, gn_gain):
    B, S, H, dk = q.shape
    dv = v.shape[-1]
    C = RET_CHUNK
    N = S // C
    pos = jnp.arange(S)
    q = rotary(q.astype(F32), pos) * (dk ** -0.5)
    k = rotary(k.astype(F32), pos)
    v = v.astype(F32)
    log_gamma = jnp.log(1.0 - 2.0 ** (-5.0 - jnp.arange(H, dtype=F32)))
    idx = jnp.arange(C, dtype=F32)
    rel = idx[:, None] - idx[None, :]
    decay_in = jnp.where(rel >= 0, jnp.exp(log_gamma[:, None, None] * jnp.maximum(rel, 0.0)), 0.0)
    q_decay = jnp.exp(log_gamma[:, None] * (idx + 1.0))
    k_decay = jnp.exp(log_gamma[:, None] * (C - 1.0 - idx))
    chunk_decay = jnp.exp(log_gamma * C)

    def to_chunks(t):
        return t.reshape(B, N, C, H, t.shape[-1]).transpose(1, 0, 3, 2, 4)

    def step(state, inp):
        qn, kn, vn = inp
        inner = jnp.einsum('bhcd,bhmd->bhcm', qn, kn) * decay_in
        o = (jnp.einsum('bhcm,bhme->bhce', inner, vn)
             + jnp.einsum('bhcd,bhde->bhce', qn * q_decay[..., None], state))
        state = state * chunk_decay[:, None, None] + jnp.einsum('bhmd,bhme->bhde', kn * k_decay[..., None], vn)
        return state, o

    _, o = lax.scan(step, jnp.zeros((B, H, dk, dv), F32), (to_chunks(q), to_chunks(k), to_chunks(v)))
    o = o.transpose(1, 0, 3, 2, 4).reshape(B, S, H, dv)
    mu = jnp.mean(o, axis=-1, keepdims=True)
    var = jnp.mean(jnp.square(o - mu), axis=-1, keepdims=True)
    o = ((o - mu) * lax.rsqrt(var + RET_GN_EPS)).reshape(B, S, H * dv) * gn_gain.astype(F32)
    return (jax.nn.silu(gate.astype(F32)) * o).astype(gate.dtype)


def chunk_gated_delta_rule(q, k, v, beta, g):
    B, S, H, dk = q.shape
    dv = v.shape[-1]
    C = GDN_CHUNK
    N = S // C

    def chunks(t):
        return jnp.moveaxis(t.astype(F32).reshape((B, N, C, H) + t.shape[3:]), 3, 1)

    q = chunks(q) * (dk ** -0.5)
    k, v, beta, g = chunks(k), chunks(v), chunks(beta), chunks(g)
    gc = jnp.cumsum(g, axis=-1)
    i = jnp.arange(C)
    causal = i[:, None] >= i[None, :]
    strict = i[:, None] > i[None, :]
    decay = jnp.exp(jnp.where(causal, gc[..., :, None] - gc[..., None, :], -jnp.inf))
    kb = k * beta[..., None]
    lower = jnp.where(strict, jnp.einsum('bhncd,bhnmd->bhncm', kb, k) * decay, 0.0)
    rhs = jnp.concatenate([v * beta[..., None], kb * jnp.exp(gc)[..., None]], axis=-1)
    sol = lax.linalg.triangular_solve(lower + jnp.eye(C, dtype=F32), rhs, left_side=True,
                                      lower=True, unit_diagonal=True)
    u, w = sol[..., :dv], sol[..., dv:]
    attn = jnp.where(causal, jnp.einsum('bhncd,bhnmd->bhncm', q, k) * decay, 0.0)
    q_dec = q * jnp.exp(gc)[..., None]
    k_dec = k * jnp.exp(gc[..., -1:] - gc)[..., None]
    last = jnp.exp(gc[..., -1])
    xs = tuple(jnp.moveaxis(t, 2, 0) for t in (u, w, attn, q_dec, k_dec, last))

    def step(state, inp):
        u_n, w_n, attn_n, qd_n, kd_n, last_n = inp
        v_new = u_n - w_n @ state
        o = qd_n @ state + attn_n @ v_new
        state = state * last_n[..., None, None] + jnp.einsum('bhcd,bhce->bhde', kd_n, v_new)
        return state, o

    _, o = lax.scan(step, jnp.zeros((B, H, dk, dv), F32), xs)
    return o.transpose(1, 0, 3, 2, 4).reshape(B, S, H, dv)


def _pad_cols(w, mult):
    n = w.shape[1]
    return jnp.pad(w, ((0, 0), (0, _round_up(n, mult) - n)))


def hybrid_mixer(h, ln_gain, w_in, w_out, q_gain, k_gain, cmp_pos, cmp_w1, cmp_w2, gn_gain, rel_bias):
    B, S, D = h.shape
    n_in = w_in.shape[1]
    proj = norm_matmul(h.reshape(B * S, D), ln_gain, _pad_cols(w_in, 512).astype(BF16))
    proj = proj[:, :n_in].reshape(B, S, n_in)
    widths = ([NSA_WIDTH] + [NSA_KV_WIDTH] * 6 + [3 * NSA_HEADS]
              + [RET_HEADS * RET_DK] * 2 + [RET_HEADS * RET_DV] * 2)
    cuts = [int(c) for c in np.cumsum(widths)[:-1]]
    (nq, kc, vc, ks, vs, kw, vw, gl, rq, rk, rv, rg) = jnp.split(proj, cuts, axis=-1)
    kcc, vcc, ksn, vsb, kwn, vwb = nsa_mixer(nq, kc, vc, ks, vs, kw, vw, gl, q_gain, k_gain,
                                             cmp_pos, cmp_w1, cmp_w2)
    a_out = nsa_attention_pallas(nq, kcc, vcc, ksn, vsb, kwn, vwb, gl, q_gain, rel_bias)
    b_out = retention(rq.reshape(B, S, RET_HEADS, RET_DK), rk.reshape(B, S, RET_HEADS, RET_DK),
                      rv.reshape(B, S, RET_HEADS, RET_DV), rg, gn_gain)
    mix = jnp.concatenate([a_out, b_out.astype(BF16)], axis=-1).reshape(B * S, -1)
    return matmul_residual(mix, w_out.astype(BF16), h.reshape(B * S, D)).reshape(B, S, D)


def gdn_mixer(h, ln_gain, w_in, conv_w, a_log, dt_bias, norm_gain, w_out):
    B, S, D = h.shape
    n_in = w_in.shape[1]
    proj = norm_matmul(h.reshape(B * S, D), ln_gain, _pad_cols(w_in, 512).astype(BF16))
    proj = proj[:, :n_in].reshape(B, S, n_in)
    cuts = [GDN_CONV_CH, GDN_CONV_CH + GDN_V_WIDTH, GDN_CONV_CH + GDN_V_WIDTH + GDN_V_HEADS]
    qkv, z, b, a = jnp.split(proj, cuts, axis=-1)
    qkv = jax.nn.silu(causal_depthwise_conv(qkv, conv_w))
    q, k, v = jnp.split(qkv, [GDN_QK_WIDTH, 2 * GDN_QK_WIDTH], axis=-1)
    rep = GDN_V_HEADS // GDN_QK_HEADS
    q = jnp.repeat(l2_norm(q.reshape(B, S, GDN_QK_HEADS, GDN_HEAD_DIM)), rep, axis=2)
    k = jnp.repeat(l2_norm(k.reshape(B, S, GDN_QK_HEADS, GDN_HEAD_DIM)), rep, axis=2)
    v = v.reshape(B, S, GDN_V_HEADS, GDN_HEAD_DIM)
    beta = jax.nn.sigmoid(b.astype(F32))
    g = -jnp.exp(a_log.astype(F32)) * jax.nn.softplus(a.astype(F32) + dt_bias.astype(F32))
    o = chunk_gated_delta_rule(q, k, v, beta, g)
    o = rms_norm(o, norm_gain) * jax.nn.silu(z.astype(F32).reshape(B, S, GDN_V_HEADS, GDN_HEAD_DIM))
    o = o.reshape(B * S, GDN_V_WIDTH).astype(BF16)
    return matmul_residual(o, w_out.astype(BF16), h.reshape(B * S, D)).reshape(B, S, D)


def memory_kv(mem, mem_gain, wkv, k_gain):
    B, M, D = mem.shape
    kv = norm_matmul(mem.reshape(B * M, D), mem_gain, wkv.astype(BF16))
    k, v = jnp.split(kv.reshape(B, M, 2 * XA_WIDTH), 2, axis=-1)
    k = rms_norm(k.reshape(B, M, XA_HEADS, XA_HEAD_DIM), k_gain).reshape(B, M, XA_WIDTH)
    return k.astype(BF16), v.astype(BF16)


def kernel(x, mem, rel_bias, ln_mix, ln_mem, ln_ffn, hyb_w_in, hyb_w_out, nsa_q_gain, nsa_k_gain,
           nsa_cmp_pos, nsa_cmp_w1, nsa_cmp_w2, ret_gn_gain, gdn_w_in, gdn_conv_w, gdn_a_log,
           gdn_dt_bias, gdn_norm_gain, gdn_w_out, xa_wq, xa_wkv, xa_q_gain, xa_k_gain, xa_mem_gain,
           xa_wo, ffn_w_in, ffn_w_out):
    B, S, D = x.shape
    h = x
    for layer in range(DEPTH):
        if layer % 2 == 0:
            e = layer // 2
            h = hybrid_mixer(h, ln_mix[layer], hyb_w_in[e], hyb_w_out[e], nsa_q_gain[e], nsa_k_gain[e],
                             nsa_cmp_pos[e], nsa_cmp_w1[e], nsa_cmp_w2[e], ret_gn_gain[e], rel_bias)
        else:
            o = layer // 2
            h = gdn_mixer(h, ln_mix[layer], gdn_w_in[o], gdn_conv_w[o], gdn_a_log[o], gdn_dt_bias[o],
                          gdn_norm_gain[o], gdn_w_out[o])
        k_mem, v_mem = memory_kv(mem, xa_mem_gain[layer], xa_wkv[layer], xa_k_gain[layer])
        h = xattn_residual(h, ln_mem[layer], xa_wq[layer].astype(BF16), xa_q_gain[layer], k_mem, v_mem,
                           xa_wo[layer].astype(BF16))
        h = ffn_residual(h.reshape(B * S, D), ln_ffn[layer], ffn_w_in[layer].astype(BF16),
                         ffn_w_out[layer].astype(BF16)).reshape(B, S, D)
    return h
```

```python
import functools
import math

import jax
import jax.numpy as jnp
import numpy as np
from jax import lax
from jax.experimental import pallas as pl
from jax.experimental.pallas import tpu as pltpu

F32 = jnp.float32
BF16 = jnp.bfloat16

D_MODEL = 2048
DEPTH = 4
NORM_EPS = 1e-6
NEG_INF = -1e30

NSA_HEADS = 8
NSA_KV_GROUPS = 2
NSA_GROUP_HEADS = NSA_HEADS // NSA_KV_GROUPS
NSA_HEAD_DIM = 128
NSA_CMP_LEN = 32
NSA_CMP_STRIDE = 16
NSA_SEL_BLOCK = 64
NSA_N_SELECT = 16
NSA_WINDOW = 512
NSA_Q_BLOCK = 64
NSA_FORCE_SCORE = 1e6
NSA_WIDTH = NSA_HEADS * NSA_HEAD_DIM
NSA_KV_WIDTH = NSA_KV_GROUPS * NSA_HEAD_DIM

RET_HEADS = 4
RET_DK = 256
RET_DV = 256
RET_CHUNK = 128
RET_ROPE_BASE = 10000.0
RET_GN_EPS = 1e-5

GDN_QK_HEADS = 16
GDN_V_HEADS = 32
GDN_HEAD_DIM = 128
GDN_CONV = 4
GDN_CHUNK = 64
GDN_QK_WIDTH = GDN_QK_HEADS * GDN_HEAD_DIM
GDN_V_WIDTH = GDN_V_HEADS * GDN_HEAD_DIM
GDN_CONV_CH = 2 * GDN_QK_WIDTH + GDN_V_WIDTH

REL_BUCKETS = 32
REL_MAX_DIST = 1024

XA_HEADS = 4
XA_HEAD_DIM = 128
XA_WIDTH = XA_HEADS * XA_HEAD_DIM

V7X_VMEM_LIMIT_BYTES = 56 * 1024 * 1024


def _cparams(*sem):
    return pltpu.CompilerParams(dimension_semantics=sem, vmem_limit_bytes=V7X_VMEM_LIMIT_BYTES)


def _round_up(n, m):
    return -(-n // m) * m


def _rms_rows(x, gain):
    return x * lax.rsqrt(jnp.mean(x * x, axis=-1, keepdims=True) + NORM_EPS) * gain


def _norm_matmul_kernel(x_ref, g_ref, w_ref, o_ref, xn_ref):
    @pl.when(pl.program_id(1) == 0)
    def _():
        xn_ref[...] = _rms_rows(x_ref[...], g_ref[...]).astype(BF16)

    o_ref[...] = jnp.dot(xn_ref[...], w_ref[...], preferred_element_type=F32).astype(o_ref.dtype)


def norm_matmul(x, gain, w, *, tm=1024, tn=512, out_dtype=F32):
    m, k = x.shape
    n = w.shape[1]
    tm = min(tm, m)
    return pl.pallas_call(
        _norm_matmul_kernel,
        grid=(m // tm, n // tn),
        in_specs=[pl.BlockSpec((tm, k), lambda i, j: (i, 0)),
                  pl.BlockSpec((1, k), lambda i, j: (0, 0)),
                  pl.BlockSpec((k, tn), lambda i, j: (0, j))],
        out_specs=pl.BlockSpec((tm, tn), lambda i, j: (i, j)),
        out_shape=jax.ShapeDtypeStruct((m, n), out_dtype),
        scratch_shapes=[pltpu.VMEM((tm, k), BF16)],
        compiler_params=_cparams("parallel", "arbitrary"),
        name="norm_matmul",
    )(x, gain.reshape(1, k), w)


def _matmul_res_kernel(x_ref, w_ref, r_ref, o_ref):
    o_ref[...] = r_ref[...] + jnp.dot(x_ref[...], w_ref[...], preferred_element_type=F32)


def matmul_residual(x, w, res, *, tm=1024, tn=512):
    m, k = x.shape
    n = w.shape[1]
    return pl.pallas_call(
        _matmul_res_kernel,
        grid=(m // tm, n // tn),
        in_specs=[pl.BlockSpec((tm, k), lambda i, j: (i, 0)),
                  pl.BlockSpec((k, tn), lambda i, j: (0, j)),
                  pl.BlockSpec((tm, tn), lambda i, j: (i, j))],
        out_specs=pl.BlockSpec((tm, tn), lambda i, j: (i, j)),
        out_shape=jax.ShapeDtypeStruct((m, n), F32),
        compiler_params=_cparams("parallel", "arbitrary"),
        name="matmul_residual",
    )(x, w, res)


def _matmul2_res_kernel(x1_ref, x2_ref, w1_ref, w2_ref, r_ref, o_ref):
    o_ref[...] = (r_ref[...] + jnp.dot(x1_ref[...], w1_ref[...], preferred_element_type=F32)
                  + jnp.dot(x2_ref[...], w2_ref[...], preferred_element_type=F32))


def matmul2_residual(x1, x2, w, res, *, tm=1024, tn=512):
    m, k1 = x1.shape
    n = w.shape[1]
    return pl.pallas_call(
        _matmul2_res_kernel,
        grid=(m // tm, n // tn),
        in_specs=[pl.BlockSpec((tm, k1), lambda i, j: (i, 0)),
                  pl.BlockSpec((tm, k1), lambda i, j: (i, 0)),
                  pl.BlockSpec((k1, tn), lambda i, j: (0, j)),
                  pl.BlockSpec((k1, tn), lambda i, j: (1, j)),
                  pl.BlockSpec((tm, tn), lambda i, j: (i, j))],
        out_specs=pl.BlockSpec((tm, tn), lambda i, j: (i, j)),
        out_shape=jax.ShapeDtypeStruct((m, n), F32),
        compiler_params=_cparams("parallel", "arbitrary"),
        name="matmul2_residual",
    )(x1, x2, w, w, res)


def _ffn_kernel(x_ref, g_ref, wg_ref, wu_ref, wo_ref, o_ref, xn_ref, acc_ref):
    j = pl.program_id(1)

    @pl.when(j == 0)
    def _():
        x = x_ref[...]
        xn_ref[...] = _rms_rows(x, g_ref[...]).astype(BF16)
        acc_ref[...] = x

    xn = xn_ref[...]
    gate = jnp.dot(xn, wg_ref[...], preferred_element_type=F32)
    up = jnp.dot(xn, wu_ref[...], preferred_element_type=F32)
    act = (gate * jax.nn.sigmoid(gate) * up).astype(BF16)
    acc_ref[...] += jnp.dot(act, wo_ref[...], preferred_element_type=F32)

    @pl.when(j == pl.num_programs(1) - 1)
    def _():
        o_ref[...] = acc_ref[...]


def ffn_residual(h, gain, w_in, w_out, *, tm=512, th=512):
    m, d = h.shape
    hidden = w_out.shape[0]
    nh = hidden // th
    return pl.pallas_call(
        _ffn_kernel,
        grid=(m // tm, nh),
        in_specs=[pl.BlockSpec((tm, d), lambda i, j: (i, 0)),
                  pl.BlockSpec((1, d), lambda i, j: (0, 0)),
                  pl.BlockSpec((d, th), lambda i, j: (0, j)),
                  pl.BlockSpec((d, th), lambda i, j: (0, j + nh)),
                  pl.BlockSpec((th, d), lambda i, j: (j, 0))],
        out_specs=pl.BlockSpec((tm, d), lambda i, j: (i, 0)),
        out_shape=jax.ShapeDtypeStruct((m, d), F32),
        scratch_shapes=[pltpu.VMEM((tm, d), BF16), pltpu.VMEM((tm, d), F32)],
        compiler_params=_cparams("parallel", "arbitrary"),
        name="ffn_residual",
    )(h, gain.reshape(1, d), w_in, w_in, w_out)


def _xattn_kernel(h_ref, g_ref, wq_ref, qg_ref, k_ref, v_ref, wo_ref, o_ref):
    x = h_ref[0]
    xn = _rms_rows(x, g_ref[...]).astype(BF16)
    q = jnp.dot(xn, wq_ref[...], preferred_element_type=F32)
    k = k_ref[0]
    v = v_ref[0]
    outs = []
    for hh in range(XA_HEADS):
        sl = slice(hh * XA_HEAD_DIM, (hh + 1) * XA_HEAD_DIM)
        qh = _rms_rows(q[:, sl], qg_ref[...]).astype(BF16)
        s = lax.dot_general(qh, k[:, sl], (((1,), (1,)), ((), ())), preferred_element_type=F32)
        s = s * (XA_HEAD_DIM ** -0.5)
        s = s - jnp.max(s, axis=-1, keepdims=True)
        p = jnp.exp(s)
        p = p / jnp.sum(p, axis=-1, keepdims=True)
        outs.append(jnp.dot(p.astype(BF16), v[:, sl], preferred_element_type=F32))
    o = jnp.concatenate(outs, axis=-1).astype(BF16)
    o_ref[0] = x + jnp.dot(o, wo_ref[...], preferred_element_type=F32)


def xattn_residual(h, gain, wq, q_gain, k, v, wo, *, tm=512):
    b, s, d = h.shape
    mt = k.shape[1]
    return pl.pallas_call(
        _xattn_kernel,
        grid=(b, s // tm),
        in_specs=[pl.BlockSpec((1, tm, d), lambda bi, i: (bi, i, 0)),
                  pl.BlockSpec((1, d), lambda bi, i: (0, 0)),
                  pl.BlockSpec((d, XA_WIDTH), lambda bi, i: (0, 0)),
                  pl.BlockSpec((1, XA_HEAD_DIM), lambda bi, i: (0, 0)),
                  pl.BlockSpec((1, mt, XA_WIDTH), lambda bi, i: (bi, 0, 0)),
                  pl.BlockSpec((1, mt, XA_WIDTH), lambda bi, i: (bi, 0, 0)),
                  pl.BlockSpec((XA_WIDTH, d), lambda bi, i: (0, 0))],
        out_specs=pl.BlockSpec((1, tm, d), lambda bi, i: (bi, i, 0)),
        out_shape=jax.ShapeDtypeStruct((b, s, d), F32),
        compiler_params=_cparams("parallel", "parallel"),
        name="xattn_residual",
    )(h, gain.reshape(1, d), wq, q_gain.reshape(1, XA_HEAD_DIM), k, v, wo)


def rms_norm(x, gain, eps=NORM_EPS):
    x32 = x.astype(F32)
    y = x32 * lax.rsqrt(jnp.mean(x32 * x32, axis=-1, keepdims=True) + eps)
    return (y * gain.astype(F32)).astype(x.dtype)


def l2_norm(x, eps=NORM_EPS):
    x32 = x.astype(F32)
    return x32 * lax.rsqrt(jnp.sum(x32 * x32, axis=-1, keepdims=True) + eps)


def masked_softmax(logits, mask):
    p = jax.nn.softmax(jnp.where(mask, logits.astype(F32), NEG_INF), axis=-1)
    return jnp.where(mask, p, 0.0)


def rel_bucket(dist):
    dist = jnp.maximum(dist, 0)
    max_exact = REL_BUCKETS // 2
    scaled = (jnp.log(jnp.maximum(dist, max_exact).astype(F32) / max_exact)
              / math.log(REL_MAX_DIST / max_exact) * (REL_BUCKETS - max_exact))
    large = jnp.minimum(max_exact + scaled.astype(jnp.int32), REL_BUCKETS - 1)
    return jnp.where(dist < max_exact, dist, large)


def rotary(x, pos):
    half = x.shape[-1] // 2
    inv = RET_ROPE_BASE ** (-jnp.arange(half, dtype=F32) / half)
    ang = pos.astype(F32)[:, None] * inv[None, :]
    cos, sin = jnp.cos(ang)[None, :, None, :], jnp.sin(ang)[None, :, None, :]
    x1, x2 = x[..., :half], x[..., half:]
    return jnp.concatenate([x1 * cos - x2 * sin, x1 * sin + x2 * cos], axis=-1)


def causal_depthwise_conv(x, w):
    k, ch = w.shape
    return lax.conv_general_dilated(x, w[:, None, :].astype(x.dtype), window_strides=(1,),
                                    padding=[(k - 1, 0)], dimension_numbers=('NWC', 'WIO', 'NWC'),
                                    feature_group_count=ch)


def compress_blocks(t, pos, w1, w2):
    b, s, g, dh = t.shape
    r = NSA_CMP_LEN // NSA_CMP_STRIDE
    n_chunks = s // NSA_CMP_STRIDE
    n_cmp = n_chunks - r + 1
    c = t.reshape(b, n_chunks, NSA_CMP_STRIDE, g, dh)
    blocks = jnp.concatenate([c[:, j:j + n_cmp] for j in range(r)], axis=2)
    blocks = blocks + pos[None, None, :, None, :].astype(t.dtype)
    flat = blocks.transpose(0, 3, 1, 2, 4).reshape(b, g, n_cmp, NSA_CMP_LEN * dh)
    return jax.nn.silu(flat @ w1) @ w2


NSA_QT = 128
NSA_KT = 512
NSA_WT = NSA_WINDOW + NSA_QT


def _dot_nt(a, b):
    return lax.dot_general(a, b, (((1,), (1,)), ((), ())), preferred_element_type=F32)


def _dot_tn(a, b):
    return lax.dot_general(a, b, (((0,), (0,)), ((), ())), preferred_element_type=F32)


def _bias_tile(tb_ref, h, first_slab, n_slabs):
    return jnp.concatenate([tb_ref[h, jnp.maximum(first_slab - j, 0)] for j in range(n_slabs)], axis=1)


def _nsa_kernel(q_ref, gl_ref, kc_ref, vc_ref, bc_ref, covt_ref, ks_ref, vs_ref, kw_ref, vw_ref,
                tb_ref, e_ref, qg_ref, o_ref, a_scr, *, n_top):
    QT, KT, WT, HG, dh = NSA_QT, NSA_KT, NSA_WT, NSA_GROUP_HEADS, NSA_HEAD_DIM
    nsb, nc = covt_ref.shape
    q0 = pl.program_id(2) * QT

    x = q_ref[0]
    qs = []
    for h in range(HG):
        xh = _rms_rows(x[:, h * dh:(h + 1) * dh], qg_ref[...]) * (dh ** -0.5)
        qs.append(xh.astype(BF16))
    qn = jnp.concatenate(qs, axis=0)

    sc = _dot_nt(qn, kc_ref[0, 0]) + bc_ref[...].reshape(HG * QT, nc)
    row = lax.broadcasted_iota(jnp.int32, (HG * QT, nc), 0)
    col = lax.broadcasted_iota(jnp.int32, (HG * QT, nc), 1)
    tq = q0 + (row & (QT - 1))
    mask_c = tq >= col * NSA_CMP_STRIDE + (NSA_CMP_LEN - 1)
    sc = jnp.where(mask_c, sc, NEG_INF)
    pc = jnp.where(mask_c, jnp.exp(sc - jnp.max(sc, axis=-1, keepdims=True)), 0.0)
    den = jnp.sum(pc, axis=-1, keepdims=True)
    pcb = (pc / jnp.maximum(den, 1e-30)).astype(BF16)
    o_c = jnp.dot(pcb, vc_ref[0, 0], preferred_element_type=F32)

    imp_all = _dot_nt(covt_ref[...], pcb)
    imp = imp_all[:, 0:QT]
    for h in range(1, HG):
        imp = imp + imp_all[:, h * QT:(h + 1) * QT]
    jj = lax.broadcasted_iota(jnp.int32, (nsb, QT), 0)
    cur = (q0 + lax.broadcasted_iota(jnp.int32, (nsb, QT), 1)) // NSA_SEL_BLOCK
    forced = (jj == 0) | (jj == cur) | (jj == cur - 1)
    a = jnp.where(forced, NSA_FORCE_SCORE, imp)
    a = jnp.where(jj > cur, -NSA_FORCE_SCORE, a)
    a_scr[...] = a

    def rank_body(i, rank):
        r = a_scr[pl.ds(i, 1), :]
        return rank + jnp.where((r > a) | ((r == a) & (jj > i)), 1.0, 0.0)

    rank = lax.fori_loop(0, nsb, rank_body, jnp.zeros((nsb, QT), F32))
    sel_t = jnp.where(rank < n_top, 1.0, 0.0)
    if nsb < 128:
        sel_t = jnp.concatenate([sel_t, jnp.zeros((128 - nsb, QT), F32)], axis=0)
    sel = sel_t.T.astype(BF16)

    cmr = (lax.broadcasted_iota(jnp.int32, (QT, KT), 1) - lax.broadcasted_iota(jnp.int32, (QT, KT), 0))

    def sel_body(kt, carry):
        m_i, l_i, acc = carry
        k0 = pl.multiple_of(kt * KT, KT)
        k = ks_ref[0, pl.ds(k0, KT), :]
        v = vs_ref[0, pl.ds(k0, KT), :]
        s = _dot_nt(qn, k)
        delta = q0 - k0
        visible = (jnp.dot(sel, e_ref[kt], preferred_element_type=F32) > 0.5) & (cmr <= delta)
        seg = delta // QT
        parts = []
        for h in range(HG):
            bias = _bias_tile(tb_ref, h, seg, KT // QT)
            parts.append(jnp.where(visible, s[h * QT:(h + 1) * QT] + bias, NEG_INF))
        s = jnp.concatenate(parts, axis=0)
        m_new = jnp.maximum(m_i, jnp.max(s, axis=-1, keepdims=True))
        alpha = jnp.exp(m_i - m_new)
        p = jnp.exp(s - m_new)
        l_new = alpha * l_i + jnp.sum(p, axis=-1, keepdims=True)
        acc = alpha * acc + jnp.dot(p.astype(BF16), v, preferred_element_type=F32)
        return m_new, l_new, acc

    init = (jnp.full((HG * QT, 1), NEG_INF, F32), jnp.zeros((HG * QT, 1), F32), jnp.zeros((HG * QT, dh), F32))
    _, l_s, acc_s = lax.fori_loop(0, q0 // KT + 1, sel_body, init)
    o_s = acc_s / l_s

    w0 = pl.multiple_of(q0, QT)
    kwin = kw_ref[0, pl.ds(w0, WT), :]
    vwin = vw_ref[0, pl.ds(w0, WT), :]
    sw = _dot_nt(qn, kwin)
    cw = lax.broadcasted_iota(jnp.int32, (QT, WT), 1)
    dist = NSA_WINDOW + lax.broadcasted_iota(jnp.int32, (QT, WT), 0) - cw
    vis_w = (dist >= 0) & (dist < NSA_WINDOW) & (cw >= NSA_WINDOW - q0)
    parts = []
    for h in range(HG):
        bias = _bias_tile(tb_ref, h, NSA_WINDOW // QT, WT // QT)
        parts.append(jnp.where(vis_w, sw[h * QT:(h + 1) * QT] + bias, NEG_INF))
    sw = jnp.concatenate(parts, axis=0)
    pw = jnp.exp(sw - jnp.max(sw, axis=-1, keepdims=True))
    pw = pw / jnp.sum(pw, axis=-1, keepdims=True)
    o_w = jnp.dot(pw.astype(BF16), vwin, preferred_element_type=F32)

    gates = jax.nn.sigmoid(gl_ref[0][:, :3 * HG])
    outs = []
    for h in range(HG):
        rs = slice(h * QT, (h + 1) * QT)
        outs.append(gates[:, 3 * h:3 * h + 1] * o_c[rs] + gates[:, 3 * h + 1:3 * h + 2] * o_s[rs]
                    + gates[:, 3 * h + 2:3 * h + 3] * o_w[rs])
    o_ref[0] = jnp.concatenate(outs, axis=-1).astype(o_ref.dtype)


def _rel_table(rel_bias, s):
    return rel_bias[rel_bucket(jnp.arange(s))].T.astype(F32)


def nsa_attention_pallas(proj, gl_blk, kc, vc, ks, vs, kw, vw, q_gain, rel_bias):
    B, S, _ = proj.shape
    G, HG, dh = NSA_KV_GROUPS, NSA_GROUP_HEADS, NSA_HEAD_DIM
    QT, KT, WT, W, SB = NSA_QT, NSA_KT, NSA_WT, NSA_WINDOW, NSA_SEL_BLOCK
    nsb, nc, nkt, nseg = S // SB, S // NSA_CMP_STRIDE, S // KT, S // QT
    n_top = min(NSA_N_SELECT, nsb)

    def bias_of(dist):
        onehot = (rel_bucket(dist)[..., None] == jnp.arange(REL_BUCKETS)) & (dist >= 0)[..., None]
        return jnp.einsum('...b,bh->h...', onehot.astype(F32), rel_bias.astype(F32),
                          precision=lax.Precision.HIGHEST)

    bc = bias_of(jnp.arange(S)[:, None] - (jnp.arange(nc) * NSA_CMP_STRIDE + NSA_CMP_LEN - 1)[None, :])
    tb = bias_of(QT * jnp.arange(nseg)[:, None, None] + jnp.arange(QT)[None, :, None] - jnp.arange(QT)[None, None, :])
    cmp_start = np.arange(nc) * NSA_CMP_STRIDE
    cmp_end = cmp_start + NSA_CMP_LEN - 1
    sel_start = np.arange(nsb) * SB
    cover_t = ((cmp_start[None, :] < sel_start[:, None] + SB) & (cmp_end[None, :] >= sel_start[:, None])
               & (np.arange(nc)[None, :] < nc - 1))
    cover_t = jnp.asarray(cover_t, BF16)
    e = (np.arange(128)[None, :, None]
         == (np.arange(nkt)[:, None, None] * (KT // SB) + np.arange(KT)[None, None, :] // SB))
    e = jnp.asarray(e, BF16)
    pad = ((0, 0), (W, 0), (0, 0))
    kw = jnp.pad(kw, pad)
    vw = jnp.pad(vw, pad)

    kern = functools.partial(_nsa_kernel, n_top=n_top)
    return pl.pallas_call(
        kern,
        grid=(B, G, S // QT),
        in_specs=[
            pl.BlockSpec((1, QT, HG * dh), lambda b, g, i: (b, i, g)),
            pl.BlockSpec((1, QT, 128), lambda b, g, i: (b, i, gl_blk + g)),
            pl.BlockSpec((1, 1, nc, dh), lambda b, g, i: (b, g, 0, 0)),
            pl.BlockSpec((1, 1, nc, dh), lambda b, g, i: (b, g, 0, 0)),
            pl.BlockSpec((HG, QT, nc), lambda b, g, i: (g, i, 0)),
            pl.BlockSpec((nsb, nc), lambda b, g, i: (0, 0)),
            pl.BlockSpec((1, S, dh), lambda b, g, i: (b, 0, g)),
            pl.BlockSpec((1, S, dh), lambda b, g, i: (b, 0, g)),
            pl.BlockSpec((1, S + W, dh), lambda b, g, i: (b, 0, g)),
            pl.BlockSpec((1, S + W, dh), lambda b, g, i: (b, 0, g)),
            pl.BlockSpec((HG, nseg, QT, QT), lambda b, g, i: (g, 0, 0, 0)),
            pl.BlockSpec((nkt, 128, KT), lambda b, g, i: (0, 0, 0)),
            pl.BlockSpec((1, dh), lambda b, g, i: (0, 0)),
        ],
        out_specs=pl.BlockSpec((1, QT, HG * dh), lambda b, g, i: (b, i, g)),
        out_shape=jax.ShapeDtypeStruct((B, S, NSA_WIDTH), BF16),
        scratch_shapes=[pltpu.VMEM((nsb, QT), F32)],
        compiler_params=_cparams("parallel", "parallel", "arbitrary"),
        name="nsa_attention",
    )(proj, proj, kc, vc, bc, cover_t, ks, vs, kw, vw, tb, e, q_gain.reshape(1, dh))


def nsa_kv_prep(kc, vc, ks, vs, kw, vw, k_gain, cmp_pos, cmp_w1, cmp_w2):
    B, S, _ = kc.shape
    kv = lambda t: t.reshape(B, S, NSA_KV_GROUPS, NSA_HEAD_DIM)
    kcc = rms_norm(compress_blocks(kv(kc), cmp_pos[0], cmp_w1[0], cmp_w2[0]), k_gain[0])
    vcc = compress_blocks(kv(vc), cmp_pos[1], cmp_w1[1], cmp_w2[1])
    padc = ((0, 0), (0, 0), (0, 1), (0, 0))
    kcc = jnp.pad(kcc, padc).astype(BF16)
    vcc = jnp.pad(vcc, padc).astype(BF16)
    ksn = rms_norm(kv(ks), k_gain[1]).reshape(B, S, -1).astype(BF16)
    kwn = rms_norm(kv(kw), k_gain[2]).reshape(B, S, -1).astype(BF16)
    return kcc, vcc, ksn, vs.astype(BF16), kwn, vw.astype(BF16)


def _rotate_half(x, cos, sin):
    half = x.shape[-1] // 2
    x1, x2 = x[:, :half], x[:, half:]
    return jnp.concatenate([x1 * cos - x2 * sin, x1 * sin + x2 * cos], axis=-1)


def _retention_kernel(q_ref, k_ref, v_ref, g_ref, cos_ref, sin_ref, din_ref, qd_ref, kd_ref, cd_ref, gn_ref,
                      o_ref, s_scr):
    @pl.when(pl.program_id(2) == 0)
    def _():
        s_scr[...] = jnp.zeros_like(s_scr)

    cos, sin = cos_ref[...], sin_ref[...]
    q = _rotate_half(q_ref[0], cos, sin) * (RET_DK ** -0.5)
    k = _rotate_half(k_ref[0], cos, sin)
    v16 = v_ref[0].astype(BF16)
    inner = _dot_nt(q.astype(BF16), k.astype(BF16)) * din_ref[0]
    s = s_scr[...]
    o = (jnp.dot(inner.astype(BF16), v16, preferred_element_type=F32)
         + jnp.dot((q * qd_ref[0]).astype(BF16), s.astype(BF16), preferred_element_type=F32))
    s_scr[...] = s * cd_ref[0] + _dot_tn((k * kd_ref[0]).astype(BF16), v16)
    mu = jnp.mean(o, axis=-1, keepdims=True)
    d = o - mu
    var = jnp.mean(d * d, axis=-1, keepdims=True)
    gate = g_ref[0]
    o_ref[0] = (gate * jax.nn.sigmoid(gate) * (d * lax.rsqrt(var + RET_GN_EPS) * gn_ref[...])).astype(o_ref.dtype)


def retention_pallas(proj, q_blk, gn_gain):
    B, S, _ = proj.shape
    H, dk, dv, C = RET_HEADS, RET_DK, RET_DV, RET_CHUNK
    N = S // C
    half = dk // 2
    inv = RET_ROPE_BASE ** (-jnp.arange(half, dtype=F32) / half)
    ang = jnp.arange(S).astype(F32)[:, None] * inv[None, :]
    cos, sin = jnp.cos(ang), jnp.sin(ang)
    log_gamma = jnp.log(1.0 - 2.0 ** (-5.0 - jnp.arange(H, dtype=F32)))
    idx = jnp.arange(C, dtype=F32)
    rel = idx[:, None] - idx[None, :]
    decay_in = jnp.where(rel >= 0, jnp.exp(log_gamma[:, None, None] * jnp.maximum(rel, 0.0)), 0.0)
    q_decay = jnp.exp(log_gamma[:, None] * (idx + 1.0))[..., None]
    k_decay = jnp.exp(log_gamma[:, None] * (C - 1.0 - idx))[..., None]
    chunk_decay = jnp.exp(log_gamma * C)[:, None, None]
    return pl.pallas_call(
        _retention_kernel,
        grid=(B, H, N),
        in_specs=[
            pl.BlockSpec((1, C, dk), lambda b, h, n: (b, n, q_blk + h)),
            pl.BlockSpec((1, C, dk), lambda b, h, n: (b, n, q_blk + H + h)),
            pl.BlockSpec((1, C, dv), lambda b, h, n: (b, n, q_blk + 2 * H + h)),
            pl.BlockSpec((1, C, dv), lambda b, h, n: (b, n, q_blk + 3 * H + h)),
            pl.BlockSpec((C, half), lambda b, h, n: (n, 0)),
            pl.BlockSpec((C, half), lambda b, h, n: (n, 0)),
            pl.BlockSpec((1, C, C), lambda b, h, n: (h, 0, 0)),
            pl.BlockSpec((1, C, 1), lambda b, h, n: (h, 0, 0)),
            pl.BlockSpec((1, C, 1), lambda b, h, n: (h, 0, 0)),
            pl.BlockSpec((1, 1, 1), lambda b, h, n: (h, 0, 0)),
            pl.BlockSpec((1, dv), lambda b, h, n: (0, h)),
        ],
        out_specs=pl.BlockSpec((1, C, dv), lambda b, h, n: (b, n, h)),
        out_shape=jax.ShapeDtypeStruct((B, S, H * dv), BF16),
        scratch_shapes=[pltpu.VMEM((dk, dv), F32)],
        compiler_params=_cparams("parallel", "parallel", "arbitrary"),
        name="retention",
    )(proj, proj, proj, proj, cos, sin, decay_in, q_decay, k_decay, chunk_decay, gn_gain.reshape(1, H * dv))


GDN_HB = 8
GDN_HALO = 8


def _conv_silu(x, halo, w):
    c = x.shape[0]
    xx = jnp.concatenate([halo, x], axis=0)
    y = w[GDN_CONV - 1:GDN_CONV] * x
    for j in range(GDN_CONV - 1):
        off = GDN_HALO - (GDN_CONV - 1) + j
        y = y + w[j:j + 1] * xx[off:off + c]
    return y * jax.nn.sigmoid(y)


def _l2_rows(x):
    return x * lax.rsqrt(jnp.sum(x * x, axis=-1, keepdims=True) + NORM_EPS)


def _gdn_kernel(q_ref, k_ref, v_ref, qh_ref, kh_ref, vh_ref, wq_ref, wk_ref, wv_ref, z_ref,
                bcol_ref, gcol_ref, grow_ref, ng_ref, o_ref, s_scr):
    C, dh, hb = GDN_CHUNK, GDN_HEAD_DIM, GDN_HB
    rep = GDN_V_HEADS // GDN_QK_HEADS
    first = pl.program_id(2) == 0

    @pl.when(first)
    def _():
        s_scr[...] = jnp.zeros_like(s_scr)

    keep = jnp.where(first, 0.0, 1.0)
    qc = _conv_silu(q_ref[0], qh_ref[0] * keep, wq_ref[...])
    kc = _conv_silu(k_ref[0], kh_ref[0] * keep, wk_ref[...])
    vc = _conv_silu(v_ref[0], vh_ref[0] * keep, wv_ref[...])

    ri = lax.broadcasted_iota(jnp.int32, (C, C), 0)
    ci = lax.broadcasted_iota(jnp.int32, (C, C), 1)
    causal = ri >= ci
    strict = ri > ci
    bcol = bcol_ref[0, 0]
    gcol = gcol_ref[0, 0]
    grow = grow_ref[0, 0]
    heads = range(hb)
    qs, ks, grams = [], [], []
    for hq in range(hb // rep):
        qh = _l2_rows(qc[:, hq * dh:(hq + 1) * dh]) * (dh ** -0.5)
        kh = _l2_rows(kc[:, hq * dh:(hq + 1) * dh])
        k16 = kh.astype(BF16)
        qs.append(qh)
        ks.append(kh)
        grams.append(_dot_nt(jnp.concatenate([qh.astype(BF16), k16], axis=0), k16))
    beta = [bcol[:, h:h + 1] for h in heads]
    gc = [gcol[:, h:h + 1] for h in heads]
    gr = [grow[h:h + 1, :] for h in heads]
    g_last = [gr[h][:, C - 1:C] for h in heads]
    eg = [jnp.exp(gc[h]) for h in heads]
    decay = [jnp.where(causal, jnp.exp(jnp.minimum(gc[h] - gr[h], 0.0)), 0.0) for h in heads]
    attn = [(grams[h // rep][:C] * decay[h]).astype(BF16) for h in heads]
    m = [jnp.where(strict, grams[h // rep][C:] * decay[h] * (-beta[h]), 0.0).astype(BF16) for h in heads]
    x = [jnp.concatenate([vc[:, h * dh:(h + 1) * dh] * beta[h], ks[h // rep] * (beta[h] * eg[h])], axis=1)
         for h in heads]
    x = [x[h] + jnp.dot(m[h], x[h].astype(BF16), preferred_element_type=F32) for h in heads]
    for _ in range(int(math.log2(C)) - 1):
        m = [jnp.dot(m[h], m[h], preferred_element_type=F32).astype(BF16) for h in heads]
        x = [x[h] + jnp.dot(m[h], x[h].astype(BF16), preferred_element_type=F32) for h in heads]
    s = [s_scr[h] for h in heads]
    ws = [jnp.dot(jnp.concatenate([x[h][:, dh:].astype(BF16), (qs[h // rep] * eg[h]).astype(BF16)], axis=0),
                  s[h].astype(BF16), preferred_element_type=F32) for h in heads]
    vn = [(x[h][:, :dh] - ws[h][:C]).astype(BF16) for h in heads]
    o = [ws[h][C:] + jnp.dot(attn[h], vn[h], preferred_element_type=F32) for h in heads]
    for h in heads:
        k_dec = (ks[h // rep] * jnp.exp(g_last[h] - gc[h])).astype(BF16)
        s_scr[h] = s[h] * jnp.exp(g_last[h]) + _dot_tn(k_dec, vn[h])
    outs = []
    for h in heads:
        zh = z_ref[0, :, h * dh:(h + 1) * dh]
        outs.append((_rms_rows(o[h], ng_ref[...]) * (zh * jax.nn.sigmoid(zh))).astype(o_ref.dtype))
    o_ref[0] = jnp.concatenate(outs, axis=1)


def gdn_delta_rule(proj, conv_w, beta, g, norm_gain):
    B, S, _ = proj.shape
    C, dh, hb, H = GDN_CHUNK, GDN_HEAD_DIM, GDN_HB, GDN_V_HEADS
    rep = GDN_V_HEADS // GDN_QK_HEADS
    N, HP = S // C, H // hb
    wqk, wv = hb // rep * dh, hb * dh
    kb0, vb0, zb0 = GDN_QK_WIDTH // wqk, 2 * GDN_QK_WIDTH // wv, GDN_CONV_CH // wv
    hr = C // GDN_HALO
    gc = jnp.cumsum(g.reshape(B, N, C, H), axis=2)
    gcol = gc.reshape(B, S, HP, hb).transpose(0, 2, 1, 3)
    bcol = beta.reshape(B, S, HP, hb).transpose(0, 2, 1, 3)
    grow = gc.reshape(B, N, C, HP, hb).transpose(0, 3, 1, 4, 2).reshape(B, HP, N * hb, C)
    halo = lambda n: jnp.maximum(n * hr - 1, 0)
    return pl.pallas_call(
        _gdn_kernel,
        grid=(B, HP, N),
        in_specs=[
            pl.BlockSpec((1, C, wqk), lambda b, p, n: (b, n, p)),
            pl.BlockSpec((1, C, wqk), lambda b, p, n: (b, n, kb0 + p)),
            pl.BlockSpec((1, C, wv), lambda b, p, n: (b, n, vb0 + p)),
            pl.BlockSpec((1, GDN_HALO, wqk), lambda b, p, n: (b, halo(n), p)),
            pl.BlockSpec((1, GDN_HALO, wqk), lambda b, p, n: (b, halo(n), kb0 + p)),
            pl.BlockSpec((1, GDN_HALO, wv), lambda b, p, n: (b, halo(n), vb0 + p)),
            pl.BlockSpec((GDN_CONV, wqk), lambda b, p, n: (0, p)),
            pl.BlockSpec((GDN_CONV, wqk), lambda b, p, n: (0, kb0 + p)),
            pl.BlockSpec((GDN_CONV, wv), lambda b, p, n: (0, vb0 + p)),
            pl.BlockSpec((1, C, wv), lambda b, p, n: (b, n, zb0 + p)),
            pl.BlockSpec((1, 1, C, hb), lambda b, p, n: (b, p, n, 0)),
            pl.BlockSpec((1, 1, C, hb), lambda b, p, n: (b, p, n, 0)),
            pl.BlockSpec((1, 1, hb, C), lambda b, p, n: (b, p, n, 0)),
            pl.BlockSpec((1, dh), lambda b, p, n: (0, 0)),
        ],
        out_specs=pl.BlockSpec((1, C, wv), lambda b, p, n: (b, n, p)),
        out_shape=jax.ShapeDtypeStruct((B, S, H * dh), BF16),
        scratch_shapes=[pltpu.VMEM((hb, dh, dh), F32)],
        compiler_params=_cparams("parallel", "parallel", "arbitrary"),
        name="gdn_delta_rule",
    )(proj, proj, proj, proj, proj, proj, conv_w, conv_w, conv_w, proj, bcol, gcol, grow, norm_gain.reshape(1, dh))


def _pad_cols(w, mult):
    n = w.shape[1]
    return jnp.pad(w, ((0, 0), (0, _round_up(n, mult) - n)))


_HYB_NSA_COLS = NSA_WIDTH + 6 * NSA_KV_WIDTH
_HYB_RET_COLS = 2 * RET_HEADS * RET_DK + 2 * RET_HEADS * RET_DV
_HYB_GATE_COL0 = _HYB_NSA_COLS + _HYB_RET_COLS
_HYB_COLS = _round_up(_HYB_GATE_COL0 + NSA_KV_GROUPS * 128, 512)


def _hybrid_w_in_layout(w_in):
    gate0 = _HYB_NSA_COLS
    ret0 = gate0 + 3 * NSA_HEADS
    per_group = 3 * NSA_GROUP_HEADS
    parts = [w_in[:, :gate0], w_in[:, ret0:ret0 + _HYB_RET_COLS]]
    for g in range(NSA_KV_GROUPS):
        parts.append(jnp.pad(w_in[:, gate0 + g * per_group:gate0 + (g + 1) * per_group],
                             ((0, 0), (0, 128 - per_group))))
    w = jnp.concatenate(parts, axis=1)
    return jnp.pad(w, ((0, 0), (0, _HYB_COLS - w.shape[1]))).astype(BF16)


def hybrid_mixer(h, ln_gain, w_in, w_out, q_gain, k_gain, cmp_pos, cmp_w1, cmp_w2, gn_gain, rel_bias):
    B, S, D = h.shape
    proj = norm_matmul(h.reshape(B * S, D), ln_gain, _hybrid_w_in_layout(w_in)).reshape(B, S, _HYB_COLS)
    kvw = NSA_KV_WIDTH
    kc, vc, ks, vs, kw, vw = (proj[..., NSA_WIDTH + i * kvw:NSA_WIDTH + (i + 1) * kvw] for i in range(6))
    kcc, vcc, ksn, vsb, kwn, vwb = nsa_kv_prep(kc, vc, ks, vs, kw, vw, k_gain, cmp_pos, cmp_w1, cmp_w2)
    a_out = nsa_attention_pallas(proj, _HYB_GATE_COL0 // 128, kcc, vcc, ksn, vsb, kwn, vwb, q_gain, rel_bias)
    b_out = retention_pallas(proj, _HYB_NSA_COLS // RET_DK, gn_gain)
    w_out = w_out.astype(BF16)
    return matmul2_residual(a_out.reshape(B * S, -1), b_out.reshape(B * S, -1), w_out,
                            h.reshape(B * S, D)).reshape(B, S, D)


def gdn_mixer(h, ln_gain, w_in, conv_w, a_log, dt_bias, norm_gain, w_out):
    B, S, D = h.shape
    n_cols = _round_up(w_in.shape[1], 512)
    proj = norm_matmul(h.reshape(B * S, D), ln_gain, _pad_cols(w_in, 512).astype(BF16)).reshape(B, S, n_cols)
    b0 = GDN_CONV_CH + GDN_V_WIDTH
    b = proj[..., b0:b0 + GDN_V_HEADS]
    a = proj[..., b0 + GDN_V_HEADS:b0 + 2 * GDN_V_HEADS]
    beta = jax.nn.sigmoid(b)
    g = -jnp.exp(a_log.astype(F32)) * jax.nn.softplus(a + dt_bias.astype(F32))
    o = gdn_delta_rule(proj, conv_w, beta, g, norm_gain)
    return matmul_residual(o.reshape(B * S, GDN_V_WIDTH), w_out.astype(BF16), h.reshape(B * S, D)).reshape(B, S, D)


def memory_kv(mem, mem_gain, wkv, k_gain):
    B, M, D = mem.shape
    kv = norm_matmul(mem.reshape(B * M, D), mem_gain, wkv.astype(BF16))
    k, v = jnp.split(kv.reshape(B, M, 2 * XA_WIDTH), 2, axis=-1)
    k = rms_norm(k.reshape(B, M, XA_HEADS, XA_HEAD_DIM), k_gain).reshape(B, M, XA_WIDTH)
    return k.astype(BF16), v.astype(BF16)


def kernel(x, mem, rel_bias, ln_mix, ln_mem, ln_ffn, hyb_w_in, hyb_w_out, nsa_q_gain, nsa_k_gain,
           nsa_cmp_pos, nsa_cmp_w1, nsa_cmp_w2, ret_gn_gain, gdn_w_in, gdn_conv_w, gdn_a_log,
           gdn_dt_bias, gdn_norm_gain, gdn_w_out, xa_wq, xa_wkv, xa_q_gain, xa_k_gain, xa_mem_gain,
           xa_wo, ffn_w_in, ffn_w_out):
    B, S, D = x.shape
    h = x
    for layer in range(DEPTH):
        if layer % 2 == 0:
            e = layer // 2
            h = hybrid_mixer(h, ln_mix[layer], hyb_w_in[e], hyb_w_out[e], nsa_q_gain[e], nsa_k_gain[e],
                             nsa_cmp_pos[e], nsa_cmp_w1[e], nsa_cmp_w2[e], ret_gn_gain[e], rel_bias)
        else:
            o = layer // 2
            h = gdn_mixer(h, ln_mix[layer], gdn_w_in[o], gdn_conv_w[o], gdn_a_log[o], gdn_dt_bias[o],
                          gdn_norm_gain[o], gdn_w_out[o])
        k_mem, v_mem = memory_kv(mem, xa_mem_gain[layer], xa_wkv[layer], xa_k_gain[layer])
        h = xattn_residual(h, ln_mem[layer], xa_wq[layer].astype(BF16), xa_q_gain[layer], k_mem, v_mem,
                           xa_wo[layer].astype(BF16))
        h = ffn_residual(h.reshape(B * S, D), ln_ffn[layer], ffn_w_in[layer].astype(BF16),
                         ffn_w_out[layer].astype(BF16)).reshape(B, S, D)
    return h
```

```python
import functools
import math

import jax
import jax.numpy as jnp
import numpy as np
from jax import lax
from jax.experimental import pallas as pl
from jax.experimental.pallas import tpu as pltpu

F32 = jnp.float32
BF16 = jnp.bfloat16

D_MODEL = 2048
DEPTH = 4
NORM_EPS = 1e-6
NEG_INF = -1e30

NSA_HEADS = 8
NSA_KV_GROUPS = 2
NSA_GROUP_HEADS = NSA_HEADS // NSA_KV_GROUPS
NSA_HEAD_DIM = 128
NSA_CMP_LEN = 32
NSA_CMP_STRIDE = 16
NSA_SEL_BLOCK = 64
NSA_N_SELECT = 16
NSA_WINDOW = 512
NSA_Q_BLOCK = 64
NSA_FORCE_SCORE = 1e6
NSA_WIDTH = NSA_HEADS * NSA_HEAD_DIM
NSA_KV_WIDTH = NSA_KV_GROUPS * NSA_HEAD_DIM

RET_HEADS = 4
RET_DK = 256
RET_DV = 256
RET_CHUNK = 128
RET_ROPE_BASE = 10000.0
RET_GN_EPS = 1e-5

GDN_QK_HEADS = 16
GDN_V_HEADS = 32
GDN_HEAD_DIM = 128
GDN_CONV = 4
GDN_CHUNK = 64
GDN_QK_WIDTH = GDN_QK_HEADS * GDN_HEAD_DIM
GDN_V_WIDTH = GDN_V_HEADS * GDN_HEAD_DIM
GDN_CONV_CH = 2 * GDN_QK_WIDTH + GDN_V_WIDTH

REL_BUCKETS = 32
REL_MAX_DIST = 1024

XA_HEADS = 4
XA_HEAD_DIM = 128
XA_WIDTH = XA_HEADS * XA_HEAD_DIM

V7X_VMEM_LIMIT_BYTES = 56 * 1024 * 1024


def _cparams(*sem):
    return pltpu.CompilerParams(dimension_semantics=sem, vmem_limit_bytes=V7X_VMEM_LIMIT_BYTES)


def _round_up(n, m):
    return -(-n // m) * m


def _rms_rows(x, gain):
    return x * lax.rsqrt(jnp.mean(x * x, axis=-1, keepdims=True) + NORM_EPS) * gain


def _norm_matmul_kernel(x_ref, g_ref, w_ref, o_ref, xn_ref):
    @pl.when(pl.program_id(1) == 0)
    def _():
        xn_ref[...] = _rms_rows(x_ref[...], g_ref[...]).astype(BF16)

    o_ref[...] = jnp.dot(xn_ref[...], w_ref[...], preferred_element_type=F32).astype(o_ref.dtype)


def norm_matmul(x, gain, w, *, tm=1024, tn=512, out_dtype=F32):
    m, k = x.shape
    n = w.shape[1]
    tm = min(tm, m)
    return pl.pallas_call(
        _norm_matmul_kernel,
        grid=(m // tm, n // tn),
        in_specs=[pl.BlockSpec((tm, k), lambda i, j: (i, 0)),
                  pl.BlockSpec((1, k), lambda i, j: (0, 0)),
                  pl.BlockSpec((k, tn), lambda i, j: (0, j))],
        out_specs=pl.BlockSpec((tm, tn), lambda i, j: (i, j)),
        out_shape=jax.ShapeDtypeStruct((m, n), out_dtype),
        scratch_shapes=[pltpu.VMEM((tm, k), BF16)],
        compiler_params=_cparams("parallel", "arbitrary"),
        name="norm_matmul",
    )(x, gain.reshape(1, k), w)


def _matmul_res_kernel(x_ref, w_ref, r_ref, o_ref):
    o_ref[...] = r_ref[...] + jnp.dot(x_ref[...], w_ref[...], preferred_element_type=F32)


def matmul_residual(x, w, res, *, tm=1024, tn=512):
    m, k = x.shape
    n = w.shape[1]
    return pl.pallas_call(
        _matmul_res_kernel,
        grid=(m // tm, n // tn),
        in_specs=[pl.BlockSpec((tm, k), lambda i, j: (i, 0)),
                  pl.BlockSpec((k, tn), lambda i, j: (0, j)),
                  pl.BlockSpec((tm, tn), lambda i, j: (i, j))],
        out_specs=pl.BlockSpec((tm, tn), lambda i, j: (i, j)),
        out_shape=jax.ShapeDtypeStruct((m, n), F32),
        compiler_params=_cparams("parallel", "arbitrary"),
        name="matmul_residual",
    )(x, w, res)


def _matmul2_res_kernel(x1_ref, x2_ref, w1_ref, w2_ref, r_ref, o_ref):
    o_ref[...] = (r_ref[...] + jnp.dot(x1_ref[...], w1_ref[...], preferred_element_type=F32)
                  + jnp.dot(x2_ref[...], w2_ref[...], preferred_element_type=F32))


def matmul2_residual(x1, x2, w, res, *, tm=1024, tn=512):
    m, k1 = x1.shape
    n = w.shape[1]
    return pl.pallas_call(
        _matmul2_res_kernel,
        grid=(m // tm, n // tn),
        in_specs=[pl.BlockSpec((tm, k1), lambda i, j: (i, 0)),
                  pl.BlockSpec((tm, k1), lambda i, j: (i, 0)),
                  pl.BlockSpec((k1, tn), lambda i, j: (0, j)),
                  pl.BlockSpec((k1, tn), lambda i, j: (1, j)),
                  pl.BlockSpec((tm, tn), lambda i, j: (i, j))],
        out_specs=pl.BlockSpec((tm, tn), lambda i, j: (i, j)),
        out_shape=jax.ShapeDtypeStruct((m, n), F32),
        compiler_params=_cparams("parallel", "arbitrary"),
        name="matmul2_residual",
    )(x1, x2, w, w, res)


def _ffn_kernel(x_ref, g_ref, wg_ref, wu_ref, wo_ref, o_ref, xn_ref, acc_ref):
    j = pl.program_id(1)

    @pl.when(j == 0)
    def _():
        x = x_ref[...]
        xn_ref[...] = _rms_rows(x, g_ref[...]).astype(BF16)
        acc_ref[...] = x

    xn = xn_ref[...]
    gate = jnp.dot(xn, wg_ref[...], preferred_element_type=F32)
    up = jnp.dot(xn, wu_ref[...], preferred_element_type=F32)
    act = (gate * jax.nn.sigmoid(gate) * up).astype(BF16)
    acc_ref[...] += jnp.dot(act, wo_ref[...], preferred_element_type=F32)

    @pl.when(j == pl.num_programs(1) - 1)
    def _():
        o_ref[...] = acc_ref[...]


def ffn_residual(h, gain, w_in, w_out, *, tm=512, th=512):
    m, d = h.shape
    hidden = w_out.shape[0]
    nh = hidden // th
    return pl.pallas_call(
        _ffn_kernel,
        grid=(m // tm, nh),
        in_specs=[pl.BlockSpec((tm, d), lambda i, j: (i, 0)),
                  pl.BlockSpec((1, d), lambda i, j: (0, 0)),
                  pl.BlockSpec((d, th), lambda i, j: (0, j)),
                  pl.BlockSpec((d, th), lambda i, j: (0, j + nh)),
                  pl.BlockSpec((th, d), lambda i, j: (j, 0))],
        out_specs=pl.BlockSpec((tm, d), lambda i, j: (i, 0)),
        out_shape=jax.ShapeDtypeStruct((m, d), F32),
        scratch_shapes=[pltpu.VMEM((tm, d), BF16), pltpu.VMEM((tm, d), F32)],
        compiler_params=_cparams("parallel", "arbitrary"),
        name="ffn_residual",
    )(h, gain.reshape(1, d), w_in, w_in, w_out)


def _xattn_kernel(h_ref, g_ref, wq_ref, qg_ref, k_ref, v_ref, wo_ref, o_ref):
    x = h_ref[0]
    xn = _rms_rows(x, g_ref[...]).astype(BF16)
    q = jnp.dot(xn, wq_ref[...], preferred_element_type=F32)
    k = k_ref[0]
    v = v_ref[0]
    outs = []
    for hh in range(XA_HEADS):
        sl = slice(hh * XA_HEAD_DIM, (hh + 1) * XA_HEAD_DIM)
        qh = _rms_rows(q[:, sl], qg_ref[...]).astype(BF16)
        s = lax.dot_general(qh, k[:, sl], (((1,), (1,)), ((), ())), preferred_element_type=F32)
        s = s * (XA_HEAD_DIM ** -0.5)
        s = s - jnp.max(s, axis=-1, keepdims=True)
        p = jnp.exp(s)
        p = p / jnp.sum(p, axis=-1, keepdims=True)
        outs.append(jnp.dot(p.astype(BF16), v[:, sl], preferred_element_type=F32))
    o = jnp.concatenate(outs, axis=-1).astype(BF16)
    o_ref[0] = x + jnp.dot(o, wo_ref[...], preferred_element_type=F32)


def xattn_residual(h, gain, wq, q_gain, k, v, wo, *, tm=512):
    b, s, d = h.shape
    mt = k.shape[1]
    return pl.pallas_call(
        _xattn_kernel,
        grid=(b, s // tm),
        in_specs=[pl.BlockSpec((1, tm, d), lambda bi, i: (bi, i, 0)),
                  pl.BlockSpec((1, d), lambda bi, i: (0, 0)),
                  pl.BlockSpec((d, XA_WIDTH), lambda bi, i: (0, 0)),
                  pl.BlockSpec((1, XA_HEAD_DIM), lambda bi, i: (0, 0)),
                  pl.BlockSpec((1, mt, XA_WIDTH), lambda bi, i: (bi, 0, 0)),
                  pl.BlockSpec((1, mt, XA_WIDTH), lambda bi, i: (bi, 0, 0)),
                  pl.BlockSpec((XA_WIDTH, d), lambda bi, i: (0, 0))],
        out_specs=pl.BlockSpec((1, tm, d), lambda bi, i: (bi, i, 0)),
        out_shape=jax.ShapeDtypeStruct((b, s, d), F32),
        compiler_params=_cparams("parallel", "parallel"),
        name="xattn_residual",
    )(h, gain.reshape(1, d), wq, q_gain.reshape(1, XA_HEAD_DIM), k, v, wo)


def rms_norm(x, gain, eps=NORM_EPS):
    x32 = x.astype(F32)
    y = x32 * lax.rsqrt(jnp.mean(x32 * x32, axis=-1, keepdims=True) + eps)
    return (y * gain.astype(F32)).astype(x.dtype)


def l2_norm(x, eps=NORM_EPS):
    x32 = x.astype(F32)
    return x32 * lax.rsqrt(jnp.sum(x32 * x32, axis=-1, keepdims=True) + eps)


def masked_softmax(logits, mask):
    p = jax.nn.softmax(jnp.where(mask, logits.astype(F32), NEG_INF), axis=-1)
    return jnp.where(mask, p, 0.0)


def rel_bucket(dist):
    dist = jnp.maximum(dist, 0)
    max_exact = REL_BUCKETS // 2
    scaled = (jnp.log(jnp.maximum(dist, max_exact).astype(F32) / max_exact)
              / math.log(REL_MAX_DIST / max_exact) * (REL_BUCKETS - max_exact))
    large = jnp.minimum(max_exact + scaled.astype(jnp.int32), REL_BUCKETS - 1)
    return jnp.where(dist < max_exact, dist, large)


def rotary(x, pos):
    half = x.shape[-1] // 2
    inv = RET_ROPE_BASE ** (-jnp.arange(half, dtype=F32) / half)
    ang = pos.astype(F32)[:, None] * inv[None, :]
    cos, sin = jnp.cos(ang)[None, :, None, :], jnp.sin(ang)[None, :, None, :]
    x1, x2 = x[..., :half], x[..., half:]
    return jnp.concatenate([x1 * cos - x2 * sin, x1 * sin + x2 * cos], axis=-1)


def causal_depthwise_conv(x, w):
    k, ch = w.shape
    return lax.conv_general_dilated(x, w[:, None, :].astype(x.dtype), window_strides=(1,),
                                    padding=[(k - 1, 0)], dimension_numbers=('NWC', 'WIO', 'NWC'),
                                    feature_group_count=ch)


def compress_blocks(t, pos, w1, w2):
    b, s, g, dh = t.shape
    r = NSA_CMP_LEN // NSA_CMP_STRIDE
    n_chunks = s // NSA_CMP_STRIDE
    n_cmp = n_chunks - r + 1
    c = t.reshape(b, n_chunks, NSA_CMP_STRIDE, g, dh)
    blocks = jnp.concatenate([c[:, j:j + n_cmp] for j in range(r)], axis=2)
    blocks = blocks + pos[None, None, :, None, :].astype(t.dtype)
    flat = blocks.transpose(0, 3, 1, 2, 4).reshape(b, g, n_cmp, NSA_CMP_LEN * dh)
    return jax.nn.silu(flat @ w1) @ w2


NSA_QT = 128
NSA_KT = 512
NSA_WT = NSA_WINDOW + NSA_QT


def _dot_nt(a, b):
    return lax.dot_general(a, b, (((1,), (1,)), ((), ())), preferred_element_type=F32)


def _dot_tn(a, b):
    return lax.dot_general(a, b, (((0,), (0,)), ((), ())), preferred_element_type=F32)


def _bias_tile(tb_ref, h, first_slab, n_slabs):
    return jnp.concatenate([tb_ref[h, jnp.maximum(first_slab - j, 0)] for j in range(n_slabs)], axis=1)


def _nsa_kernel(q_ref, gl_ref, kc_ref, vc_ref, bc_ref, covt_ref, ks_ref, vs_ref, kw_ref, vw_ref,
                tb_ref, e_ref, qg_ref, o_ref, a_scr, *, n_top):
    QT, KT, WT, HG, dh = NSA_QT, NSA_KT, NSA_WT, NSA_GROUP_HEADS, NSA_HEAD_DIM
    nsb, nc = covt_ref.shape
    q0 = pl.program_id(2) * QT

    x = q_ref[0]
    qs = []
    for h in range(HG):
        xh = _rms_rows(x[:, h * dh:(h + 1) * dh], qg_ref[...]) * (dh ** -0.5)
        qs.append(xh.astype(BF16))
    qn = jnp.concatenate(qs, axis=0)

    sc = _dot_nt(qn, kc_ref[0, 0]) + bc_ref[...].reshape(HG * QT, nc)
    row = lax.broadcasted_iota(jnp.int32, (HG * QT, nc), 0)
    col = lax.broadcasted_iota(jnp.int32, (HG * QT, nc), 1)
    tq = q0 + (row & (QT - 1))
    mask_c = tq >= col * NSA_CMP_STRIDE + (NSA_CMP_LEN - 1)
    sc = jnp.where(mask_c, sc, NEG_INF)
    pc = jnp.where(mask_c, jnp.exp(sc - jnp.max(sc, axis=-1, keepdims=True)), 0.0)
    den = jnp.sum(pc, axis=-1, keepdims=True)
    pcb = (pc / jnp.maximum(den, 1e-30)).astype(BF16)
    o_c = jnp.dot(pcb, vc_ref[0, 0], preferred_element_type=F32)

    imp_all = _dot_nt(covt_ref[...], pcb)
    imp = imp_all[:, 0:QT]
    for h in range(1, HG):
        imp = imp + imp_all[:, h * QT:(h + 1) * QT]
    jj = lax.broadcasted_iota(jnp.int32, (nsb, QT), 0)
    cur = (q0 + lax.broadcasted_iota(jnp.int32, (nsb, QT), 1)) // NSA_SEL_BLOCK
    forced = (jj == 0) | (jj == cur) | (jj == cur - 1)
    a = jnp.where(forced, NSA_FORCE_SCORE, imp)
    a = jnp.where(jj > cur, -NSA_FORCE_SCORE, a)
    a_scr[...] = a

    def rank_body(i, rank):
        r = a_scr[pl.ds(i, 1), :]
        return rank + jnp.where((r > a) | ((r == a) & (jj > i)), 1.0, 0.0)

    n_live = jnp.minimum((q0 + QT - 1) // NSA_SEL_BLOCK + 1, nsb)
    rank = lax.fori_loop(0, n_live, rank_body, jnp.zeros((nsb, QT), F32))
    sel_t = jnp.where(rank < n_top, 1.0, 0.0)
    if nsb < 128:
        sel_t = jnp.concatenate([sel_t, jnp.zeros((128 - nsb, QT), F32)], axis=0)
    sel = sel_t.T.astype(BF16)

    cmr = (lax.broadcasted_iota(jnp.int32, (QT, KT), 1) - lax.broadcasted_iota(jnp.int32, (QT, KT), 0))

    def sel_body(kt, carry):
        m_i, l_i, acc = carry
        k0 = pl.multiple_of(kt * KT, KT)
        k = ks_ref[0, pl.ds(k0, KT), :]
        v = vs_ref[0, pl.ds(k0, KT), :]
        s = _dot_nt(qn, k)
        delta = q0 - k0
        visible = (jnp.dot(sel, e_ref[kt], preferred_element_type=F32) > 0.5) & (cmr <= delta)
        seg = delta // QT
        parts = []
        for h in range(HG):
            bias = _bias_tile(tb_ref, h, seg, KT // QT)
            parts.append(jnp.where(visible, s[h * QT:(h + 1) * QT] + bias, NEG_INF))
        s = jnp.concatenate(parts, axis=0)
        m_new = jnp.maximum(m_i, jnp.max(s, axis=-1, keepdims=True))
        alpha = jnp.exp(m_i - m_new)
        p = jnp.exp(s - m_new)
        l_new = alpha * l_i + jnp.sum(p, axis=-1, keepdims=True)
        acc = alpha * acc + jnp.dot(p.astype(BF16), v, preferred_element_type=F32)
        return m_new, l_new, acc

    init = (jnp.full((HG * QT, 1), NEG_INF, F32), jnp.zeros((HG * QT, 1), F32), jnp.zeros((HG * QT, dh), F32))
    _, l_s, acc_s = lax.fori_loop(0, q0 // KT + 1, sel_body, init)
    o_s = acc_s / l_s

    w0 = pl.multiple_of(q0, QT)
    kwin = kw_ref[0, pl.ds(w0, WT), :]
    vwin = vw_ref[0, pl.ds(w0, WT), :]
    sw = _dot_nt(qn, kwin)
    cw = lax.broadcasted_iota(jnp.int32, (QT, WT), 1)
    dist = NSA_WINDOW + lax.broadcasted_iota(jnp.int32, (QT, WT), 0) - cw
    vis_w = (dist >= 0) & (dist < NSA_WINDOW) & (cw >= NSA_WINDOW - q0)
    parts = []
    for h in range(HG):
        bias = _bias_tile(tb_ref, h, NSA_WINDOW // QT, WT // QT)
        parts.append(jnp.where(vis_w, sw[h * QT:(h + 1) * QT] + bias, NEG_INF))
    sw = jnp.concatenate(parts, axis=0)
    pw = jnp.exp(sw - jnp.max(sw, axis=-1, keepdims=True))
    pw = pw / jnp.sum(pw, axis=-1, keepdims=True)
    o_w = jnp.dot(pw.astype(BF16), vwin, preferred_element_type=F32)

    gates = jax.nn.sigmoid(gl_ref[0][:, :3 * HG])
    outs = []
    for h in range(HG):
        rs = slice(h * QT, (h + 1) * QT)
        outs.append(gates[:, 3 * h:3 * h + 1] * o_c[rs] + gates[:, 3 * h + 1:3 * h + 2] * o_s[rs]
                    + gates[:, 3 * h + 2:3 * h + 3] * o_w[rs])
    o_ref[0] = jnp.concatenate(outs, axis=-1).astype(o_ref.dtype)


def _rel_table(rel_bias, s):
    return rel_bias[rel_bucket(jnp.arange(s))].T.astype(F32)


def nsa_attention_pallas(proj, gl_blk, kc, vc, ks, vs, kw, vw, q_gain, rel_bias):
    B, S, _ = proj.shape
    G, HG, dh = NSA_KV_GROUPS, NSA_GROUP_HEADS, NSA_HEAD_DIM
    QT, KT, WT, W, SB = NSA_QT, NSA_KT, NSA_WT, NSA_WINDOW, NSA_SEL_BLOCK
    nsb, nc, nkt, nseg = S // SB, S // NSA_CMP_STRIDE, S // KT, max(S // QT, W // QT + 1)
    n_top = min(NSA_N_SELECT, nsb)

    def bias_of(dist):
        onehot = (rel_bucket(dist)[..., None] == jnp.arange(REL_BUCKETS)) & (dist >= 0)[..., None]
        return jnp.einsum('...b,bh->h...', onehot.astype(F32), rel_bias.astype(F32),
                          precision=lax.Precision.HIGHEST)

    bc = bias_of(jnp.arange(S)[:, None] - (jnp.arange(nc) * NSA_CMP_STRIDE + NSA_CMP_LEN - 1)[None, :])
    tb = bias_of(QT * jnp.arange(nseg)[:, None, None] + jnp.arange(QT)[None, :, None] - jnp.arange(QT)[None, None, :])
    cmp_start = np.arange(nc) * NSA_CMP_STRIDE
    cmp_end = cmp_start + NSA_CMP_LEN - 1
    sel_start = np.arange(nsb) * SB
    cover_t = ((cmp_start[None, :] < sel_start[:, None] + SB) & (cmp_end[None, :] >= sel_start[:, None])
               & (np.arange(nc)[None, :] < nc - 1))
    cover_t = jnp.asarray(cover_t, BF16)
    e = (np.arange(128)[None, :, None]
         == (np.arange(nkt)[:, None, None] * (KT // SB) + np.arange(KT)[None, None, :] // SB))
    e = jnp.asarray(e, BF16)
    pad = ((0, 0), (W, 0), (0, 0))
    kw = jnp.pad(kw, pad)
    vw = jnp.pad(vw, pad)

    kern = functools.partial(_nsa_kernel, n_top=n_top)
    return pl.pallas_call(
        kern,
        grid=(B, G, S // QT),
        in_specs=[
            pl.BlockSpec((1, QT, HG * dh), lambda b, g, i: (b, i, g)),
            pl.BlockSpec((1, QT, 128), lambda b, g, i: (b, i, gl_blk + g)),
            pl.BlockSpec((1, 1, nc, dh), lambda b, g, i: (b, g, 0, 0)),
            pl.BlockSpec((1, 1, nc, dh), lambda b, g, i: (b, g, 0, 0)),
            pl.BlockSpec((HG, QT, nc), lambda b, g, i: (g, i, 0)),
            pl.BlockSpec((nsb, nc), lambda b, g, i: (0, 0)),
            pl.BlockSpec((1, S, dh), lambda b, g, i: (b, 0, g)),
            pl.BlockSpec((1, S, dh), lambda b, g, i: (b, 0, g)),
            pl.BlockSpec((1, S + W, dh), lambda b, g, i: (b, 0, g)),
            pl.BlockSpec((1, S + W, dh), lambda b, g, i: (b, 0, g)),
            pl.BlockSpec((HG, nseg, QT, QT), lambda b, g, i: (g, 0, 0, 0)),
            pl.BlockSpec((nkt, 128, KT), lambda b, g, i: (0, 0, 0)),
            pl.BlockSpec((1, dh), lambda b, g, i: (0, 0)),
        ],
        out_specs=pl.BlockSpec((1, QT, HG * dh), lambda b, g, i: (b, i, g)),
        out_shape=jax.ShapeDtypeStruct((B, S, NSA_WIDTH), BF16),
        scratch_shapes=[pltpu.VMEM((nsb, QT), F32)],
        compiler_params=_cparams("parallel", "parallel", "arbitrary"),
        name="nsa_attention",
    )(proj, proj, kc, vc, bc, cover_t, ks, vs, kw, vw, tb, e, q_gain.reshape(1, dh))


def nsa_kv_prep(kc, vc, ks, vs, kw, vw, k_gain, cmp_pos, cmp_w1, cmp_w2):
    B, S, _ = kc.shape
    kv = lambda t: t.reshape(B, S, NSA_KV_GROUPS, NSA_HEAD_DIM)
    kcc = rms_norm(compress_blocks(kv(kc), cmp_pos[0], cmp_w1[0], cmp_w2[0]), k_gain[0])
    vcc = compress_blocks(kv(vc), cmp_pos[1], cmp_w1[1], cmp_w2[1])
    padc = ((0, 0), (0, 0), (0, 1), (0, 0))
    kcc = jnp.pad(kcc, padc).astype(BF16)
    vcc = jnp.pad(vcc, padc).astype(BF16)
    ksn = rms_norm(kv(ks), k_gain[1]).reshape(B, S, -1).astype(BF16)
    kwn = rms_norm(kv(kw), k_gain[2]).reshape(B, S, -1).astype(BF16)
    return kcc, vcc, ksn, vs.astype(BF16), kwn, vw.astype(BF16)


def _rotate_half(x, cos, sin):
    half = x.shape[-1] // 2
    x1, x2 = x[:, :half], x[:, half:]
    return jnp.concatenate([x1 * cos - x2 * sin, x1 * sin + x2 * cos], axis=-1)


def _retention_kernel(q_ref, k_ref, v_ref, g_ref, cos_ref, sin_ref, din_ref, qd_ref, kd_ref, cd_ref, gn_ref,
                      o_ref, s_scr):
    @pl.when(pl.program_id(2) == 0)
    def _():
        s_scr[...] = jnp.zeros_like(s_scr)

    cos, sin = cos_ref[...], sin_ref[...]
    q = _rotate_half(q_ref[0], cos, sin) * (RET_DK ** -0.5)
    k = _rotate_half(k_ref[0], cos, sin)
    v16 = v_ref[0].astype(BF16)
    inner = _dot_nt(q.astype(BF16), k.astype(BF16)) * din_ref[0]
    s = s_scr[...]
    o = (jnp.dot(inner.astype(BF16), v16, preferred_element_type=F32)
         + jnp.dot((q * qd_ref[0]).astype(BF16), s.astype(BF16), preferred_element_type=F32))
    s_scr[...] = s * cd_ref[0] + _dot_tn((k * kd_ref[0]).astype(BF16), v16)
    mu = jnp.mean(o, axis=-1, keepdims=True)
    d = o - mu
    var = jnp.mean(d * d, axis=-1, keepdims=True)
    gate = g_ref[0]
    o_ref[0] = (gate * jax.nn.sigmoid(gate) * (d * lax.rsqrt(var + RET_GN_EPS) * gn_ref[...])).astype(o_ref.dtype)


def retention_pallas(proj, q_blk, gn_gain):
    B, S, _ = proj.shape
    H, dk, dv, C = RET_HEADS, RET_DK, RET_DV, RET_CHUNK
    N = S // C
    half = dk // 2
    inv = RET_ROPE_BASE ** (-jnp.arange(half, dtype=F32) / half)
    ang = jnp.arange(S).astype(F32)[:, None] * inv[None, :]
    cos, sin = jnp.cos(ang), jnp.sin(ang)
    log_gamma = jnp.log(1.0 - 2.0 ** (-5.0 - jnp.arange(H, dtype=F32)))
    idx = jnp.arange(C, dtype=F32)
    rel = idx[:, None] - idx[None, :]
    decay_in = jnp.where(rel >= 0, jnp.exp(log_gamma[:, None, None] * jnp.maximum(rel, 0.0)), 0.0)
    q_decay = jnp.exp(log_gamma[:, None] * (idx + 1.0))[..., None]
    k_decay = jnp.exp(log_gamma[:, None] * (C - 1.0 - idx))[..., None]
    chunk_decay = jnp.exp(log_gamma * C)[:, None, None]
    return pl.pallas_call(
        _retention_kernel,
        grid=(B, H, N),
        in_specs=[
            pl.BlockSpec((1, C, dk), lambda b, h, n: (b, n, q_blk + h)),
            pl.BlockSpec((1, C, dk), lambda b, h, n: (b, n, q_blk + H + h)),
            pl.BlockSpec((1, C, dv), lambda b, h, n: (b, n, q_blk + 2 * H + h)),
            pl.BlockSpec((1, C, dv), lambda b, h, n: (b, n, q_blk + 3 * H + h)),
            pl.BlockSpec((C, half), lambda b, h, n: (n, 0)),
            pl.BlockSpec((C, half), lambda b, h, n: (n, 0)),
            pl.BlockSpec((1, C, C), lambda b, h, n: (h, 0, 0)),
            pl.BlockSpec((1, C, 1), lambda b, h, n: (h, 0, 0)),
            pl.BlockSpec((1, C, 1), lambda b, h, n: (h, 0, 0)),
            pl.BlockSpec((1, 1, 1), lambda b, h, n: (h, 0, 0)),
            pl.BlockSpec((1, dv), lambda b, h, n: (0, h)),
        ],
        out_specs=pl.BlockSpec((1, C, dv), lambda b, h, n: (b, n, h)),
        out_shape=jax.ShapeDtypeStruct((B, S, H * dv), BF16),
        scratch_shapes=[pltpu.VMEM((dk, dv), F32)],
        compiler_params=_cparams("parallel", "parallel", "arbitrary"),
        name="retention",
    )(proj, proj, proj, proj, cos, sin, decay_in, q_decay, k_decay, chunk_decay, gn_gain.reshape(1, H * dv))


GDN_HB = 16
GDN_HALO = 8


def _conv_silu(x_ref, halo_ref, keep, w, stage_ref):
    c = x_ref.shape[1]
    stage_ref[0:GDN_HALO, :] = halo_ref[0] * keep
    stage_ref[GDN_HALO:GDN_HALO + c, :] = x_ref[0]
    y = w[GDN_CONV - 1:GDN_CONV] * x_ref[0]
    for j in range(GDN_CONV - 1):
        off = GDN_HALO - (GDN_CONV - 1) + j
        y = y + w[j:j + 1] * stage_ref[off:off + c, :]
    return y * jax.nn.sigmoid(y)


def _l2_rows(x):
    return x * lax.rsqrt(jnp.sum(x * x, axis=-1, keepdims=True) + NORM_EPS)


def _gdn_kernel(q_ref, k_ref, v_ref, qh_ref, kh_ref, vh_ref, wq_ref, wk_ref, wv_ref, z_ref,
                bcol_ref, gcol_ref, grow_ref, ng_ref, o_ref, s_scr, qst_scr, kst_scr, vst_scr):
    C, dh, hb = GDN_CHUNK, GDN_HEAD_DIM, GDN_HB
    rep = GDN_V_HEADS // GDN_QK_HEADS
    first = pl.program_id(2) == 0

    @pl.when(first)
    def _():
        s_scr[...] = jnp.zeros_like(s_scr)

    keep = jnp.where(first, 0.0, 1.0)
    qc = _conv_silu(q_ref, qh_ref, keep, wq_ref[...], qst_scr)
    kc = _conv_silu(k_ref, kh_ref, keep, wk_ref[...], kst_scr)
    vc = _conv_silu(v_ref, vh_ref, keep, wv_ref[...], vst_scr)

    ri = lax.broadcasted_iota(jnp.int32, (C, C), 0)
    ci = lax.broadcasted_iota(jnp.int32, (C, C), 1)
    causal = ri >= ci
    strict = ri > ci
    bcol = bcol_ref[0, 0]
    gcol = gcol_ref[0, 0]
    grow = grow_ref[0, 0]
    heads = range(hb)
    qs, ks, grams = [], [], []
    for hq in range(hb // rep):
        qh = _l2_rows(qc[:, hq * dh:(hq + 1) * dh]) * (dh ** -0.5)
        kh = _l2_rows(kc[:, hq * dh:(hq + 1) * dh])
        k16 = kh.astype(BF16)
        qs.append(qh)
        ks.append(kh)
        grams.append(_dot_nt(jnp.concatenate([qh.astype(BF16), k16], axis=0), k16))
    beta = [bcol[:, h:h + 1] for h in heads]
    gc = [gcol[:, h:h + 1] for h in heads]
    gr = [grow[h:h + 1, :] for h in heads]
    g_last = [gr[h][:, C - 1:C] for h in heads]
    eg = [jnp.exp(gc[h]) for h in heads]
    decay = [jnp.where(causal, jnp.exp(jnp.minimum(gc[h] - gr[h], 0.0)), 0.0) for h in heads]
    attn = [(grams[h // rep][:C] * decay[h]).astype(BF16) for h in heads]
    nm = [jnp.where(strict, grams[h // rep][C:] * decay[h] * (-beta[h]), 0.0) for h in heads]
    m = [nm[h].astype(BF16) for h in heads]
    for _ in range(int(math.log2(C)) - 1):
        mf = [jnp.dot(m[h], m[h], preferred_element_type=F32) for h in heads]
        m = [mf[h].astype(BF16) for h in heads]
        nm = [nm[h] + mf[h] + jnp.dot(m[h], nm[h].astype(BF16), preferred_element_type=F32) for h in heads]
    x = [jnp.concatenate([vc[:, h * dh:(h + 1) * dh] * beta[h], ks[h // rep] * (beta[h] * eg[h])], axis=1)
         for h in heads]
    x = [x[h] + jnp.dot(nm[h].astype(BF16), x[h].astype(BF16), preferred_element_type=F32) for h in heads]
    s = [s_scr[h] for h in heads]
    ws = [jnp.dot(jnp.concatenate([x[h][:, dh:].astype(BF16), (qs[h // rep] * eg[h]).astype(BF16)], axis=0),
                  s[h].astype(BF16), preferred_element_type=F32) for h in heads]
    vn = [(x[h][:, :dh] - ws[h][:C]).astype(BF16) for h in heads]
    o = [ws[h][C:] + jnp.dot(attn[h], vn[h], preferred_element_type=F32) for h in heads]
    for h in heads:
        k_dec = (ks[h // rep] * jnp.exp(g_last[h] - gc[h])).astype(BF16)
        s_scr[h] = s[h] * jnp.exp(g_last[h]) + _dot_tn(k_dec, vn[h])
    outs = []
    for h in heads:
        zh = z_ref[0, :, h * dh:(h + 1) * dh]
        outs.append((_rms_rows(o[h], ng_ref[...]) * (zh * jax.nn.sigmoid(zh))).astype(o_ref.dtype))
    o_ref[0] = jnp.concatenate(outs, axis=1)


def gdn_delta_rule(proj, conv_w, beta, g, norm_gain):
    B, S, _ = proj.shape
    C, dh, hb, H = GDN_CHUNK, GDN_HEAD_DIM, GDN_HB, GDN_V_HEADS
    rep = GDN_V_HEADS // GDN_QK_HEADS
    N, HP = S // C, H // hb
    wqk, wv = hb // rep * dh, hb * dh
    kb0, vb0, zb0 = GDN_QK_WIDTH // wqk, 2 * GDN_QK_WIDTH // wv, GDN_CONV_CH // wv
    hr = C // GDN_HALO
    gc = jnp.cumsum(g.reshape(B, N, C, H), axis=2)
    gcol = gc.reshape(B, S, HP, hb).transpose(0, 2, 1, 3)
    bcol = beta.reshape(B, S, HP, hb).transpose(0, 2, 1, 3)
    grow = gc.reshape(B, N, C, HP, hb).transpose(0, 3, 1, 4, 2).reshape(B, HP, N * hb, C)
    halo = lambda n: jnp.maximum(n * hr - 1, 0)
    return pl.pallas_call(
        _gdn_kernel,
        grid=(B, HP, N),
        in_specs=[
            pl.BlockSpec((1, C, wqk), lambda b, p, n: (b, n, p)),
            pl.BlockSpec((1, C, wqk), lambda b, p, n: (b, n, kb0 + p)),
            pl.BlockSpec((1, C, wv), lambda b, p, n: (b, n, vb0 + p)),
            pl.BlockSpec((1, GDN_HALO, wqk), lambda b, p, n: (b, halo(n), p)),
            pl.BlockSpec((1, GDN_HALO, wqk), lambda b, p, n: (b, halo(n), kb0 + p)),
            pl.BlockSpec((1, GDN_HALO, wv), lambda b, p, n: (b, halo(n), vb0 + p)),
            pl.BlockSpec((GDN_CONV, wqk), lambda b, p, n: (0, p)),
            pl.BlockSpec((GDN_CONV, wqk), lambda b, p, n: (0, kb0 + p)),
            pl.BlockSpec((GDN_CONV, wv), lambda b, p, n: (0, vb0 + p)),
            pl.BlockSpec((1, C, wv), lambda b, p, n: (b, n, zb0 + p)),
            pl.BlockSpec((1, 1, C, hb), lambda b, p, n: (b, p, n, 0)),
            pl.BlockSpec((1, 1, C, hb), lambda b, p, n: (b, p, n, 0)),
            pl.BlockSpec((1, 1, hb, C), lambda b, p, n: (b, p, n, 0)),
            pl.BlockSpec((1, dh), lambda b, p, n: (0, 0)),
        ],
        out_specs=pl.BlockSpec((1, C, wv), lambda b, p, n: (b, n, p)),
        out_shape=jax.ShapeDtypeStruct((B, S, H * dh), BF16),
        scratch_shapes=[pltpu.VMEM((hb, dh, dh), F32), pltpu.VMEM((GDN_HALO + C, wqk), F32),
                        pltpu.VMEM((GDN_HALO + C, wqk), F32), pltpu.VMEM((GDN_HALO + C, wv), F32)],
        compiler_params=_cparams("parallel", "parallel", "arbitrary"),
        name="gdn_delta_rule",
    )(proj, proj, proj, proj, proj, proj, conv_w, conv_w, conv_w, proj, bcol, gcol, grow, norm_gain.reshape(1, dh))


def _pad_cols(w, mult):
    n = w.shape[1]
    return jnp.pad(w, ((0, 0), (0, _round_up(n, mult) - n)))


_HYB_NSA_COLS = NSA_WIDTH + 6 * NSA_KV_WIDTH
_HYB_RET_COLS = 2 * RET_HEADS * RET_DK + 2 * RET_HEADS * RET_DV
_HYB_GATE_COL0 = _HYB_NSA_COLS + _HYB_RET_COLS
_HYB_COLS = _round_up(_HYB_GATE_COL0 + NSA_KV_GROUPS * 128, 512)


def _hybrid_w_in_layout(w_in):
    gate0 = _HYB_NSA_COLS
    ret0 = gate0 + 3 * NSA_HEADS
    per_group = 3 * NSA_GROUP_HEADS
    parts = [w_in[:, :gate0], w_in[:, ret0:ret0 + _HYB_RET_COLS]]
    for g in range(NSA_KV_GROUPS):
        parts.append(jnp.pad(w_in[:, gate0 + g * per_group:gate0 + (g + 1) * per_group],
                             ((0, 0), (0, 128 - per_group))))
    w = jnp.concatenate(parts, axis=1)
    return jnp.pad(w, ((0, 0), (0, _HYB_COLS - w.shape[1]))).astype(BF16)


def hybrid_mixer(h, ln_gain, w_in, w_out, q_gain, k_gain, cmp_pos, cmp_w1, cmp_w2, gn_gain, rel_bias):
    B, S, D = h.shape
    proj = norm_matmul(h.reshape(B * S, D), ln_gain, _hybrid_w_in_layout(w_in)).reshape(B, S, _HYB_COLS)
    kvw = NSA_KV_WIDTH
    kc, vc, ks, vs, kw, vw = (proj[..., NSA_WIDTH + i * kvw:NSA_WIDTH + (i + 1) * kvw] for i in range(6))
    kcc, vcc, ksn, vsb, kwn, vwb = nsa_kv_prep(kc, vc, ks, vs, kw, vw, k_gain, cmp_pos, cmp_w1, cmp_w2)
    a_out = nsa_attention_pallas(proj, _HYB_GATE_COL0 // 128, kcc, vcc, ksn, vsb, kwn, vwb, q_gain, rel_bias)
    b_out = retention_pallas(proj, _HYB_NSA_COLS // RET_DK, gn_gain)
    w_out = w_out.astype(BF16)
    return matmul2_residual(a_out.reshape(B * S, -1), b_out.reshape(B * S, -1), w_out,
                            h.reshape(B * S, D)).reshape(B, S, D)


def gdn_mixer(h, ln_gain, w_in, conv_w, a_log, dt_bias, norm_gain, w_out):
    B, S, D = h.shape
    n_cols = _round_up(w_in.shape[1], 512)
    proj = norm_matmul(h.reshape(B * S, D), ln_gain, _pad_cols(w_in, 512).astype(BF16)).reshape(B, S, n_cols)
    b0 = GDN_CONV_CH + GDN_V_WIDTH
    b = proj[..., b0:b0 + GDN_V_HEADS]
    a = proj[..., b0 + GDN_V_HEADS:b0 + 2 * GDN_V_HEADS]
    beta = jax.nn.sigmoid(b)
    g = -jnp.exp(a_log.astype(F32)) * jax.nn.softplus(a + dt_bias.astype(F32))
    o = gdn_delta_rule(proj, conv_w, beta, g, norm_gain)
    return matmul_residual(o.reshape(B * S, GDN_V_WIDTH), w_out.astype(BF16), h.reshape(B * S, D)).reshape(B, S, D)


def memory_kv(mem, mem_gain, wkv, k_gain):
    B, M, D = mem.shape
    kv = norm_matmul(mem.reshape(B * M, D), mem_gain, wkv.astype(BF16))
    k, v = jnp.split(kv.reshape(B, M, 2 * XA_WIDTH), 2, axis=-1)
    k = rms_norm(k.reshape(B, M, XA_HEADS, XA_HEAD_DIM), k_gain).reshape(B, M, XA_WIDTH)
    return k.astype(BF16), v.astype(BF16)


def kernel(x, mem, rel_bias, ln_mix, ln_mem, ln_ffn, hyb_w_in, hyb_w_out, nsa_q_gain, nsa_k_gain,
           nsa_cmp_pos, nsa_cmp_w1, nsa_cmp_w2, ret_gn_gain, gdn_w_in, gdn_conv_w, gdn_a_log,
           gdn_dt_bias, gdn_norm_gain, gdn_w_out, xa_wq, xa_wkv, xa_q_gain, xa_k_gain, xa_mem_gain,
           xa_wo, ffn_w_in, ffn_w_out):
    B, S, D = x.shape
    h = x
    for layer in range(DEPTH):
        if layer % 2 == 0:
            e = layer // 2
            h = hybrid_mixer(h, ln_mix[layer], hyb_w_in[e], hyb_w_out[e], nsa_q_gain[e], nsa_k_gain[e],
                             nsa_cmp_pos[e], nsa_cmp_w1[e], nsa_cmp_w2[e], ret_gn_gain[e], rel_bias)
        else:
            o = layer // 2
            h = gdn_mixer(h, ln_mix[layer], gdn_w_in[o], gdn_conv_w[o], gdn_a_log[o], gdn_dt_bias[o],
                          gdn_norm_gain[o], gdn_w_out[o])
        k_mem, v_mem = memory_kv(mem, xa_mem_gain[layer], xa_wkv[layer], xa_k_gain[layer])
        h = xattn_residual(h, ln_mem[layer], xa_wq[layer].astype(BF16), xa_q_gain[layer], k_mem, v_mem,
                           xa_wo[layer].astype(BF16))
        h = ffn_residual(h.reshape(B * S, D), ln_ffn[layer], ffn_w_in[layer].astype(BF16),
                         ffn_w_out[layer].astype(BF16)).reshape(B, S, D)
    return h
```

```python
import functools
import math

import jax
import jax.numpy as jnp
import numpy as np
from jax import lax
from jax.experimental import pallas as pl
from jax.experimental.pallas import tpu as pltpu

F32 = jnp.float32
BF16 = jnp.bfloat16

D_MODEL = 2048
DEPTH = 4
NORM_EPS = 1e-6
NEG_INF = -1e30

NSA_HEADS = 8
NSA_KV_GROUPS = 2
NSA_GROUP_HEADS = NSA_HEADS // NSA_KV_GROUPS
NSA_HEAD_DIM = 128
NSA_CMP_LEN = 32
NSA_CMP_STRIDE = 16
NSA_SEL_BLOCK = 64
NSA_N_SELECT = 16
NSA_WINDOW = 512
NSA_Q_BLOCK = 64
NSA_FORCE_SCORE = 1e6
NSA_WIDTH = NSA_HEADS * NSA_HEAD_DIM
NSA_KV_WIDTH = NSA_KV_GROUPS * NSA_HEAD_DIM

RET_HEADS = 4
RET_DK = 256
RET_DV = 256
RET_CHUNK = 128
RET_ROPE_BASE = 10000.0
RET_GN_EPS = 1e-5

GDN_QK_HEADS = 16
GDN_V_HEADS = 32
GDN_HEAD_DIM = 128
GDN_CONV = 4
GDN_CHUNK = 64
GDN_QK_WIDTH = GDN_QK_HEADS * GDN_HEAD_DIM
GDN_V_WIDTH = GDN_V_HEADS * GDN_HEAD_DIM
GDN_CONV_CH = 2 * GDN_QK_WIDTH + GDN_V_WIDTH

REL_BUCKETS = 32
REL_MAX_DIST = 1024

XA_HEADS = 4
XA_HEAD_DIM = 128
XA_WIDTH = XA_HEADS * XA_HEAD_DIM

V7X_VMEM_LIMIT_BYTES = 56 * 1024 * 1024


def _cparams(*sem):
    return pltpu.CompilerParams(dimension_semantics=sem, vmem_limit_bytes=V7X_VMEM_LIMIT_BYTES)


def _round_up(n, m):
    return -(-n // m) * m


def _rms_rows(x, gain):
    return x * lax.rsqrt(jnp.mean(x * x, axis=-1, keepdims=True) + NORM_EPS) * gain


def _cast_kernel(x_ref, o_ref):
    o_ref[...] = x_ref[...].astype(o_ref.dtype)


def cast_bf16(w, *, block_bytes=8 * 1024 * 1024):
    n = w.shape[-1]
    x = w.reshape(-1, n)
    rows = x.shape[0]
    tr = rows
    while tr * n * 4 > block_bytes and tr % 32 == 0:
        tr //= 2
    out = pl.pallas_call(
        _cast_kernel,
        grid=(rows // tr,),
        in_specs=[pl.BlockSpec((tr, n), lambda i: (i, 0))],
        out_specs=pl.BlockSpec((tr, n), lambda i: (i, 0)),
        out_shape=jax.ShapeDtypeStruct((rows, n), BF16),
        compiler_params=_cparams("parallel"),
        name="cast_bf16",
    )(x)
    return out.reshape(w.shape)


def _norm_matmul_kernel(x_ref, g_ref, w_ref, o_ref, xn_ref):
    @pl.when(pl.program_id(1) == 0)
    def _():
        xn_ref[...] = _rms_rows(x_ref[...], g_ref[...]).astype(BF16)

    o_ref[...] = jnp.dot(xn_ref[...], w_ref[...], preferred_element_type=F32).astype(o_ref.dtype)


def norm_matmul(x, gain, w, *, tm=1024, tn=512, out_dtype=F32):
    m, k = x.shape
    n = w.shape[1]
    tm = min(tm, m)
    return pl.pallas_call(
        _norm_matmul_kernel,
        grid=(m // tm, n // tn),
        in_specs=[pl.BlockSpec((tm, k), lambda i, j: (i, 0)),
                  pl.BlockSpec((1, k), lambda i, j: (0, 0)),
                  pl.BlockSpec((k, tn), lambda i, j: (0, j))],
        out_specs=pl.BlockSpec((tm, tn), lambda i, j: (i, j)),
        out_shape=jax.ShapeDtypeStruct((m, n), out_dtype),
        scratch_shapes=[pltpu.VMEM((tm, k), BF16)],
        compiler_params=_cparams("parallel", "arbitrary"),
        name="norm_matmul",
    )(x, gain.reshape(1, k), w)


def _matmul_res_kernel(x_ref, w_ref, r_ref, o_ref):
    o_ref[...] = r_ref[...] + jnp.dot(x_ref[...], w_ref[...], preferred_element_type=F32)


def matmul_residual(x, w, res, *, tm=1024, tn=512):
    m, k = x.shape
    n = w.shape[1]
    return pl.pallas_call(
        _matmul_res_kernel,
        grid=(m // tm, n // tn),
        in_specs=[pl.BlockSpec((tm, k), lambda i, j: (i, 0)),
                  pl.BlockSpec((k, tn), lambda i, j: (0, j)),
                  pl.BlockSpec((tm, tn), lambda i, j: (i, j))],
        out_specs=pl.BlockSpec((tm, tn), lambda i, j: (i, j)),
        out_shape=jax.ShapeDtypeStruct((m, n), F32),
        compiler_params=_cparams("parallel", "arbitrary"),
        name="matmul_residual",
    )(x, w, res)


def _matmul2_res_kernel(x1_ref, x2_ref, w1_ref, w2_ref, r_ref, o_ref):
    o_ref[...] = (r_ref[...] + jnp.dot(x1_ref[...], w1_ref[...], preferred_element_type=F32)
                  + jnp.dot(x2_ref[...], w2_ref[...], preferred_element_type=F32))


def matmul2_residual(x1, x2, w, res, *, tm=1024, tn=512):
    m, k1 = x1.shape
    n = w.shape[1]
    return pl.pallas_call(
        _matmul2_res_kernel,
        grid=(m // tm, n // tn),
        in_specs=[pl.BlockSpec((tm, k1), lambda i, j: (i, 0)),
                  pl.BlockSpec((tm, k1), lambda i, j: (i, 0)),
                  pl.BlockSpec((k1, tn), lambda i, j: (0, j)),
                  pl.BlockSpec((k1, tn), lambda i, j: (1, j)),
                  pl.BlockSpec((tm, tn), lambda i, j: (i, j))],
        out_specs=pl.BlockSpec((tm, tn), lambda i, j: (i, j)),
        out_shape=jax.ShapeDtypeStruct((m, n), F32),
        compiler_params=_cparams("parallel", "arbitrary"),
        name="matmul2_residual",
    )(x1, x2, w, w, res)


def _ffn_kernel(x_ref, g_ref, wg_ref, wu_ref, wo_ref, o_ref, xn_ref, acc_ref):
    j = pl.program_id(1)

    @pl.when(j == 0)
    def _():
        x = x_ref[...]
        xn_ref[...] = _rms_rows(x, g_ref[...]).astype(BF16)
        acc_ref[...] = x

    xn = xn_ref[...]
    gate = jnp.dot(xn, wg_ref[...], preferred_element_type=F32)
    up = jnp.dot(xn, wu_ref[...], preferred_element_type=F32)
    act = (gate * jax.nn.sigmoid(gate) * up).astype(BF16)
    acc_ref[...] += jnp.dot(act, wo_ref[...], preferred_element_type=F32)

    @pl.when(j == pl.num_programs(1) - 1)
    def _():
        o_ref[...] = acc_ref[...]


def ffn_residual(h, gain, w_in, w_out, *, tm=512, th=512):
    m, d = h.shape
    hidden = w_out.shape[0]
    nh = hidden // th
    return pl.pallas_call(
        _ffn_kernel,
        grid=(m // tm, nh),
        in_specs=[pl.BlockSpec((tm, d), lambda i, j: (i, 0)),
                  pl.BlockSpec((1, d), lambda i, j: (0, 0)),
                  pl.BlockSpec((d, th), lambda i, j: (0, j)),
                  pl.BlockSpec((d, th), lambda i, j: (0, j + nh)),
                  pl.BlockSpec((th, d), lambda i, j: (j, 0))],
        out_specs=pl.BlockSpec((tm, d), lambda i, j: (i, 0)),
        out_shape=jax.ShapeDtypeStruct((m, d), F32),
        scratch_shapes=[pltpu.VMEM((tm, d), BF16), pltpu.VMEM((tm, d), F32)],
        compiler_params=_cparams("parallel", "arbitrary"),
        name="ffn_residual",
    )(h, gain.reshape(1, d), w_in, w_in, w_out)


def _xattn_kernel(h_ref, g_ref, wq_ref, qg_ref, k_ref, v_ref, wo_ref, o_ref):
    x = h_ref[0]
    xn = _rms_rows(x, g_ref[...]).astype(BF16)
    q = jnp.dot(xn, wq_ref[...], preferred_element_type=F32)
    k = k_ref[0]
    v = v_ref[0]
    outs = []
    for hh in range(XA_HEADS):
        sl = slice(hh * XA_HEAD_DIM, (hh + 1) * XA_HEAD_DIM)
        qh = _rms_rows(q[:, sl], qg_ref[...]).astype(BF16)
        s = lax.dot_general(qh, k[:, sl], (((1,), (1,)), ((), ())), preferred_element_type=F32)
        s = s * (XA_HEAD_DIM ** -0.5)
        s = s - jnp.max(s, axis=-1, keepdims=True)
        p = jnp.exp(s)
        p = p / jnp.sum(p, axis=-1, keepdims=True)
        outs.append(jnp.dot(p.astype(BF16), v[:, sl], preferred_element_type=F32))
    o = jnp.concatenate(outs, axis=-1).astype(BF16)
    o_ref[0] = x + jnp.dot(o, wo_ref[...], preferred_element_type=F32)


def xattn_residual(h, gain, wq, q_gain, k, v, wo, *, tm=512):
    b, s, d = h.shape
    mt = k.shape[1]
    return pl.pallas_call(
        _xattn_kernel,
        grid=(b, s // tm),
        in_specs=[pl.BlockSpec((1, tm, d), lambda bi, i: (bi, i, 0)),
                  pl.BlockSpec((1, d), lambda bi, i: (0, 0)),
                  pl.BlockSpec((d, XA_WIDTH), lambda bi, i: (0, 0)),
                  pl.BlockSpec((1, XA_HEAD_DIM), lambda bi, i: (0, 0)),
                  pl.BlockSpec((1, mt, XA_WIDTH), lambda bi, i: (bi, 0, 0)),
                  pl.BlockSpec((1, mt, XA_WIDTH), lambda bi, i: (bi, 0, 0)),
                  pl.BlockSpec((XA_WIDTH, d), lambda bi, i: (0, 0))],
        out_specs=pl.BlockSpec((1, tm, d), lambda bi, i: (bi, i, 0)),
        out_shape=jax.ShapeDtypeStruct((b, s, d), F32),
        compiler_params=_cparams("parallel", "parallel"),
        name="xattn_residual",
    )(h, gain.reshape(1, d), wq, q_gain.reshape(1, XA_HEAD_DIM), k, v, wo)


def rms_norm(x, gain, eps=NORM_EPS):
    x32 = x.astype(F32)
    y = x32 * lax.rsqrt(jnp.mean(x32 * x32, axis=-1, keepdims=True) + eps)
    return (y * gain.astype(F32)).astype(x.dtype)


def l2_norm(x, eps=NORM_EPS):
    x32 = x.astype(F32)
    return x32 * lax.rsqrt(jnp.sum(x32 * x32, axis=-1, keepdims=True) + eps)


def masked_softmax(logits, mask):
    p = jax.nn.softmax(jnp.where(mask, logits.astype(F32), NEG_INF), axis=-1)
    return jnp.where(mask, p, 0.0)


def rel_bucket(dist):
    dist = jnp.maximum(dist, 0)
    max_exact = REL_BUCKETS // 2
    scaled = (jnp.log(jnp.maximum(dist, max_exact).astype(F32) / max_exact)
              / math.log(REL_MAX_DIST / max_exact) * (REL_BUCKETS - max_exact))
    large = jnp.minimum(max_exact + scaled.astype(jnp.int32), REL_BUCKETS - 1)
    return jnp.where(dist < max_exact, dist, large)


def rotary(x, pos):
    half = x.shape[-1] // 2
    inv = RET_ROPE_BASE ** (-jnp.arange(half, dtype=F32) / half)
    ang = pos.astype(F32)[:, None] * inv[None, :]
    cos, sin = jnp.cos(ang)[None, :, None, :], jnp.sin(ang)[None, :, None, :]
    x1, x2 = x[..., :half], x[..., half:]
    return jnp.concatenate([x1 * cos - x2 * sin, x1 * sin + x2 * cos], axis=-1)


def causal_depthwise_conv(x, w):
    k, ch = w.shape
    return lax.conv_general_dilated(x, w[:, None, :].astype(x.dtype), window_strides=(1,),
                                    padding=[(k - 1, 0)], dimension_numbers=('NWC', 'WIO', 'NWC'),
                                    feature_group_count=ch)


def compress_blocks(t, pos, w1, w2):
    b, s, g, dh = t.shape
    r = NSA_CMP_LEN // NSA_CMP_STRIDE
    n_chunks = s // NSA_CMP_STRIDE
    n_cmp = n_chunks - r + 1
    c = t.reshape(b, n_chunks, NSA_CMP_STRIDE, g, dh)
    blocks = jnp.concatenate([c[:, j:j + n_cmp] for j in range(r)], axis=2)
    blocks = blocks + pos[None, None, :, None, :].astype(t.dtype)
    flat = blocks.transpose(0, 3, 1, 2, 4).reshape(b, g, n_cmp, NSA_CMP_LEN * dh)
    return jax.nn.silu(flat @ w1) @ w2


NSA_QT = 128
NSA_KT = 512
NSA_WT = NSA_WINDOW + NSA_QT


def _dot_nt(a, b):
    return lax.dot_general(a, b, (((1,), (1,)), ((), ())), preferred_element_type=F32)


def _dot_tn(a, b):
    return lax.dot_general(a, b, (((0,), (0,)), ((), ())), preferred_element_type=F32)


def _bias_tile(tb_ref, h, first_slab, n_slabs):
    return jnp.concatenate([tb_ref[h, jnp.maximum(first_slab - j, 0)] for j in range(n_slabs)], axis=1)


def _nsa_kernel(q_ref, gl_ref, kc_ref, vc_ref, bc_ref, covt_ref, ks_ref, vs_ref, kw_ref, vw_ref,
                tb_ref, e_ref, qg_ref, o_ref, a_scr, *, n_top):
    QT, KT, WT, HG, dh = NSA_QT, NSA_KT, NSA_WT, NSA_GROUP_HEADS, NSA_HEAD_DIM
    nsb, nc = covt_ref.shape
    q0 = pl.program_id(2) * QT

    x = q_ref[0]
    qs = []
    for h in range(HG):
        xh = _rms_rows(x[:, h * dh:(h + 1) * dh], qg_ref[...]) * (dh ** -0.5)
        qs.append(xh.astype(BF16))
    qn = jnp.concatenate(qs, axis=0)

    sc = _dot_nt(qn, kc_ref[0, 0]) + bc_ref[...].reshape(HG * QT, nc)
    row = lax.broadcasted_iota(jnp.int32, (HG * QT, nc), 0)
    col = lax.broadcasted_iota(jnp.int32, (HG * QT, nc), 1)
    tq = q0 + (row & (QT - 1))
    mask_c = tq >= col * NSA_CMP_STRIDE + (NSA_CMP_LEN - 1)
    sc = jnp.where(mask_c, sc, NEG_INF)
    pc = jnp.where(mask_c, jnp.exp(sc - jnp.max(sc, axis=-1, keepdims=True)), 0.0)
    den = jnp.sum(pc, axis=-1, keepdims=True)
    pcb = (pc / jnp.maximum(den, 1e-30)).astype(BF16)
    o_c = jnp.dot(pcb, vc_ref[0, 0], preferred_element_type=F32)

    imp_all = _dot_nt(covt_ref[...], pcb)
    imp = imp_all[:, 0:QT]
    for h in range(1, HG):
        imp = imp + imp_all[:, h * QT:(h + 1) * QT]
    jj = lax.broadcasted_iota(jnp.int32, (nsb, QT), 0)
    cur = (q0 + lax.broadcasted_iota(jnp.int32, (nsb, QT), 1)) // NSA_SEL_BLOCK
    forced = (jj == 0) | (jj == cur) | (jj == cur - 1)
    a = jnp.where(forced, NSA_FORCE_SCORE, imp)
    a = jnp.where(jj > cur, -NSA_FORCE_SCORE, a)
    a_scr[...] = a

    def rank_body(i, rank):
        r = a_scr[pl.ds(i, 1), :]
        return rank + jnp.where((r > a) | ((r == a) & (jj > i)), 1.0, 0.0)

    n_live = jnp.minimum((q0 + QT - 1) // NSA_SEL_BLOCK + 1, nsb)
    rank = lax.fori_loop(0, n_live, rank_body, jnp.zeros((nsb, QT), F32))
    sel_t = jnp.where(rank < n_top, 1.0, 0.0)
    if nsb < 128:
        sel_t = jnp.concatenate([sel_t, jnp.zeros((128 - nsb, QT), F32)], axis=0)
    sel = sel_t.T.astype(BF16)

    cmr = (lax.broadcasted_iota(jnp.int32, (QT, KT), 1) - lax.broadcasted_iota(jnp.int32, (QT, KT), 0))

    def sel_body(kt, carry):
        m_i, l_i, acc = carry
        k0 = pl.multiple_of(kt * KT, KT)
        k = ks_ref[0, pl.ds(k0, KT), :]
        v = vs_ref[0, pl.ds(k0, KT), :]
        s = _dot_nt(qn, k)
        delta = q0 - k0
        visible = (jnp.dot(sel, e_ref[kt], preferred_element_type=F32) > 0.5) & (cmr <= delta)
        seg = delta // QT
        parts = []
        for h in range(HG):
            bias = _bias_tile(tb_ref, h, seg, KT // QT)
            parts.append(jnp.where(visible, s[h * QT:(h + 1) * QT] + bias, NEG_INF))
        s = jnp.concatenate(parts, axis=0)
        m_new = jnp.maximum(m_i, jnp.max(s, axis=-1, keepdims=True))
        alpha = jnp.exp(m_i - m_new)
        p = jnp.exp(s - m_new)
        l_new = alpha * l_i + jnp.sum(p, axis=-1, keepdims=True)
        acc = alpha * acc + jnp.dot(p.astype(BF16), v, preferred_element_type=F32)
        return m_new, l_new, acc

    init = (jnp.full((HG * QT, 1), NEG_INF, F32), jnp.zeros((HG * QT, 1), F32), jnp.zeros((HG * QT, dh), F32))
    _, l_s, acc_s = lax.fori_loop(0, q0 // KT + 1, sel_body, init)
    o_s = acc_s / l_s

    w0 = pl.multiple_of(q0, QT)
    kwin = kw_ref[0, pl.ds(w0, WT), :]
    vwin = vw_ref[0, pl.ds(w0, WT), :]
    sw = _dot_nt(qn, kwin)
    cw = lax.broadcasted_iota(jnp.int32, (QT, WT), 1)
    dist = NSA_WINDOW + lax.broadcasted_iota(jnp.int32, (QT, WT), 0) - cw
    vis_w = (dist >= 0) & (dist < NSA_WINDOW) & (cw >= NSA_WINDOW - q0)
    parts = []
    for h in range(HG):
        bias = _bias_tile(tb_ref, h, NSA_WINDOW // QT, WT // QT)
        parts.append(jnp.where(vis_w, sw[h * QT:(h + 1) * QT] + bias, NEG_INF))
    sw = jnp.concatenate(parts, axis=0)
    pw = jnp.exp(sw - jnp.max(sw, axis=-1, keepdims=True))
    pw = pw / jnp.sum(pw, axis=-1, keepdims=True)
    o_w = jnp.dot(pw.astype(BF16), vwin, preferred_element_type=F32)

    gates = jax.nn.sigmoid(gl_ref[0][:, :3 * HG])
    outs = []
    for h in range(HG):
        rs = slice(h * QT, (h + 1) * QT)
        outs.append(gates[:, 3 * h:3 * h + 1] * o_c[rs] + gates[:, 3 * h + 1:3 * h + 2] * o_s[rs]
                    + gates[:, 3 * h + 2:3 * h + 3] * o_w[rs])
    o_ref[0] = jnp.concatenate(outs, axis=-1).astype(o_ref.dtype)


def _rel_table(rel_bias, s):
    return rel_bias[rel_bucket(jnp.arange(s))].T.astype(F32)


def nsa_attention_pallas(proj, gl_blk, kc, vc, ks, vs, kw, vw, q_gain, rel_bias):
    B, S, _ = proj.shape
    G, HG, dh = NSA_KV_GROUPS, NSA_GROUP_HEADS, NSA_HEAD_DIM
    QT, KT, WT, W, SB = NSA_QT, NSA_KT, NSA_WT, NSA_WINDOW, NSA_SEL_BLOCK
    nsb, nc, nkt, nseg = S // SB, S // NSA_CMP_STRIDE, S // KT, max(S // QT, W // QT + 1)
    n_top = min(NSA_N_SELECT, nsb)

    def bias_of(dist):
        onehot = (rel_bucket(dist)[..., None] == jnp.arange(REL_BUCKETS)) & (dist >= 0)[..., None]
        return jnp.einsum('...b,bh->h...', onehot.astype(F32), rel_bias.astype(F32),
                          precision=lax.Precision.HIGHEST)

    bc = bias_of(jnp.arange(S)[:, None] - (jnp.arange(nc) * NSA_CMP_STRIDE + NSA_CMP_LEN - 1)[None, :])
    tb = bias_of(QT * jnp.arange(nseg)[:, None, None] + jnp.arange(QT)[None, :, None] - jnp.arange(QT)[None, None, :])
    cmp_start = np.arange(nc) * NSA_CMP_STRIDE
    cmp_end = cmp_start + NSA_CMP_LEN - 1
    sel_start = np.arange(nsb) * SB
    cover_t = ((cmp_start[None, :] < sel_start[:, None] + SB) & (cmp_end[None, :] >= sel_start[:, None])
               & (np.arange(nc)[None, :] < nc - 1))
    cover_t = jnp.asarray(cover_t, BF16)
    e = (np.arange(128)[None, :, None]
         == (np.arange(nkt)[:, None, None] * (KT // SB) + np.arange(KT)[None, None, :] // SB))
    e = jnp.asarray(e, BF16)
    pad = ((0, 0), (W, 0), (0, 0))
    kw = jnp.pad(kw, pad)
    vw = jnp.pad(vw, pad)

    kern = functools.partial(_nsa_kernel, n_top=n_top)
    return pl.pallas_call(
        kern,
        grid=(B, G, S // QT),
        in_specs=[
            pl.BlockSpec((1, QT, HG * dh), lambda b, g, i: (b, i, g)),
            pl.BlockSpec((1, QT, 128), lambda b, g, i: (b, i, gl_blk + g)),
            pl.BlockSpec((1, 1, nc, dh), lambda b, g, i: (b, g, 0, 0)),
            pl.BlockSpec((1, 1, nc, dh), lambda b, g, i: (b, g, 0, 0)),
            pl.BlockSpec((HG, QT, nc), lambda b, g, i: (g, i, 0)),
            pl.BlockSpec((nsb, nc), lambda b, g, i: (0, 0)),
            pl.BlockSpec((1, S, dh), lambda b, g, i: (b, 0, g)),
            pl.BlockSpec((1, S, dh), lambda b, g, i: (b, 0, g)),
            pl.BlockSpec((1, S + W, dh), lambda b, g, i: (b, 0, g)),
            pl.BlockSpec((1, S + W, dh), lambda b, g, i: (b, 0, g)),
            pl.BlockSpec((HG, nseg, QT, QT), lambda b, g, i: (g, 0, 0, 0)),
            pl.BlockSpec((nkt, 128, KT), lambda b, g, i: (0, 0, 0)),
            pl.BlockSpec((1, dh), lambda b, g, i: (0, 0)),
        ],
        out_specs=pl.BlockSpec((1, QT, HG * dh), lambda b, g, i: (b, i, g)),
        out_shape=jax.ShapeDtypeStruct((B, S, NSA_WIDTH), BF16),
        scratch_shapes=[pltpu.VMEM((nsb, QT), F32)],
        compiler_params=_cparams("parallel", "parallel", "arbitrary"),
        name="nsa_attention",
    )(proj, proj, kc, vc, bc, cover_t, ks, vs, kw, vw, tb, e, q_gain.reshape(1, dh))


def nsa_kv_prep(kc, vc, ks, vs, kw, vw, k_gain, cmp_pos, cmp_w1, cmp_w2):
    B, S, _ = kc.shape
    kv = lambda t: t.reshape(B, S, NSA_KV_GROUPS, NSA_HEAD_DIM)
    kcc = rms_norm(compress_blocks(kv(kc), cmp_pos[0], cmp_w1[0], cmp_w2[0]), k_gain[0])
    vcc = compress_blocks(kv(vc), cmp_pos[1], cmp_w1[1], cmp_w2[1])
    padc = ((0, 0), (0, 0), (0, 1), (0, 0))
    kcc = jnp.pad(kcc, padc).astype(BF16)
    vcc = jnp.pad(vcc, padc).astype(BF16)
    ksn = rms_norm(kv(ks), k_gain[1]).reshape(B, S, -1).astype(BF16)
    kwn = rms_norm(kv(kw), k_gain[2]).reshape(B, S, -1).astype(BF16)
    return kcc, vcc, ksn, vs.astype(BF16), kwn, vw.astype(BF16)


RET_HB = 2


def _rotate_half(x, cos, sin):
    half = x.shape[-1] // 2
    x1, x2 = x[:, :half], x[:, half:]
    return jnp.concatenate([x1 * cos - x2 * sin, x1 * sin + x2 * cos], axis=-1)


def _retention_kernel(q_ref, k_ref, v_ref, g_ref, cos_ref, sin_ref, din_ref, qd_ref, kd_ref, cd_ref, gn_ref,
                      o_ref, s_scr):
    @pl.when(pl.program_id(2) == 0)
    def _():
        s_scr[...] = jnp.zeros_like(s_scr)

    cos, sin = cos_ref[...], sin_ref[...]
    heads = range(RET_HB)
    sl = [slice(h * RET_DK, (h + 1) * RET_DK) for h in heads]
    q = [_rotate_half(q_ref[0, :, sl[h]], cos, sin) * (RET_DK ** -0.5) for h in heads]
    k = [_rotate_half(k_ref[0, :, sl[h]], cos, sin) for h in heads]
    v16 = [v_ref[0, :, sl[h]].astype(BF16) for h in heads]
    inner = [_dot_nt(q[h].astype(BF16), k[h].astype(BF16)) * din_ref[h] for h in heads]
    s = [s_scr[h] for h in heads]
    o = [jnp.dot(inner[h].astype(BF16), v16[h], preferred_element_type=F32)
         + jnp.dot((q[h] * qd_ref[h]).astype(BF16), s[h].astype(BF16), preferred_element_type=F32) for h in heads]
    for h in heads:
        s_scr[h] = s[h] * cd_ref[h] + _dot_tn((k[h] * kd_ref[h]).astype(BF16), v16[h])
    outs = []
    for h in heads:
        mu = jnp.mean(o[h], axis=-1, keepdims=True)
        d = o[h] - mu
        var = jnp.mean(d * d, axis=-1, keepdims=True)
        gate = g_ref[0, :, sl[h]]
        outs.append((gate * jax.nn.sigmoid(gate)
                     * (d * lax.rsqrt(var + RET_GN_EPS) * gn_ref[:, sl[h]])).astype(o_ref.dtype))
    o_ref[0] = jnp.concatenate(outs, axis=-1)


def retention_pallas(proj, q_blk, gn_gain):
    B, S, _ = proj.shape
    H, dk, dv, C = RET_HEADS, RET_DK, RET_DV, RET_CHUNK
    N = S // C
    half = dk // 2
    inv = RET_ROPE_BASE ** (-jnp.arange(half, dtype=F32) / half)
    ang = jnp.arange(S).astype(F32)[:, None] * inv[None, :]
    cos, sin = jnp.cos(ang), jnp.sin(ang)
    log_gamma = jnp.log(1.0 - 2.0 ** (-5.0 - jnp.arange(H, dtype=F32)))
    idx = jnp.arange(C, dtype=F32)
    rel = idx[:, None] - idx[None, :]
    decay_in = jnp.where(rel >= 0, jnp.exp(log_gamma[:, None, None] * jnp.maximum(rel, 0.0)), 0.0)
    q_decay = jnp.exp(log_gamma[:, None] * (idx + 1.0))[..., None]
    k_decay = jnp.exp(log_gamma[:, None] * (C - 1.0 - idx))[..., None]
    chunk_decay = jnp.exp(log_gamma * C)[:, None, None]
    hb = RET_HB
    HP = H // hb
    qb0 = q_blk // hb
    return pl.pallas_call(
        _retention_kernel,
        grid=(B, HP, N),
        in_specs=[
            pl.BlockSpec((1, C, hb * dk), lambda b, p, n: (b, n, qb0 + p)),
            pl.BlockSpec((1, C, hb * dk), lambda b, p, n: (b, n, qb0 + HP + p)),
            pl.BlockSpec((1, C, hb * dv), lambda b, p, n: (b, n, qb0 + 2 * HP + p)),
            pl.BlockSpec((1, C, hb * dv), lambda b, p, n: (b, n, qb0 + 3 * HP + p)),
            pl.BlockSpec((C, half), lambda b, p, n: (n, 0)),
            pl.BlockSpec((C, half), lambda b, p, n: (n, 0)),
            pl.BlockSpec((hb, C, C), lambda b, p, n: (p, 0, 0)),
            pl.BlockSpec((hb, C, 1), lambda b, p, n: (p, 0, 0)),
            pl.BlockSpec((hb, C, 1), lambda b, p, n: (p, 0, 0)),
            pl.BlockSpec((hb, 1, 1), lambda b, p, n: (p, 0, 0)),
            pl.BlockSpec((1, hb * dv), lambda b, p, n: (0, p)),
        ],
        out_specs=pl.BlockSpec((1, C, hb * dv), lambda b, p, n: (b, n, p)),
        out_shape=jax.ShapeDtypeStruct((B, S, H * dv), BF16),
        scratch_shapes=[pltpu.VMEM((hb, dk, dv), F32)],
        compiler_params=_cparams("parallel", "parallel", "arbitrary"),
        name="retention",
    )(proj, proj, proj, proj, cos, sin, decay_in, q_decay, k_decay, chunk_decay, gn_gain.reshape(1, H * dv))


GDN_HB = 16
GDN_HALO = 8


def _conv_silu(x_ref, halo_ref, keep, w, stage_ref):
    c = x_ref.shape[1]
    stage_ref[0:GDN_HALO, :] = halo_ref[0] * keep
    stage_ref[GDN_HALO:GDN_HALO + c, :] = x_ref[0]
    y = w[GDN_CONV - 1:GDN_CONV] * x_ref[0]
    for j in range(GDN_CONV - 1):
        off = GDN_HALO - (GDN_CONV - 1) + j
        y = y + w[j:j + 1] * stage_ref[off:off + c, :]
    return y * jax.nn.sigmoid(y)


def _l2_rows(x):
    return x * lax.rsqrt(jnp.sum(x * x, axis=-1, keepdims=True) + NORM_EPS)


def _gdn_kernel(q_ref, k_ref, v_ref, qh_ref, kh_ref, vh_ref, wq_ref, wk_ref, wv_ref, z_ref,
                ba_ref, alog_ref, dtb_ref, ng_ref, o_ref, s_scr, qst_scr, kst_scr, vst_scr):
    C, dh, hb = GDN_CHUNK, GDN_HEAD_DIM, GDN_HB
    rep = GDN_V_HEADS // GDN_QK_HEADS
    first = pl.program_id(2) == 0

    @pl.when(first)
    def _():
        s_scr[...] = jnp.zeros_like(s_scr)

    keep = jnp.where(first, 0.0, 1.0)
    qc = _conv_silu(q_ref, qh_ref, keep, wq_ref[...], qst_scr)
    kc = _conv_silu(k_ref, kh_ref, keep, wk_ref[...], kst_scr)
    vc = _conv_silu(v_ref, vh_ref, keep, wv_ref[...], vst_scr)

    ri = lax.broadcasted_iota(jnp.int32, (C, C), 0)
    ci = lax.broadcasted_iota(jnp.int32, (C, C), 1)
    causal = ri >= ci
    strict = ri > ci
    ba = ba_ref[0]
    bcol = jax.nn.sigmoid(ba[:, :hb])
    g = -jnp.exp(alog_ref[0]) * jax.nn.softplus(ba[:, hb:2 * hb] + dtb_ref[0])
    gcol = jnp.dot(jnp.where(causal, 1.0, 0.0), g, preferred_element_type=F32, precision=lax.Precision.HIGHEST)
    grow = lax.dot_general(gcol, jnp.where(ri == ci, 1.0, 0.0), (((0,), (0,)), ((), ())),
                           preferred_element_type=F32, precision=lax.Precision.HIGHEST)
    heads = range(hb)
    qs, ks, grams = [], [], []
    for hq in range(hb // rep):
        qh = _l2_rows(qc[:, hq * dh:(hq + 1) * dh]) * (dh ** -0.5)
        kh = _l2_rows(kc[:, hq * dh:(hq + 1) * dh])
        k16 = kh.astype(BF16)
        qs.append(qh)
        ks.append(kh)
        grams.append(_dot_nt(jnp.concatenate([qh.astype(BF16), k16], axis=0), k16))
    beta = [bcol[:, h:h + 1] for h in heads]
    gc = [gcol[:, h:h + 1] for h in heads]
    gr = [grow[h:h + 1, :] for h in heads]
    g_last = [gr[h][:, C - 1:C] for h in heads]
    eg = [jnp.exp(gc[h]) for h in heads]
    decay = [jnp.where(causal, jnp.exp(jnp.minimum(gc[h] - gr[h], 0.0)), 0.0) for h in heads]
    attn = [(grams[h // rep][:C] * decay[h]).astype(BF16) for h in heads]
    nm = [jnp.where(strict, grams[h // rep][C:] * decay[h] * (-beta[h]), 0.0) for h in heads]
    m = [nm[h].astype(BF16) for h in heads]
    for _ in range(int(math.log2(C)) - 1):
        mf = [jnp.dot(m[h], m[h], preferred_element_type=F32) for h in heads]
        m = [mf[h].astype(BF16) for h in heads]
        nm = [nm[h] + mf[h] + jnp.dot(m[h], nm[h].astype(BF16), preferred_element_type=F32) for h in heads]
    x = [jnp.concatenate([vc[:, h * dh:(h + 1) * dh] * beta[h], ks[h // rep] * (beta[h] * eg[h])], axis=1)
         for h in heads]
    x = [x[h] + jnp.dot(nm[h].astype(BF16), x[h].astype(BF16), preferred_element_type=F32) for h in heads]
    s = [s_scr[h] for h in heads]
    ws = [jnp.dot(jnp.concatenate([x[h][:, dh:].astype(BF16), (qs[h // rep] * eg[h]).astype(BF16)], axis=0),
                  s[h].astype(BF16), preferred_element_type=F32) for h in heads]
    vn = [(x[h][:, :dh] - ws[h][:C]).astype(BF16) for h in heads]
    o = [ws[h][C:] + jnp.dot(attn[h], vn[h], preferred_element_type=F32) for h in heads]
    for h in heads:
        k_dec = (ks[h // rep] * jnp.exp(g_last[h] - gc[h])).astype(BF16)
        s_scr[h] = s[h] * jnp.exp(g_last[h]) + _dot_tn(k_dec, vn[h])
    outs = []
    for h in heads:
        zh = z_ref[0, :, h * dh:(h + 1) * dh]
        outs.append((_rms_rows(o[h], ng_ref[...]) * (zh * jax.nn.sigmoid(zh))).astype(o_ref.dtype))
    o_ref[0] = jnp.concatenate(outs, axis=1)


_GDN_GATE_COL0 = GDN_CONV_CH + GDN_V_WIDTH
_GDN_COLS = _round_up(_GDN_GATE_COL0 + GDN_V_HEADS // GDN_HB * 128, 512)


def _gdn_w_in_layout(w_in):
    b0 = _GDN_GATE_COL0
    a0 = b0 + GDN_V_HEADS
    parts = [w_in[:, :b0]]
    for p in range(GDN_V_HEADS // GDN_HB):
        parts.append(w_in[:, b0 + p * GDN_HB:b0 + (p + 1) * GDN_HB])
        parts.append(jnp.pad(w_in[:, a0 + p * GDN_HB:a0 + (p + 1) * GDN_HB], ((0, 0), (0, 128 - 2 * GDN_HB))))
    w = jnp.concatenate(parts, axis=1)
    return jnp.pad(w, ((0, 0), (0, _GDN_COLS - w.shape[1]))).astype(BF16)


def gdn_delta_rule(proj, conv_w, a_log, dt_bias, norm_gain):
    B, S, _ = proj.shape
    C, dh, hb, H = GDN_CHUNK, GDN_HEAD_DIM, GDN_HB, GDN_V_HEADS
    rep = GDN_V_HEADS // GDN_QK_HEADS
    N, HP = S // C, H // hb
    wqk, wv = hb // rep * dh, hb * dh
    kb0, vb0, zb0 = GDN_QK_WIDTH // wqk, 2 * GDN_QK_WIDTH // wv, GDN_CONV_CH // wv
    gb0 = _GDN_GATE_COL0 // 128
    hr = C // GDN_HALO
    halo = lambda n: jnp.maximum(n * hr - 1, 0)
    return pl.pallas_call(
        _gdn_kernel,
        grid=(B, HP, N),
        in_specs=[
            pl.BlockSpec((1, C, wqk), lambda b, p, n: (b, n, p)),
            pl.BlockSpec((1, C, wqk), lambda b, p, n: (b, n, kb0 + p)),
            pl.BlockSpec((1, C, wv), lambda b, p, n: (b, n, vb0 + p)),
            pl.BlockSpec((1, GDN_HALO, wqk), lambda b, p, n: (b, halo(n), p)),
            pl.BlockSpec((1, GDN_HALO, wqk), lambda b, p, n: (b, halo(n), kb0 + p)),
            pl.BlockSpec((1, GDN_HALO, wv), lambda b, p, n: (b, halo(n), vb0 + p)),
            pl.BlockSpec((GDN_CONV, wqk), lambda b, p, n: (0, p)),
            pl.BlockSpec((GDN_CONV, wqk), lambda b, p, n: (0, kb0 + p)),
            pl.BlockSpec((GDN_CONV, wv), lambda b, p, n: (0, vb0 + p)),
            pl.BlockSpec((1, C, wv), lambda b, p, n: (b, n, zb0 + p)),
            pl.BlockSpec((1, C, 128), lambda b, p, n: (b, n, gb0 + p)),
            pl.BlockSpec((1, 1, hb), lambda b, p, n: (p, 0, 0)),
            pl.BlockSpec((1, 1, hb), lambda b, p, n: (p, 0, 0)),
            pl.BlockSpec((1, dh), lambda b, p, n: (0, 0)),
        ],
        out_specs=pl.BlockSpec((1, C, wv), lambda b, p, n: (b, n, p)),
        out_shape=jax.ShapeDtypeStruct((B, S, H * dh), BF16),
        scratch_shapes=[pltpu.VMEM((hb, dh, dh), F32), pltpu.VMEM((GDN_HALO + C, wqk), F32),
                        pltpu.VMEM((GDN_HALO + C, wqk), F32), pltpu.VMEM((GDN_HALO + C, wv), F32)],
        compiler_params=_cparams("parallel", "parallel", "arbitrary"),
        name="gdn_delta_rule",
    )(proj, proj, proj, proj, proj, proj, conv_w, conv_w, conv_w, proj, proj,
      a_log.astype(F32).reshape(HP, 1, hb), dt_bias.astype(F32).reshape(HP, 1, hb), norm_gain.reshape(1, dh))


def _pad_cols(w, mult):
    n = w.shape[1]
    return jnp.pad(w, ((0, 0), (0, _round_up(n, mult) - n)))


_HYB_NSA_COLS = NSA_WIDTH + 6 * NSA_KV_WIDTH
_HYB_RET_COLS = 2 * RET_HEADS * RET_DK + 2 * RET_HEADS * RET_DV
_HYB_GATE_COL0 = _HYB_NSA_COLS + _HYB_RET_COLS
_HYB_COLS = _round_up(_HYB_GATE_COL0 + NSA_KV_GROUPS * 128, 512)


def _hybrid_w_in_layout(w_in):
    gate0 = _HYB_NSA_COLS
    ret0 = gate0 + 3 * NSA_HEADS
    per_group = 3 * NSA_GROUP_HEADS
    parts = [w_in[:, :gate0], w_in[:, ret0:ret0 + _HYB_RET_COLS]]
    for g in range(NSA_KV_GROUPS):
        parts.append(jnp.pad(w_in[:, gate0 + g * per_group:gate0 + (g + 1) * per_group],
                             ((0, 0), (0, 128 - per_group))))
    w = jnp.concatenate(parts, axis=1)
    return jnp.pad(w, ((0, 0), (0, _HYB_COLS - w.shape[1]))).astype(BF16)


def hybrid_mixer(h, ln_gain, w_in, w_out, q_gain, k_gain, cmp_pos, cmp_w1, cmp_w2, gn_gain, rel_bias):
    B, S, D = h.shape
    proj = norm_matmul(h.reshape(B * S, D), ln_gain, _hybrid_w_in_layout(w_in)).reshape(B, S, _HYB_COLS)
    kvw = NSA_KV_WIDTH
    kc, vc, ks, vs, kw, vw = (proj[..., NSA_WIDTH + i * kvw:NSA_WIDTH + (i + 1) * kvw] for i in range(6))
    kcc, vcc, ksn, vsb, kwn, vwb = nsa_kv_prep(kc, vc, ks, vs, kw, vw, k_gain, cmp_pos, cmp_w1, cmp_w2)
    a_out = nsa_attention_pallas(proj, _HYB_GATE_COL0 // 128, kcc, vcc, ksn, vsb, kwn, vwb, q_gain, rel_bias)
    b_out = retention_pallas(proj, _HYB_NSA_COLS // RET_DK, gn_gain)
    w_out = w_out.astype(BF16)
    return matmul2_residual(a_out.reshape(B * S, -1), b_out.reshape(B * S, -1), w_out,
                            h.reshape(B * S, D)).reshape(B, S, D)


def gdn_mixer(h, ln_gain, w_in, conv_w, a_log, dt_bias, norm_gain, w_out):
    B, S, D = h.shape
    proj = norm_matmul(h.reshape(B * S, D), ln_gain, _gdn_w_in_layout(w_in)).reshape(B, S, _GDN_COLS)
    o = gdn_delta_rule(proj, conv_w, a_log, dt_bias, norm_gain)
    return matmul_residual(o.reshape(B * S, GDN_V_WIDTH), w_out.astype(BF16), h.reshape(B * S, D)).reshape(B, S, D)


def memory_kv(mem, mem_gain, wkv, k_gain):
    B, M, D = mem.shape
    kv = norm_matmul(mem.reshape(B * M, D), mem_gain, wkv.astype(BF16))
    k, v = jnp.split(kv.reshape(B, M, 2 * XA_WIDTH), 2, axis=-1)
    k = rms_norm(k.reshape(B, M, XA_HEADS, XA_HEAD_DIM), k_gain).reshape(B, M, XA_WIDTH)
    return k.astype(BF16), v.astype(BF16)


def kernel(x, mem, rel_bias, ln_mix, ln_mem, ln_ffn, hyb_w_in, hyb_w_out, nsa_q_gain, nsa_k_gain,
           nsa_cmp_pos, nsa_cmp_w1, nsa_cmp_w2, ret_gn_gain, gdn_w_in, gdn_conv_w, gdn_a_log,
           gdn_dt_bias, gdn_norm_gain, gdn_w_out, xa_wq, xa_wkv, xa_q_gain, xa_k_gain, xa_mem_gain,
           xa_wo, ffn_w_in, ffn_w_out):
    B, S, D = x.shape
    h = x
    ffn_w_in, ffn_w_out, gdn_w_out = cast_bf16(ffn_w_in), cast_bf16(ffn_w_out), cast_bf16(gdn_w_out)
    for layer in range(DEPTH):
        if layer % 2 == 0:
            e = layer // 2
            h = hybrid_mixer(h, ln_mix[layer], hyb_w_in[e], hyb_w_out[e], nsa_q_gain[e], nsa_k_gain[e],
                             nsa_cmp_pos[e], nsa_cmp_w1[e], nsa_cmp_w2[e], ret_gn_gain[e], rel_bias)
        else:
            o = layer // 2
            h = gdn_mixer(h, ln_mix[layer], gdn_w_in[o], gdn_conv_w[o], gdn_a_log[o], gdn_dt_bias[o],
                          gdn_norm_gain[o], gdn_w_out[o])
        k_mem, v_mem = memory_kv(mem, xa_mem_gain[layer], xa_wkv[layer], xa_k_gain[layer])
        h = xattn_residual(h, ln_mem[layer], xa_wq[layer].astype(BF16), xa_q_gain[layer], k_mem, v_mem,
                           xa_wo[layer].astype(BF16))
        h = ffn_residual(h.reshape(B * S, D), ln_ffn[layer], ffn_w_in[layer].astype(BF16),
                         ffn_w_out[layer].astype(BF16)).reshape(B, S, D)
    return h
```

```python
import functools
import math

import jax
import jax.numpy as jnp
import numpy as np
from jax import lax
from jax.experimental import pallas as pl
from jax.experimental.pallas import tpu as pltpu

F32 = jnp.float32
BF16 = jnp.bfloat16

D_MODEL = 2048
DEPTH = 4
NORM_EPS = 1e-6
NEG_INF = -1e30

NSA_HEADS = 8
NSA_KV_GROUPS = 2
NSA_GROUP_HEADS = NSA_HEADS // NSA_KV_GROUPS
NSA_HEAD_DIM = 128
NSA_CMP_LEN = 32
NSA_CMP_STRIDE = 16
NSA_SEL_BLOCK = 64
NSA_N_SELECT = 16
NSA_WINDOW = 512
NSA_Q_BLOCK = 64
NSA_FORCE_SCORE = 1e6
NSA_WIDTH = NSA_HEADS * NSA_HEAD_DIM
NSA_KV_WIDTH = NSA_KV_GROUPS * NSA_HEAD_DIM

RET_HEADS = 4
RET_DK = 256
RET_DV = 256
RET_CHUNK = 128
RET_ROPE_BASE = 10000.0
RET_GN_EPS = 1e-5

GDN_QK_HEADS = 16
GDN_V_HEADS = 32
GDN_HEAD_DIM = 128
GDN_CONV = 4
GDN_CHUNK = 64
GDN_QK_WIDTH = GDN_QK_HEADS * GDN_HEAD_DIM
GDN_V_WIDTH = GDN_V_HEADS * GDN_HEAD_DIM
GDN_CONV_CH = 2 * GDN_QK_WIDTH + GDN_V_WIDTH

REL_BUCKETS = 32
REL_MAX_DIST = 1024

XA_HEADS = 4
XA_HEAD_DIM = 128
XA_WIDTH = XA_HEADS * XA_HEAD_DIM

V7X_VMEM_LIMIT_BYTES = 56 * 1024 * 1024


def _cparams(*sem):
    return pltpu.CompilerParams(dimension_semantics=sem, vmem_limit_bytes=V7X_VMEM_LIMIT_BYTES)


def _round_up(n, m):
    return -(-n // m) * m


def _rms_rows(x, gain):
    return x * lax.rsqrt(jnp.mean(x * x, axis=-1, keepdims=True) + NORM_EPS) * gain


def _cast_kernel(x_ref, o_ref):
    o_ref[...] = x_ref[...].astype(o_ref.dtype)


def cast_bf16(w, *, block_bytes=8 * 1024 * 1024):
    n = w.shape[-1]
    x = w.reshape(-1, n)
    rows = x.shape[0]
    tr = rows
    while tr * n * 4 > block_bytes and tr % 32 == 0:
        tr //= 2
    out = pl.pallas_call(
        _cast_kernel,
        grid=(rows // tr,),
        in_specs=[pl.BlockSpec((tr, n), lambda i: (i, 0))],
        out_specs=pl.BlockSpec((tr, n), lambda i: (i, 0)),
        out_shape=jax.ShapeDtypeStruct((rows, n), BF16),
        compiler_params=_cparams("parallel"),
        name="cast_bf16",
    )(x)
    return out.reshape(w.shape)


def _norm_matmul_kernel(x_ref, g_ref, w_ref, o_ref, xn_ref):
    @pl.when(pl.program_id(1) == 0)
    def _():
        xn_ref[...] = _rms_rows(x_ref[...], g_ref[...]).astype(BF16)

    o_ref[...] = jnp.dot(xn_ref[...], w_ref[...], preferred_element_type=F32).astype(o_ref.dtype)


def norm_matmul(x, gain, w, *, tm=1024, tn=512, out_dtype=F32):
    m, k = x.shape
    n = w.shape[1]
    tm = min(tm, m)
    return pl.pallas_call(
        _norm_matmul_kernel,
        grid=(m // tm, n // tn),
        in_specs=[pl.BlockSpec((tm, k), lambda i, j: (i, 0)),
                  pl.BlockSpec((1, k), lambda i, j: (0, 0)),
                  pl.BlockSpec((k, tn), lambda i, j: (0, j))],
        out_specs=pl.BlockSpec((tm, tn), lambda i, j: (i, j)),
        out_shape=jax.ShapeDtypeStruct((m, n), out_dtype),
        scratch_shapes=[pltpu.VMEM((tm, k), BF16)],
        compiler_params=_cparams("parallel", "arbitrary"),
        name="norm_matmul",
    )(x, gain.reshape(1, k), w)


def _matmul_res_kernel(x_ref, w_ref, r_ref, o_ref):
    o_ref[...] = r_ref[...] + jnp.dot(x_ref[...], w_ref[...], preferred_element_type=F32)


def matmul_residual(x, w, res, *, tm=1024, tn=512):
    m, k = x.shape
    n = w.shape[1]
    return pl.pallas_call(
        _matmul_res_kernel,
        grid=(m // tm, n // tn),
        in_specs=[pl.BlockSpec((tm, k), lambda i, j: (i, 0)),
                  pl.BlockSpec((k, tn), lambda i, j: (0, j)),
                  pl.BlockSpec((tm, tn), lambda i, j: (i, j))],
        out_specs=pl.BlockSpec((tm, tn), lambda i, j: (i, j)),
        out_shape=jax.ShapeDtypeStruct((m, n), F32),
        compiler_params=_cparams("parallel", "arbitrary"),
        name="matmul_residual",
    )(x, w, res)


def _matmul2_res_kernel(x1_ref, x2_ref, w1_ref, w2_ref, r_ref, o_ref):
    o_ref[...] = (r_ref[...] + jnp.dot(x1_ref[...], w1_ref[...], preferred_element_type=F32)
                  + jnp.dot(x2_ref[...], w2_ref[...], preferred_element_type=F32))


def matmul2_residual(x1, x2, w, res, *, tm=1024, tn=512):
    m, k1 = x1.shape
    n = w.shape[1]
    return pl.pallas_call(
        _matmul2_res_kernel,
        grid=(m // tm, n // tn),
        in_specs=[pl.BlockSpec((tm, k1), lambda i, j: (i, 0)),
                  pl.BlockSpec((tm, k1), lambda i, j: (i, 0)),
                  pl.BlockSpec((k1, tn), lambda i, j: (0, j)),
                  pl.BlockSpec((k1, tn), lambda i, j: (1, j)),
                  pl.BlockSpec((tm, tn), lambda i, j: (i, j))],
        out_specs=pl.BlockSpec((tm, tn), lambda i, j: (i, j)),
        out_shape=jax.ShapeDtypeStruct((m, n), F32),
        compiler_params=_cparams("parallel", "arbitrary"),
        name="matmul2_residual",
    )(x1, x2, w, w, res)


def _ffn_kernel(x_ref, g_ref, wg_ref, wu_ref, wo_ref, o_ref, xn_ref, acc_ref):
    j = pl.program_id(1)

    @pl.when(j == 0)
    def _():
        x = x_ref[...]
        xn_ref[...] = _rms_rows(x, g_ref[...]).astype(BF16)
        acc_ref[...] = x

    xn = xn_ref[...]
    gate = jnp.dot(xn, wg_ref[...], preferred_element_type=F32)
    up = jnp.dot(xn, wu_ref[...], preferred_element_type=F32)
    act = (gate * jax.nn.sigmoid(gate) * up).astype(BF16)
    acc_ref[...] += jnp.dot(act, wo_ref[...], preferred_element_type=F32)

    @pl.when(j == pl.num_programs(1) - 1)
    def _():
        o_ref[...] = acc_ref[...]


def ffn_residual(h, gain, w_in, w_out, *, tm=512, th=512):
    m, d = h.shape
    hidden = w_out.shape[0]
    nh = hidden // th
    return pl.pallas_call(
        _ffn_kernel,
        grid=(m // tm, nh),
        in_specs=[pl.BlockSpec((tm, d), lambda i, j: (i, 0)),
                  pl.BlockSpec((1, d), lambda i, j: (0, 0)),
                  pl.BlockSpec((d, th), lambda i, j: (0, j)),
                  pl.BlockSpec((d, th), lambda i, j: (0, j + nh)),
                  pl.BlockSpec((th, d), lambda i, j: (j, 0))],
        out_specs=pl.BlockSpec((tm, d), lambda i, j: (i, 0)),
        out_shape=jax.ShapeDtypeStruct((m, d), F32),
        scratch_shapes=[pltpu.VMEM((tm, d), BF16), pltpu.VMEM((tm, d), F32)],
        compiler_params=_cparams("parallel", "arbitrary"),
        name="ffn_residual",
    )(h, gain.reshape(1, d), w_in, w_in, w_out)


def _xattn_kernel(h_ref, g_ref, wq_ref, qg_ref, k_ref, v_ref, wo_ref, o_ref):
    x = h_ref[0]
    xn = _rms_rows(x, g_ref[...]).astype(BF16)
    q = jnp.dot(xn, wq_ref[...], preferred_element_type=F32)
    k = k_ref[0]
    v = v_ref[0]
    outs = []
    for hh in range(XA_HEADS):
        sl = slice(hh * XA_HEAD_DIM, (hh + 1) * XA_HEAD_DIM)
        qh = _rms_rows(q[:, sl], qg_ref[...]).astype(BF16)
        s = lax.dot_general(qh, k[:, sl], (((1,), (1,)), ((), ())), preferred_element_type=F32)
        s = s * (XA_HEAD_DIM ** -0.5)
        s = s - jnp.max(s, axis=-1, keepdims=True)
        p = jnp.exp(s)
        p = p / jnp.sum(p, axis=-1, keepdims=True)
        outs.append(jnp.dot(p.astype(BF16), v[:, sl], preferred_element_type=F32))
    o = jnp.concatenate(outs, axis=-1).astype(BF16)
    o_ref[0] = x + jnp.dot(o, wo_ref[...], preferred_element_type=F32)


def xattn_residual(h, gain, wq, q_gain, k, v, wo, *, tm=512):
    b, s, d = h.shape
    mt = k.shape[1]
    return pl.pallas_call(
        _xattn_kernel,
        grid=(b, s // tm),
        in_specs=[pl.BlockSpec((1, tm, d), lambda bi, i: (bi, i, 0)),
                  pl.BlockSpec((1, d), lambda bi, i: (0, 0)),
                  pl.BlockSpec((d, XA_WIDTH), lambda bi, i: (0, 0)),
                  pl.BlockSpec((1, XA_HEAD_DIM), lambda bi, i: (0, 0)),
                  pl.BlockSpec((1, mt, XA_WIDTH), lambda bi, i: (bi, 0, 0)),
                  pl.BlockSpec((1, mt, XA_WIDTH), lambda bi, i: (bi, 0, 0)),
                  pl.BlockSpec((XA_WIDTH, d), lambda bi, i: (0, 0))],
        out_specs=pl.BlockSpec((1, tm, d), lambda bi, i: (bi, i, 0)),
        out_shape=jax.ShapeDtypeStruct((b, s, d), F32),
        compiler_params=_cparams("parallel", "parallel"),
        name="xattn_residual",
    )(h, gain.reshape(1, d), wq, q_gain.reshape(1, XA_HEAD_DIM), k, v, wo)


def rms_norm(x, gain, eps=NORM_EPS):
    x32 = x.astype(F32)
    y = x32 * lax.rsqrt(jnp.mean(x32 * x32, axis=-1, keepdims=True) + eps)
    return (y * gain.astype(F32)).astype(x.dtype)


def rel_bucket(dist):
    dist = jnp.maximum(dist, 0)
    max_exact = REL_BUCKETS // 2
    scaled = (jnp.log(jnp.maximum(dist, max_exact).astype(F32) / max_exact)
              / math.log(REL_MAX_DIST / max_exact) * (REL_BUCKETS - max_exact))
    large = jnp.minimum(max_exact + scaled.astype(jnp.int32), REL_BUCKETS - 1)
    return jnp.where(dist < max_exact, dist, large)


NSA_QT = 128
NSA_KT = 512
NSA_WT = NSA_WINDOW + NSA_QT


def _dot_nt(a, b):
    return lax.dot_general(a, b, (((1,), (1,)), ((), ())), preferred_element_type=F32)


def _dot_tn(a, b):
    return lax.dot_general(a, b, (((0,), (0,)), ((), ())), preferred_element_type=F32)


def _bias_tile(tb_ref, h, first_slab, n_slabs):
    return jnp.concatenate([tb_ref[h, jnp.maximum(first_slab - j, 0)] for j in range(n_slabs)], axis=1)


def _nsa_kernel(q_ref, gl_ref, kc_ref, vc_ref, bc_ref, covt_ref, ks_ref, vs_ref, kw_ref, vw_ref,
                tb_ref, e_ref, qg_ref, o_ref, a_scr, *, n_top):
    QT, KT, WT, HG, dh = NSA_QT, NSA_KT, NSA_WT, NSA_GROUP_HEADS, NSA_HEAD_DIM
    nsb, nc = covt_ref.shape
    q0 = pl.program_id(2) * QT

    x = q_ref[0]
    qs = []
    for h in range(HG):
        xh = _rms_rows(x[:, h * dh:(h + 1) * dh], qg_ref[...]) * (dh ** -0.5)
        qs.append(xh.astype(BF16))
    qn = jnp.concatenate(qs, axis=0)

    sc = _dot_nt(qn, kc_ref[0, 0]) + bc_ref[...].reshape(HG * QT, nc)
    row = lax.broadcasted_iota(jnp.int32, (HG * QT, nc), 0)
    col = lax.broadcasted_iota(jnp.int32, (HG * QT, nc), 1)
    tq = q0 + (row & (QT - 1))
    mask_c = tq >= col * NSA_CMP_STRIDE + (NSA_CMP_LEN - 1)
    sc = jnp.where(mask_c, sc, NEG_INF)
    pc = jnp.where(mask_c, jnp.exp(sc - jnp.max(sc, axis=-1, keepdims=True)), 0.0)
    den = jnp.sum(pc, axis=-1, keepdims=True)
    pcb = (pc / jnp.maximum(den, 1e-30)).astype(BF16)
    o_c = jnp.dot(pcb, vc_ref[0, 0], preferred_element_type=F32)

    imp_all = _dot_nt(covt_ref[...], pcb)
    imp = imp_all[:, 0:QT]
    for h in range(1, HG):
        imp = imp + imp_all[:, h * QT:(h + 1) * QT]
    jj = lax.broadcasted_iota(jnp.int32, (nsb, QT), 0)
    cur = (q0 + lax.broadcasted_iota(jnp.int32, (nsb, QT), 1)) // NSA_SEL_BLOCK
    forced = (jj == 0) | (jj == cur) | (jj == cur - 1)
    a = jnp.where(forced, NSA_FORCE_SCORE, imp)
    a = jnp.where(jj > cur, -NSA_FORCE_SCORE, a)
    a_scr[...] = a

    def rank_body(i, rank):
        r = a_scr[pl.ds(i, 1), :]
        return rank + jnp.where((r > a) | ((r == a) & (jj > i)), 1.0, 0.0)

    n_live = jnp.minimum((q0 + QT - 1) // NSA_SEL_BLOCK + 1, nsb)
    rank = lax.fori_loop(0, n_live, rank_body, jnp.zeros((nsb, QT), F32))
    sel_t = jnp.where(rank < n_top, 1.0, 0.0)
    if nsb < 128:
        sel_t = jnp.concatenate([sel_t, jnp.zeros((128 - nsb, QT), F32)], axis=0)
    sel = sel_t.T.astype(BF16)

    cmr = (lax.broadcasted_iota(jnp.int32, (QT, KT), 1) - lax.broadcasted_iota(jnp.int32, (QT, KT), 0))

    def sel_body(kt, carry):
        m_i, l_i, acc = carry
        k0 = pl.multiple_of(kt * KT, KT)
        k = ks_ref[0, pl.ds(k0, KT), :]
        v = vs_ref[0, pl.ds(k0, KT), :]
        s = _dot_nt(qn, k)
        delta = q0 - k0
        visible = (jnp.dot(sel, e_ref[kt], preferred_element_type=F32) > 0.5) & (cmr <= delta)
        seg = delta // QT
        parts = []
        for h in range(HG):
            bias = _bias_tile(tb_ref, h, seg, KT // QT)
            parts.append(jnp.where(visible, s[h * QT:(h + 1) * QT] + bias, NEG_INF))
        s = jnp.concatenate(parts, axis=0)
        m_new = jnp.maximum(m_i, jnp.max(s, axis=-1, keepdims=True))
        alpha = jnp.exp(m_i - m_new)
        p = jnp.exp(s - m_new)
        l_new = alpha * l_i + jnp.sum(p, axis=-1, keepdims=True)
        acc = alpha * acc + jnp.dot(p.astype(BF16), v, preferred_element_type=F32)
        return m_new, l_new, acc

    init = (jnp.full((HG * QT, 1), NEG_INF, F32), jnp.zeros((HG * QT, 1), F32), jnp.zeros((HG * QT, dh), F32))
    _, l_s, acc_s = lax.fori_loop(0, q0 // KT + 1, sel_body, init)
    o_s = acc_s / l_s

    w0 = pl.multiple_of(q0, QT)
    kwin = kw_ref[0, pl.ds(w0, WT), :]
    vwin = vw_ref[0, pl.ds(w0, WT), :]
    sw = _dot_nt(qn, kwin)
    cw = lax.broadcasted_iota(jnp.int32, (QT, WT), 1)
    dist = NSA_WINDOW + lax.broadcasted_iota(jnp.int32, (QT, WT), 0) - cw
    vis_w = (dist >= 0) & (dist < NSA_WINDOW) & (cw >= NSA_WINDOW - q0)
    parts = []
    for h in range(HG):
        bias = _bias_tile(tb_ref, h, NSA_WINDOW // QT, WT // QT)
        parts.append(jnp.where(vis_w, sw[h * QT:(h + 1) * QT] + bias, NEG_INF))
    sw = jnp.concatenate(parts, axis=0)
    pw = jnp.exp(sw - jnp.max(sw, axis=-1, keepdims=True))
    pw = pw / jnp.sum(pw, axis=-1, keepdims=True)
    o_w = jnp.dot(pw.astype(BF16), vwin, preferred_element_type=F32)

    gates = jax.nn.sigmoid(gl_ref[0][:, :3 * HG])
    outs = []
    for h in range(HG):
        rs = slice(h * QT, (h + 1) * QT)
        outs.append(gates[:, 3 * h:3 * h + 1] * o_c[rs] + gates[:, 3 * h + 1:3 * h + 2] * o_s[rs]
                    + gates[:, 3 * h + 2:3 * h + 3] * o_w[rs])
    o_ref[0] = jnp.concatenate(outs, axis=-1).astype(o_ref.dtype)


def nsa_attention_pallas(proj, gl_blk, kc, vc, ks, vs, kw, vw, q_gain, rel_bias):
    B, S, _ = proj.shape
    G, HG, dh = NSA_KV_GROUPS, NSA_GROUP_HEADS, NSA_HEAD_DIM
    QT, KT, WT, W, SB = NSA_QT, NSA_KT, NSA_WT, NSA_WINDOW, NSA_SEL_BLOCK
    nsb, nc, nkt, nseg = S // SB, S // NSA_CMP_STRIDE, S // KT, max(S // QT, W // QT + 1)
    n_top = min(NSA_N_SELECT, nsb)

    def bias_of(dist):
        onehot = (rel_bucket(dist)[..., None] == jnp.arange(REL_BUCKETS)) & (dist >= 0)[..., None]
        return jnp.einsum('...b,bh->h...', onehot.astype(F32), rel_bias.astype(F32),
                          precision=lax.Precision.HIGHEST)

    bc = bias_of(jnp.arange(S)[:, None] - (jnp.arange(nc) * NSA_CMP_STRIDE + NSA_CMP_LEN - 1)[None, :])
    tb = bias_of(QT * jnp.arange(nseg)[:, None, None] + jnp.arange(QT)[None, :, None] - jnp.arange(QT)[None, None, :])
    cmp_start = np.arange(nc) * NSA_CMP_STRIDE
    cmp_end = cmp_start + NSA_CMP_LEN - 1
    sel_start = np.arange(nsb) * SB
    cover_t = ((cmp_start[None, :] < sel_start[:, None] + SB) & (cmp_end[None, :] >= sel_start[:, None])
               & (np.arange(nc)[None, :] < nc - 1))
    cover_t = jnp.asarray(cover_t, BF16)
    e = (np.arange(128)[None, :, None]
         == (np.arange(nkt)[:, None, None] * (KT // SB) + np.arange(KT)[None, None, :] // SB))
    e = jnp.asarray(e, BF16)

    kern = functools.partial(_nsa_kernel, n_top=n_top)
    return pl.pallas_call(
        kern,
        grid=(B, G, S // QT),
        in_specs=[
            pl.BlockSpec((1, QT, HG * dh), lambda b, g, i: (b, i, g)),
            pl.BlockSpec((1, QT, 128), lambda b, g, i: (b, i, gl_blk + g)),
            pl.BlockSpec((1, 1, nc, dh), lambda b, g, i: (b, g, 0, 0)),
            pl.BlockSpec((1, 1, nc, dh), lambda b, g, i: (b, g, 0, 0)),
            pl.BlockSpec((HG, QT, nc), lambda b, g, i: (g, i, 0)),
            pl.BlockSpec((nsb, nc), lambda b, g, i: (0, 0)),
            pl.BlockSpec((1, S, dh), lambda b, g, i: (b, 0, g)),
            pl.BlockSpec((1, S, dh), lambda b, g, i: (b, 0, g)),
            pl.BlockSpec((1, S + W, dh), lambda b, g, i: (b, 0, g)),
            pl.BlockSpec((1, S + W, dh), lambda b, g, i: (b, 0, g)),
            pl.BlockSpec((HG, nseg, QT, QT), lambda b, g, i: (g, 0, 0, 0)),
            pl.BlockSpec((nkt, 128, KT), lambda b, g, i: (0, 0, 0)),
            pl.BlockSpec((1, dh), lambda b, g, i: (0, 0)),
        ],
        out_specs=pl.BlockSpec((1, QT, HG * dh), lambda b, g, i: (b, i, g)),
        out_shape=jax.ShapeDtypeStruct((B, S, NSA_WIDTH), BF16),
        scratch_shapes=[pltpu.VMEM((nsb, QT), F32)],
        compiler_params=_cparams("parallel", "parallel", "arbitrary"),
        name="nsa_attention",
    )(proj, proj, kc, vc, bc, cover_t, ks, vs, kw, vw, tb, e, q_gain.reshape(1, dh))


def _nsa_compress_kernel(x_ref, pos_ref, w1_ref, w2_ref, kg_ref, o_ref):
    stride, dh = NSA_CMP_STRIDE, NSA_HEAD_DIM
    n_chunks = x_ref.shape[1] // stride
    hid = w1_ref.shape[2]
    h_lo = jnp.zeros((n_chunks, hid), F32)
    h_hi = jnp.zeros((n_chunks, hid), F32)
    for p in range(stride):
        xp = x_ref[0, pl.ds(p, n_chunks, stride=stride), :]
        h_lo = h_lo + jnp.dot((xp + pos_ref[0, p:p + 1, :]).astype(BF16), w1_ref[0, p * dh:(p + 1) * dh, :],
                              preferred_element_type=F32)
        h_hi = h_hi + jnp.dot((xp + pos_ref[0, stride + p:stride + p + 1, :]).astype(BF16),
                              w1_ref[0, (stride + p) * dh:(stride + p + 1) * dh, :], preferred_element_type=F32)
    hidden = h_lo + jnp.concatenate([h_hi[1:], jnp.zeros((1, hid), F32)], axis=0)
    out = jnp.dot((hidden * jax.nn.sigmoid(hidden)).astype(BF16), w2_ref[0], preferred_element_type=F32)
    out = jnp.where(pl.program_id(2) == 0, _rms_rows(out, kg_ref[...]), out)
    row = lax.broadcasted_iota(jnp.int32, out.shape, 0)
    o_ref[0, 0, 0] = jnp.where(row < n_chunks - 1, out, 0.0).astype(o_ref.dtype)


def nsa_compress(proj, col_blk, cmp_pos, cmp_w1, cmp_w2, k_gain):
    B, S, _ = proj.shape
    G, dh = NSA_KV_GROUPS, NSA_HEAD_DIM
    nc = S // NSA_CMP_STRIDE
    hid = cmp_w1.shape[-1]
    return pl.pallas_call(
        _nsa_compress_kernel,
        grid=(B, G, 2),
        in_specs=[
            pl.BlockSpec((1, S, dh), lambda b, g, w: (b, 0, col_blk + w * G + g)),
            pl.BlockSpec((1, NSA_CMP_LEN, dh), lambda b, g, w: (w, 0, 0)),
            pl.BlockSpec((1, NSA_CMP_LEN * dh, hid), lambda b, g, w: (w, 0, 0)),
            pl.BlockSpec((1, hid, dh), lambda b, g, w: (w, 0, 0)),
            pl.BlockSpec((1, dh), lambda b, g, w: (0, 0)),
        ],
        out_specs=pl.BlockSpec((1, 1, 1, nc, dh), lambda b, g, w: (w, b, g, 0, 0)),
        out_shape=jax.ShapeDtypeStruct((2, B, G, nc, dh), BF16),
        compiler_params=_cparams("parallel", "parallel", "arbitrary"),
        name="nsa_compress",
    )(proj, cmp_pos, cmp_w1.astype(BF16), cmp_w2.astype(BF16), k_gain.reshape(1, dh))


NSA_PREP_TM = NSA_WINDOW


def _nsa_kv_kernel(ks_ref, vs_ref, kw_ref, vw_ref, gs_ref, gw_ref, kso_ref, vso_ref, kwo_ref, vwo_ref):
    G, dh = NSA_KV_GROUPS, NSA_HEAD_DIM
    i = pl.program_id(1)
    n_in = pl.num_programs(1) - 1

    def normed(ref, gain_ref):
        x = ref[0]
        return jnp.concatenate([_rms_rows(x[:, g * dh:(g + 1) * dh], gain_ref[...]) for g in range(G)], axis=1)

    @pl.when(i < n_in)
    def _():
        kso_ref[0] = normed(ks_ref, gs_ref).astype(kso_ref.dtype)
        vso_ref[0] = vs_ref[0].astype(vso_ref.dtype)

    @pl.when(i == 0)
    def _():
        kwo_ref[0] = jnp.zeros(kwo_ref.shape[1:], kwo_ref.dtype)
        vwo_ref[0] = jnp.zeros(vwo_ref.shape[1:], vwo_ref.dtype)

    @pl.when(i > 0)
    def _():
        kwo_ref[0] = normed(kw_ref, gw_ref).astype(kwo_ref.dtype)
        vwo_ref[0] = vw_ref[0].astype(vwo_ref.dtype)


def nsa_kv_prep(proj, col_blk, k_gain):
    B, S, _ = proj.shape
    tm, kvw, W = NSA_PREP_TM, NSA_KV_WIDTH, NSA_WINDOW
    n_in = S // tm
    cur = lambda i: jnp.minimum(i, n_in - 1)
    prev = lambda i: jnp.maximum(i - 1, 0)
    return pl.pallas_call(
        _nsa_kv_kernel,
        grid=(B, n_in + 1),
        in_specs=[
            pl.BlockSpec((1, tm, kvw), lambda b, i: (b, cur(i), col_blk)),
            pl.BlockSpec((1, tm, kvw), lambda b, i: (b, cur(i), col_blk + 1)),
            pl.BlockSpec((1, tm, kvw), lambda b, i: (b, prev(i), col_blk + 2)),
            pl.BlockSpec((1, tm, kvw), lambda b, i: (b, prev(i), col_blk + 3)),
            pl.BlockSpec((1, NSA_HEAD_DIM), lambda b, i: (0, 0)),
            pl.BlockSpec((1, NSA_HEAD_DIM), lambda b, i: (0, 0)),
        ],
        out_specs=[
            pl.BlockSpec((1, tm, kvw), lambda b, i: (b, cur(i), 0)),
            pl.BlockSpec((1, tm, kvw), lambda b, i: (b, cur(i), 0)),
            pl.BlockSpec((1, tm, kvw), lambda b, i: (b, i, 0)),
            pl.BlockSpec((1, tm, kvw), lambda b, i: (b, i, 0)),
        ],
        out_shape=[jax.ShapeDtypeStruct((B, S, kvw), BF16), jax.ShapeDtypeStruct((B, S, kvw), BF16),
                   jax.ShapeDtypeStruct((B, S + W, kvw), BF16), jax.ShapeDtypeStruct((B, S + W, kvw), BF16)],
        compiler_params=_cparams("parallel", "arbitrary"),
        name="nsa_kv_prep",
    )(proj, proj, proj, proj, k_gain[1:2], k_gain[2:3])


RET_HB = 2


def _rotate_half(x, cos, sin):
    half = x.shape[-1] // 2
    x1, x2 = x[:, :half], x[:, half:]
    return jnp.concatenate([x1 * cos - x2 * sin, x1 * sin + x2 * cos], axis=-1)


def _retention_kernel(q_ref, k_ref, v_ref, g_ref, cos_ref, sin_ref, din_ref, qd_ref, kd_ref, cd_ref, gn_ref,
                      o_ref, s_scr):
    @pl.when(pl.program_id(2) == 0)
    def _():
        s_scr[...] = jnp.zeros_like(s_scr)

    cos, sin = cos_ref[...], sin_ref[...]
    heads = range(RET_HB)
    sl = [slice(h * RET_DK, (h + 1) * RET_DK) for h in heads]
    q = [_rotate_half(q_ref[0, :, sl[h]], cos, sin) * (RET_DK ** -0.5) for h in heads]
    k = [_rotate_half(k_ref[0, :, sl[h]], cos, sin) for h in heads]
    v16 = [v_ref[0, :, sl[h]].astype(BF16) for h in heads]
    inner = [_dot_nt(q[h].astype(BF16), k[h].astype(BF16)) * din_ref[h] for h in heads]
    s = [s_scr[h] for h in heads]
    o = [jnp.dot(inner[h].astype(BF16), v16[h], preferred_element_type=F32)
         + jnp.dot((q[h] * qd_ref[h]).astype(BF16), s[h].astype(BF16), preferred_element_type=F32) for h in heads]
    for h in heads:
        s_scr[h] = s[h] * cd_ref[h] + _dot_tn((k[h] * kd_ref[h]).astype(BF16), v16[h])
    outs = []
    for h in heads:
        mu = jnp.mean(o[h], axis=-1, keepdims=True)
        d = o[h] - mu
        var = jnp.mean(d * d, axis=-1, keepdims=True)
        gate = g_ref[0, :, sl[h]]
        outs.append((gate * jax.nn.sigmoid(gate)
                     * (d * lax.rsqrt(var + RET_GN_EPS) * gn_ref[:, sl[h]])).astype(o_ref.dtype))
    o_ref[0] = jnp.concatenate(outs, axis=-1)


def retention_pallas(proj, q_blk, gn_gain):
    B, S, _ = proj.shape
    H, dk, dv, C = RET_HEADS, RET_DK, RET_DV, RET_CHUNK
    N = S // C
    half = dk // 2
    inv = RET_ROPE_BASE ** (-jnp.arange(half, dtype=F32) / half)
    ang = jnp.arange(S).astype(F32)[:, None] * inv[None, :]
    cos, sin = jnp.cos(ang), jnp.sin(ang)
    log_gamma = jnp.log(1.0 - 2.0 ** (-5.0 - jnp.arange(H, dtype=F32)))
    idx = jnp.arange(C, dtype=F32)
    rel = idx[:, None] - idx[None, :]
    decay_in = jnp.where(rel >= 0, jnp.exp(log_gamma[:, None, None] * jnp.maximum(rel, 0.0)), 0.0)
    q_decay = jnp.exp(log_gamma[:, None] * (idx + 1.0))[..., None]
    k_decay = jnp.exp(log_gamma[:, None] * (C - 1.0 - idx))[..., None]
    chunk_decay = jnp.exp(log_gamma * C)[:, None, None]
    hb = RET_HB
    HP = H // hb
    qb0 = q_blk // hb
    return pl.pallas_call(
        _retention_kernel,
        grid=(B, HP, N),
        in_specs=[
            pl.BlockSpec((1, C, hb * dk), lambda b, p, n: (b, n, qb0 + p)),
            pl.BlockSpec((1, C, hb * dk), lambda b, p, n: (b, n, qb0 + HP + p)),
            pl.BlockSpec((1, C, hb * dv), lambda b, p, n: (b, n, qb0 + 2 * HP + p)),
            pl.BlockSpec((1, C, hb * dv), lambda b, p, n: (b, n, qb0 + 3 * HP + p)),
            pl.BlockSpec((C, half), lambda b, p, n: (n, 0)),
            pl.BlockSpec((C, half), lambda b, p, n: (n, 0)),
            pl.BlockSpec((hb, C, C), lambda b, p, n: (p, 0, 0)),
            pl.BlockSpec((hb, C, 1), lambda b, p, n: (p, 0, 0)),
            pl.BlockSpec((hb, C, 1), lambda b, p, n: (p, 0, 0)),
            pl.BlockSpec((hb, 1, 1), lambda b, p, n: (p, 0, 0)),
            pl.BlockSpec((1, hb * dv), lambda b, p, n: (0, p)),
        ],
        out_specs=pl.BlockSpec((1, C, hb * dv), lambda b, p, n: (b, n, p)),
        out_shape=jax.ShapeDtypeStruct((B, S, H * dv), BF16),
        scratch_shapes=[pltpu.VMEM((hb, dk, dv), F32)],
        compiler_params=_cparams("parallel", "parallel", "arbitrary"),
        name="retention",
    )(proj, proj, proj, proj, cos, sin, decay_in, q_decay, k_decay, chunk_decay, gn_gain.reshape(1, H * dv))


GDN_HB = 16
GDN_HALO = 8


def _conv_silu(x_ref, halo_ref, keep, w, stage_ref):
    c = x_ref.shape[1]
    stage_ref[0:GDN_HALO, :] = halo_ref[0] * keep
    stage_ref[GDN_HALO:GDN_HALO + c, :] = x_ref[0]
    y = w[GDN_CONV - 1:GDN_CONV] * x_ref[0]
    for j in range(GDN_CONV - 1):
        off = GDN_HALO - (GDN_CONV - 1) + j
        y = y + w[j:j + 1] * stage_ref[off:off + c, :]
    return y * jax.nn.sigmoid(y)


def _l2_rows(x):
    return x * lax.rsqrt(jnp.sum(x * x, axis=-1, keepdims=True) + NORM_EPS)


def _gdn_kernel(q_ref, k_ref, v_ref, qh_ref, kh_ref, vh_ref, wq_ref, wk_ref, wv_ref, z_ref,
                ba_ref, alog_ref, dtb_ref, ng_ref, o_ref, s_scr, qst_scr, kst_scr, vst_scr):
    C, dh, hb = GDN_CHUNK, GDN_HEAD_DIM, GDN_HB
    rep = GDN_V_HEADS // GDN_QK_HEADS
    first = pl.program_id(2) == 0

    @pl.when(first)
    def _():
        s_scr[...] = jnp.zeros_like(s_scr)

    keep = jnp.where(first, 0.0, 1.0)
    qc = _conv_silu(q_ref, qh_ref, keep, wq_ref[...], qst_scr)
    kc = _conv_silu(k_ref, kh_ref, keep, wk_ref[...], kst_scr)
    vc = _conv_silu(v_ref, vh_ref, keep, wv_ref[...], vst_scr)

    ri = lax.broadcasted_iota(jnp.int32, (C, C), 0)
    ci = lax.broadcasted_iota(jnp.int32, (C, C), 1)
    causal = ri >= ci
    strict = ri > ci
    ba = ba_ref[0]
    bcol = jax.nn.sigmoid(ba[:, :hb])
    g = -jnp.exp(alog_ref[0]) * jax.nn.softplus(ba[:, hb:2 * hb] + dtb_ref[0])
    gcol = jnp.dot(jnp.where(causal, 1.0, 0.0), g, preferred_element_type=F32, precision=lax.Precision.HIGHEST)
    grow = lax.dot_general(g, jnp.where(ri <= ci, 1.0, 0.0), (((0,), (0,)), ((), ())),
                           preferred_element_type=F32, precision=lax.Precision.HIGHEST)
    heads = range(hb)
    qs, ks, grams = [], [], []
    for hq in range(hb // rep):
        qh = _l2_rows(qc[:, hq * dh:(hq + 1) * dh]) * (dh ** -0.5)
        kh = _l2_rows(kc[:, hq * dh:(hq + 1) * dh])
        k16 = kh.astype(BF16)
        qs.append(qh)
        ks.append(kh)
        grams.append(_dot_nt(jnp.concatenate([qh.astype(BF16), k16], axis=0), k16))
    beta = [bcol[:, h:h + 1] for h in heads]
    gc = [gcol[:, h:h + 1] for h in heads]
    gr = [grow[h:h + 1, :] for h in heads]
    g_last = [gr[h][:, C - 1:C] for h in heads]
    eg = [jnp.exp(gc[h]) for h in heads]
    decay = [jnp.where(causal, jnp.exp(jnp.minimum(gc[h] - gr[h], 0.0)), 0.0) for h in heads]
    attn = [(grams[h // rep][:C] * decay[h]).astype(BF16) for h in heads]
    nm = [jnp.where(strict, grams[h // rep][C:] * decay[h] * (-beta[h]), 0.0) for h in heads]
    m = [nm[h].astype(BF16) for h in heads]
    for _ in range(int(math.log2(C)) - 1):
        mf = [jnp.dot(m[h], m[h], preferred_element_type=F32) for h in heads]
        m = [mf[h].astype(BF16) for h in heads]
        nm = [nm[h] + mf[h] + jnp.dot(m[h], nm[h].astype(BF16), preferred_element_type=F32) for h in heads]
    x = [jnp.concatenate([vc[:, h * dh:(h + 1) * dh] * beta[h], ks[h // rep] * (beta[h] * eg[h])], axis=1)
         for h in heads]
    x = [x[h] + jnp.dot(nm[h].astype(BF16), x[h].astype(BF16), preferred_element_type=F32) for h in heads]
    s = [s_scr[h] for h in heads]
    ws = [jnp.dot(jnp.concatenate([x[h][:, dh:].astype(BF16), (qs[h // rep] * eg[h]).astype(BF16)], axis=0),
                  s[h].astype(BF16), preferred_element_type=F32) for h in heads]
    vn = [(x[h][:, :dh] - ws[h][:C]).astype(BF16) for h in heads]
    o = [ws[h][C:] + jnp.dot(attn[h], vn[h], preferred_element_type=F32) for h in heads]
    for h in heads:
        k_dec = (ks[h // rep] * jnp.exp(g_last[h] - gc[h])).astype(BF16)
        s_scr[h] = s[h] * jnp.exp(g_last[h]) + _dot_tn(k_dec, vn[h])
    outs = []
    for h in heads:
        zh = z_ref[0, :, h * dh:(h + 1) * dh]
        outs.append((_rms_rows(o[h], ng_ref[...]) * (zh * jax.nn.sigmoid(zh))).astype(o_ref.dtype))
    o_ref[0] = jnp.concatenate(outs, axis=1)


_GDN_GATE_COL0 = GDN_CONV_CH + GDN_V_WIDTH
_GDN_COLS = _round_up(_GDN_GATE_COL0 + GDN_V_HEADS // GDN_HB * 128, 512)


def _gdn_w_in_layout(w_in):
    b0 = _GDN_GATE_COL0
    a0 = b0 + GDN_V_HEADS
    parts = [w_in[:, :b0]]
    for p in range(GDN_V_HEADS // GDN_HB):
        parts.append(w_in[:, b0 + p * GDN_HB:b0 + (p + 1) * GDN_HB])
        parts.append(jnp.pad(w_in[:, a0 + p * GDN_HB:a0 + (p + 1) * GDN_HB], ((0, 0), (0, 128 - 2 * GDN_HB))))
    w = jnp.concatenate(parts, axis=1)
    return jnp.pad(w, ((0, 0), (0, _GDN_COLS - w.shape[1]))).astype(BF16)


def gdn_delta_rule(proj, conv_w, a_log, dt_bias, norm_gain):
    B, S, _ = proj.shape
    C, dh, hb, H = GDN_CHUNK, GDN_HEAD_DIM, GDN_HB, GDN_V_HEADS
    rep = GDN_V_HEADS // GDN_QK_HEADS
    N, HP = S // C, H // hb
    wqk, wv = hb // rep * dh, hb * dh
    kb0, vb0, zb0 = GDN_QK_WIDTH // wqk, 2 * GDN_QK_WIDTH // wv, GDN_CONV_CH // wv
    gb0 = _GDN_GATE_COL0 // 128
    hr = C // GDN_HALO
    halo = lambda n: jnp.maximum(n * hr - 1, 0)
    return pl.pallas_call(
        _gdn_kernel,
        grid=(B, HP, N),
        in_specs=[
            pl.BlockSpec((1, C, wqk), lambda b, p, n: (b, n, p)),
            pl.BlockSpec((1, C, wqk), lambda b, p, n: (b, n, kb0 + p)),
            pl.BlockSpec((1, C, wv), lambda b, p, n: (b, n, vb0 + p)),
            pl.BlockSpec((1, GDN_HALO, wqk), lambda b, p, n: (b, halo(n), p)),
            pl.BlockSpec((1, GDN_HALO, wqk), lambda b, p, n: (b, halo(n), kb0 + p)),
            pl.BlockSpec((1, GDN_HALO, wv), lambda b, p, n: (b, halo(n), vb0 + p)),
            pl.BlockSpec((GDN_CONV, wqk), lambda b, p, n: (0, p)),
            pl.BlockSpec((GDN_CONV, wqk), lambda b, p, n: (0, kb0 + p)),
            pl.BlockSpec((GDN_CONV, wv), lambda b, p, n: (0, vb0 + p)),
            pl.BlockSpec((1, C, wv), lambda b, p, n: (b, n, zb0 + p)),
            pl.BlockSpec((1, C, 128), lambda b, p, n: (b, n, gb0 + p)),
            pl.BlockSpec((1, 1, hb), lambda b, p, n: (p, 0, 0)),
            pl.BlockSpec((1, 1, hb), lambda b, p, n: (p, 0, 0)),
            pl.BlockSpec((1, dh), lambda b, p, n: (0, 0)),
        ],
        out_specs=pl.BlockSpec((1, C, wv), lambda b, p, n: (b, n, p)),
        out_shape=jax.ShapeDtypeStruct((B, S, H * dh), BF16),
        scratch_shapes=[pltpu.VMEM((hb, dh, dh), F32), pltpu.VMEM((GDN_HALO + C, wqk), F32),
                        pltpu.VMEM((GDN_HALO + C, wqk), F32), pltpu.VMEM((GDN_HALO + C, wv), F32)],
        compiler_params=_cparams("parallel", "parallel", "arbitrary"),
        name="gdn_delta_rule",
    )(proj, proj, proj, proj, proj, proj, conv_w, conv_w, conv_w, proj, proj,
      a_log.astype(F32).reshape(HP, 1, hb), dt_bias.astype(F32).reshape(HP, 1, hb), norm_gain.reshape(1, dh))


_HYB_NSA_COLS = NSA_WIDTH + 6 * NSA_KV_WIDTH
_HYB_RET_COLS = 2 * RET_HEADS * RET_DK + 2 * RET_HEADS * RET_DV
_HYB_GATE_COL0 = _HYB_NSA_COLS + _HYB_RET_COLS
_HYB_COLS = _round_up(_HYB_GATE_COL0 + NSA_KV_GROUPS * 128, 512)


def _hybrid_w_in_layout(w_in):
    gate0 = _HYB_NSA_COLS
    ret0 = gate0 + 3 * NSA_HEADS
    per_group = 3 * NSA_GROUP_HEADS
    parts = [w_in[:, :gate0], w_in[:, ret0:ret0 + _HYB_RET_COLS]]
    for g in range(NSA_KV_GROUPS):
        parts.append(jnp.pad(w_in[:, gate0 + g * per_group:gate0 + (g + 1) * per_group],
                             ((0, 0), (0, 128 - per_group))))
    w = jnp.concatenate(parts, axis=1)
    return jnp.pad(w, ((0, 0), (0, _HYB_COLS - w.shape[1]))).astype(BF16)


def hybrid_mixer(h, ln_gain, w_in, w_out, q_gain, k_gain, cmp_pos, cmp_w1, cmp_w2, gn_gain, rel_bias):
    B, S, D = h.shape
    proj = norm_matmul(h.reshape(B * S, D), ln_gain, _hybrid_w_in_layout(w_in)).reshape(B, S, _HYB_COLS)
    cmp = nsa_compress(proj, NSA_WIDTH // NSA_HEAD_DIM, cmp_pos, cmp_w1, cmp_w2, k_gain[0])
    ksn, vsb, kwn, vwb = nsa_kv_prep(proj, (NSA_WIDTH + 2 * NSA_KV_WIDTH) // NSA_KV_WIDTH, k_gain)
    a_out = nsa_attention_pallas(proj, _HYB_GATE_COL0 // 128, cmp[0], cmp[1], ksn, vsb, kwn, vwb, q_gain, rel_bias)
    b_out = retention_pallas(proj, _HYB_NSA_COLS // RET_DK, gn_gain)
    w_out = w_out.astype(BF16)
    return matmul2_residual(a_out.reshape(B * S, -1), b_out.reshape(B * S, -1), w_out,
                            h.reshape(B * S, D)).reshape(B, S, D)


def gdn_mixer(h, ln_gain, w_in, conv_w, a_log, dt_bias, norm_gain, w_out):
    B, S, D = h.shape
    proj = norm_matmul(h.reshape(B * S, D), ln_gain, _gdn_w_in_layout(w_in)).reshape(B, S, _GDN_COLS)
    o = gdn_delta_rule(proj, conv_w, a_log, dt_bias, norm_gain)
    return matmul_residual(o.reshape(B * S, GDN_V_WIDTH), w_out.astype(BF16), h.reshape(B * S, D)).reshape(B, S, D)


def memory_kv(mem, mem_gain, wkv, k_gain):
    B, M, D = mem.shape
    kv = norm_matmul(mem.reshape(B * M, D), mem_gain, wkv.astype(BF16))
    k, v = jnp.split(kv.reshape(B, M, 2 * XA_WIDTH), 2, axis=-1)
    k = rms_norm(k.reshape(B, M, XA_HEADS, XA_HEAD_DIM), k_gain).reshape(B, M, XA_WIDTH)
    return k.astype(BF16), v.astype(BF16)


def kernel(x, mem, rel_bias, ln_mix, ln_mem, ln_ffn, hyb_w_in, hyb_w_out, nsa_q_gain, nsa_k_gain,
           nsa_cmp_pos, nsa_cmp_w1, nsa_cmp_w2, ret_gn_gain, gdn_w_in, gdn_conv_w, gdn_a_log,
           gdn_dt_bias, gdn_norm_gain, gdn_w_out, xa_wq, xa_wkv, xa_q_gain, xa_k_gain, xa_mem_gain,
           xa_wo, ffn_w_in, ffn_w_out):
    B, S, D = x.shape
    h = x
    ffn_w_in, ffn_w_out, gdn_w_out = cast_bf16(ffn_w_in), cast_bf16(ffn_w_out), cast_bf16(gdn_w_out)
    for layer in range(DEPTH):
        if layer % 2 == 0:
            e = layer // 2
            h = hybrid_mixer(h, ln_mix[layer], hyb_w_in[e], hyb_w_out[e], nsa_q_gain[e], nsa_k_gain[e],
                             nsa_cmp_pos[e], nsa_cmp_w1[e], nsa_cmp_w2[e], ret_gn_gain[e], rel_bias)
        else:
            o = layer // 2
            h = gdn_mixer(h, ln_mix[layer], gdn_w_in[o], gdn_conv_w[o], gdn_a_log[o], gdn_dt_bias[o],
                          gdn_norm_gain[o], gdn_w_out[o])
        k_mem, v_mem = memory_kv(mem, xa_mem_gain[layer], xa_wkv[layer], xa_k_gain[layer])
        h = xattn_residual(h, ln_mem[layer], xa_wq[layer].astype(BF16), xa_q_gain[layer], k_mem, v_mem,
                           xa_wo[layer].astype(BF16))
        h = ffn_residual(h.reshape(B * S, D), ln_ffn[layer], ffn_w_in[layer].astype(BF16),
                         ffn_w_out[layer].astype(BF16)).reshape(B, S, D)
    return h
```

```python
import functools
import math

import jax
import jax.numpy as jnp
import numpy as np
from jax import lax
from jax.experimental import pallas as pl
from jax.experimental.pallas import tpu as pltpu

F32 = jnp.float32
BF16 = jnp.bfloat16

D_MODEL = 2048
DEPTH = 4
NORM_EPS = 1e-6
NEG_INF = -1e30

NSA_HEADS = 8
NSA_KV_GROUPS = 2
NSA_GROUP_HEADS = NSA_HEADS // NSA_KV_GROUPS
NSA_HEAD_DIM = 128
NSA_CMP_LEN = 32
NSA_CMP_STRIDE = 16
NSA_SEL_BLOCK = 64
NSA_N_SELECT = 16
NSA_WINDOW = 512
NSA_Q_BLOCK = 64
NSA_FORCE_SCORE = 1e6
NSA_WIDTH = NSA_HEADS * NSA_HEAD_DIM
NSA_KV_WIDTH = NSA_KV_GROUPS * NSA_HEAD_DIM

RET_HEADS = 4
RET_DK = 256
RET_DV = 256
RET_CHUNK = 128
RET_ROPE_BASE = 10000.0
RET_GN_EPS = 1e-5

GDN_QK_HEADS = 16
GDN_V_HEADS = 32
GDN_HEAD_DIM = 128
GDN_CONV = 4
GDN_CHUNK = 64
GDN_QK_WIDTH = GDN_QK_HEADS * GDN_HEAD_DIM
GDN_V_WIDTH = GDN_V_HEADS * GDN_HEAD_DIM
GDN_CONV_CH = 2 * GDN_QK_WIDTH + GDN_V_WIDTH

REL_BUCKETS = 32
REL_MAX_DIST = 1024

XA_HEADS = 4
XA_HEAD_DIM = 128
XA_WIDTH = XA_HEADS * XA_HEAD_DIM

V7X_VMEM_LIMIT_BYTES = 56 * 1024 * 1024


def _cparams(*sem):
    return pltpu.CompilerParams(dimension_semantics=sem, vmem_limit_bytes=V7X_VMEM_LIMIT_BYTES)


def _round_up(n, m):
    return -(-n // m) * m


def _rms_rows(x, gain):
    return x * lax.rsqrt(jnp.mean(x * x, axis=-1, keepdims=True) + NORM_EPS) * gain


def _cast_kernel(x_ref, o_ref):
    o_ref[...] = x_ref[...].astype(o_ref.dtype)


def cast_bf16(w, *, block_bytes=8 * 1024 * 1024):
    n = w.shape[-1]
    x = w.reshape(-1, n)
    rows = x.shape[0]
    tr = rows
    while tr * n * 4 > block_bytes and tr % 32 == 0:
        tr //= 2
    out = pl.pallas_call(
        _cast_kernel,
        grid=(rows // tr,),
        in_specs=[pl.BlockSpec((tr, n), lambda i: (i, 0))],
        out_specs=pl.BlockSpec((tr, n), lambda i: (i, 0)),
        out_shape=jax.ShapeDtypeStruct((rows, n), BF16),
        compiler_params=_cparams("parallel"),
        name="cast_bf16",
    )(x)
    return out.reshape(w.shape)


def _norm_matmul_kernel(x_ref, g_ref, w_ref, o_ref, xn_ref):
    @pl.when(pl.program_id(1) == 0)
    def _():
        xn_ref[...] = _rms_rows(x_ref[...], g_ref[...]).astype(BF16)

    o_ref[...] = jnp.dot(xn_ref[...], w_ref[...], preferred_element_type=F32).astype(o_ref.dtype)


def norm_matmul(x, gain, w, *, tm=1024, tn=512, out_dtype=F32):
    m, k = x.shape
    n = w.shape[1]
    tm = min(tm, m)
    return pl.pallas_call(
        _norm_matmul_kernel,
        grid=(m // tm, n // tn),
        in_specs=[pl.BlockSpec((tm, k), lambda i, j: (i, 0)),
                  pl.BlockSpec((1, k), lambda i, j: (0, 0)),
                  pl.BlockSpec((k, tn), lambda i, j: (0, j))],
        out_specs=pl.BlockSpec((tm, tn), lambda i, j: (i, j)),
        out_shape=jax.ShapeDtypeStruct((m, n), out_dtype),
        scratch_shapes=[pltpu.VMEM((tm, k), BF16)],
        compiler_params=_cparams("parallel", "arbitrary"),
        name="norm_matmul",
    )(x, gain.reshape(1, k), w)


def _matmul_res_kernel(x_ref, w_ref, r_ref, o_ref):
    o_ref[...] = r_ref[...] + jnp.dot(x_ref[...], w_ref[...], preferred_element_type=F32)


def matmul_residual(x, w, res, *, tm=1024, tn=512):
    m, k = x.shape
    n = w.shape[1]
    return pl.pallas_call(
        _matmul_res_kernel,
        grid=(m // tm, n // tn),
        in_specs=[pl.BlockSpec((tm, k), lambda i, j: (i, 0)),
                  pl.BlockSpec((k, tn), lambda i, j: (0, j)),
                  pl.BlockSpec((tm, tn), lambda i, j: (i, j))],
        out_specs=pl.BlockSpec((tm, tn), lambda i, j: (i, j)),
        out_shape=jax.ShapeDtypeStruct((m, n), F32),
        compiler_params=_cparams("parallel", "arbitrary"),
        name="matmul_residual",
    )(x, w, res)


def _matmul2_res_kernel(x1_ref, x2_ref, w1_ref, w2_ref, r_ref, o_ref):
    o_ref[...] = (r_ref[...] + jnp.dot(x1_ref[...], w1_ref[...], preferred_element_type=F32)
                  + jnp.dot(x2_ref[...], w2_ref[...], preferred_element_type=F32))


def matmul2_residual(x1, x2, w, res, *, tm=1024, tn=512):
    m, k1 = x1.shape
    n = w.shape[1]
    return pl.pallas_call(
        _matmul2_res_kernel,
        grid=(m // tm, n // tn),
        in_specs=[pl.BlockSpec((tm, k1), lambda i, j: (i, 0)),
                  pl.BlockSpec((tm, k1), lambda i, j: (i, 0)),
                  pl.BlockSpec((k1, tn), lambda i, j: (0, j)),
                  pl.BlockSpec((k1, tn), lambda i, j: (1, j)),
                  pl.BlockSpec((tm, tn), lambda i, j: (i, j))],
        out_specs=pl.BlockSpec((tm, tn), lambda i, j: (i, j)),
        out_shape=jax.ShapeDtypeStruct((m, n), F32),
        compiler_params=_cparams("parallel", "arbitrary"),
        name="matmul2_residual",
    )(x1, x2, w, w, res)


def _ffn_kernel(x_ref, g_ref, wg_ref, wu_ref, wo_ref, o_ref, xn_ref, acc_ref):
    j = pl.program_id(1)

    @pl.when(j == 0)
    def _():
        x = x_ref[...]
        xn_ref[...] = _rms_rows(x, g_ref[...]).astype(BF16)
        acc_ref[...] = x

    xn = xn_ref[...]
    gate = jnp.dot(xn, wg_ref[...], preferred_element_type=F32)
    up = jnp.dot(xn, wu_ref[...], preferred_element_type=F32)
    act = (gate * jax.nn.sigmoid(gate) * up).astype(BF16)
    acc_ref[...] += jnp.dot(act, wo_ref[...], preferred_element_type=F32)

    @pl.when(j == pl.num_programs(1) - 1)
    def _():
        o_ref[...] = acc_ref[...]


def ffn_residual(h, gain, w_in, w_out, *, tm=512, th=512):
    m, d = h.shape
    hidden = w_out.shape[0]
    nh = hidden // th
    return pl.pallas_call(
        _ffn_kernel,
        grid=(m // tm, nh),
        in_specs=[pl.BlockSpec((tm, d), lambda i, j: (i, 0)),
                  pl.BlockSpec((1, d), lambda i, j: (0, 0)),
                  pl.BlockSpec((d, th), lambda i, j: (0, j)),
                  pl.BlockSpec((d, th), lambda i, j: (0, j + nh)),
                  pl.BlockSpec((th, d), lambda i, j: (j, 0))],
        out_specs=pl.BlockSpec((tm, d), lambda i, j: (i, 0)),
        out_shape=jax.ShapeDtypeStruct((m, d), F32),
        scratch_shapes=[pltpu.VMEM((tm, d), BF16), pltpu.VMEM((tm, d), F32)],
        compiler_params=_cparams("parallel", "arbitrary"),
        name="ffn_residual",
    )(h, gain.reshape(1, d), w_in, w_in, w_out)


def _xattn_kernel(h_ref, g_ref, wq_ref, qg_ref, k_ref, v_ref, wo_ref, o_ref):
    x = h_ref[0]
    xn = _rms_rows(x, g_ref[...]).astype(BF16)
    q = jnp.dot(xn, wq_ref[...], preferred_element_type=F32)
    k = k_ref[0]
    v = v_ref[0]
    outs = []
    for hh in range(XA_HEADS):
        sl = slice(hh * XA_HEAD_DIM, (hh + 1) * XA_HEAD_DIM)
        qh = _rms_rows(q[:, sl], qg_ref[...]).astype(BF16)
        s = lax.dot_general(qh, k[:, sl], (((1,), (1,)), ((), ())), preferred_element_type=F32)
        s = s * (XA_HEAD_DIM ** -0.5)
        s = s - jnp.max(s, axis=-1, keepdims=True)
        p = jnp.exp(s)
        p = p / jnp.sum(p, axis=-1, keepdims=True)
        outs.append(jnp.dot(p.astype(BF16), v[:, sl], preferred_element_type=F32))
    o = jnp.concatenate(outs, axis=-1).astype(BF16)
    o_ref[0] = x + jnp.dot(o, wo_ref[...], preferred_element_type=F32)


def xattn_residual(h, gain, wq, q_gain, k, v, wo, *, tm=512):
    b, s, d = h.shape
    mt = k.shape[1]
    return pl.pallas_call(
        _xattn_kernel,
        grid=(b, s // tm),
        in_specs=[pl.BlockSpec((1, tm, d), lambda bi, i: (bi, i, 0)),
                  pl.BlockSpec((1, d), lambda bi, i: (0, 0)),
                  pl.BlockSpec((d, XA_WIDTH), lambda bi, i: (0, 0)),
                  pl.BlockSpec((1, XA_HEAD_DIM), lambda bi, i: (0, 0)),
                  pl.BlockSpec((1, mt, XA_WIDTH), lambda bi, i: (bi, 0, 0)),
                  pl.BlockSpec((1, mt, XA_WIDTH), lambda bi, i: (bi, 0, 0)),
                  pl.BlockSpec((XA_WIDTH, d), lambda bi, i: (0, 0))],
        out_specs=pl.BlockSpec((1, tm, d), lambda bi, i: (bi, i, 0)),
        out_shape=jax.ShapeDtypeStruct((b, s, d), F32),
        compiler_params=_cparams("parallel", "parallel"),
        name="xattn_residual",
    )(h, gain.reshape(1, d), wq, q_gain.reshape(1, XA_HEAD_DIM), k, v, wo)


def rms_norm(x, gain, eps=NORM_EPS):
    x32 = x.astype(F32)
    y = x32 * lax.rsqrt(jnp.mean(x32 * x32, axis=-1, keepdims=True) + eps)
    return (y * gain.astype(F32)).astype(x.dtype)


def rel_bucket(dist):
    dist = jnp.maximum(dist, 0)
    max_exact = REL_BUCKETS // 2
    scaled = (jnp.log(jnp.maximum(dist, max_exact).astype(F32) / max_exact)
              / math.log(REL_MAX_DIST / max_exact) * (REL_BUCKETS - max_exact))
    large = jnp.minimum(max_exact + scaled.astype(jnp.int32), REL_BUCKETS - 1)
    return jnp.where(dist < max_exact, dist, large)


NSA_QT = 128
NSA_KT = 512
NSA_WT = NSA_WINDOW + NSA_QT


def _dot_nt(a, b):
    return lax.dot_general(a, b, (((1,), (1,)), ((), ())), preferred_element_type=F32)


def _dot_tn(a, b):
    return lax.dot_general(a, b, (((0,), (0,)), ((), ())), preferred_element_type=F32)


def _bias_tile(tb_ref, h, first_slab, n_slabs):
    return jnp.concatenate([tb_ref[h, jnp.maximum(first_slab - j, 0)] for j in range(n_slabs)], axis=1)


def _nsa_kernel(q_ref, gl_ref, kc_ref, vc_ref, bc_ref, covt_ref, ks_ref, vs_ref, kw_ref, vw_ref,
                tb_ref, e_ref, qg_ref, o_ref, a_scr, *, n_top):
    QT, KT, WT, HG, dh = NSA_QT, NSA_KT, NSA_WT, NSA_GROUP_HEADS, NSA_HEAD_DIM
    nsb, nc = covt_ref.shape
    q0 = pl.program_id(2) * QT

    x = q_ref[0]
    qs = []
    for h in range(HG):
        xh = _rms_rows(x[:, h * dh:(h + 1) * dh], qg_ref[...]) * (dh ** -0.5)
        qs.append(xh.astype(BF16))
    qn = jnp.concatenate(qs, axis=0)

    sc = _dot_nt(qn, kc_ref[0, 0]) + bc_ref[...].reshape(HG * QT, nc)
    row = lax.broadcasted_iota(jnp.int32, (HG * QT, nc), 0)
    col = lax.broadcasted_iota(jnp.int32, (HG * QT, nc), 1)
    tq = q0 + (row & (QT - 1))
    mask_c = tq >= col * NSA_CMP_STRIDE + (NSA_CMP_LEN - 1)
    sc = jnp.where(mask_c, sc, NEG_INF)
    pc = jnp.where(mask_c, jnp.exp(sc - jnp.max(sc, axis=-1, keepdims=True)), 0.0)
    den = jnp.sum(pc, axis=-1, keepdims=True)
    pcb = (pc / jnp.maximum(den, 1e-30)).astype(BF16)
    o_c = jnp.dot(pcb, vc_ref[0, 0], preferred_element_type=F32)

    imp_all = _dot_nt(covt_ref[...], pcb)
    imp = imp_all[:, 0:QT]
    for h in range(1, HG):
        imp = imp + imp_all[:, h * QT:(h + 1) * QT]
    jj = lax.broadcasted_iota(jnp.int32, (nsb, QT), 0)
    cur = (q0 + lax.broadcasted_iota(jnp.int32, (nsb, QT), 1)) // NSA_SEL_BLOCK
    forced = (jj == 0) | (jj == cur) | (jj == cur - 1)
    a = jnp.where(forced, NSA_FORCE_SCORE, imp)
    a = jnp.where(jj > cur, -NSA_FORCE_SCORE, a)
    a_scr[...] = a

    def rank_body(i, rank):
        r = a_scr[pl.ds(i, 1), :]
        return rank + jnp.where((r > a) | ((r == a) & (jj > i)), 1.0, 0.0)

    n_live = jnp.minimum((q0 + QT - 1) // NSA_SEL_BLOCK + 1, nsb)
    rank = lax.fori_loop(0, n_live, rank_body, jnp.zeros((nsb, QT), F32))
    sel_t = jnp.where(rank < n_top, 1.0, 0.0)
    if nsb < 128:
        sel_t = jnp.concatenate([sel_t, jnp.zeros((128 - nsb, QT), F32)], axis=0)
    sel = sel_t.T.astype(BF16)

    cmr = (lax.broadcasted_iota(jnp.int32, (QT, KT), 1) - lax.broadcasted_iota(jnp.int32, (QT, KT), 0))

    def sel_body(kt, carry):
        m_i, l_i, acc = carry
        k0 = pl.multiple_of(kt * KT, KT)
        k = ks_ref[0, pl.ds(k0, KT), :]
        v = vs_ref[0, pl.ds(k0, KT), :]
        s = _dot_nt(qn, k)
        delta = q0 - k0
        visible = (jnp.dot(sel, e_ref[kt], preferred_element_type=F32) > 0.5) & (cmr <= delta)
        seg = delta // QT
        parts = []
        for h in range(HG):
            bias = _bias_tile(tb_ref, h, seg, KT // QT)
            parts.append(jnp.where(visible, s[h * QT:(h + 1) * QT] + bias, NEG_INF))
        s = jnp.concatenate(parts, axis=0)
        m_new = jnp.maximum(m_i, jnp.max(s, axis=-1, keepdims=True))
        alpha = jnp.exp(m_i - m_new)
        p = jnp.exp(s - m_new)
        l_new = alpha * l_i + jnp.sum(p, axis=-1, keepdims=True)
        acc = alpha * acc + jnp.dot(p.astype(BF16), v, preferred_element_type=F32)
        return m_new, l_new, acc

    init = (jnp.full((HG * QT, 1), NEG_INF, F32), jnp.zeros((HG * QT, 1), F32), jnp.zeros((HG * QT, dh), F32))
    _, l_s, acc_s = lax.fori_loop(0, q0 // KT + 1, sel_body, init)
    o_s = acc_s / l_s

    w0 = pl.multiple_of(q0, QT)
    kwin = kw_ref[0, pl.ds(w0, WT), :]
    vwin = vw_ref[0, pl.ds(w0, WT), :]
    sw = _dot_nt(qn, kwin)
    cw = lax.broadcasted_iota(jnp.int32, (QT, WT), 1)
    dist = NSA_WINDOW + lax.broadcasted_iota(jnp.int32, (QT, WT), 0) - cw
    vis_w = (dist >= 0) & (dist < NSA_WINDOW) & (cw >= NSA_WINDOW - q0)
    parts = []
    for h in range(HG):
        bias = _bias_tile(tb_ref, h, NSA_WINDOW // QT, WT // QT)
        parts.append(jnp.where(vis_w, sw[h * QT:(h + 1) * QT] + bias, NEG_INF))
    sw = jnp.concatenate(parts, axis=0)
    pw = jnp.exp(sw - jnp.max(sw, axis=-1, keepdims=True))
    o_w = jnp.dot(pw.astype(BF16), vwin, preferred_element_type=F32) / jnp.sum(pw, axis=-1, keepdims=True)

    gates = jax.nn.sigmoid(gl_ref[0][:, :3 * HG])
    outs = []
    for h in range(HG):
        rs = slice(h * QT, (h + 1) * QT)
        outs.append(gates[:, 3 * h:3 * h + 1] * o_c[rs] + gates[:, 3 * h + 1:3 * h + 2] * o_s[rs]
                    + gates[:, 3 * h + 2:3 * h + 3] * o_w[rs])
    o_ref[0] = jnp.concatenate(outs, axis=-1).astype(o_ref.dtype)


def nsa_attention_pallas(proj, gl_blk, kc, vc, ks, vs, kw, vw, q_gain, rel_bias):
    B, S, _ = proj.shape
    G, HG, dh = NSA_KV_GROUPS, NSA_GROUP_HEADS, NSA_HEAD_DIM
    QT, KT, WT, W, SB = NSA_QT, NSA_KT, NSA_WT, NSA_WINDOW, NSA_SEL_BLOCK
    nsb, nc, nkt, nseg = S // SB, S // NSA_CMP_STRIDE, S // KT, max(S // QT, W // QT + 1)
    n_top = min(NSA_N_SELECT, nsb)

    def bias_of(dist):
        onehot = (rel_bucket(dist)[..., None] == jnp.arange(REL_BUCKETS)) & (dist >= 0)[..., None]
        return jnp.einsum('...b,bh->h...', onehot.astype(F32), rel_bias.astype(F32),
                          precision=lax.Precision.HIGHEST)

    bc = bias_of(jnp.arange(S)[:, None] - (jnp.arange(nc) * NSA_CMP_STRIDE + NSA_CMP_LEN - 1)[None, :])
    tb = bias_of(QT * jnp.arange(nseg)[:, None, None] + jnp.arange(QT)[None, :, None] - jnp.arange(QT)[None, None, :])
    cmp_start = np.arange(nc) * NSA_CMP_STRIDE
    cmp_end = cmp_start + NSA_CMP_LEN - 1
    sel_start = np.arange(nsb) * SB
    cover_t = ((cmp_start[None, :] < sel_start[:, None] + SB) & (cmp_end[None, :] >= sel_start[:, None])
               & (np.arange(nc)[None, :] < nc - 1))
    cover_t = jnp.asarray(cover_t, BF16)
    e = (np.arange(128)[None, :, None]
         == (np.arange(nkt)[:, None, None] * (KT // SB) + np.arange(KT)[None, None, :] // SB))
    e = jnp.asarray(e, BF16)

    kern = functools.partial(_nsa_kernel, n_top=n_top)
    return pl.pallas_call(
        kern,
        grid=(B, G, S // QT),
        in_specs=[
            pl.BlockSpec((1, QT, HG * dh), lambda b, g, i: (b, i, g)),
            pl.BlockSpec((1, QT, 128), lambda b, g, i: (b, i, gl_blk + g)),
            pl.BlockSpec((1, 1, nc, dh), lambda b, g, i: (b, g, 0, 0)),
            pl.BlockSpec((1, 1, nc, dh), lambda b, g, i: (b, g, 0, 0)),
            pl.BlockSpec((HG, QT, nc), lambda b, g, i: (g, i, 0)),
            pl.BlockSpec((nsb, nc), lambda b, g, i: (0, 0)),
            pl.BlockSpec((1, S, dh), lambda b, g, i: (b, 0, g)),
            pl.BlockSpec((1, S, dh), lambda b, g, i: (b, 0, g)),
            pl.BlockSpec((1, S + W, dh), lambda b, g, i: (b, 0, g)),
            pl.BlockSpec((1, S + W, dh), lambda b, g, i: (b, 0, g)),
            pl.BlockSpec((HG, nseg, QT, QT), lambda b, g, i: (g, 0, 0, 0)),
            pl.BlockSpec((nkt, 128, KT), lambda b, g, i: (0, 0, 0)),
            pl.BlockSpec((1, dh), lambda b, g, i: (0, 0)),
        ],
        out_specs=pl.BlockSpec((1, QT, HG * dh), lambda b, g, i: (b, i, g)),
        out_shape=jax.ShapeDtypeStruct((B, S, NSA_WIDTH), BF16),
        scratch_shapes=[pltpu.VMEM((nsb, QT), F32)],
        compiler_params=_cparams("parallel", "parallel", "arbitrary"),
        name="nsa_attention",
    )(proj, proj, kc, vc, bc, cover_t, ks, vs, kw, vw, tb, e, q_gain.reshape(1, dh))


def _nsa_compress_kernel(x_ref, pos_ref, w1_ref, w2_ref, kg_ref, o_ref):
    stride, dh = NSA_CMP_STRIDE, NSA_HEAD_DIM
    n_chunks = x_ref.shape[1] // stride
    hid = w1_ref.shape[2]
    h_lo = jnp.zeros((n_chunks, hid), F32)
    h_hi = jnp.zeros((n_chunks, hid), F32)
    for p in range(stride):
        xp = x_ref[0, pl.ds(p, n_chunks, stride=stride), :]
        h_lo = h_lo + jnp.dot((xp + pos_ref[0, p:p + 1, :]).astype(BF16), w1_ref[0, p * dh:(p + 1) * dh, :],
                              preferred_element_type=F32)
        h_hi = h_hi + jnp.dot((xp + pos_ref[0, stride + p:stride + p + 1, :]).astype(BF16),
                              w1_ref[0, (stride + p) * dh:(stride + p + 1) * dh, :], preferred_element_type=F32)
    hidden = h_lo + jnp.concatenate([h_hi[1:], jnp.zeros((1, hid), F32)], axis=0)
    out = jnp.dot((hidden * jax.nn.sigmoid(hidden)).astype(BF16), w2_ref[0], preferred_element_type=F32)
    out = jnp.where(pl.program_id(2) == 0, _rms_rows(out, kg_ref[...]), out)
    row = lax.broadcasted_iota(jnp.int32, out.shape, 0)
    o_ref[0, 0, 0] = jnp.where(row < n_chunks - 1, out, 0.0).astype(o_ref.dtype)


def nsa_compress(proj, col_blk, cmp_pos, cmp_w1, cmp_w2, k_gain):
    B, S, _ = proj.shape
    G, dh = NSA_KV_GROUPS, NSA_HEAD_DIM
    nc = S // NSA_CMP_STRIDE
    hid = cmp_w1.shape[-1]
    return pl.pallas_call(
        _nsa_compress_kernel,
        grid=(B, G, 2),
        in_specs=[
            pl.BlockSpec((1, S, dh), lambda b, g, w: (b, 0, col_blk + w * G + g)),
            pl.BlockSpec((1, NSA_CMP_LEN, dh), lambda b, g, w: (w, 0, 0)),
            pl.BlockSpec((1, NSA_CMP_LEN * dh, hid), lambda b, g, w: (w, 0, 0)),
            pl.BlockSpec((1, hid, dh), lambda b, g, w: (w, 0, 0)),
            pl.BlockSpec((1, dh), lambda b, g, w: (0, 0)),
        ],
        out_specs=pl.BlockSpec((1, 1, 1, nc, dh), lambda b, g, w: (w, b, g, 0, 0)),
        out_shape=jax.ShapeDtypeStruct((2, B, G, nc, dh), BF16),
        compiler_params=_cparams("parallel", "parallel", "arbitrary"),
        name="nsa_compress",
    )(proj, cmp_pos, cmp_w1.astype(BF16), cmp_w2.astype(BF16), k_gain.reshape(1, dh))


NSA_PREP_TM = NSA_WINDOW


def _nsa_kv_kernel(ks_ref, vs_ref, kw_ref, vw_ref, gs_ref, gw_ref, kso_ref, vso_ref, kwo_ref, vwo_ref):
    G, dh = NSA_KV_GROUPS, NSA_HEAD_DIM
    i = pl.program_id(1)
    n_in = pl.num_programs(1) - 1

    def normed(ref, gain_ref):
        x = ref[0]
        return jnp.concatenate([_rms_rows(x[:, g * dh:(g + 1) * dh], gain_ref[...]) for g in range(G)], axis=1)

    @pl.when(i < n_in)
    def _():
        kso_ref[0] = normed(ks_ref, gs_ref).astype(kso_ref.dtype)
        vso_ref[0] = vs_ref[0].astype(vso_ref.dtype)

    @pl.when(i == 0)
    def _():
        kwo_ref[0] = jnp.zeros(kwo_ref.shape[1:], kwo_ref.dtype)
        vwo_ref[0] = jnp.zeros(vwo_ref.shape[1:], vwo_ref.dtype)

    @pl.when(i > 0)
    def _():
        kwo_ref[0] = normed(kw_ref, gw_ref).astype(kwo_ref.dtype)
        vwo_ref[0] = vw_ref[0].astype(vwo_ref.dtype)


def nsa_kv_prep(proj, col_blk, k_gain):
    B, S, _ = proj.shape
    tm, kvw, W = NSA_PREP_TM, NSA_KV_WIDTH, NSA_WINDOW
    n_in = S // tm
    cur = lambda i: jnp.minimum(i, n_in - 1)
    prev = lambda i: jnp.maximum(i - 1, 0)
    return pl.pallas_call(
        _nsa_kv_kernel,
        grid=(B, n_in + 1),
        in_specs=[
            pl.BlockSpec((1, tm, kvw), lambda b, i: (b, cur(i), col_blk)),
            pl.BlockSpec((1, tm, kvw), lambda b, i: (b, cur(i), col_blk + 1)),
            pl.BlockSpec((1, tm, kvw), lambda b, i: (b, prev(i), col_blk + 2)),
            pl.BlockSpec((1, tm, kvw), lambda b, i: (b, prev(i), col_blk + 3)),
            pl.BlockSpec((1, NSA_HEAD_DIM), lambda b, i: (0, 0)),
            pl.BlockSpec((1, NSA_HEAD_DIM), lambda b, i: (0, 0)),
        ],
        out_specs=[
            pl.BlockSpec((1, tm, kvw), lambda b, i: (b, cur(i), 0)),
            pl.BlockSpec((1, tm, kvw), lambda b, i: (b, cur(i), 0)),
            pl.BlockSpec((1, tm, kvw), lambda b, i: (b, i, 0)),
            pl.BlockSpec((1, tm, kvw), lambda b, i: (b, i, 0)),
        ],
        out_shape=[jax.ShapeDtypeStruct((B, S, kvw), BF16), jax.ShapeDtypeStruct((B, S, kvw), BF16),
                   jax.ShapeDtypeStruct((B, S + W, kvw), BF16), jax.ShapeDtypeStruct((B, S + W, kvw), BF16)],
        compiler_params=_cparams("parallel", "arbitrary"),
        name="nsa_kv_prep",
    )(proj, proj, proj, proj, k_gain[1:2], k_gain[2:3])


RET_HB = 2


def _rotate_half(x, cos, sin):
    half = x.shape[-1] // 2
    x1, x2 = x[:, :half], x[:, half:]
    return jnp.concatenate([x1 * cos - x2 * sin, x1 * sin + x2 * cos], axis=-1)


def _retention_kernel(q_ref, k_ref, v_ref, g_ref, cos_ref, sin_ref, din_ref, qd_ref, kd_ref, cd_ref, gn_ref,
                      o_ref, s_scr):
    @pl.when(pl.program_id(2) == 0)
    def _():
        s_scr[...] = jnp.zeros_like(s_scr)

    cos, sin = cos_ref[...], sin_ref[...]
    heads = range(RET_HB)
    sl = [slice(h * RET_DK, (h + 1) * RET_DK) for h in heads]
    q = [_rotate_half(q_ref[0, :, sl[h]], cos, sin) * (RET_DK ** -0.5) for h in heads]
    k = [_rotate_half(k_ref[0, :, sl[h]], cos, sin) for h in heads]
    v16 = [v_ref[0, :, sl[h]].astype(BF16) for h in heads]
    inner = [_dot_nt(q[h].astype(BF16), k[h].astype(BF16)) * din_ref[h] for h in heads]
    s = [s_scr[h] for h in heads]
    o = [jnp.dot(inner[h].astype(BF16), v16[h], preferred_element_type=F32)
         + jnp.dot((q[h] * qd_ref[h]).astype(BF16), s[h].astype(BF16), preferred_element_type=F32) for h in heads]
    for h in heads:
        s_scr[h] = s[h] * cd_ref[h] + _dot_tn((k[h] * kd_ref[h]).astype(BF16), v16[h])
    outs = []
    for h in heads:
        mu = jnp.mean(o[h], axis=-1, keepdims=True)
        d = o[h] - mu
        var = jnp.mean(d * d, axis=-1, keepdims=True)
        gate = g_ref[0, :, sl[h]]
        outs.append((gate * jax.nn.sigmoid(gate)
                     * (d * lax.rsqrt(var + RET_GN_EPS) * gn_ref[:, sl[h]])).astype(o_ref.dtype))
    o_ref[0] = jnp.concatenate(outs, axis=-1)


def retention_pallas(proj, q_blk, gn_gain):
    B, S, _ = proj.shape
    H, dk, dv, C = RET_HEADS, RET_DK, RET_DV, RET_CHUNK
    N = S // C
    half = dk // 2
    inv = RET_ROPE_BASE ** (-jnp.arange(half, dtype=F32) / half)
    ang = jnp.arange(S).astype(F32)[:, None] * inv[None, :]
    cos, sin = jnp.cos(ang), jnp.sin(ang)
    log_gamma = jnp.log(1.0 - 2.0 ** (-5.0 - jnp.arange(H, dtype=F32)))
    idx = jnp.arange(C, dtype=F32)
    rel = idx[:, None] - idx[None, :]
    decay_in = jnp.where(rel >= 0, jnp.exp(log_gamma[:, None, None] * jnp.maximum(rel, 0.0)), 0.0)
    q_decay = jnp.exp(log_gamma[:, None] * (idx + 1.0))[..., None]
    k_decay = jnp.exp(log_gamma[:, None] * (C - 1.0 - idx))[..., None]
    chunk_decay = jnp.exp(log_gamma * C)[:, None, None]
    hb = RET_HB
    HP = H // hb
    qb0 = q_blk // hb
    return pl.pallas_call(
        _retention_kernel,
        grid=(B, HP, N),
        in_specs=[
            pl.BlockSpec((1, C, hb * dk), lambda b, p, n: (b, n, qb0 + p)),
            pl.BlockSpec((1, C, hb * dk), lambda b, p, n: (b, n, qb0 + HP + p)),
            pl.BlockSpec((1, C, hb * dv), lambda b, p, n: (b, n, qb0 + 2 * HP + p)),
            pl.BlockSpec((1, C, hb * dv), lambda b, p, n: (b, n, qb0 + 3 * HP + p)),
            pl.BlockSpec((C, half), lambda b, p, n: (n, 0)),
            pl.BlockSpec((C, half), lambda b, p, n: (n, 0)),
            pl.BlockSpec((hb, C, C), lambda b, p, n: (p, 0, 0)),
            pl.BlockSpec((hb, C, 1), lambda b, p, n: (p, 0, 0)),
            pl.BlockSpec((hb, C, 1), lambda b, p, n: (p, 0, 0)),
            pl.BlockSpec((hb, 1, 1), lambda b, p, n: (p, 0, 0)),
            pl.BlockSpec((1, hb * dv), lambda b, p, n: (0, p)),
        ],
        out_specs=pl.BlockSpec((1, C, hb * dv), lambda b, p, n: (b, n, p)),
        out_shape=jax.ShapeDtypeStruct((B, S, H * dv), BF16),
        scratch_shapes=[pltpu.VMEM((hb, dk, dv), F32)],
        compiler_params=_cparams("parallel", "parallel", "arbitrary"),
        name="retention",
    )(proj, proj, proj, proj, cos, sin, decay_in, q_decay, k_decay, chunk_decay, gn_gain.reshape(1, H * dv))


GDN_HB = 32
GDN_HALO = 8


def _conv_silu(x_ref, halo_ref, keep, w, stage_ref):
    c = x_ref.shape[1]
    stage_ref[0:GDN_HALO, :] = halo_ref[0] * keep
    stage_ref[GDN_HALO:GDN_HALO + c, :] = x_ref[0]
    y = w[GDN_CONV - 1:GDN_CONV] * x_ref[0]
    for j in range(GDN_CONV - 1):
        off = GDN_HALO - (GDN_CONV - 1) + j
        y = y + w[j:j + 1] * stage_ref[off:off + c, :]
    return y * jax.nn.sigmoid(y)


def _l2_rows(x):
    return x * lax.rsqrt(jnp.sum(x * x, axis=-1, keepdims=True) + NORM_EPS)


def _gdn_kernel(q_ref, k_ref, v_ref, qh_ref, kh_ref, vh_ref, wq_ref, wk_ref, wv_ref, z_ref,
                ba_ref, alog_ref, dtb_ref, ng_ref, o_ref, s_scr, qst_scr, kst_scr, vst_scr):
    C, dh, hb = GDN_CHUNK, GDN_HEAD_DIM, GDN_HB
    rep = GDN_V_HEADS // GDN_QK_HEADS
    first = pl.program_id(2) == 0

    @pl.when(first)
    def _():
        s_scr[...] = jnp.zeros_like(s_scr)

    keep = jnp.where(first, 0.0, 1.0)
    qc = _conv_silu(q_ref, qh_ref, keep, wq_ref[...], qst_scr)
    kc = _conv_silu(k_ref, kh_ref, keep, wk_ref[...], kst_scr)
    vc = _conv_silu(v_ref, vh_ref, keep, wv_ref[...], vst_scr)

    ri = lax.broadcasted_iota(jnp.int32, (C, C), 0)
    ci = lax.broadcasted_iota(jnp.int32, (C, C), 1)
    causal = ri >= ci
    strict = ri > ci
    ba = ba_ref[0]
    bcol = jax.nn.sigmoid(ba[:, :hb])
    g = -jnp.exp(alog_ref[0]) * jax.nn.softplus(ba[:, hb:2 * hb] + dtb_ref[0])
    gcol = jnp.dot(jnp.where(causal, 1.0, 0.0), g, preferred_element_type=F32, precision=lax.Precision.HIGHEST)
    grow = lax.dot_general(g, jnp.where(ri <= ci, 1.0, 0.0), (((0,), (0,)), ((), ())),
                           preferred_element_type=F32, precision=lax.Precision.HIGHEST)
    heads = range(hb)
    qs, ks, grams = [], [], []
    for hq in range(hb // rep):
        qh = _l2_rows(qc[:, hq * dh:(hq + 1) * dh]) * (dh ** -0.5)
        kh = _l2_rows(kc[:, hq * dh:(hq + 1) * dh])
        k16 = kh.astype(BF16)
        qs.append(qh)
        ks.append(kh)
        grams.append(_dot_nt(jnp.concatenate([qh.astype(BF16), k16], axis=0), k16))
    beta = [bcol[:, h:h + 1] for h in heads]
    gc = [gcol[:, h:h + 1] for h in heads]
    gr = [grow[h:h + 1, :] for h in heads]
    g_last = [gr[h][:, C - 1:C] for h in heads]
    eg = [jnp.exp(gc[h]) for h in heads]
    decay = [jnp.where(causal, jnp.exp(jnp.minimum(gc[h] - gr[h], 0.0)), 0.0) for h in heads]
    attn = [(grams[h // rep][:C] * decay[h]).astype(BF16) for h in heads]
    nm = [jnp.where(strict, grams[h // rep][C:] * decay[h] * (-beta[h]), 0.0) for h in heads]
    m = [nm[h].astype(BF16) for h in heads]
    for _ in range(int(math.log2(C)) - 1):
        mf = [jnp.dot(m[h], m[h], preferred_element_type=F32) for h in heads]
        m = [mf[h].astype(BF16) for h in heads]
        nm = [nm[h] + mf[h] + jnp.dot(m[h], nm[h].astype(BF16), preferred_element_type=F32) for h in heads]
    x = [jnp.concatenate([vc[:, h * dh:(h + 1) * dh] * beta[h], ks[h // rep] * (beta[h] * eg[h])], axis=1)
         for h in heads]
    x = [x[h] + jnp.dot(nm[h].astype(BF16), x[h].astype(BF16), preferred_element_type=F32) for h in heads]
    s = [s_scr[h] for h in heads]
    ws = [jnp.dot(jnp.concatenate([x[h][:, dh:].astype(BF16), (qs[h // rep] * eg[h]).astype(BF16)], axis=0),
                  s[h].astype(BF16), preferred_element_type=F32) for h in heads]
    vn = [(x[h][:, :dh] - ws[h][:C]).astype(BF16) for h in heads]
    o = [ws[h][C:] + jnp.dot(attn[h], vn[h], preferred_element_type=F32) for h in heads]
    for h in heads:
        k_dec = (ks[h // rep] * jnp.exp(g_last[h] - gc[h])).astype(BF16)
        s_scr[h] = s[h] * jnp.exp(g_last[h]) + _dot_tn(k_dec, vn[h])
    outs = []
    for h in heads:
        zh = z_ref[0, :, h * dh:(h + 1) * dh]
        outs.append((_rms_rows(o[h], ng_ref[...]) * (zh * jax.nn.sigmoid(zh))).astype(o_ref.dtype))
    o_ref[0] = jnp.concatenate(outs, axis=1)


_GDN_GATE_COL0 = GDN_CONV_CH + GDN_V_WIDTH
_GDN_COLS = _round_up(_GDN_GATE_COL0 + GDN_V_HEADS // GDN_HB * 128, 512)


def _gdn_w_in_layout(w_in):
    b0 = _GDN_GATE_COL0
    a0 = b0 + GDN_V_HEADS
    parts = [w_in[:, :b0]]
    for p in range(GDN_V_HEADS // GDN_HB):
        parts.append(w_in[:, b0 + p * GDN_HB:b0 + (p + 1) * GDN_HB])
        parts.append(jnp.pad(w_in[:, a0 + p * GDN_HB:a0 + (p + 1) * GDN_HB], ((0, 0), (0, 128 - 2 * GDN_HB))))
    w = jnp.concatenate(parts, axis=1)
    return jnp.pad(w, ((0, 0), (0, _GDN_COLS - w.shape[1]))).astype(BF16)


def gdn_delta_rule(proj, conv_w, a_log, dt_bias, norm_gain):
    B, S, _ = proj.shape
    C, dh, hb, H = GDN_CHUNK, GDN_HEAD_DIM, GDN_HB, GDN_V_HEADS
    rep = GDN_V_HEADS // GDN_QK_HEADS
    N, HP = S // C, H // hb
    wqk, wv = hb // rep * dh, hb * dh
    kb0, vb0, zb0 = GDN_QK_WIDTH // wqk, 2 * GDN_QK_WIDTH // wv, GDN_CONV_CH // wv
    gb0 = _GDN_GATE_COL0 // 128
    hr = C // GDN_HALO
    halo = lambda n: jnp.maximum(n * hr - 1, 0)
    return pl.pallas_call(
        _gdn_kernel,
        grid=(B, HP, N),
        in_specs=[
            pl.BlockSpec((1, C, wqk), lambda b, p, n: (b, n, p)),
            pl.BlockSpec((1, C, wqk), lambda b, p, n: (b, n, kb0 + p)),
            pl.BlockSpec((1, C, wv), lambda b, p, n: (b, n, vb0 + p)),
            pl.BlockSpec((1, GDN_HALO, wqk), lambda b, p, n: (b, halo(n), p)),
            pl.BlockSpec((1, GDN_HALO, wqk), lambda b, p, n: (b, halo(n), kb0 + p)),
            pl.BlockSpec((1, GDN_HALO, wv), lambda b, p, n: (b, halo(n), vb0 + p)),
            pl.BlockSpec((GDN_CONV, wqk), lambda b, p, n: (0, p)),
            pl.BlockSpec((GDN_CONV, wqk), lambda b, p, n: (0, kb0 + p)),
            pl.BlockSpec((GDN_CONV, wv), lambda b, p, n: (0, vb0 + p)),
            pl.BlockSpec((1, C, wv), lambda b, p, n: (b, n, zb0 + p)),
            pl.BlockSpec((1, C, 128), lambda b, p, n: (b, n, gb0 + p)),
            pl.BlockSpec((1, 1, hb), lambda b, p, n: (p, 0, 0)),
            pl.BlockSpec((1, 1, hb), lambda b, p, n: (p, 0, 0)),
            pl.BlockSpec((1, dh), lambda b, p, n: (0, 0)),
        ],
        out_specs=pl.BlockSpec((1, C, wv), lambda b, p, n: (b, n, p)),
        out_shape=jax.ShapeDtypeStruct((B, S, H * dh), BF16),
        scratch_shapes=[pltpu.VMEM((hb, dh, dh), F32), pltpu.VMEM((GDN_HALO + C, wqk), F32),
                        pltpu.VMEM((GDN_HALO + C, wqk), F32), pltpu.VMEM((GDN_HALO + C, wv), F32)],
        compiler_params=_cparams("parallel", "parallel", "arbitrary"),
        name="gdn_delta_rule",
    )(proj, proj, proj, proj, proj, proj, conv_w, conv_w, conv_w, proj, proj,
      a_log.astype(F32).reshape(HP, 1, hb), dt_bias.astype(F32).reshape(HP, 1, hb), norm_gain.reshape(1, dh))


_HYB_NSA_COLS = NSA_WIDTH + 6 * NSA_KV_WIDTH
_HYB_RET_COLS = 2 * RET_HEADS * RET_DK + 2 * RET_HEADS * RET_DV
_HYB_GATE_COL0 = _HYB_NSA_COLS + _HYB_RET_COLS
_HYB_COLS = _round_up(_HYB_GATE_COL0 + NSA_KV_GROUPS * 128, 512)


def _hybrid_w_in_layout(w_in):
    gate0 = _HYB_NSA_COLS
    ret0 = gate0 + 3 * NSA_HEADS
    per_group = 3 * NSA_GROUP_HEADS
    parts = [w_in[:, :gate0], w_in[:, ret0:ret0 + _HYB_RET_COLS]]
    for g in range(NSA_KV_GROUPS):
        parts.append(jnp.pad(w_in[:, gate0 + g * per_group:gate0 + (g + 1) * per_group],
                             ((0, 0), (0, 128 - per_group))))
    w = jnp.concatenate(parts, axis=1)
    return jnp.pad(w, ((0, 0), (0, _HYB_COLS - w.shape[1]))).astype(BF16)


def hybrid_mixer(h, ln_gain, w_in, w_out, q_gain, k_gain, cmp_pos, cmp_w1, cmp_w2, gn_gain, rel_bias):
    B, S, D = h.shape
    proj = norm_matmul(h.reshape(B * S, D), ln_gain, _hybrid_w_in_layout(w_in)).reshape(B, S, _HYB_COLS)
    cmp = nsa_compress(proj, NSA_WIDTH // NSA_HEAD_DIM, cmp_pos, cmp_w1, cmp_w2, k_gain[0])
    ksn, vsb, kwn, vwb = nsa_kv_prep(proj, (NSA_WIDTH + 2 * NSA_KV_WIDTH) // NSA_KV_WIDTH, k_gain)
    a_out = nsa_attention_pallas(proj, _HYB_GATE_COL0 // 128, cmp[0], cmp[1], ksn, vsb, kwn, vwb, q_gain, rel_bias)
    b_out = retention_pallas(proj, _HYB_NSA_COLS // RET_DK, gn_gain)
    w_out = w_out.astype(BF16)
    return matmul2_residual(a_out.reshape(B * S, -1), b_out.reshape(B * S, -1), w_out,
                            h.reshape(B * S, D)).reshape(B, S, D)


def gdn_mixer(h, ln_gain, w_in, conv_w, a_log, dt_bias, norm_gain, w_out):
    B, S, D = h.shape
    proj = norm_matmul(h.reshape(B * S, D), ln_gain, _gdn_w_in_layout(w_in)).reshape(B, S, _GDN_COLS)
    o = gdn_delta_rule(proj, conv_w, a_log, dt_bias, norm_gain)
    return matmul_residual(o.reshape(B * S, GDN_V_WIDTH), w_out.astype(BF16), h.reshape(B * S, D)).reshape(B, S, D)


def memory_kv(mem, mem_gain, wkv, k_gain):
    B, M, D = mem.shape
    kv = norm_matmul(mem.reshape(B * M, D), mem_gain, wkv.astype(BF16))
    k, v = jnp.split(kv.reshape(B, M, 2 * XA_WIDTH), 2, axis=-1)
    k = rms_norm(k.reshape(B, M, XA_HEADS, XA_HEAD_DIM), k_gain).reshape(B, M, XA_WIDTH)
    return k.astype(BF16), v.astype(BF16)


def kernel(x, mem, rel_bias, ln_mix, ln_mem, ln_ffn, hyb_w_in, hyb_w_out, nsa_q_gain, nsa_k_gain,
           nsa_cmp_pos, nsa_cmp_w1, nsa_cmp_w2, ret_gn_gain, gdn_w_in, gdn_conv_w, gdn_a_log,
           gdn_dt_bias, gdn_norm_gain, gdn_w_out, xa_wq, xa_wkv, xa_q_gain, xa_k_gain, xa_mem_gain,
           xa_wo, ffn_w_in, ffn_w_out):
    B, S, D = x.shape
    h = x
    ffn_w_in, ffn_w_out, gdn_w_out = cast_bf16(ffn_w_in), cast_bf16(ffn_w_out), cast_bf16(gdn_w_out)
    for layer in range(DEPTH):
        if layer % 2 == 0:
            e = layer // 2
            h = hybrid_mixer(h, ln_mix[layer], hyb_w_in[e], hyb_w_out[e], nsa_q_gain[e], nsa_k_gain[e],
                             nsa_cmp_pos[e], nsa_cmp_w1[e], nsa_cmp_w2[e], ret_gn_gain[e], rel_bias)
        else:
            o = layer // 2
            h = gdn_mixer(h, ln_mix[layer], gdn_w_in[o], gdn_conv_w[o], gdn_a_log[o], gdn_dt_bias[o],
                          gdn_norm_gain[o], gdn_w_out[o])
        k_mem, v_mem = memory_kv(mem, xa_mem_gain[layer], xa_wkv[layer], xa_k_gain[layer])
        h = xattn_residual(h, ln_mem[layer], xa_wq[layer].astype(BF16), xa_q_gain[layer], k_mem, v_mem,
                           xa_wo[layer].astype(BF16))
        h = ffn_residual(h.reshape(B * S, D), ln_ffn[layer], ffn_w_in[layer].astype(BF16),
                         ffn_w_out[layer].astype(BF16)).reshape(B, S, D)
    return h
```

```python
import functools
import math

import jax
import jax.numpy as jnp
import numpy as np
from jax import lax
from jax.experimental import pallas as pl
from jax.experimental.pallas import tpu as pltpu

F32 = jnp.float32
BF16 = jnp.bfloat16

D_MODEL = 2048
DEPTH = 4
NORM_EPS = 1e-6
NEG_INF = -1e30

NSA_HEADS = 8
NSA_KV_GROUPS = 2
NSA_GROUP_HEADS = NSA_HEADS // NSA_KV_GROUPS
NSA_HEAD_DIM = 128
NSA_CMP_LEN = 32
NSA_CMP_STRIDE = 16
NSA_SEL_BLOCK = 64
NSA_N_SELECT = 16
NSA_WINDOW = 512
NSA_Q_BLOCK = 64
NSA_FORCE_SCORE = 1e6
NSA_WIDTH = NSA_HEADS * NSA_HEAD_DIM
NSA_KV_WIDTH = NSA_KV_GROUPS * NSA_HEAD_DIM

RET_HEADS = 4
RET_DK = 256
RET_DV = 256
RET_CHUNK = 128
RET_ROPE_BASE = 10000.0
RET_GN_EPS = 1e-5

GDN_QK_HEADS = 16
GDN_V_HEADS = 32
GDN_HEAD_DIM = 128
GDN_CONV = 4
GDN_CHUNK = 64
GDN_QK_WIDTH = GDN_QK_HEADS * GDN_HEAD_DIM
GDN_V_WIDTH = GDN_V_HEADS * GDN_HEAD_DIM
GDN_CONV_CH = 2 * GDN_QK_WIDTH + GDN_V_WIDTH

REL_BUCKETS = 32
REL_MAX_DIST = 1024

XA_HEADS = 4
XA_HEAD_DIM = 128
XA_WIDTH = XA_HEADS * XA_HEAD_DIM

V7X_VMEM_LIMIT_BYTES = 56 * 1024 * 1024


def _cparams(*sem):
    return pltpu.CompilerParams(dimension_semantics=sem, vmem_limit_bytes=V7X_VMEM_LIMIT_BYTES)


def _round_up(n, m):
    return -(-n // m) * m


def _rms_rows(x, gain):
    return x * lax.rsqrt(jnp.mean(x * x, axis=-1, keepdims=True) + NORM_EPS) * gain


def _cast_kernel(x_ref, o_ref):
    o_ref[...] = x_ref[...].astype(o_ref.dtype)


def cast_bf16(w, *, block_bytes=8 * 1024 * 1024):
    n = w.shape[-1]
    x = w.reshape(-1, n)
    rows = x.shape[0]
    tr = rows
    while tr * n * 4 > block_bytes and tr % 32 == 0:
        tr //= 2
    out = pl.pallas_call(
        _cast_kernel,
        grid=(rows // tr,),
        in_specs=[pl.BlockSpec((tr, n), lambda i: (i, 0))],
        out_specs=pl.BlockSpec((tr, n), lambda i: (i, 0)),
        out_shape=jax.ShapeDtypeStruct((rows, n), BF16),
        compiler_params=_cparams("parallel"),
        name="cast_bf16",
    )(x)
    return out.reshape(w.shape)


def _norm_matmul_kernel(x_ref, g_ref, w_ref, o_ref, xn_ref):
    @pl.when(pl.program_id(1) == 0)
    def _():
        xn_ref[...] = _rms_rows(x_ref[...], g_ref[...]).astype(BF16)

    o_ref[...] = jnp.dot(xn_ref[...], w_ref[...], preferred_element_type=F32).astype(o_ref.dtype)


def norm_matmul(x, gain, w, *, tm=1024, tn=512, out_dtype=F32):
    m, k = x.shape
    n = w.shape[1]
    tm = min(tm, m)
    return pl.pallas_call(
        _norm_matmul_kernel,
        grid=(m // tm, n // tn),
        in_specs=[pl.BlockSpec((tm, k), lambda i, j: (i, 0)),
                  pl.BlockSpec((1, k), lambda i, j: (0, 0)),
                  pl.BlockSpec((k, tn), lambda i, j: (0, j))],
        out_specs=pl.BlockSpec((tm, tn), lambda i, j: (i, j)),
        out_shape=jax.ShapeDtypeStruct((m, n), out_dtype),
        scratch_shapes=[pltpu.VMEM((tm, k), BF16)],
        compiler_params=_cparams("parallel", "arbitrary"),
        name="norm_matmul",
    )(x, gain.reshape(1, k), w)


def _matmul_res_kernel(x_ref, w_ref, r_ref, o_ref):
    o_ref[...] = r_ref[...] + jnp.dot(x_ref[...], w_ref[...], preferred_element_type=F32)


def matmul_residual(x, w, res, *, tm=1024, tn=512):
    m, k = x.shape
    n = w.shape[1]
    return pl.pallas_call(
        _matmul_res_kernel,
        grid=(m // tm, n // tn),
        in_specs=[pl.BlockSpec((tm, k), lambda i, j: (i, 0)),
                  pl.BlockSpec((k, tn), lambda i, j: (0, j)),
                  pl.BlockSpec((tm, tn), lambda i, j: (i, j))],
        out_specs=pl.BlockSpec((tm, tn), lambda i, j: (i, j)),
        out_shape=jax.ShapeDtypeStruct((m, n), F32),
        compiler_params=_cparams("parallel", "arbitrary"),
        name="matmul_residual",
    )(x, w, res)


def _matmul2_res_kernel(x1_ref, x2_ref, w1_ref, w2_ref, r_ref, o_ref):
    o_ref[...] = (r_ref[...] + jnp.dot(x1_ref[...], w1_ref[...], preferred_element_type=F32)
                  + jnp.dot(x2_ref[...], w2_ref[...], preferred_element_type=F32))


def matmul2_residual(x1, x2, w, res, *, tm=1024, tn=512):
    m, k1 = x1.shape
    n = w.shape[1]
    return pl.pallas_call(
        _matmul2_res_kernel,
        grid=(m // tm, n // tn),
        in_specs=[pl.BlockSpec((tm, k1), lambda i, j: (i, 0)),
                  pl.BlockSpec((tm, k1), lambda i, j: (i, 0)),
                  pl.BlockSpec((k1, tn), lambda i, j: (0, j)),
                  pl.BlockSpec((k1, tn), lambda i, j: (1, j)),
                  pl.BlockSpec((tm, tn), lambda i, j: (i, j))],
        out_specs=pl.BlockSpec((tm, tn), lambda i, j: (i, j)),
        out_shape=jax.ShapeDtypeStruct((m, n), F32),
        compiler_params=_cparams("parallel", "arbitrary"),
        name="matmul2_residual",
    )(x1, x2, w, w, res)


def _ffn_kernel(x_ref, g_ref, wg_ref, wu_ref, wo_ref, o_ref, xn_ref, acc_ref):
    j = pl.program_id(1)

    @pl.when(j == 0)
    def _():
        x = x_ref[...]
        xn_ref[...] = _rms_rows(x, g_ref[...]).astype(BF16)
        acc_ref[...] = x

    xn = xn_ref[...]
    gate = jnp.dot(xn, wg_ref[...], preferred_element_type=F32)
    up = jnp.dot(xn, wu_ref[...], preferred_element_type=F32)
    act = (gate * jax.nn.sigmoid(gate) * up).astype(BF16)
    acc_ref[...] += jnp.dot(act, wo_ref[...], preferred_element_type=F32)

    @pl.when(j == pl.num_programs(1) - 1)
    def _():
        o_ref[...] = acc_ref[...]


def ffn_residual(h, gain, w_in, w_out, *, tm=512, th=512):
    m, d = h.shape
    hidden = w_out.shape[0]
    nh = hidden // th
    return pl.pallas_call(
        _ffn_kernel,
        grid=(m // tm, nh),
        in_specs=[pl.BlockSpec((tm, d), lambda i, j: (i, 0)),
                  pl.BlockSpec((1, d), lambda i, j: (0, 0)),
                  pl.BlockSpec((d, th), lambda i, j: (0, j)),
                  pl.BlockSpec((d, th), lambda i, j: (0, j + nh)),
                  pl.BlockSpec((th, d), lambda i, j: (j, 0))],
        out_specs=pl.BlockSpec((tm, d), lambda i, j: (i, 0)),
        out_shape=jax.ShapeDtypeStruct((m, d), F32),
        scratch_shapes=[pltpu.VMEM((tm, d), BF16), pltpu.VMEM((tm, d), F32)],
        compiler_params=_cparams("parallel", "arbitrary"),
        name="ffn_residual",
    )(h, gain.reshape(1, d), w_in, w_in, w_out)


def _xattn_kernel(h_ref, g_ref, wq_ref, qg_ref, k_ref, v_ref, wo_ref, o_ref):
    x = h_ref[0]
    xn = _rms_rows(x, g_ref[...]).astype(BF16)
    q = jnp.dot(xn, wq_ref[...], preferred_element_type=F32)
    k = k_ref[0]
    v = v_ref[0]
    outs = []
    for hh in range(XA_HEADS):
        sl = slice(hh * XA_HEAD_DIM, (hh + 1) * XA_HEAD_DIM)
        qh = _rms_rows(q[:, sl], qg_ref[...]).astype(BF16)
        s = lax.dot_general(qh, k[:, sl], (((1,), (1,)), ((), ())), preferred_element_type=F32)
        s = s * (XA_HEAD_DIM ** -0.5)
        s = s - jnp.max(s, axis=-1, keepdims=True)
        p = jnp.exp(s)
        p = p / jnp.sum(p, axis=-1, keepdims=True)
        outs.append(jnp.dot(p.astype(BF16), v[:, sl], preferred_element_type=F32))
    o = jnp.concatenate(outs, axis=-1).astype(BF16)
    o_ref[0] = x + jnp.dot(o, wo_ref[...], preferred_element_type=F32)


def xattn_residual(h, gain, wq, q_gain, k, v, wo, *, tm=512):
    b, s, d = h.shape
    mt = k.shape[1]
    return pl.pallas_call(
        _xattn_kernel,
        grid=(b, s // tm),
        in_specs=[pl.BlockSpec((1, tm, d), lambda bi, i: (bi, i, 0)),
                  pl.BlockSpec((1, d), lambda bi, i: (0, 0)),
                  pl.BlockSpec((d, XA_WIDTH), lambda bi, i: (0, 0)),
                  pl.BlockSpec((1, XA_HEAD_DIM), lambda bi, i: (0, 0)),
                  pl.BlockSpec((1, mt, XA_WIDTH), lambda bi, i: (bi, 0, 0)),
                  pl.BlockSpec((1, mt, XA_WIDTH), lambda bi, i: (bi, 0, 0)),
                  pl.BlockSpec((XA_WIDTH, d), lambda bi, i: (0, 0))],
        out_specs=pl.BlockSpec((1, tm, d), lambda bi, i: (bi, i, 0)),
        out_shape=jax.ShapeDtypeStruct((b, s, d), F32),
        compiler_params=_cparams("parallel", "parallel"),
        name="xattn_residual",
    )(h, gain.reshape(1, d), wq, q_gain.reshape(1, XA_HEAD_DIM), k, v, wo)


def rms_norm(x, gain, eps=NORM_EPS):
    x32 = x.astype(F32)
    y = x32 * lax.rsqrt(jnp.mean(x32 * x32, axis=-1, keepdims=True) + eps)
    return (y * gain.astype(F32)).astype(x.dtype)


def rel_bucket(dist):
    dist = jnp.maximum(dist, 0)
    max_exact = REL_BUCKETS // 2
    scaled = (jnp.log(jnp.maximum(dist, max_exact).astype(F32) / max_exact)
              / math.log(REL_MAX_DIST / max_exact) * (REL_BUCKETS - max_exact))
    large = jnp.minimum(max_exact + scaled.astype(jnp.int32), REL_BUCKETS - 1)
    return jnp.where(dist < max_exact, dist, large)


NSA_QT = 128
NSA_KT = 512
NSA_WT = NSA_WINDOW + NSA_QT


def _dot_nt(a, b):
    return lax.dot_general(a, b, (((1,), (1,)), ((), ())), preferred_element_type=F32)


def _dot_tn(a, b):
    return lax.dot_general(a, b, (((0,), (0,)), ((), ())), preferred_element_type=F32)


def _bias_tile(tb_ref, h, first_slab, n_slabs):
    return jnp.concatenate([tb_ref[h, jnp.maximum(first_slab - j, 0)] for j in range(n_slabs)], axis=1)


def _nsa_kernel(q_ref, gl_ref, kc_ref, vc_ref, bc_ref, covt_ref, ks_ref, vs_ref, kw_ref, vw_ref,
                tb_ref, e_ref, qg_ref, o_ref, a_scr, *, n_top):
    QT, KT, WT, HG, dh = NSA_QT, NSA_KT, NSA_WT, NSA_GROUP_HEADS, NSA_HEAD_DIM
    nsb, nc = covt_ref.shape
    q0 = pl.program_id(2) * QT

    x = q_ref[0]
    qs = []
    for h in range(HG):
        xh = _rms_rows(x[:, h * dh:(h + 1) * dh], qg_ref[...]) * (dh ** -0.5)
        qs.append(xh.astype(BF16))
    qn = jnp.concatenate(qs, axis=0)

    sc = _dot_nt(qn, kc_ref[0, 0]) + bc_ref[...].reshape(HG * QT, nc)
    row = lax.broadcasted_iota(jnp.int32, (HG * QT, nc), 0)
    col = lax.broadcasted_iota(jnp.int32, (HG * QT, nc), 1)
    tq = q0 + (row & (QT - 1))
    mask_c = tq >= col * NSA_CMP_STRIDE + (NSA_CMP_LEN - 1)
    sc = jnp.where(mask_c, sc, NEG_INF)
    pc = jnp.where(mask_c, jnp.exp(sc - jnp.max(sc, axis=-1, keepdims=True)), 0.0)
    den = jnp.sum(pc, axis=-1, keepdims=True)
    pcb = (pc / jnp.maximum(den, 1e-30)).astype(BF16)
    o_c = jnp.dot(pcb, vc_ref[0, 0], preferred_element_type=F32)

    imp_all = _dot_nt(covt_ref[...], pcb)
    imp = imp_all[:, 0:QT]
    for h in range(1, HG):
        imp = imp + imp_all[:, h * QT:(h + 1) * QT]
    jj = lax.broadcasted_iota(jnp.int32, (nsb, QT), 0)
    cur = (q0 + lax.broadcasted_iota(jnp.int32, (nsb, QT), 1)) // NSA_SEL_BLOCK
    forced = (jj == 0) | (jj == cur) | (jj == cur - 1)
    a = jnp.where(forced, NSA_FORCE_SCORE, imp)
    a = jnp.where(jj > cur, -NSA_FORCE_SCORE, a)
    a_scr[...] = a

    def rank_body(i, rank):
        r = a_scr[pl.ds(i, 1), :]
        return rank + jnp.where((r > a) | ((r == a) & (jj > i)), 1.0, 0.0)

    n_live = jnp.minimum((q0 + QT - 1) // NSA_SEL_BLOCK + 1, nsb)
    rank = lax.fori_loop(0, n_live, rank_body, jnp.zeros((nsb, QT), F32))
    sel_t = jnp.where(rank < n_top, 1.0, 0.0)
    if nsb < 128:
        sel_t = jnp.concatenate([sel_t, jnp.zeros((128 - nsb, QT), F32)], axis=0)
    sel = sel_t.T.astype(BF16)

    cmr = (lax.broadcasted_iota(jnp.int32, (QT, KT), 1) - lax.broadcasted_iota(jnp.int32, (QT, KT), 0))

    def sel_body(kt, carry):
        m_i, l_i, acc = carry
        k0 = pl.multiple_of(kt * KT, KT)
        k = ks_ref[0, pl.ds(k0, KT), :]
        v = vs_ref[0, pl.ds(k0, KT), :]
        s = _dot_nt(qn, k)
        delta = q0 - k0
        visible = (jnp.dot(sel, e_ref[kt], preferred_element_type=F32) > 0.5) & (cmr <= delta)
        seg = delta // QT
        parts = []
        for h in range(HG):
            bias = _bias_tile(tb_ref, h, seg, KT // QT)
            parts.append(jnp.where(visible, s[h * QT:(h + 1) * QT] + bias, NEG_INF))
        s = jnp.concatenate(parts, axis=0)
        m_new = jnp.maximum(m_i, jnp.max(s, axis=-1, keepdims=True))
        alpha = jnp.exp(m_i - m_new)
        p = jnp.exp(s - m_new)
        l_new = alpha * l_i + jnp.sum(p, axis=-1, keepdims=True)
        acc = alpha * acc + jnp.dot(p.astype(BF16), v, preferred_element_type=F32)
        return m_new, l_new, acc

    init = (jnp.full((HG * QT, 1), NEG_INF, F32), jnp.zeros((HG * QT, 1), F32), jnp.zeros((HG * QT, dh), F32))
    _, l_s, acc_s = lax.fori_loop(0, q0 // KT + 1, sel_body, init)
    o_s = acc_s / l_s

    w0 = pl.multiple_of(q0, QT)
    kwin = kw_ref[0, pl.ds(w0, WT), :]
    vwin = vw_ref[0, pl.ds(w0, WT), :]
    sw = _dot_nt(qn, kwin)
    cw = lax.broadcasted_iota(jnp.int32, (QT, WT), 1)
    dist = NSA_WINDOW + lax.broadcasted_iota(jnp.int32, (QT, WT), 0) - cw
    vis_w = (dist >= 0) & (dist < NSA_WINDOW) & (cw >= NSA_WINDOW - q0)
    parts = []
    for h in range(HG):
        bias = _bias_tile(tb_ref, h, NSA_WINDOW // QT, WT // QT)
        parts.append(jnp.where(vis_w, sw[h * QT:(h + 1) * QT] + bias, NEG_INF))
    sw = jnp.concatenate(parts, axis=0)
    pw = jnp.exp(sw - jnp.max(sw, axis=-1, keepdims=True))
    o_w = jnp.dot(pw.astype(BF16), vwin, preferred_element_type=F32) / jnp.sum(pw, axis=-1, keepdims=True)

    gates = jax.nn.sigmoid(gl_ref[0][:, :3 * HG])
    outs = []
    for h in range(HG):
        rs = slice(h * QT, (h + 1) * QT)
        outs.append(gates[:, 3 * h:3 * h + 1] * o_c[rs] + gates[:, 3 * h + 1:3 * h + 2] * o_s[rs]
                    + gates[:, 3 * h + 2:3 * h + 3] * o_w[rs])
    o_ref[0] = jnp.concatenate(outs, axis=-1).astype(o_ref.dtype)


def nsa_attention_pallas(proj, gl_blk, kc, vc, ks, vs, kw, vw, q_gain, rel_bias):
    B, S, _ = proj.shape
    G, HG, dh = NSA_KV_GROUPS, NSA_GROUP_HEADS, NSA_HEAD_DIM
    QT, KT, WT, W, SB = NSA_QT, NSA_KT, NSA_WT, NSA_WINDOW, NSA_SEL_BLOCK
    nsb, nc, nkt, nseg = S // SB, S // NSA_CMP_STRIDE, S // KT, max(S // QT, W // QT + 1)
    n_top = min(NSA_N_SELECT, nsb)

    def bias_of(dist):
        onehot = (rel_bucket(dist)[..., None] == jnp.arange(REL_BUCKETS)) & (dist >= 0)[..., None]
        return jnp.einsum('...b,bh->h...', onehot.astype(F32), rel_bias.astype(F32),
                          precision=lax.Precision.HIGHEST)

    bc = bias_of(jnp.arange(S)[:, None] - (jnp.arange(nc) * NSA_CMP_STRIDE + NSA_CMP_LEN - 1)[None, :])
    tb = bias_of(QT * jnp.arange(nseg)[:, None, None] + jnp.arange(QT)[None, :, None] - jnp.arange(QT)[None, None, :])
    cmp_start = np.arange(nc) * NSA_CMP_STRIDE
    cmp_end = cmp_start + NSA_CMP_LEN - 1
    sel_start = np.arange(nsb) * SB
    cover_t = ((cmp_start[None, :] < sel_start[:, None] + SB) & (cmp_end[None, :] >= sel_start[:, None])
               & (np.arange(nc)[None, :] < nc - 1))
    cover_t = jnp.asarray(cover_t, BF16)
    e = (np.arange(128)[None, :, None]
         == (np.arange(nkt)[:, None, None] * (KT // SB) + np.arange(KT)[None, None, :] // SB))
    e = jnp.asarray(e, BF16)

    kern = functools.partial(_nsa_kernel, n_top=n_top)
    return pl.pallas_call(
        kern,
        grid=(B, G, S // QT),
        in_specs=[
            pl.BlockSpec((1, QT, HG * dh), lambda b, g, i: (b, i, g)),
            pl.BlockSpec((1, QT, 128), lambda b, g, i: (b, i, gl_blk + g)),
            pl.BlockSpec((1, 1, nc, dh), lambda b, g, i: (b, g, 0, 0)),
            pl.BlockSpec((1, 1, nc, dh), lambda b, g, i: (b, g, 0, 0)),
            pl.BlockSpec((HG, QT, nc), lambda b, g, i: (g, i, 0)),
            pl.BlockSpec((nsb, nc), lambda b, g, i: (0, 0)),
            pl.BlockSpec((1, S, dh), lambda b, g, i: (b, 0, g)),
            pl.BlockSpec((1, S, dh), lambda b, g, i: (b, 0, g)),
            pl.BlockSpec((1, S + W, dh), lambda b, g, i: (b, 0, g)),
            pl.BlockSpec((1, S + W, dh), lambda b, g, i: (b, 0, g)),
            pl.BlockSpec((HG, nseg, QT, QT), lambda b, g, i: (g, 0, 0, 0)),
            pl.BlockSpec((nkt, 128, KT), lambda b, g, i: (0, 0, 0)),
            pl.BlockSpec((1, dh), lambda b, g, i: (0, 0)),
        ],
        out_specs=pl.BlockSpec((1, QT, HG * dh), lambda b, g, i: (b, i, g)),
        out_shape=jax.ShapeDtypeStruct((B, S, NSA_WIDTH), BF16),
        scratch_shapes=[pltpu.VMEM((nsb, QT), F32)],
        compiler_params=_cparams("parallel", "parallel", "arbitrary"),
        name="nsa_attention",
    )(proj, proj, kc, vc, bc, cover_t, ks, vs, kw, vw, tb, e, q_gain.reshape(1, dh))


def _nsa_compress_kernel(x_ref, pos_ref, w1_ref, w2_ref, kg_ref, o_ref):
    stride, dh = NSA_CMP_STRIDE, NSA_HEAD_DIM
    n_chunks = x_ref.shape[1] // stride
    hid = w1_ref.shape[2]
    h_lo = jnp.zeros((n_chunks, hid), F32)
    h_hi = jnp.zeros((n_chunks, hid), F32)
    for p in range(stride):
        xp = x_ref[0, pl.ds(p, n_chunks, stride=stride), :]
        h_lo = h_lo + jnp.dot((xp + pos_ref[0, p:p + 1, :]).astype(BF16), w1_ref[0, p * dh:(p + 1) * dh, :],
                              preferred_element_type=F32)
        h_hi = h_hi + jnp.dot((xp + pos_ref[0, stride + p:stride + p + 1, :]).astype(BF16),
                              w1_ref[0, (stride + p) * dh:(stride + p + 1) * dh, :], preferred_element_type=F32)
    hidden = h_lo + jnp.concatenate([h_hi[1:], jnp.zeros((1, hid), F32)], axis=0)
    out = jnp.dot((hidden * jax.nn.sigmoid(hidden)).astype(BF16), w2_ref[0], preferred_element_type=F32)
    out = jnp.where(pl.program_id(2) == 0, _rms_rows(out, kg_ref[...]), out)
    row = lax.broadcasted_iota(jnp.int32, out.shape, 0)
    o_ref[0, 0, 0] = jnp.where(row < n_chunks - 1, out, 0.0).astype(o_ref.dtype)


def nsa_compress(proj, col_blk, cmp_pos, cmp_w1, cmp_w2, k_gain):
    B, S, _ = proj.shape
    G, dh = NSA_KV_GROUPS, NSA_HEAD_DIM
    nc = S // NSA_CMP_STRIDE
    hid = cmp_w1.shape[-1]
    return pl.pallas_call(
        _nsa_compress_kernel,
        grid=(B, G, 2),
        in_specs=[
            pl.BlockSpec((1, S, dh), lambda b, g, w: (b, 0, col_blk + w * G + g)),
            pl.BlockSpec((1, NSA_CMP_LEN, dh), lambda b, g, w: (w, 0, 0)),
            pl.BlockSpec((1, NSA_CMP_LEN * dh, hid), lambda b, g, w: (w, 0, 0)),
            pl.BlockSpec((1, hid, dh), lambda b, g, w: (w, 0, 0)),
            pl.BlockSpec((1, dh), lambda b, g, w: (0, 0)),
        ],
        out_specs=pl.BlockSpec((1, 1, 1, nc, dh), lambda b, g, w: (w, b, g, 0, 0)),
        out_shape=jax.ShapeDtypeStruct((2, B, G, nc, dh), BF16),
        compiler_params=_cparams("parallel", "parallel", "arbitrary"),
        name="nsa_compress",
    )(proj, cmp_pos, cmp_w1.astype(BF16), cmp_w2.astype(BF16), k_gain.reshape(1, dh))


NSA_PREP_TM = NSA_WINDOW


def _nsa_kv_kernel(ks_ref, vs_ref, kw_ref, vw_ref, gs_ref, gw_ref, kso_ref, vso_ref, kwo_ref, vwo_ref):
    G, dh = NSA_KV_GROUPS, NSA_HEAD_DIM
    i = pl.program_id(1)
    n_in = pl.num_programs(1) - 1

    def normed(ref, gain_ref):
        x = ref[0]
        return jnp.concatenate([_rms_rows(x[:, g * dh:(g + 1) * dh], gain_ref[...]) for g in range(G)], axis=1)

    @pl.when(i < n_in)
    def _():
        kso_ref[0] = normed(ks_ref, gs_ref).astype(kso_ref.dtype)
        vso_ref[0] = vs_ref[0].astype(vso_ref.dtype)

    @pl.when(i == 0)
    def _():
        kwo_ref[0] = jnp.zeros(kwo_ref.shape[1:], kwo_ref.dtype)
        vwo_ref[0] = jnp.zeros(vwo_ref.shape[1:], vwo_ref.dtype)

    @pl.when(i > 0)
    def _():
        kwo_ref[0] = normed(kw_ref, gw_ref).astype(kwo_ref.dtype)
        vwo_ref[0] = vw_ref[0].astype(vwo_ref.dtype)


def nsa_kv_prep(proj, col_blk, k_gain):
    B, S, _ = proj.shape
    tm, kvw, W = NSA_PREP_TM, NSA_KV_WIDTH, NSA_WINDOW
    n_in = S // tm
    cur = lambda i: jnp.minimum(i, n_in - 1)
    prev = lambda i: jnp.maximum(i - 1, 0)
    return pl.pallas_call(
        _nsa_kv_kernel,
        grid=(B, n_in + 1),
        in_specs=[
            pl.BlockSpec((1, tm, kvw), lambda b, i: (b, cur(i), col_blk)),
            pl.BlockSpec((1, tm, kvw), lambda b, i: (b, cur(i), col_blk + 1)),
            pl.BlockSpec((1, tm, kvw), lambda b, i: (b, prev(i), col_blk + 2)),
            pl.BlockSpec((1, tm, kvw), lambda b, i: (b, prev(i), col_blk + 3)),
            pl.BlockSpec((1, NSA_HEAD_DIM), lambda b, i: (0, 0)),
            pl.BlockSpec((1, NSA_HEAD_DIM), lambda b, i: (0, 0)),
        ],
        out_specs=[
            pl.BlockSpec((1, tm, kvw), lambda b, i: (b, cur(i), 0)),
            pl.BlockSpec((1, tm, kvw), lambda b, i: (b, cur(i), 0)),
            pl.BlockSpec((1, tm, kvw), lambda b, i: (b, i, 0)),
            pl.BlockSpec((1, tm, kvw), lambda b, i: (b, i, 0)),
        ],
        out_shape=[jax.ShapeDtypeStruct((B, S, kvw), BF16), jax.ShapeDtypeStruct((B, S, kvw), BF16),
                   jax.ShapeDtypeStruct((B, S + W, kvw), BF16), jax.ShapeDtypeStruct((B, S + W, kvw), BF16)],
        compiler_params=_cparams("parallel", "arbitrary"),
        name="nsa_kv_prep",
    )(proj, proj, proj, proj, k_gain[1:2], k_gain[2:3])


RET_HB = 2


def _rotate_half(x, cos, sin):
    half = x.shape[-1] // 2
    x1, x2 = x[:, :half], x[:, half:]
    return jnp.concatenate([x1 * cos - x2 * sin, x1 * sin + x2 * cos], axis=-1)


def _retention_kernel(q_ref, k_ref, v_ref, g_ref, cos_ref, sin_ref, din_ref, qd_ref, kd_ref, cd_ref, gn_ref,
                      o_ref, s_scr):
    @pl.when(pl.program_id(2) == 0)
    def _():
        s_scr[...] = jnp.zeros_like(s_scr)

    cos, sin = cos_ref[...], sin_ref[...]
    heads = range(RET_HB)
    sl = [slice(h * RET_DK, (h + 1) * RET_DK) for h in heads]
    q = [_rotate_half(q_ref[0, :, sl[h]], cos, sin) * (RET_DK ** -0.5) for h in heads]
    k = [_rotate_half(k_ref[0, :, sl[h]], cos, sin) for h in heads]
    v16 = [v_ref[0, :, sl[h]].astype(BF16) for h in heads]
    inner = [_dot_nt(q[h].astype(BF16), k[h].astype(BF16)) * din_ref[h] for h in heads]
    s = [s_scr[h] for h in heads]
    o = [jnp.dot(inner[h].astype(BF16), v16[h], preferred_element_type=F32)
         + jnp.dot((q[h] * qd_ref[h]).astype(BF16), s[h].astype(BF16), preferred_element_type=F32) for h in heads]
    for h in heads:
        s_scr[h] = s[h] * cd_ref[h] + _dot_tn((k[h] * kd_ref[h]).astype(BF16), v16[h])
    outs = []
    for h in heads:
        mu = jnp.mean(o[h], axis=-1, keepdims=True)
        d = o[h] - mu
        var = jnp.mean(d * d, axis=-1, keepdims=True)
        gate = g_ref[0, :, sl[h]]
        outs.append((gate * jax.nn.sigmoid(gate)
                     * (d * lax.rsqrt(var + RET_GN_EPS) * gn_ref[:, sl[h]])).astype(o_ref.dtype))
    o_ref[0] = jnp.concatenate(outs, axis=-1)


def retention_pallas(proj, q_blk, gn_gain):
    B, S, _ = proj.shape
    H, dk, dv, C = RET_HEADS, RET_DK, RET_DV, RET_CHUNK
    N = S // C
    half = dk // 2
    inv = RET_ROPE_BASE ** (-jnp.arange(half, dtype=F32) / half)
    ang = jnp.arange(S).astype(F32)[:, None] * inv[None, :]
    cos, sin = jnp.cos(ang), jnp.sin(ang)
    log_gamma = jnp.log(1.0 - 2.0 ** (-5.0 - jnp.arange(H, dtype=F32)))
    idx = jnp.arange(C, dtype=F32)
    rel = idx[:, None] - idx[None, :]
    decay_in = jnp.where(rel >= 0, jnp.exp(log_gamma[:, None, None] * jnp.maximum(rel, 0.0)), 0.0)
    q_decay = jnp.exp(log_gamma[:, None] * (idx + 1.0))[..., None]
    k_decay = jnp.exp(log_gamma[:, None] * (C - 1.0 - idx))[..., None]
    chunk_decay = jnp.exp(log_gamma * C)[:, None, None]
    hb = RET_HB
    HP = H // hb
    qb0 = q_blk // hb
    return pl.pallas_call(
        _retention_kernel,
        grid=(B, HP, N),
        in_specs=[
            pl.BlockSpec((1, C, hb * dk), lambda b, p, n: (b, n, qb0 + p)),
            pl.BlockSpec((1, C, hb * dk), lambda b, p, n: (b, n, qb0 + HP + p)),
            pl.BlockSpec((1, C, hb * dv), lambda b, p, n: (b, n, qb0 + 2 * HP + p)),
            pl.BlockSpec((1, C, hb * dv), lambda b, p, n: (b, n, qb0 + 3 * HP + p)),
            pl.BlockSpec((C, half), lambda b, p, n: (n, 0)),
            pl.BlockSpec((C, half), lambda b, p, n: (n, 0)),
            pl.BlockSpec((hb, C, C), lambda b, p, n: (p, 0, 0)),
            pl.BlockSpec((hb, C, 1), lambda b, p, n: (p, 0, 0)),
            pl.BlockSpec((hb, C, 1), lambda b, p, n: (p, 0, 0)),
            pl.BlockSpec((hb, 1, 1), lambda b, p, n: (p, 0, 0)),
            pl.BlockSpec((1, hb * dv), lambda b, p, n: (0, p)),
        ],
        out_specs=pl.BlockSpec((1, C, hb * dv), lambda b, p, n: (b, n, p)),
        out_shape=jax.ShapeDtypeStruct((B, S, H * dv), BF16),
        scratch_shapes=[pltpu.VMEM((hb, dk, dv), F32)],
        compiler_params=_cparams("parallel", "parallel", "arbitrary"),
        name="retention",
    )(proj, proj, proj, proj, cos, sin, decay_in, q_decay, k_decay, chunk_decay, gn_gain.reshape(1, H * dv))


GDN_HB = 32
GDN_HALO = 8


def _conv_silu(x_ref, halo_ref, keep, w, stage_ref):
    c = x_ref.shape[1]
    stage_ref[0:GDN_HALO, :] = halo_ref[0] * keep
    stage_ref[GDN_HALO:GDN_HALO + c, :] = x_ref[0]
    y = w[GDN_CONV - 1:GDN_CONV] * x_ref[0]
    for j in range(GDN_CONV - 1):
        off = GDN_HALO - (GDN_CONV - 1) + j
        y = y + w[j:j + 1] * stage_ref[off:off + c, :]
    return y * jax.nn.sigmoid(y)


def _l2_rows(x):
    return x * lax.rsqrt(jnp.sum(x * x, axis=-1, keepdims=True) + NORM_EPS)


def _gdn_kernel(q_ref, k_ref, v_ref, qh_ref, kh_ref, vh_ref, wq_ref, wk_ref, wv_ref, z_ref,
                ba_ref, alog_ref, dtb_ref, ng_ref, o_ref, s_scr, qst_scr, kst_scr, vst_scr):
    C, dh, hb = GDN_CHUNK, GDN_HEAD_DIM, GDN_HB
    rep = GDN_V_HEADS // GDN_QK_HEADS
    first = pl.program_id(2) == 0

    @pl.when(first)
    def _():
        s_scr[...] = jnp.zeros_like(s_scr)

    keep = jnp.where(first, 0.0, 1.0)
    qc = _conv_silu(q_ref, qh_ref, keep, wq_ref[...], qst_scr)
    kc = _conv_silu(k_ref, kh_ref, keep, wk_ref[...], kst_scr)
    vc = _conv_silu(v_ref, vh_ref, keep, wv_ref[...], vst_scr)

    ri = lax.broadcasted_iota(jnp.int32, (C, C), 0)
    ci = lax.broadcasted_iota(jnp.int32, (C, C), 1)
    causal = ri >= ci
    strict = ri > ci
    ba = ba_ref[0]
    bcol = jax.nn.sigmoid(ba[:, :hb])
    g = -jnp.exp(alog_ref[0]) * jax.nn.softplus(ba[:, hb:2 * hb] + dtb_ref[0])
    gcol = jnp.dot(jnp.where(causal, 1.0, 0.0), g, preferred_element_type=F32, precision=lax.Precision.HIGHEST)
    grow = lax.dot_general(g, jnp.where(ri <= ci, 1.0, 0.0), (((0,), (0,)), ((), ())),
                           preferred_element_type=F32, precision=lax.Precision.HIGHEST)
    heads = range(hb)
    qs, ks, grams = [], [], []
    for hq in range(hb // rep):
        qh = _l2_rows(qc[:, hq * dh:(hq + 1) * dh]) * (dh ** -0.5)
        kh = _l2_rows(kc[:, hq * dh:(hq + 1) * dh])
        k16 = kh.astype(BF16)
        qs.append(qh)
        ks.append(kh)
        grams.append(_dot_nt(jnp.concatenate([qh.astype(BF16), k16], axis=0), k16))
    beta = [bcol[:, h:h + 1] for h in heads]
    gc = [gcol[:, h:h + 1] for h in heads]
    gr = [grow[h:h + 1, :] for h in heads]
    g_last = [gr[h][:, C - 1:C] for h in heads]
    eg = [jnp.exp(gc[h]) for h in heads]
    decay = [jnp.where(causal, jnp.exp(jnp.minimum(gc[h] - gr[h], 0.0)), 0.0) for h in heads]
    attn = [(grams[h // rep][:C] * decay[h]).astype(BF16) for h in heads]
    nm = [jnp.where(strict, grams[h // rep][C:] * decay[h] * (-beta[h]), 0.0) for h in heads]
    m = [nm[h].astype(BF16) for h in heads]
    for _ in range(int(math.log2(C)) - 1):
        mf = [jnp.dot(m[h], m[h], preferred_element_type=F32) for h in heads]
        m = [mf[h].astype(BF16) for h in heads]
        nm = [nm[h] + mf[h] + jnp.dot(m[h], nm[h].astype(BF16), preferred_element_type=F32) for h in heads]
    x = [jnp.concatenate([vc[:, h * dh:(h + 1) * dh] * beta[h], ks[h // rep] * (beta[h] * eg[h])], axis=1)
         for h in heads]
    x = [x[h] + jnp.dot(nm[h].astype(BF16), x[h].astype(BF16), preferred_element_type=F32) for h in heads]
    s = [s_scr[h] for h in heads]
    ws = [jnp.dot(jnp.concatenate([x[h][:, dh:].astype(BF16), (qs[h // rep] * eg[h]).astype(BF16)], axis=0),
                  s[h].astype(BF16), preferred_element_type=F32) for h in heads]
    vn = [(x[h][:, :dh] - ws[h][:C]).astype(BF16) for h in heads]
    o = [ws[h][C:] + jnp.dot(attn[h], vn[h], preferred_element_type=F32) for h in heads]
    for h in heads:
        k_dec = (ks[h // rep] * jnp.exp(g_last[h] - gc[h])).astype(BF16)
        s_scr[h] = s[h] * jnp.exp(g_last[h]) + _dot_tn(k_dec, vn[h])
    outs = []
    for h in heads:
        zh = z_ref[0, :, h * dh:(h + 1) * dh]
        outs.append((_rms_rows(o[h], ng_ref[...]) * (zh * jax.nn.sigmoid(zh))).astype(o_ref.dtype))
    o_ref[0] = jnp.concatenate(outs, axis=1)


_GDN_GATE_COL0 = GDN_CONV_CH + GDN_V_WIDTH
_GDN_TN = 1792
_GDN_COLS = _round_up(_GDN_GATE_COL0 + GDN_V_HEADS // GDN_HB * 128, _GDN_TN)


def _gdn_w_in_layout(w_in):
    b0 = _GDN_GATE_COL0
    a0 = b0 + GDN_V_HEADS
    parts = [w_in[:, :b0]]
    for p in range(GDN_V_HEADS // GDN_HB):
        parts.append(w_in[:, b0 + p * GDN_HB:b0 + (p + 1) * GDN_HB])
        parts.append(jnp.pad(w_in[:, a0 + p * GDN_HB:a0 + (p + 1) * GDN_HB], ((0, 0), (0, 128 - 2 * GDN_HB))))
    w = jnp.concatenate(parts, axis=1)
    return jnp.pad(w, ((0, 0), (0, _GDN_COLS - w.shape[1]))).astype(BF16)


def gdn_delta_rule(proj, conv_w, a_log, dt_bias, norm_gain):
    B, S, _ = proj.shape
    C, dh, hb, H = GDN_CHUNK, GDN_HEAD_DIM, GDN_HB, GDN_V_HEADS
    rep = GDN_V_HEADS // GDN_QK_HEADS
    N, HP = S // C, H // hb
    wqk, wv = hb // rep * dh, hb * dh
    kb0, vb0, zb0 = GDN_QK_WIDTH // wqk, 2 * GDN_QK_WIDTH // wv, GDN_CONV_CH // wv
    gb0 = _GDN_GATE_COL0 // 128
    hr = C // GDN_HALO
    halo = lambda n: jnp.maximum(n * hr - 1, 0)
    return pl.pallas_call(
        _gdn_kernel,
        grid=(B, HP, N),
        in_specs=[
            pl.BlockSpec((1, C, wqk), lambda b, p, n: (b, n, p)),
            pl.BlockSpec((1, C, wqk), lambda b, p, n: (b, n, kb0 + p)),
            pl.BlockSpec((1, C, wv), lambda b, p, n: (b, n, vb0 + p)),
            pl.BlockSpec((1, GDN_HALO, wqk), lambda b, p, n: (b, halo(n), p)),
            pl.BlockSpec((1, GDN_HALO, wqk), lambda b, p, n: (b, halo(n), kb0 + p)),
            pl.BlockSpec((1, GDN_HALO, wv), lambda b, p, n: (b, halo(n), vb0 + p)),
            pl.BlockSpec((GDN_CONV, wqk), lambda b, p, n: (0, p)),
            pl.BlockSpec((GDN_CONV, wqk), lambda b, p, n: (0, kb0 + p)),
            pl.BlockSpec((GDN_CONV, wv), lambda b, p, n: (0, vb0 + p)),
            pl.BlockSpec((1, C, wv), lambda b, p, n: (b, n, zb0 + p)),
            pl.BlockSpec((1, C, 128), lambda b, p, n: (b, n, gb0 + p)),
            pl.BlockSpec((1, 1, hb), lambda b, p, n: (p, 0, 0)),
            pl.BlockSpec((1, 1, hb), lambda b, p, n: (p, 0, 0)),
            pl.BlockSpec((1, dh), lambda b, p, n: (0, 0)),
        ],
        out_specs=pl.BlockSpec((1, C, wv), lambda b, p, n: (b, n, p)),
        out_shape=jax.ShapeDtypeStruct((B, S, H * dh), BF16),
        scratch_shapes=[pltpu.VMEM((hb, dh, dh), F32), pltpu.VMEM((GDN_HALO + C, wqk), F32),
                        pltpu.VMEM((GDN_HALO + C, wqk), F32), pltpu.VMEM((GDN_HALO + C, wv), F32)],
        compiler_params=_cparams("parallel", "parallel", "arbitrary"),
        name="gdn_delta_rule",
    )(proj, proj, proj, proj, proj, proj, conv_w, conv_w, conv_w, proj, proj,
      a_log.astype(F32).reshape(HP, 1, hb), dt_bias.astype(F32).reshape(HP, 1, hb), norm_gain.reshape(1, dh))


_HYB_NSA_COLS = NSA_WIDTH + 6 * NSA_KV_WIDTH
_HYB_RET_COLS = 2 * RET_HEADS * RET_DK + 2 * RET_HEADS * RET_DV
_HYB_GATE_COL0 = _HYB_NSA_COLS + _HYB_RET_COLS
_HYB_TN = 768
_HYB_COLS = _round_up(_HYB_GATE_COL0 + NSA_KV_GROUPS * 128, _HYB_TN)


def _hybrid_w_in_layout(w_in):
    gate0 = _HYB_NSA_COLS
    ret0 = gate0 + 3 * NSA_HEADS
    per_group = 3 * NSA_GROUP_HEADS
    parts = [w_in[:, :gate0], w_in[:, ret0:ret0 + _HYB_RET_COLS]]
    for g in range(NSA_KV_GROUPS):
        parts.append(jnp.pad(w_in[:, gate0 + g * per_group:gate0 + (g + 1) * per_group],
                             ((0, 0), (0, 128 - per_group))))
    w = jnp.concatenate(parts, axis=1)
    return jnp.pad(w, ((0, 0), (0, _HYB_COLS - w.shape[1]))).astype(BF16)


def hybrid_mixer(h, ln_gain, w_in, w_out, q_gain, k_gain, cmp_pos, cmp_w1, cmp_w2, gn_gain, rel_bias):
    B, S, D = h.shape
    proj = norm_matmul(h.reshape(B * S, D), ln_gain, _hybrid_w_in_layout(w_in), tn=_HYB_TN).reshape(B, S, _HYB_COLS)
    cmp = nsa_compress(proj, NSA_WIDTH // NSA_HEAD_DIM, cmp_pos, cmp_w1, cmp_w2, k_gain[0])
    ksn, vsb, kwn, vwb = nsa_kv_prep(proj, (NSA_WIDTH + 2 * NSA_KV_WIDTH) // NSA_KV_WIDTH, k_gain)
    a_out = nsa_attention_pallas(proj, _HYB_GATE_COL0 // 128, cmp[0], cmp[1], ksn, vsb, kwn, vwb, q_gain, rel_bias)
    b_out = retention_pallas(proj, _HYB_NSA_COLS // RET_DK, gn_gain)
    w_out = w_out.astype(BF16)
    return matmul2_residual(a_out.reshape(B * S, -1), b_out.reshape(B * S, -1), w_out,
                            h.reshape(B * S, D)).reshape(B, S, D)


def gdn_mixer(h, ln_gain, w_in, conv_w, a_log, dt_bias, norm_gain, w_out):
    B, S, D = h.shape
    proj = norm_matmul(h.reshape(B * S, D), ln_gain, _gdn_w_in_layout(w_in), tn=_GDN_TN).reshape(B, S, _GDN_COLS)
    o = gdn_delta_rule(proj, conv_w, a_log, dt_bias, norm_gain)
    return matmul_residual(o.reshape(B * S, GDN_V_WIDTH), w_out.astype(BF16), h.reshape(B * S, D)).reshape(B, S, D)


def memory_kv(mem, mem_gain, wkv, k_gain):
    B, M, D = mem.shape
    kv = norm_matmul(mem.reshape(B * M, D), mem_gain, wkv.astype(BF16))
    k, v = jnp.split(kv.reshape(B, M, 2 * XA_WIDTH), 2, axis=-1)
    k = rms_norm(k.reshape(B, M, XA_HEADS, XA_HEAD_DIM), k_gain).reshape(B, M, XA_WIDTH)
    return k.astype(BF16), v.astype(BF16)


def kernel(x, mem, rel_bias, ln_mix, ln_mem, ln_ffn, hyb_w_in, hyb_w_out, nsa_q_gain, nsa_k_gain,
           nsa_cmp_pos, nsa_cmp_w1, nsa_cmp_w2, ret_gn_gain, gdn_w_in, gdn_conv_w, gdn_a_log,
           gdn_dt_bias, gdn_norm_gain, gdn_w_out, xa_wq, xa_wkv, xa_q_gain, xa_k_gain, xa_mem_gain,
           xa_wo, ffn_w_in, ffn_w_out):
    B, S, D = x.shape
    h = x
    ffn_w_in, ffn_w_out, gdn_w_out = cast_bf16(ffn_w_in), cast_bf16(ffn_w_out), cast_bf16(gdn_w_out)
    for layer in range(DEPTH):
        if layer % 2 == 0:
            e = layer // 2
            h = hybrid_mixer(h, ln_mix[layer], hyb_w_in[e], hyb_w_out[e], nsa_q_gain[e], nsa_k_gain[e],
                             nsa_cmp_pos[e], nsa_cmp_w1[e], nsa_cmp_w2[e], ret_gn_gain[e], rel_bias)
        else:
            o = layer // 2
            h = gdn_mixer(h, ln_mix[layer], gdn_w_in[o], gdn_conv_w[o], gdn_a_log[o], gdn_dt_bias[o],
                          gdn_norm_gain[o], gdn_w_out[o])
        k_mem, v_mem = memory_kv(mem, xa_mem_gain[layer], xa_wkv[layer], xa_k_gain[layer])
        h = xattn_residual(h, ln_mem[layer], xa_wq[layer].astype(BF16), xa_q_gain[layer], k_mem, v_mem,
                           xa_wo[layer].astype(BF16))
        h = ffn_residual(h.reshape(B * S, D), ln_ffn[layer], ffn_w_in[layer].astype(BF16),
                         ffn_w_out[layer].astype(BF16)).reshape(B, S, D)
    return h
```

```python
import functools
import math

import jax
import jax.numpy as jnp
import numpy as np
from jax import lax
from jax.experimental import pallas as pl
from jax.experimental.pallas import tpu as pltpu

F32 = jnp.float32
BF16 = jnp.bfloat16

D_MODEL = 2048
DEPTH = 4
NORM_EPS = 1e-6
NEG_INF = -1e30

NSA_HEADS = 8
NSA_KV_GROUPS = 2
NSA_GROUP_HEADS = NSA_HEADS // NSA_KV_GROUPS
NSA_HEAD_DIM = 128
NSA_CMP_LEN = 32
NSA_CMP_STRIDE = 16
NSA_SEL_BLOCK = 64
NSA_N_SELECT = 16
NSA_WINDOW = 512
NSA_Q_BLOCK = 64
NSA_FORCE_SCORE = 1e6
NSA_WIDTH = NSA_HEADS * NSA_HEAD_DIM
NSA_KV_WIDTH = NSA_KV_GROUPS * NSA_HEAD_DIM

RET_HEADS = 4
RET_DK = 256
RET_DV = 256
RET_CHUNK = 128
RET_ROPE_BASE = 10000.0
RET_GN_EPS = 1e-5

GDN_QK_HEADS = 16
GDN_V_HEADS = 32
GDN_HEAD_DIM = 128
GDN_CONV = 4
GDN_CHUNK = 64
GDN_QK_WIDTH = GDN_QK_HEADS * GDN_HEAD_DIM
GDN_V_WIDTH = GDN_V_HEADS * GDN_HEAD_DIM
GDN_CONV_CH = 2 * GDN_QK_WIDTH + GDN_V_WIDTH

REL_BUCKETS = 32
REL_MAX_DIST = 1024

XA_HEADS = 4
XA_HEAD_DIM = 128
XA_WIDTH = XA_HEADS * XA_HEAD_DIM

V7X_VMEM_LIMIT_BYTES = 56 * 1024 * 1024


def _cparams(*sem):
    return pltpu.CompilerParams(dimension_semantics=sem, vmem_limit_bytes=V7X_VMEM_LIMIT_BYTES)


def _round_up(n, m):
    return -(-n // m) * m


def _rms_rows(x, gain):
    return x * lax.rsqrt(jnp.mean(x * x, axis=-1, keepdims=True) + NORM_EPS) * gain


def _cast_kernel(x_ref, o_ref):
    o_ref[...] = x_ref[...].astype(o_ref.dtype)


def cast_bf16(w, *, block_bytes=8 * 1024 * 1024):
    n = w.shape[-1]
    x = w.reshape(-1, n)
    rows = x.shape[0]
    tr = rows
    while tr * n * 4 > block_bytes and tr % 32 == 0:
        tr //= 2
    out = pl.pallas_call(
        _cast_kernel,
        grid=(rows // tr,),
        in_specs=[pl.BlockSpec((tr, n), lambda i: (i, 0))],
        out_specs=pl.BlockSpec((tr, n), lambda i: (i, 0)),
        out_shape=jax.ShapeDtypeStruct((rows, n), BF16),
        compiler_params=_cparams("parallel"),
        name="cast_bf16",
    )(x)
    return out.reshape(w.shape)


def _norm_matmul_kernel(x_ref, g_ref, w_ref, o_ref, xn_ref):
    @pl.when(pl.program_id(1) == 0)
    def _():
        xn_ref[...] = _rms_rows(x_ref[...], g_ref[...]).astype(BF16)

    o_ref[...] = jnp.dot(xn_ref[...], w_ref[...], preferred_element_type=F32).astype(o_ref.dtype)


def norm_matmul(x, gain, w, *, tm=1024, tn=512, out_dtype=F32):
    m, k = x.shape
    n = w.shape[1]
    tm = min(tm, m)
    return pl.pallas_call(
        _norm_matmul_kernel,
        grid=(m // tm, n // tn),
        in_specs=[pl.BlockSpec((tm, k), lambda i, j: (i, 0)),
                  pl.BlockSpec((1, k), lambda i, j: (0, 0)),
                  pl.BlockSpec((k, tn), lambda i, j: (0, j))],
        out_specs=pl.BlockSpec((tm, tn), lambda i, j: (i, j)),
        out_shape=jax.ShapeDtypeStruct((m, n), out_dtype),
        scratch_shapes=[pltpu.VMEM((tm, k), BF16)],
        compiler_params=_cparams("parallel", "arbitrary"),
        name="norm_matmul",
    )(x, gain.reshape(1, k), w)


def _matmul_res_kernel(x_ref, w_ref, r_ref, o_ref):
    o_ref[...] = r_ref[...] + jnp.dot(x_ref[...], w_ref[...], preferred_element_type=F32)


def matmul_residual(x, w, res, *, tm=1024, tn=512):
    m, k = x.shape
    n = w.shape[1]
    return pl.pallas_call(
        _matmul_res_kernel,
        grid=(m // tm, n // tn),
        in_specs=[pl.BlockSpec((tm, k), lambda i, j: (i, 0)),
                  pl.BlockSpec((k, tn), lambda i, j: (0, j)),
                  pl.BlockSpec((tm, tn), lambda i, j: (i, j))],
        out_specs=pl.BlockSpec((tm, tn), lambda i, j: (i, j)),
        out_shape=jax.ShapeDtypeStruct((m, n), F32),
        compiler_params=_cparams("parallel", "arbitrary"),
        name="matmul_residual",
    )(x, w, res)


def _matmul2_res_kernel(x1_ref, x2_ref, w1_ref, w2_ref, r_ref, o_ref):
    o_ref[...] = (r_ref[...] + jnp.dot(x1_ref[...], w1_ref[...], preferred_element_type=F32)
                  + jnp.dot(x2_ref[...], w2_ref[...], preferred_element_type=F32))


def matmul2_residual(x1, x2, w, res, *, tm=1024, tn=512):
    m, k1 = x1.shape
    n = w.shape[1]
    return pl.pallas_call(
        _matmul2_res_kernel,
        grid=(m // tm, n // tn),
        in_specs=[pl.BlockSpec((tm, k1), lambda i, j: (i, 0)),
                  pl.BlockSpec((tm, k1), lambda i, j: (i, 0)),
                  pl.BlockSpec((k1, tn), lambda i, j: (0, j)),
                  pl.BlockSpec((k1, tn), lambda i, j: (1, j)),
                  pl.BlockSpec((tm, tn), lambda i, j: (i, j))],
        out_specs=pl.BlockSpec((tm, tn), lambda i, j: (i, j)),
        out_shape=jax.ShapeDtypeStruct((m, n), F32),
        compiler_params=_cparams("parallel", "arbitrary"),
        name="matmul2_residual",
    )(x1, x2, w, w, res)


def _ffn_kernel(x_ref, g_ref, wg_ref, wu_ref, wo_ref, o_ref, xn_ref):
    @pl.when(pl.program_id(1) == 0)
    def _():
        x = x_ref[...]
        xn_ref[...] = _rms_rows(x, g_ref[...]).astype(BF16)
        o_ref[...] = x

    xn = xn_ref[...]
    gate = jnp.dot(xn, wg_ref[...], preferred_element_type=F32)
    up = jnp.dot(xn, wu_ref[...], preferred_element_type=F32)
    act = (gate * jax.nn.sigmoid(gate) * up).astype(BF16)
    o_ref[...] += jnp.dot(act, wo_ref[...], preferred_element_type=F32)


def ffn_residual(h, gain, w_in, w_out, *, tm=1024, th=512):
    m, d = h.shape
    hidden = w_out.shape[0]
    nh = hidden // th
    return pl.pallas_call(
        _ffn_kernel,
        grid=(m // tm, nh),
        in_specs=[pl.BlockSpec((tm, d), lambda i, j: (i, 0), pipeline_mode=pl.Buffered(1)),
                  pl.BlockSpec((1, d), lambda i, j: (0, 0)),
                  pl.BlockSpec((d, th), lambda i, j: (0, j)),
                  pl.BlockSpec((d, th), lambda i, j: (0, j + nh)),
                  pl.BlockSpec((th, d), lambda i, j: (j, 0))],
        out_specs=pl.BlockSpec((tm, d), lambda i, j: (i, 0)),
        out_shape=jax.ShapeDtypeStruct((m, d), F32),
        scratch_shapes=[pltpu.VMEM((tm, d), BF16)],
        compiler_params=_cparams("parallel", "arbitrary"),
        name="ffn_residual",
    )(h, gain.reshape(1, d), w_in, w_in, w_out)


def _xattn_kernel(h_ref, g_ref, wq_ref, qg_ref, k_ref, v_ref, wo_ref, o_ref):
    x = h_ref[0]
    xn = _rms_rows(x, g_ref[...]).astype(BF16)
    q = jnp.dot(xn, wq_ref[...], preferred_element_type=F32)
    k = k_ref[0]
    v = v_ref[0]
    outs = []
    for hh in range(XA_HEADS):
        sl = slice(hh * XA_HEAD_DIM, (hh + 1) * XA_HEAD_DIM)
        qh = _rms_rows(q[:, sl], qg_ref[...]).astype(BF16)
        s = lax.dot_general(qh, k[:, sl], (((1,), (1,)), ((), ())), preferred_element_type=F32)
        s = s * (XA_HEAD_DIM ** -0.5)
        s = s - jnp.max(s, axis=-1, keepdims=True)
        p = jnp.exp(s)
        p = p / jnp.sum(p, axis=-1, keepdims=True)
        outs.append(jnp.dot(p.astype(BF16), v[:, sl], preferred_element_type=F32))
    o = jnp.concatenate(outs, axis=-1).astype(BF16)
    o_ref[0] = x + jnp.dot(o, wo_ref[...], preferred_element_type=F32)


def xattn_residual(h, gain, wq, q_gain, k, v, wo, *, tm=512):
    b, s, d = h.shape
    mt = k.shape[1]
    return pl.pallas_call(
        _xattn_kernel,
        grid=(b, s // tm),
        in_specs=[pl.BlockSpec((1, tm, d), lambda bi, i: (bi, i, 0)),
                  pl.BlockSpec((1, d), lambda bi, i: (0, 0)),
                  pl.BlockSpec((d, XA_WIDTH), lambda bi, i: (0, 0)),
                  pl.BlockSpec((1, XA_HEAD_DIM), lambda bi, i: (0, 0)),
                  pl.BlockSpec((1, mt, XA_WIDTH), lambda bi, i: (bi, 0, 0)),
                  pl.BlockSpec((1, mt, XA_WIDTH), lambda bi, i: (bi, 0, 0)),
                  pl.BlockSpec((XA_WIDTH, d), lambda bi, i: (0, 0))],
        out_specs=pl.BlockSpec((1, tm, d), lambda bi, i: (bi, i, 0)),
        out_shape=jax.ShapeDtypeStruct((b, s, d), F32),
        compiler_params=_cparams("parallel", "parallel"),
        name="xattn_residual",
    )(h, gain.reshape(1, d), wq, q_gain.reshape(1, XA_HEAD_DIM), k, v, wo)


def rms_norm(x, gain, eps=NORM_EPS):
    x32 = x.astype(F32)
    y = x32 * lax.rsqrt(jnp.mean(x32 * x32, axis=-1, keepdims=True) + eps)
    return (y * gain.astype(F32)).astype(x.dtype)


def rel_bucket(dist):
    dist = jnp.maximum(dist, 0)
    max_exact = REL_BUCKETS // 2
    scaled = (jnp.log(jnp.maximum(dist, max_exact).astype(F32) / max_exact)
              / math.log(REL_MAX_DIST / max_exact) * (REL_BUCKETS - max_exact))
    large = jnp.minimum(max_exact + scaled.astype(jnp.int32), REL_BUCKETS - 1)
    return jnp.where(dist < max_exact, dist, large)


NSA_QT = 128
NSA_KT = 512
NSA_WT = NSA_WINDOW + NSA_QT


def _dot_nt(a, b):
    return lax.dot_general(a, b, (((1,), (1,)), ((), ())), preferred_element_type=F32)


def _dot_tn(a, b):
    return lax.dot_general(a, b, (((0,), (0,)), ((), ())), preferred_element_type=F32)


def _bias_tile(tb_ref, h, first_slab, n_slabs):
    return jnp.concatenate([tb_ref[h, jnp.maximum(first_slab - j, 0)] for j in range(n_slabs)], axis=1)


def _nsa_kernel(q_ref, gl_ref, kc_ref, vc_ref, bc_ref, covt_ref, ks_ref, vs_ref, kw_ref, vw_ref,
                tb_ref, e_ref, qg_ref, o_ref, a_scr, *, n_top):
    QT, KT, WT, HG, dh = NSA_QT, NSA_KT, NSA_WT, NSA_GROUP_HEADS, NSA_HEAD_DIM
    nsb, nc = covt_ref.shape
    q0 = pl.program_id(2) * QT

    x = q_ref[0]
    qs = []
    for h in range(HG):
        xh = _rms_rows(x[:, h * dh:(h + 1) * dh], qg_ref[...]) * (dh ** -0.5)
        qs.append(xh.astype(BF16))
    qn = jnp.concatenate(qs, axis=0)

    sc = _dot_nt(qn, kc_ref[0, 0]) + bc_ref[...].reshape(HG * QT, nc)
    row = lax.broadcasted_iota(jnp.int32, (HG * QT, nc), 0)
    col = lax.broadcasted_iota(jnp.int32, (HG * QT, nc), 1)
    tq = q0 + (row & (QT - 1))
    mask_c = tq >= col * NSA_CMP_STRIDE + (NSA_CMP_LEN - 1)
    sc = jnp.where(mask_c, sc, NEG_INF)
    pc = jnp.where(mask_c, jnp.exp(sc - jnp.max(sc, axis=-1, keepdims=True)), 0.0)
    den = jnp.sum(pc, axis=-1, keepdims=True)
    pcb = (pc / jnp.maximum(den, 1e-30)).astype(BF16)
    o_c = jnp.dot(pcb, vc_ref[0, 0], preferred_element_type=F32)

    imp_all = _dot_nt(covt_ref[...], pcb)
    imp = imp_all[:, 0:QT]
    for h in range(1, HG):
        imp = imp + imp_all[:, h * QT:(h + 1) * QT]
    jj = lax.broadcasted_iota(jnp.int32, (nsb, QT), 0)
    cur = (q0 + lax.broadcasted_iota(jnp.int32, (nsb, QT), 1)) // NSA_SEL_BLOCK
    forced = (jj == 0) | (jj == cur) | (jj == cur - 1)
    a = jnp.where(forced, NSA_FORCE_SCORE, imp)
    a = jnp.where(jj > cur, -NSA_FORCE_SCORE, a)
    a_scr[...] = a

    def rank_body(i, rank):
        r = a_scr[pl.ds(i, 1), :]
        return rank + jnp.where((r > a) | ((r == a) & (jj > i)), 1.0, 0.0)

    n_live = jnp.minimum((q0 + QT - 1) // NSA_SEL_BLOCK + 1, nsb)
    rank = lax.fori_loop(0, n_live, rank_body, jnp.zeros((nsb, QT), F32))
    sel_t = jnp.where(rank < n_top, 1.0, 0.0)
    if nsb < 128:
        sel_t = jnp.concatenate([sel_t, jnp.zeros((128 - nsb, QT), F32)], axis=0)
    sel = sel_t.T.astype(BF16)

    cmr = (lax.broadcasted_iota(jnp.int32, (QT, KT), 1) - lax.broadcasted_iota(jnp.int32, (QT, KT), 0))

    def sel_body(kt, carry):
        m_i, l_i, acc = carry
        k0 = pl.multiple_of(kt * KT, KT)
        k = ks_ref[0, pl.ds(k0, KT), :]
        v = vs_ref[0, pl.ds(k0, KT), :]
        s = _dot_nt(qn, k)
        delta = q0 - k0
        visible = (jnp.dot(sel, e_ref[kt], preferred_element_type=F32) > 0.5) & (cmr <= delta)
        seg = delta // QT
        parts = []
        for h in range(HG):
            bias = _bias_tile(tb_ref, h, seg, KT // QT)
            parts.append(jnp.where(visible, s[h * QT:(h + 1) * QT] + bias, NEG_INF))
        s = jnp.concatenate(parts, axis=0)
        m_new = jnp.maximum(m_i, jnp.max(s, axis=-1, keepdims=True))
        alpha = jnp.exp(m_i - m_new)
        p = jnp.exp(s - m_new)
        l_new = alpha * l_i + jnp.sum(p, axis=-1, keepdims=True)
        acc = alpha * acc + jnp.dot(p.astype(BF16), v, preferred_element_type=F32)
        return m_new, l_new, acc

    init = (jnp.full((HG * QT, 1), NEG_INF, F32), jnp.zeros((HG * QT, 1), F32), jnp.zeros((HG * QT, dh), F32))
    _, l_s, acc_s = lax.fori_loop(0, q0 // KT + 1, sel_body, init)
    o_s = acc_s / l_s

    w0 = pl.multiple_of(q0, QT)
    kwin = kw_ref[0, pl.ds(w0, WT), :]
    vwin = vw_ref[0, pl.ds(w0, WT), :]
    sw = _dot_nt(qn, kwin)
    cw = lax.broadcasted_iota(jnp.int32, (QT, WT), 1)
    dist = NSA_WINDOW + lax.broadcasted_iota(jnp.int32, (QT, WT), 0) - cw
    vis_w = (dist >= 0) & (dist < NSA_WINDOW) & (cw >= NSA_WINDOW - q0)
    parts = []
    for h in range(HG):
        bias = _bias_tile(tb_ref, h, NSA_WINDOW // QT, WT // QT)
        parts.append(jnp.where(vis_w, sw[h * QT:(h + 1) * QT] + bias, NEG_INF))
    sw = jnp.concatenate(parts, axis=0)
    pw = jnp.exp(sw - jnp.max(sw, axis=-1, keepdims=True))
    o_w = jnp.dot(pw.astype(BF16), vwin, preferred_element_type=F32) / jnp.sum(pw, axis=-1, keepdims=True)

    gates = jax.nn.sigmoid(gl_ref[0][:, :3 * HG])
    outs = []
    for h in range(HG):
        rs = slice(h * QT, (h + 1) * QT)
        outs.append(gates[:, 3 * h:3 * h + 1] * o_c[rs] + gates[:, 3 * h + 1:3 * h + 2] * o_s[rs]
                    + gates[:, 3 * h + 2:3 * h + 3] * o_w[rs])
    o_ref[0] = jnp.concatenate(outs, axis=-1).astype(o_ref.dtype)


def nsa_attention_pallas(proj, gl_blk, kc, vc, ks, vs, kw, vw, q_gain, rel_bias):
    B, S, _ = proj.shape
    G, HG, dh = NSA_KV_GROUPS, NSA_GROUP_HEADS, NSA_HEAD_DIM
    QT, KT, WT, W, SB = NSA_QT, NSA_KT, NSA_WT, NSA_WINDOW, NSA_SEL_BLOCK
    nsb, nc, nkt, nseg = S // SB, S // NSA_CMP_STRIDE, S // KT, max(S // QT, W // QT + 1)
    n_top = min(NSA_N_SELECT, nsb)

    def bias_of(dist):
        onehot = (rel_bucket(dist)[..., None] == jnp.arange(REL_BUCKETS)) & (dist >= 0)[..., None]
        return jnp.einsum('...b,bh->h...', onehot.astype(F32), rel_bias.astype(F32),
                          precision=lax.Precision.HIGHEST)

    bc = bias_of(jnp.arange(S)[:, None] - (jnp.arange(nc) * NSA_CMP_STRIDE + NSA_CMP_LEN - 1)[None, :])
    tb = bias_of(QT * jnp.arange(nseg)[:, None, None] + jnp.arange(QT)[None, :, None] - jnp.arange(QT)[None, None, :])
    cmp_start = np.arange(nc) * NSA_CMP_STRIDE
    cmp_end = cmp_start + NSA_CMP_LEN - 1
    sel_start = np.arange(nsb) * SB
    cover_t = ((cmp_start[None, :] < sel_start[:, None] + SB) & (cmp_end[None, :] >= sel_start[:, None])
               & (np.arange(nc)[None, :] < nc - 1))
    cover_t = jnp.asarray(cover_t, BF16)
    e = (np.arange(128)[None, :, None]
         == (np.arange(nkt)[:, None, None] * (KT // SB) + np.arange(KT)[None, None, :] // SB))
    e = jnp.asarray(e, BF16)

    kern = functools.partial(_nsa_kernel, n_top=n_top)
    return pl.pallas_call(
        kern,
        grid=(B, G, S // QT),
        in_specs=[
            pl.BlockSpec((1, QT, HG * dh), lambda b, g, i: (b, i, g)),
            pl.BlockSpec((1, QT, 128), lambda b, g, i: (b, i, gl_blk + g)),
            pl.BlockSpec((1, 1, nc, dh), lambda b, g, i: (b, g, 0, 0)),
            pl.BlockSpec((1, 1, nc, dh), lambda b, g, i: (b, g, 0, 0)),
            pl.BlockSpec((HG, QT, nc), lambda b, g, i: (g, i, 0)),
            pl.BlockSpec((nsb, nc), lambda b, g, i: (0, 0)),
            pl.BlockSpec((1, S, dh), lambda b, g, i: (b, 0, g)),
            pl.BlockSpec((1, S, dh), lambda b, g, i: (b, 0, g)),
            pl.BlockSpec((1, S + W, dh), lambda b, g, i: (b, 0, g)),
            pl.BlockSpec((1, S + W, dh), lambda b, g, i: (b, 0, g)),
            pl.BlockSpec((HG, nseg, QT, QT), lambda b, g, i: (g, 0, 0, 0)),
            pl.BlockSpec((nkt, 128, KT), lambda b, g, i: (0, 0, 0)),
            pl.BlockSpec((1, dh), lambda b, g, i: (0, 0)),
        ],
        out_specs=pl.BlockSpec((1, QT, HG * dh), lambda b, g, i: (b, i, g)),
        out_shape=jax.ShapeDtypeStruct((B, S, NSA_WIDTH), BF16),
        scratch_shapes=[pltpu.VMEM((nsb, QT), F32)],
        compiler_params=_cparams("parallel", "parallel", "arbitrary"),
        name="nsa_attention",
    )(proj, proj, kc, vc, bc, cover_t, ks, vs, kw, vw, tb, e, q_gain.reshape(1, dh))


def _nsa_compress_kernel(x_ref, pos_ref, w1_ref, w2_ref, kg_ref, o_ref):
    stride, dh = NSA_CMP_STRIDE, NSA_HEAD_DIM
    n_chunks = x_ref.shape[1] // stride
    hid = w1_ref.shape[2]
    h_lo = jnp.zeros((n_chunks, hid), F32)
    h_hi = jnp.zeros((n_chunks, hid), F32)
    for p in range(stride):
        xp = x_ref[0, pl.ds(p, n_chunks, stride=stride), :]
        h_lo = h_lo + jnp.dot((xp + pos_ref[0, p:p + 1, :]).astype(BF16), w1_ref[0, p * dh:(p + 1) * dh, :],
                              preferred_element_type=F32)
        h_hi = h_hi + jnp.dot((xp + pos_ref[0, stride + p:stride + p + 1, :]).astype(BF16),
                              w1_ref[0, (stride + p) * dh:(stride + p + 1) * dh, :], preferred_element_type=F32)
    hidden = h_lo + jnp.concatenate([h_hi[1:], jnp.zeros((1, hid), F32)], axis=0)
    out = jnp.dot((hidden * jax.nn.sigmoid(hidden)).astype(BF16), w2_ref[0], preferred_element_type=F32)
    out = jnp.where(pl.program_id(2) == 0, _rms_rows(out, kg_ref[...]), out)
    row = lax.broadcasted_iota(jnp.int32, out.shape, 0)
    o_ref[0, 0, 0] = jnp.where(row < n_chunks - 1, out, 0.0).astype(o_ref.dtype)


def nsa_compress(proj, col_blk, cmp_pos, cmp_w1, cmp_w2, k_gain):
    B, S, _ = proj.shape
    G, dh = NSA_KV_GROUPS, NSA_HEAD_DIM
    nc = S // NSA_CMP_STRIDE
    hid = cmp_w1.shape[-1]
    return pl.pallas_call(
        _nsa_compress_kernel,
        grid=(B, G, 2),
        in_specs=[
            pl.BlockSpec((1, S, dh), lambda b, g, w: (b, 0, col_blk + w * G + g)),
            pl.BlockSpec((1, NSA_CMP_LEN, dh), lambda b, g, w: (w, 0, 0)),
            pl.BlockSpec((1, NSA_CMP_LEN * dh, hid), lambda b, g, w: (w, 0, 0)),
            pl.BlockSpec((1, hid, dh), lambda b, g, w: (w, 0, 0)),
            pl.BlockSpec((1, dh), lambda b, g, w: (0, 0)),
        ],
        out_specs=pl.BlockSpec((1, 1, 1, nc, dh), lambda b, g, w: (w, b, g, 0, 0)),
        out_shape=jax.ShapeDtypeStruct((2, B, G, nc, dh), BF16),
        compiler_params=_cparams("parallel", "parallel", "arbitrary"),
        name="nsa_compress",
    )(proj, cmp_pos, cmp_w1.astype(BF16), cmp_w2.astype(BF16), k_gain.reshape(1, dh))


NSA_PREP_TM = NSA_WINDOW


def _nsa_kv_kernel(ks_ref, vs_ref, kw_ref, vw_ref, gs_ref, gw_ref, kso_ref, vso_ref, kwo_ref, vwo_ref):
    G, dh = NSA_KV_GROUPS, NSA_HEAD_DIM
    i = pl.program_id(1)
    n_in = pl.num_programs(1) - 1

    def normed(ref, gain_ref):
        x = ref[0]
        return jnp.concatenate([_rms_rows(x[:, g * dh:(g + 1) * dh], gain_ref[...]) for g in range(G)], axis=1)

    @pl.when(i < n_in)
    def _():
        kso_ref[0] = normed(ks_ref, gs_ref).astype(kso_ref.dtype)
        vso_ref[0] = vs_ref[0].astype(vso_ref.dtype)

    @pl.when(i == 0)
    def _():
        kwo_ref[0] = jnp.zeros(kwo_ref.shape[1:], kwo_ref.dtype)
        vwo_ref[0] = jnp.zeros(vwo_ref.shape[1:], vwo_ref.dtype)

    @pl.when(i > 0)
    def _():
        kwo_ref[0] = normed(kw_ref, gw_ref).astype(kwo_ref.dtype)
        vwo_ref[0] = vw_ref[0].astype(vwo_ref.dtype)


def nsa_kv_prep(proj, col_blk, k_gain):
    B, S, _ = proj.shape
    tm, kvw, W = NSA_PREP_TM, NSA_KV_WIDTH, NSA_WINDOW
    n_in = S // tm
    cur = lambda i: jnp.minimum(i, n_in - 1)
    prev = lambda i: jnp.maximum(i - 1, 0)
    return pl.pallas_call(
        _nsa_kv_kernel,
        grid=(B, n_in + 1),
        in_specs=[
            pl.BlockSpec((1, tm, kvw), lambda b, i: (b, cur(i), col_blk)),
            pl.BlockSpec((1, tm, kvw), lambda b, i: (b, cur(i), col_blk + 1)),
            pl.BlockSpec((1, tm, kvw), lambda b, i: (b, prev(i), col_blk + 2)),
            pl.BlockSpec((1, tm, kvw), lambda b, i: (b, prev(i), col_blk + 3)),
            pl.BlockSpec((1, NSA_HEAD_DIM), lambda b, i: (0, 0)),
            pl.BlockSpec((1, NSA_HEAD_DIM), lambda b, i: (0, 0)),
        ],
        out_specs=[
            pl.BlockSpec((1, tm, kvw), lambda b, i: (b, cur(i), 0)),
            pl.BlockSpec((1, tm, kvw), lambda b, i: (b, cur(i), 0)),
            pl.BlockSpec((1, tm, kvw), lambda b, i: (b, i, 0)),
            pl.BlockSpec((1, tm, kvw), lambda b, i: (b, i, 0)),
        ],
        out_shape=[jax.ShapeDtypeStruct((B, S, kvw), BF16), jax.ShapeDtypeStruct((B, S, kvw), BF16),
                   jax.ShapeDtypeStruct((B, S + W, kvw), BF16), jax.ShapeDtypeStruct((B, S + W, kvw), BF16)],
        compiler_params=_cparams("parallel", "arbitrary"),
        name="nsa_kv_prep",
    )(proj, proj, proj, proj, k_gain[1:2], k_gain[2:3])


RET_HB = 2


def _rotate_half(x, cos, sin):
    half = x.shape[-1] // 2
    x1, x2 = x[:, :half], x[:, half:]
    return jnp.concatenate([x1 * cos - x2 * sin, x1 * sin + x2 * cos], axis=-1)


def _retention_kernel(q_ref, k_ref, v_ref, g_ref, cos_ref, sin_ref, din_ref, qd_ref, kd_ref, cd_ref, gn_ref,
                      o_ref, s_scr):
    @pl.when(pl.program_id(2) == 0)
    def _():
        s_scr[...] = jnp.zeros_like(s_scr)

    cos, sin = cos_ref[...], sin_ref[...]
    heads = range(RET_HB)
    sl = [slice(h * RET_DK, (h + 1) * RET_DK) for h in heads]
    q = [_rotate_half(q_ref[0, :, sl[h]], cos, sin) * (RET_DK ** -0.5) for h in heads]
    k = [_rotate_half(k_ref[0, :, sl[h]], cos, sin) for h in heads]
    v16 = [v_ref[0, :, sl[h]].astype(BF16) for h in heads]
    inner = [_dot_nt(q[h].astype(BF16), k[h].astype(BF16)) * din_ref[h] for h in heads]
    s = [s_scr[h] for h in heads]
    o = [jnp.dot(inner[h].astype(BF16), v16[h], preferred_element_type=F32)
         + jnp.dot((q[h] * qd_ref[h]).astype(BF16), s[h].astype(BF16), preferred_element_type=F32) for h in heads]
    for h in heads:
        s_scr[h] = s[h] * cd_ref[h] + _dot_tn((k[h] * kd_ref[h]).astype(BF16), v16[h])
    outs = []
    for h in heads:
        mu = jnp.mean(o[h], axis=-1, keepdims=True)
        d = o[h] - mu
        var = jnp.mean(d * d, axis=-1, keepdims=True)
        gate = g_ref[0, :, sl[h]]
        outs.append((gate * jax.nn.sigmoid(gate)
                     * (d * lax.rsqrt(var + RET_GN_EPS) * gn_ref[:, sl[h]])).astype(o_ref.dtype))
    o_ref[0] = jnp.concatenate(outs, axis=-1)


def retention_pallas(proj, q_blk, gn_gain):
    B, S, _ = proj.shape
    H, dk, dv, C = RET_HEADS, RET_DK, RET_DV, RET_CHUNK
    N = S // C
    half = dk // 2
    inv = RET_ROPE_BASE ** (-jnp.arange(half, dtype=F32) / half)
    ang = jnp.arange(S).astype(F32)[:, None] * inv[None, :]
    cos, sin = jnp.cos(ang), jnp.sin(ang)
    log_gamma = jnp.log(1.0 - 2.0 ** (-5.0 - jnp.arange(H, dtype=F32)))
    idx = jnp.arange(C, dtype=F32)
    rel = idx[:, None] - idx[None, :]
    decay_in = jnp.where(rel >= 0, jnp.exp(log_gamma[:, None, None] * jnp.maximum(rel, 0.0)), 0.0)
    q_decay = jnp.exp(log_gamma[:, None] * (idx + 1.0))[..., None]
    k_decay = jnp.exp(log_gamma[:, None] * (C - 1.0 - idx))[..., None]
    chunk_decay = jnp.exp(log_gamma * C)[:, None, None]
    hb = RET_HB
    HP = H // hb
    qb0 = q_blk // hb
    return pl.pallas_call(
        _retention_kernel,
        grid=(B, HP, N),
        in_specs=[
            pl.BlockSpec((1, C, hb * dk), lambda b, p, n: (b, n, qb0 + p)),
            pl.BlockSpec((1, C, hb * dk), lambda b, p, n: (b, n, qb0 + HP + p)),
            pl.BlockSpec((1, C, hb * dv), lambda b, p, n: (b, n, qb0 + 2 * HP + p)),
            pl.BlockSpec((1, C, hb * dv), lambda b, p, n: (b, n, qb0 + 3 * HP + p)),
            pl.BlockSpec((C, half), lambda b, p, n: (n, 0)),
            pl.BlockSpec((C, half), lambda b, p, n: (n, 0)),
            pl.BlockSpec((hb, C, C), lambda b, p, n: (p, 0, 0)),
            pl.BlockSpec((hb, C, 1), lambda b, p, n: (p, 0, 0)),
            pl.BlockSpec((hb, C, 1), lambda b, p, n: (p, 0, 0)),
            pl.BlockSpec((hb, 1, 1), lambda b, p, n: (p, 0, 0)),
            pl.BlockSpec((1, hb * dv), lambda b, p, n: (0, p)),
        ],
        out_specs=pl.BlockSpec((1, C, hb * dv), lambda b, p, n: (b, n, p)),
        out_shape=jax.ShapeDtypeStruct((B, S, H * dv), BF16),
        scratch_shapes=[pltpu.VMEM((hb, dk, dv), F32)],
        compiler_params=_cparams("parallel", "parallel", "arbitrary"),
        name="retention",
    )(proj, proj, proj, proj, cos, sin, decay_in, q_decay, k_decay, chunk_decay, gn_gain.reshape(1, H * dv))


GDN_HB = 32
GDN_HALO = 8


def _conv_silu(x_ref, halo_ref, keep, w, stage_ref):
    c = x_ref.shape[1]
    stage_ref[0:GDN_HALO, :] = halo_ref[0] * keep
    stage_ref[GDN_HALO:GDN_HALO + c, :] = x_ref[0]
    y = w[GDN_CONV - 1:GDN_CONV] * x_ref[0]
    for j in range(GDN_CONV - 1):
        off = GDN_HALO - (GDN_CONV - 1) + j
        y = y + w[j:j + 1] * stage_ref[off:off + c, :]
    return y * jax.nn.sigmoid(y)


def _l2_rows(x):
    return x * lax.rsqrt(jnp.sum(x * x, axis=-1, keepdims=True) + NORM_EPS)


def _gdn_kernel(q_ref, k_ref, v_ref, qh_ref, kh_ref, vh_ref, wq_ref, wk_ref, wv_ref, z_ref,
                ba_ref, alog_ref, dtb_ref, ng_ref, o_ref, s_scr, qst_scr, kst_scr, vst_scr):
    C, dh, hb = GDN_CHUNK, GDN_HEAD_DIM, GDN_HB
    rep = GDN_V_HEADS // GDN_QK_HEADS
    first = pl.program_id(2) == 0

    @pl.when(first)
    def _():
        s_scr[...] = jnp.zeros_like(s_scr)

    keep = jnp.where(first, 0.0, 1.0)
    qc = _conv_silu(q_ref, qh_ref, keep, wq_ref[...], qst_scr)
    kc = _conv_silu(k_ref, kh_ref, keep, wk_ref[...], kst_scr)
    vc = _conv_silu(v_ref, vh_ref, keep, wv_ref[...], vst_scr)

    ri = lax.broadcasted_iota(jnp.int32, (C, C), 0)
    ci = lax.broadcasted_iota(jnp.int32, (C, C), 1)
    causal = ri >= ci
    strict = ri > ci
    ba = ba_ref[0]
    bcol = jax.nn.sigmoid(ba[:, :hb])
    g = -jnp.exp(alog_ref[0]) * jax.nn.softplus(ba[:, hb:2 * hb] + dtb_ref[0])
    gcol = jnp.dot(jnp.where(causal, 1.0, 0.0), g, preferred_element_type=F32, precision=lax.Precision.HIGHEST)
    grow = lax.dot_general(g, jnp.where(ri <= ci, 1.0, 0.0), (((0,), (0,)), ((), ())),
                           preferred_element_type=F32, precision=lax.Precision.HIGHEST)
    heads = range(hb)
    qs, ks, grams = [], [], []
    for hq in range(hb // rep):
        qh = _l2_rows(qc[:, hq * dh:(hq + 1) * dh]) * (dh ** -0.5)
        kh = _l2_rows(kc[:, hq * dh:(hq + 1) * dh])
        k16 = kh.astype(BF16)
        qs.append(qh)
        ks.append(kh)
        grams.append(_dot_nt(jnp.concatenate([qh.astype(BF16), k16], axis=0), k16))
    beta = [bcol[:, h:h + 1] for h in heads]
    gc = [gcol[:, h:h + 1] for h in heads]
    gr = [grow[h:h + 1, :] for h in heads]
    g_last = [gr[h][:, C - 1:C] for h in heads]
    eg = [jnp.exp(gc[h]) for h in heads]
    decay = [jnp.where(causal, jnp.exp(jnp.minimum(gc[h] - gr[h], 0.0)), 0.0) for h in heads]
    attn = [(grams[h // rep][:C] * decay[h]).astype(BF16) for h in heads]
    nm = [jnp.where(strict, grams[h // rep][C:] * decay[h] * (-beta[h]), 0.0) for h in heads]
    m = [nm[h].astype(BF16) for h in heads]
    for _ in range(int(math.log2(C)) - 1):
        mf = [jnp.dot(m[h], m[h], preferred_element_type=F32) for h in heads]
        m = [mf[h].astype(BF16) for h in heads]
        nm = [nm[h] + mf[h] + jnp.dot(m[h], nm[h].astype(BF16), preferred_element_type=F32) for h in heads]
    x = [jnp.concatenate([vc[:, h * dh:(h + 1) * dh] * beta[h], ks[h // rep] * (beta[h] * eg[h])], axis=1)
         for h in heads]
    x = [x[h] + jnp.dot(nm[h].astype(BF16), x[h].astype(BF16), preferred_element_type=F32) for h in heads]
    s = [s_scr[h] for h in heads]
    ws = [jnp.dot(jnp.concatenate([x[h][:, dh:].astype(BF16), (qs[h // rep] * eg[h]).astype(BF16)], axis=0),
                  s[h].astype(BF16), preferred_element_type=F32) for h in heads]
    vn = [(x[h][:, :dh] - ws[h][:C]).astype(BF16) for h in heads]
    o = [ws[h][C:] + jnp.dot(attn[h], vn[h], preferred_element_type=F32) for h in heads]
    for h in heads:
        k_dec = (ks[h // rep] * jnp.exp(g_last[h] - gc[h])).astype(BF16)
        s_scr[h] = s[h] * jnp.exp(g_last[h]) + _dot_tn(k_dec, vn[h])
    outs = []
    for h in heads:
        zh = z_ref[0, :, h * dh:(h + 1) * dh]
        outs.append((_rms_rows(o[h], ng_ref[...]) * (zh * jax.nn.sigmoid(zh))).astype(o_ref.dtype))
    o_ref[0] = jnp.concatenate(outs, axis=1)


_GDN_GATE_COL0 = GDN_CONV_CH + GDN_V_WIDTH
_GDN_TN = 1792
_GDN_COLS = _round_up(_GDN_GATE_COL0 + GDN_V_HEADS // GDN_HB * 128, _GDN_TN)


def _gdn_w_in_layout(w_in):
    b0 = _GDN_GATE_COL0
    a0 = b0 + GDN_V_HEADS
    parts = [w_in[:, :b0]]
    for p in range(GDN_V_HEADS // GDN_HB):
        parts.append(w_in[:, b0 + p * GDN_HB:b0 + (p + 1) * GDN_HB])
        parts.append(jnp.pad(w_in[:, a0 + p * GDN_HB:a0 + (p + 1) * GDN_HB], ((0, 0), (0, 128 - 2 * GDN_HB))))
    w = jnp.concatenate(parts, axis=1)
    return jnp.pad(w, ((0, 0), (0, _GDN_COLS - w.shape[1]))).astype(BF16)


def gdn_delta_rule(proj, conv_w, a_log, dt_bias, norm_gain):
    B, S, _ = proj.shape
    C, dh, hb, H = GDN_CHUNK, GDN_HEAD_DIM, GDN_HB, GDN_V_HEADS
    rep = GDN_V_HEADS // GDN_QK_HEADS
    N, HP = S // C, H // hb
    wqk, wv = hb // rep * dh, hb * dh
    kb0, vb0, zb0 = GDN_QK_WIDTH // wqk, 2 * GDN_QK_WIDTH // wv, GDN_CONV_CH // wv
    gb0 = _GDN_GATE_COL0 // 128
    hr = C // GDN_HALO
    halo = lambda n: jnp.maximum(n * hr - 1, 0)
    return pl.pallas_call(
        _gdn_kernel,
        grid=(B, HP, N),
        in_specs=[
            pl.BlockSpec((1, C, wqk), lambda b, p, n: (b, n, p)),
            pl.BlockSpec((1, C, wqk), lambda b, p, n: (b, n, kb0 + p)),
            pl.BlockSpec((1, C, wv), lambda b, p, n: (b, n, vb0 + p)),
            pl.BlockSpec((1, GDN_HALO, wqk), lambda b, p, n: (b, halo(n), p)),
            pl.BlockSpec((1, GDN_HALO, wqk), lambda b, p, n: (b, halo(n), kb0 + p)),
            pl.BlockSpec((1, GDN_HALO, wv), lambda b, p, n: (b, halo(n), vb0 + p)),
            pl.BlockSpec((GDN_CONV, wqk), lambda b, p, n: (0, p)),
            pl.BlockSpec((GDN_CONV, wqk), lambda b, p, n: (0, kb0 + p)),
            pl.BlockSpec((GDN_CONV, wv), lambda b, p, n: (0, vb0 + p)),
            pl.BlockSpec((1, C, wv), lambda b, p, n: (b, n, zb0 + p)),
            pl.BlockSpec((1, C, 128), lambda b, p, n: (b, n, gb0 + p)),
            pl.BlockSpec((1, 1, hb), lambda b, p, n: (p, 0, 0)),
            pl.BlockSpec((1, 1, hb), lambda b, p, n: (p, 0, 0)),
            pl.BlockSpec((1, dh), lambda b, p, n: (0, 0)),
        ],
        out_specs=pl.BlockSpec((1, C, wv), lambda b, p, n: (b, n, p)),
        out_shape=jax.ShapeDtypeStruct((B, S, H * dh), BF16),
        scratch_shapes=[pltpu.VMEM((hb, dh, dh), F32), pltpu.VMEM((GDN_HALO + C, wqk), F32),
                        pltpu.VMEM((GDN_HALO + C, wqk), F32), pltpu.VMEM((GDN_HALO + C, wv), F32)],
        compiler_params=_cparams("parallel", "parallel", "arbitrary"),
        name="gdn_delta_rule",
    )(proj, proj, proj, proj, proj, proj, conv_w, conv_w, conv_w, proj, proj,
      a_log.astype(F32).reshape(HP, 1, hb), dt_bias.astype(F32).reshape(HP, 1, hb), norm_gain.reshape(1, dh))


_HYB_NSA_COLS = NSA_WIDTH + 6 * NSA_KV_WIDTH
_HYB_RET_COLS = 2 * RET_HEADS * RET_DK + 2 * RET_HEADS * RET_DV
_HYB_GATE_COL0 = _HYB_NSA_COLS + _HYB_RET_COLS
_HYB_TN = 768
_HYB_COLS = _round_up(_HYB_GATE_COL0 + NSA_KV_GROUPS * 128, _HYB_TN)


def _hybrid_w_in_layout(w_in):
    gate0 = _HYB_NSA_COLS
    ret0 = gate0 + 3 * NSA_HEADS
    per_group = 3 * NSA_GROUP_HEADS
    parts = [w_in[:, :gate0], w_in[:, ret0:ret0 + _HYB_RET_COLS]]
    for g in range(NSA_KV_GROUPS):
        parts.append(jnp.pad(w_in[:, gate0 + g * per_group:gate0 + (g + 1) * per_group],
                             ((0, 0), (0, 128 - per_group))))
    w = jnp.concatenate(parts, axis=1)
    return jnp.pad(w, ((0, 0), (0, _HYB_COLS - w.shape[1]))).astype(BF16)


def hybrid_mixer(h, ln_gain, w_in, w_out, q_gain, k_gain, cmp_pos, cmp_w1, cmp_w2, gn_gain, rel_bias):
    B, S, D = h.shape
    proj = norm_matmul(h.reshape(B * S, D), ln_gain, _hybrid_w_in_layout(w_in), tn=_HYB_TN).reshape(B, S, _HYB_COLS)
    cmp = nsa_compress(proj, NSA_WIDTH // NSA_HEAD_DIM, cmp_pos, cmp_w1, cmp_w2, k_gain[0])
    ksn, vsb, kwn, vwb = nsa_kv_prep(proj, (NSA_WIDTH + 2 * NSA_KV_WIDTH) // NSA_KV_WIDTH, k_gain)
    a_out = nsa_attention_pallas(proj, _HYB_GATE_COL0 // 128, cmp[0], cmp[1], ksn, vsb, kwn, vwb, q_gain, rel_bias)
    b_out = retention_pallas(proj, _HYB_NSA_COLS // RET_DK, gn_gain)
    w_out = w_out.astype(BF16)
    return matmul2_residual(a_out.reshape(B * S, -1), b_out.reshape(B * S, -1), w_out,
                            h.reshape(B * S, D)).reshape(B, S, D)


def gdn_mixer(h, ln_gain, w_in, conv_w, a_log, dt_bias, norm_gain, w_out):
    B, S, D = h.shape
    proj = norm_matmul(h.reshape(B * S, D), ln_gain, _gdn_w_in_layout(w_in), tn=_GDN_TN).reshape(B, S, _GDN_COLS)
    o = gdn_delta_rule(proj, conv_w, a_log, dt_bias, norm_gain)
    return matmul_residual(o.reshape(B * S, GDN_V_WIDTH), w_out.astype(BF16), h.reshape(B * S, D)).reshape(B, S, D)


def memory_kv(mem, mem_gain, wkv, k_gain):
    B, M, D = mem.shape
    kv = norm_matmul(mem.reshape(B * M, D), mem_gain, wkv.astype(BF16))
    k, v = jnp.split(kv.reshape(B, M, 2 * XA_WIDTH), 2, axis=-1)
    k = rms_norm(k.reshape(B, M, XA_HEADS, XA_HEAD_DIM), k_gain).reshape(B, M, XA_WIDTH)
    return k.astype(BF16), v.astype(BF16)


def kernel(x, mem, rel_bias, ln_mix, ln_mem, ln_ffn, hyb_w_in, hyb_w_out, nsa_q_gain, nsa_k_gain,
           nsa_cmp_pos, nsa_cmp_w1, nsa_cmp_w2, ret_gn_gain, gdn_w_in, gdn_conv_w, gdn_a_log,
           gdn_dt_bias, gdn_norm_gain, gdn_w_out, xa_wq, xa_wkv, xa_q_gain, xa_k_gain, xa_mem_gain,
           xa_wo, ffn_w_in, ffn_w_out):
    B, S, D = x.shape
    h = x
    ffn_w_in, ffn_w_out, gdn_w_out = cast_bf16(ffn_w_in), cast_bf16(ffn_w_out), cast_bf16(gdn_w_out)
    for layer in range(DEPTH):
        if layer % 2 == 0:
            e = layer // 2
            h = hybrid_mixer(h, ln_mix[layer], hyb_w_in[e], hyb_w_out[e], nsa_q_gain[e], nsa_k_gain[e],
                             nsa_cmp_pos[e], nsa_cmp_w1[e], nsa_cmp_w2[e], ret_gn_gain[e], rel_bias)
        else:
            o = layer // 2
            h = gdn_mixer(h, ln_mix[layer], gdn_w_in[o], gdn_conv_w[o], gdn_a_log[o], gdn_dt_bias[o],
                          gdn_norm_gain[o], gdn_w_out[o])
        k_mem, v_mem = memory_kv(mem, xa_mem_gain[layer], xa_wkv[layer], xa_k_gain[layer])
        h = xattn_residual(h, ln_mem[layer], xa_wq[layer].astype(BF16), xa_q_gain[layer], k_mem, v_mem,
                           xa_wo[layer].astype(BF16))
        h = ffn_residual(h.reshape(B * S, D), ln_ffn[layer], ffn_w_in[layer].astype(BF16),
                         ffn_w_out[layer].astype(BF16)).reshape(B, S, D)
    return h
```

```python
import functools
import math

import jax
import jax.numpy as jnp
import numpy as np
from jax import lax
from jax.experimental import pallas as pl
from jax.experimental.pallas import tpu as pltpu

F32 = jnp.float32
BF16 = jnp.bfloat16

D_MODEL = 2048
DEPTH = 4
NORM_EPS = 1e-6
NEG_INF = -1e30

NSA_HEADS = 8
NSA_KV_GROUPS = 2
NSA_GROUP_HEADS = NSA_HEADS // NSA_KV_GROUPS
NSA_HEAD_DIM = 128
NSA_CMP_LEN = 32
NSA_CMP_STRIDE = 16
NSA_SEL_BLOCK = 64
NSA_N_SELECT = 16
NSA_WINDOW = 512
NSA_Q_BLOCK = 64
NSA_FORCE_SCORE = 1e6
NSA_WIDTH = NSA_HEADS * NSA_HEAD_DIM
NSA_KV_WIDTH = NSA_KV_GROUPS * NSA_HEAD_DIM

RET_HEADS = 4
RET_DK = 256
RET_DV = 256
RET_CHUNK = 128
RET_ROPE_BASE = 10000.0
RET_GN_EPS = 1e-5

GDN_QK_HEADS = 16
GDN_V_HEADS = 32
GDN_HEAD_DIM = 128
GDN_CONV = 4
GDN_CHUNK = 64
GDN_QK_WIDTH = GDN_QK_HEADS * GDN_HEAD_DIM
GDN_V_WIDTH = GDN_V_HEADS * GDN_HEAD_DIM
GDN_CONV_CH = 2 * GDN_QK_WIDTH + GDN_V_WIDTH

REL_BUCKETS = 32
REL_MAX_DIST = 1024

XA_HEADS = 4
XA_HEAD_DIM = 128
XA_WIDTH = XA_HEADS * XA_HEAD_DIM

V7X_VMEM_LIMIT_BYTES = 56 * 1024 * 1024
V7X_LANES = 128


def _cparams(*sem):
    return pltpu.CompilerParams(dimension_semantics=sem, vmem_limit_bytes=V7X_VMEM_LIMIT_BYTES)


def _round_up(n, m):
    return -(-n // m) * m


def _rms_rows(x, gain):
    return x * lax.rsqrt(jnp.mean(x * x, axis=-1, keepdims=True) + NORM_EPS) * gain


def _cast_kernel(x_ref, o_ref):
    o_ref[...] = x_ref[...].astype(o_ref.dtype)


def cast_bf16(w, *, block_bytes=8 * 1024 * 1024):
    n = w.shape[-1]
    x = w.reshape(-1, n)
    rows = x.shape[0]
    tr = rows
    while tr * n * 4 > block_bytes and tr % 32 == 0:
        tr //= 2
    out = pl.pallas_call(
        _cast_kernel,
        grid=(rows // tr,),
        in_specs=[pl.BlockSpec((tr, n), lambda i: (i, 0))],
        out_specs=pl.BlockSpec((tr, n), lambda i: (i, 0)),
        out_shape=jax.ShapeDtypeStruct((rows, n), BF16),
        compiler_params=_cparams("parallel"),
        name="cast_bf16",
    )(x)
    return out.reshape(w.shape)


def _norm_matmul_kernel(x_ref, g_ref, w_ref, o_ref, xn_ref):
    @pl.when(pl.program_id(1) == 0)
    def _():
        xn_ref[...] = _rms_rows(x_ref[...], g_ref[...]).astype(BF16)

    o_ref[...] = jnp.dot(xn_ref[...], w_ref[...], preferred_element_type=F32).astype(o_ref.dtype)


def norm_matmul(x, gain, w, *, tm=1024, tn=512, out_dtype=F32):
    m, k = x.shape
    n = w.shape[1]
    tm = min(tm, m)
    return pl.pallas_call(
        _norm_matmul_kernel,
        grid=(m // tm, n // tn),
        in_specs=[pl.BlockSpec((tm, k), lambda i, j: (i, 0)),
                  pl.BlockSpec((1, k), lambda i, j: (0, 0)),
                  pl.BlockSpec((k, tn), lambda i, j: (0, j))],
        out_specs=pl.BlockSpec((tm, tn), lambda i, j: (i, j)),
        out_shape=jax.ShapeDtypeStruct((m, n), out_dtype),
        scratch_shapes=[pltpu.VMEM((tm, k), BF16)],
        compiler_params=_cparams("parallel", "arbitrary"),
        name="norm_matmul",
    )(x, gain.reshape(1, k), w)


def _matmul_res_kernel(x_ref, w_ref, r_ref, o_ref):
    o_ref[...] = r_ref[...] + jnp.dot(x_ref[...], w_ref[...], preferred_element_type=F32)


def matmul_residual(x, w, res, *, tm=1024, tn=1024):
    m, k = x.shape
    n = w.shape[1]
    return pl.pallas_call(
        _matmul_res_kernel,
        grid=(m // tm, n // tn),
        in_specs=[pl.BlockSpec((tm, k), lambda i, j: (i, 0)),
                  pl.BlockSpec((k, tn), lambda i, j: (0, j)),
                  pl.BlockSpec((tm, tn), lambda i, j: (i, j))],
        out_specs=pl.BlockSpec((tm, tn), lambda i, j: (i, j)),
        out_shape=jax.ShapeDtypeStruct((m, n), F32),
        compiler_params=_cparams("parallel", "arbitrary"),
        name="matmul_residual",
    )(x, w, res)


def _matmul2_res_kernel(x1_ref, x2_ref, w1_ref, w2_ref, r_ref, o_ref):
    o_ref[...] = (r_ref[...] + jnp.dot(x1_ref[...], w1_ref[...], preferred_element_type=F32)
                  + jnp.dot(x2_ref[...], w2_ref[...], preferred_element_type=F32))


def matmul2_residual(x1, x2, w, res, *, tm=1024, tn=1024):
    m, k1 = x1.shape
    n = w.shape[1]
    return pl.pallas_call(
        _matmul2_res_kernel,
        grid=(m // tm, n // tn),
        in_specs=[pl.BlockSpec((tm, k1), lambda i, j: (i, 0)),
                  pl.BlockSpec((tm, k1), lambda i, j: (i, 0)),
                  pl.BlockSpec((k1, tn), lambda i, j: (0, j)),
                  pl.BlockSpec((k1, tn), lambda i, j: (1, j)),
                  pl.BlockSpec((tm, tn), lambda i, j: (i, j))],
        out_specs=pl.BlockSpec((tm, tn), lambda i, j: (i, j)),
        out_shape=jax.ShapeDtypeStruct((m, n), F32),
        compiler_params=_cparams("parallel", "arbitrary"),
        name="matmul2_residual",
    )(x1, x2, w, w, res)


def _ffn_kernel(x_ref, g_ref, wg_ref, wu_ref, wo_ref, o_ref, xn_ref, acc_ref):
    j = pl.program_id(1)

    @pl.when(j == 0)
    def _():
        x = x_ref[...]
        xn_ref[...] = _rms_rows(x, g_ref[...]).astype(BF16)
        acc_ref[...] = x

    xn = xn_ref[...]
    gate = jnp.dot(xn, wg_ref[...], preferred_element_type=F32)
    up = jnp.dot(xn, wu_ref[...], preferred_element_type=F32)
    act = (gate * jax.nn.sigmoid(gate) * up).astype(BF16)
    acc_ref[...] += jnp.dot(act, wo_ref[...], preferred_element_type=F32)

    @pl.when(j == pl.num_programs(1) - 1)
    def _():
        o_ref[...] = acc_ref[...]


def ffn_residual(h, gain, w_in, w_out, *, tm=512, th=512):
    m, d = h.shape
    hidden = w_out.shape[0]
    nh = hidden // th
    return pl.pallas_call(
        _ffn_kernel,
        grid=(m // tm, nh),
        in_specs=[pl.BlockSpec((tm, d), lambda i, j: (i, 0)),
                  pl.BlockSpec((1, d), lambda i, j: (0, 0)),
                  pl.BlockSpec((d, th), lambda i, j: (0, j)),
                  pl.BlockSpec((d, th), lambda i, j: (0, j + nh)),
                  pl.BlockSpec((th, d), lambda i, j: (j, 0))],
        out_specs=pl.BlockSpec((tm, d), lambda i, j: (i, 0)),
        out_shape=jax.ShapeDtypeStruct((m, d), F32),
        scratch_shapes=[pltpu.VMEM((tm, d), BF16), pltpu.VMEM((tm, d), F32)],
        compiler_params=_cparams("parallel", "arbitrary"),
        name="ffn_residual",
    )(h, gain.reshape(1, d), w_in, w_in, w_out)


def _xattn_kernel(h_ref, g_ref, wq_ref, qg_ref, k_ref, v_ref, wo_ref, o_ref):
    x = h_ref[0]
    xn = _rms_rows(x, g_ref[...]).astype(BF16)
    q = jnp.dot(xn, wq_ref[...], preferred_element_type=F32)
    k = k_ref[0]
    v = v_ref[0]
    outs = []
    for hh in range(XA_HEADS):
        sl = slice(hh * XA_HEAD_DIM, (hh + 1) * XA_HEAD_DIM)
        qh = _rms_rows(q[:, sl], qg_ref[...]).astype(BF16)
        s = lax.dot_general(qh, k[:, sl], (((1,), (1,)), ((), ())), preferred_element_type=F32)
        s = s * (XA_HEAD_DIM ** -0.5)
        s = s - jnp.max(s, axis=-1, keepdims=True)
        p = jnp.exp(s)
        p = p / jnp.sum(p, axis=-1, keepdims=True)
        outs.append(jnp.dot(p.astype(BF16), v[:, sl], preferred_element_type=F32))
    o = jnp.concatenate(outs, axis=-1).astype(BF16)
    o_ref[0] = x + jnp.dot(o, wo_ref[...], preferred_element_type=F32)


def xattn_residual(h, gain, wq, q_gain, k, v, wo, *, tm=512):
    b, s, d = h.shape
    mt = k.shape[1]
    return pl.pallas_call(
        _xattn_kernel,
        grid=(b, s // tm),
        in_specs=[pl.BlockSpec((1, tm, d), lambda bi, i: (bi, i, 0)),
                  pl.BlockSpec((1, d), lambda bi, i: (0, 0)),
                  pl.BlockSpec((d, XA_WIDTH), lambda bi, i: (0, 0)),
                  pl.BlockSpec((1, XA_HEAD_DIM), lambda bi, i: (0, 0)),
                  pl.BlockSpec((1, mt, XA_WIDTH), lambda bi, i: (bi, 0, 0)),
                  pl.BlockSpec((1, mt, XA_WIDTH), lambda bi, i: (bi, 0, 0)),
                  pl.BlockSpec((XA_WIDTH, d), lambda bi, i: (0, 0))],
        out_specs=pl.BlockSpec((1, tm, d), lambda bi, i: (bi, i, 0)),
        out_shape=jax.ShapeDtypeStruct((b, s, d), F32),
        compiler_params=_cparams("parallel", "parallel"),
        name="xattn_residual",
    )(h, gain.reshape(1, d), wq, q_gain.reshape(1, XA_HEAD_DIM), k, v, wo)


def rms_norm(x, gain, eps=NORM_EPS):
    x32 = x.astype(F32)
    y = x32 * lax.rsqrt(jnp.mean(x32 * x32, axis=-1, keepdims=True) + eps)
    return (y * gain.astype(F32)).astype(x.dtype)


def rel_bucket(dist):
    dist = jnp.maximum(dist, 0)
    max_exact = REL_BUCKETS // 2
    scaled = (jnp.log(jnp.maximum(dist, max_exact).astype(F32) / max_exact)
              / math.log(REL_MAX_DIST / max_exact) * (REL_BUCKETS - max_exact))
    large = jnp.minimum(max_exact + scaled.astype(jnp.int32), REL_BUCKETS - 1)
    return jnp.where(dist < max_exact, dist, large)


NSA_QT = 128
NSA_KT = 512
NSA_WT = NSA_WINDOW + NSA_QT


def _dot_nt(a, b):
    return lax.dot_general(a, b, (((1,), (1,)), ((), ())), preferred_element_type=F32)


def _dot_tn(a, b):
    return lax.dot_general(a, b, (((0,), (0,)), ((), ())), preferred_element_type=F32)


def _bias_tile(tb_ref, h, first_slab, n_slabs):
    return jnp.concatenate([tb_ref[h, jnp.maximum(first_slab - j, 0)] for j in range(n_slabs)], axis=1)


def _nsa_kernel(q_ref, gl_ref, kc_ref, vc_ref, bc_ref, covt_ref, ks_ref, vs_ref, kw_ref, vw_ref,
                tb_ref, e_ref, qg_ref, o_ref, a_scr, *, n_top):
    QT, KT, WT, HG, dh = NSA_QT, NSA_KT, NSA_WT, NSA_GROUP_HEADS, NSA_HEAD_DIM
    nsb, nc = covt_ref.shape
    q0 = pl.program_id(2) * QT

    x = q_ref[0]
    qs = []
    for h in range(HG):
        xh = _rms_rows(x[:, h * dh:(h + 1) * dh], qg_ref[...]) * (dh ** -0.5)
        qs.append(xh.astype(BF16))
    qn = jnp.concatenate(qs, axis=0)

    sc = _dot_nt(qn, kc_ref[0, 0]) + bc_ref[...].reshape(HG * QT, nc)
    row = lax.broadcasted_iota(jnp.int32, (HG * QT, nc), 0)
    col = lax.broadcasted_iota(jnp.int32, (HG * QT, nc), 1)
    tq = q0 + (row & (QT - 1))
    mask_c = tq >= col * NSA_CMP_STRIDE + (NSA_CMP_LEN - 1)
    sc = jnp.where(mask_c, sc, NEG_INF)
    pc = jnp.where(mask_c, jnp.exp(sc - jnp.max(sc, axis=-1, keepdims=True)), 0.0)
    den = jnp.sum(pc, axis=-1, keepdims=True)
    pcb = (pc / jnp.maximum(den, 1e-30)).astype(BF16)
    o_c = jnp.dot(pcb, vc_ref[0, 0], preferred_element_type=F32)

    imp_all = _dot_nt(covt_ref[...], pcb)
    imp = imp_all[:, 0:QT]
    for h in range(1, HG):
        imp = imp + imp_all[:, h * QT:(h + 1) * QT]
    jj = lax.broadcasted_iota(jnp.int32, (nsb, QT), 0)
    cur = (q0 + lax.broadcasted_iota(jnp.int32, (nsb, QT), 1)) // NSA_SEL_BLOCK
    forced = (jj == 0) | (jj == cur) | (jj == cur - 1)
    a = jnp.where(forced, NSA_FORCE_SCORE, imp)
    a = jnp.where(jj > cur, -NSA_FORCE_SCORE, a)
    a_scr[...] = a

    def rank_body(i, rank):
        r = a_scr[pl.ds(i, 1), :]
        return rank + jnp.where((r > a) | ((r == a) & (jj > i)), 1.0, 0.0)

    n_live = jnp.minimum((q0 + QT - 1) // NSA_SEL_BLOCK + 1, nsb)
    rank = lax.fori_loop(0, n_live, rank_body, jnp.zeros((nsb, QT), F32))
    sel_t = jnp.where(rank < n_top, 1.0, 0.0)
    if nsb < V7X_LANES:
        sel_t = jnp.concatenate([sel_t, jnp.zeros((V7X_LANES - nsb, QT), F32)], axis=0)
    sel = sel_t.T.astype(BF16)

    cmr = (lax.broadcasted_iota(jnp.int32, (QT, KT), 1) - lax.broadcasted_iota(jnp.int32, (QT, KT), 0))

    def sel_body(kt, carry):
        m_i, l_i, acc = carry
        k0 = pl.multiple_of(kt * KT, KT)
        k = ks_ref[0, pl.ds(k0, KT), :]
        v = vs_ref[0, pl.ds(k0, KT), :]
        s = _dot_nt(qn, k)
        delta = q0 - k0
        visible = (jnp.dot(sel, e_ref[kt], preferred_element_type=F32) > 0.5) & (cmr <= delta)
        seg = delta // QT
        parts = []
        for h in range(HG):
            bias = _bias_tile(tb_ref, h, seg, KT // QT)
            parts.append(jnp.where(visible, s[h * QT:(h + 1) * QT] + bias, NEG_INF))
        s = jnp.concatenate(parts, axis=0)
        m_new = jnp.maximum(m_i, jnp.max(s, axis=-1, keepdims=True))
        alpha = jnp.exp(m_i - m_new)
        p = jnp.exp(s - m_new)
        l_new = alpha * l_i + jnp.sum(p, axis=-1, keepdims=True)
        acc = alpha * acc + jnp.dot(p.astype(BF16), v, preferred_element_type=F32)
        return m_new, l_new, acc

    init = (jnp.full((HG * QT, 1), NEG_INF, F32), jnp.zeros((HG * QT, 1), F32), jnp.zeros((HG * QT, dh), F32))
    _, l_s, acc_s = lax.fori_loop(0, q0 // KT + 1, sel_body, init)
    o_s = acc_s / l_s

    w0 = pl.multiple_of(q0, QT)
    kwin = kw_ref[0, pl.ds(w0, WT), :]
    vwin = vw_ref[0, pl.ds(w0, WT), :]
    sw = _dot_nt(qn, kwin)
    cw = lax.broadcasted_iota(jnp.int32, (QT, WT), 1)
    dist = NSA_WINDOW + lax.broadcasted_iota(jnp.int32, (QT, WT), 0) - cw
    vis_w = (dist >= 0) & (dist < NSA_WINDOW) & (cw >= NSA_WINDOW - q0)
    parts = []
    for h in range(HG):
        bias = _bias_tile(tb_ref, h, NSA_WINDOW // QT, WT // QT)
        parts.append(jnp.where(vis_w, sw[h * QT:(h + 1) * QT] + bias, NEG_INF))
    sw = jnp.concatenate(parts, axis=0)
    pw = jnp.exp(sw - jnp.max(sw, axis=-1, keepdims=True))
    o_w = jnp.dot(pw.astype(BF16), vwin, preferred_element_type=F32) / jnp.sum(pw, axis=-1, keepdims=True)

    gates = jax.nn.sigmoid(gl_ref[0][:, :3 * HG])
    outs = []
    for h in range(HG):
        rs = slice(h * QT, (h + 1) * QT)
        outs.append(gates[:, 3 * h:3 * h + 1] * o_c[rs] + gates[:, 3 * h + 1:3 * h + 2] * o_s[rs]
                    + gates[:, 3 * h + 2:3 * h + 3] * o_w[rs])
    o_ref[0] = jnp.concatenate(outs, axis=-1).astype(o_ref.dtype)


def nsa_attention_pallas(proj, q_blk, gl_blk, kc, vc, ks, vs, kw, vw, q_gain, rel_bias):
    B, S, _ = proj.shape
    G, HG, dh = NSA_KV_GROUPS, NSA_GROUP_HEADS, NSA_HEAD_DIM
    QT, KT, WT, W, SB = NSA_QT, NSA_KT, NSA_WT, NSA_WINDOW, NSA_SEL_BLOCK
    nsb, nc, nkt, nseg = S // SB, S // NSA_CMP_STRIDE, S // KT, max(S // QT, W // QT + 1)
    n_top = min(NSA_N_SELECT, nsb)

    def bias_of(dist):
        onehot = (rel_bucket(dist)[..., None] == jnp.arange(REL_BUCKETS)) & (dist >= 0)[..., None]
        return jnp.einsum('...b,bh->h...', onehot.astype(F32), rel_bias.astype(F32),
                          precision=lax.Precision.HIGHEST)

    bc = bias_of(jnp.arange(S)[:, None] - (jnp.arange(nc) * NSA_CMP_STRIDE + NSA_CMP_LEN - 1)[None, :])
    tb = bias_of(QT * jnp.arange(nseg)[:, None, None] + jnp.arange(QT)[None, :, None] - jnp.arange(QT)[None, None, :])
    cmp_start = np.arange(nc) * NSA_CMP_STRIDE
    cmp_end = cmp_start + NSA_CMP_LEN - 1
    sel_start = np.arange(nsb) * SB
    cover_t = ((cmp_start[None, :] < sel_start[:, None] + SB) & (cmp_end[None, :] >= sel_start[:, None])
               & (np.arange(nc)[None, :] < nc - 1))
    cover_t = jnp.asarray(cover_t, BF16)
    e = (np.arange(V7X_LANES)[None, :, None]
         == (np.arange(nkt)[:, None, None] * (KT // SB) + np.arange(KT)[None, None, :] // SB))
    e = jnp.asarray(e, BF16)

    kern = functools.partial(_nsa_kernel, n_top=n_top)
    return pl.pallas_call(
        kern,
        grid=(B, G, S // QT),
        in_specs=[
            pl.BlockSpec((1, QT, HG * dh), lambda b, g, i: (b, i, q_blk + g)),
            pl.BlockSpec((1, QT, V7X_LANES), lambda b, g, i: (b, i, gl_blk + g)),
            pl.BlockSpec((1, 1, nc, dh), lambda b, g, i: (b, g, 0, 0)),
            pl.BlockSpec((1, 1, nc, dh), lambda b, g, i: (b, g, 0, 0)),
            pl.BlockSpec((HG, QT, nc), lambda b, g, i: (g, i, 0)),
            pl.BlockSpec((nsb, nc), lambda b, g, i: (0, 0)),
            pl.BlockSpec((1, S, dh), lambda b, g, i: (b, 0, g)),
            pl.BlockSpec((1, S, dh), lambda b, g, i: (b, 0, g)),
            pl.BlockSpec((1, S + W, dh), lambda b, g, i: (b, 0, g)),
            pl.BlockSpec((1, S + W, dh), lambda b, g, i: (b, 0, g)),
            pl.BlockSpec((HG, nseg, QT, QT), lambda b, g, i: (g, 0, 0, 0)),
            pl.BlockSpec((nkt, V7X_LANES, KT), lambda b, g, i: (0, 0, 0)),
            pl.BlockSpec((1, dh), lambda b, g, i: (0, 0)),
        ],
        out_specs=pl.BlockSpec((1, QT, HG * dh), lambda b, g, i: (b, i, g)),
        out_shape=jax.ShapeDtypeStruct((B, S, NSA_WIDTH), BF16),
        scratch_shapes=[pltpu.VMEM((nsb, QT), F32)],
        compiler_params=_cparams("parallel", "parallel", "arbitrary"),
        name="nsa_attention",
    )(proj, proj, kc, vc, bc, cover_t, ks, vs, kw, vw, tb, e, q_gain.reshape(1, dh))


def _nsa_compress_kernel(x_ref, pos_ref, w1_ref, w2_ref, kg_ref, o_ref):
    stride, dh = NSA_CMP_STRIDE, NSA_HEAD_DIM
    n_chunks = x_ref.shape[1] // stride
    hid = w1_ref.shape[2]
    h_lo = jnp.zeros((n_chunks, hid), F32)
    h_hi = jnp.zeros((n_chunks, hid), F32)
    for p in range(stride):
        xp = x_ref[0, pl.ds(p, n_chunks, stride=stride), :]
        h_lo = h_lo + jnp.dot((xp + pos_ref[0, p:p + 1, :]).astype(BF16), w1_ref[0, p * dh:(p + 1) * dh, :],
                              preferred_element_type=F32)
        h_hi = h_hi + jnp.dot((xp + pos_ref[0, stride + p:stride + p + 1, :]).astype(BF16),
                              w1_ref[0, (stride + p) * dh:(stride + p + 1) * dh, :], preferred_element_type=F32)
    hidden = h_lo + jnp.concatenate([h_hi[1:], jnp.zeros((1, hid), F32)], axis=0)
    out = jnp.dot((hidden * jax.nn.sigmoid(hidden)).astype(BF16), w2_ref[0], preferred_element_type=F32)
    out = jnp.where(pl.program_id(2) == 0, _rms_rows(out, kg_ref[...]), out)
    row = lax.broadcasted_iota(jnp.int32, out.shape, 0)
    o_ref[0, 0, 0] = jnp.where(row < n_chunks - 1, out, 0.0).astype(o_ref.dtype)


def nsa_compress(proj, col_blk, cmp_pos, cmp_w1, cmp_w2, k_gain):
    B, S, _ = proj.shape
    G, dh = NSA_KV_GROUPS, NSA_HEAD_DIM
    nc = S // NSA_CMP_STRIDE
    hid = cmp_w1.shape[-1]
    return pl.pallas_call(
        _nsa_compress_kernel,
        grid=(B, G, 2),
        in_specs=[
            pl.BlockSpec((1, S, dh), lambda b, g, w: (b, 0, col_blk + w * G + g)),
            pl.BlockSpec((1, NSA_CMP_LEN, dh), lambda b, g, w: (w, 0, 0)),
            pl.BlockSpec((1, NSA_CMP_LEN * dh, hid), lambda b, g, w: (w, 0, 0)),
            pl.BlockSpec((1, hid, dh), lambda b, g, w: (w, 0, 0)),
            pl.BlockSpec((1, dh), lambda b, g, w: (0, 0)),
        ],
        out_specs=pl.BlockSpec((1, 1, 1, nc, dh), lambda b, g, w: (w, b, g, 0, 0)),
        out_shape=jax.ShapeDtypeStruct((2, B, G, nc, dh), BF16),
        compiler_params=_cparams("parallel", "parallel", "arbitrary"),
        name="nsa_compress",
    )(proj, cmp_pos, cmp_w1.astype(BF16), cmp_w2.astype(BF16), k_gain.reshape(1, dh))


NSA_PREP_TM = NSA_WINDOW


def _nsa_kv_kernel(ks_ref, vs_ref, kw_ref, vw_ref, gs_ref, gw_ref, kso_ref, vso_ref, kwo_ref, vwo_ref):
    G, dh = NSA_KV_GROUPS, NSA_HEAD_DIM
    i = pl.program_id(1)
    n_in = pl.num_programs(1) - 1

    def normed(ref, gain_ref):
        x = ref[0]
        return jnp.concatenate([_rms_rows(x[:, g * dh:(g + 1) * dh], gain_ref[...]) for g in range(G)], axis=1)

    @pl.when(i < n_in)
    def _():
        kso_ref[0] = normed(ks_ref, gs_ref).astype(kso_ref.dtype)
        vso_ref[0] = vs_ref[0].astype(vso_ref.dtype)

    @pl.when(i == 0)
    def _():
        kwo_ref[0] = jnp.zeros(kwo_ref.shape[1:], kwo_ref.dtype)
        vwo_ref[0] = jnp.zeros(vwo_ref.shape[1:], vwo_ref.dtype)

    @pl.when(i > 0)
    def _():
        kwo_ref[0] = normed(kw_ref, gw_ref).astype(kwo_ref.dtype)
        vwo_ref[0] = vw_ref[0].astype(vwo_ref.dtype)


def nsa_kv_prep(proj, col_blk, k_gain):
    B, S, _ = proj.shape
    tm, kvw, W = NSA_PREP_TM, NSA_KV_WIDTH, NSA_WINDOW
    n_in = S // tm
    cur = lambda i: jnp.minimum(i, n_in - 1)
    prev = lambda i: jnp.maximum(i - 1, 0)
    return pl.pallas_call(
        _nsa_kv_kernel,
        grid=(B, n_in + 1),
        in_specs=[
            pl.BlockSpec((1, tm, kvw), lambda b, i: (b, cur(i), col_blk)),
            pl.BlockSpec((1, tm, kvw), lambda b, i: (b, cur(i), col_blk + 1)),
            pl.BlockSpec((1, tm, kvw), lambda b, i: (b, prev(i), col_blk + 2)),
            pl.BlockSpec((1, tm, kvw), lambda b, i: (b, prev(i), col_blk + 3)),
            pl.BlockSpec((1, NSA_HEAD_DIM), lambda b, i: (0, 0)),
            pl.BlockSpec((1, NSA_HEAD_DIM), lambda b, i: (0, 0)),
        ],
        out_specs=[
            pl.BlockSpec((1, tm, kvw), lambda b, i: (b, cur(i), 0)),
            pl.BlockSpec((1, tm, kvw), lambda b, i: (b, cur(i), 0)),
            pl.BlockSpec((1, tm, kvw), lambda b, i: (b, i, 0)),
            pl.BlockSpec((1, tm, kvw), lambda b, i: (b, i, 0)),
        ],
        out_shape=[jax.ShapeDtypeStruct((B, S, kvw), BF16), jax.ShapeDtypeStruct((B, S, kvw), BF16),
                   jax.ShapeDtypeStruct((B, S + W, kvw), BF16), jax.ShapeDtypeStruct((B, S + W, kvw), BF16)],
        compiler_params=_cparams("parallel", "arbitrary"),
        name="nsa_kv_prep",
    )(proj, proj, proj, proj, k_gain[1:2], k_gain[2:3])


RET_HB = 4


def _rotate_half(x, cos, sin):
    half = x.shape[-1] // 2
    x1, x2 = x[:, :half], x[:, half:]
    return jnp.concatenate([x1 * cos - x2 * sin, x1 * sin + x2 * cos], axis=-1)


def _retention_kernel(q_ref, k_ref, v_ref, g_ref, cos_ref, sin_ref, din_ref, qd_ref, kd_ref, cd_ref, gn_ref,
                      o_ref, s_scr):
    @pl.when(pl.program_id(2) == 0)
    def _():
        s_scr[...] = jnp.zeros_like(s_scr)

    cos, sin = cos_ref[...], sin_ref[...]
    heads = range(RET_HB)
    sl = [slice(h * RET_DK, (h + 1) * RET_DK) for h in heads]
    q = [_rotate_half(q_ref[0, :, sl[h]], cos, sin) * (RET_DK ** -0.5) for h in heads]
    k = [_rotate_half(k_ref[0, :, sl[h]], cos, sin) for h in heads]
    v16 = [v_ref[0, :, sl[h]].astype(BF16) for h in heads]
    inner = [_dot_nt(q[h].astype(BF16), k[h].astype(BF16)) * din_ref[h] for h in heads]
    s = [s_scr[h] for h in heads]
    o = [jnp.dot(inner[h].astype(BF16), v16[h], preferred_element_type=F32)
         + jnp.dot((q[h] * qd_ref[h]).astype(BF16), s[h].astype(BF16), preferred_element_type=F32) for h in heads]
    for h in heads:
        s_scr[h] = s[h] * cd_ref[h] + _dot_tn((k[h] * kd_ref[h]).astype(BF16), v16[h])
    outs = []
    for h in heads:
        mu = jnp.mean(o[h], axis=-1, keepdims=True)
        d = o[h] - mu
        var = jnp.mean(d * d, axis=-1, keepdims=True)
        gate = g_ref[0, :, sl[h]]
        outs.append((gate * jax.nn.sigmoid(gate)
                     * (d * lax.rsqrt(var + RET_GN_EPS) * gn_ref[:, sl[h]])).astype(o_ref.dtype))
    o_ref[0] = jnp.concatenate(outs, axis=-1)


def retention_pallas(proj, q_blk, gn_gain):
    B, S, _ = proj.shape
    H, dk, dv, C = RET_HEADS, RET_DK, RET_DV, RET_CHUNK
    N = S // C
    half = dk // 2
    inv = RET_ROPE_BASE ** (-jnp.arange(half, dtype=F32) / half)
    ang = jnp.arange(S).astype(F32)[:, None] * inv[None, :]
    cos, sin = jnp.cos(ang), jnp.sin(ang)
    log_gamma = jnp.log(1.0 - 2.0 ** (-5.0 - jnp.arange(H, dtype=F32)))
    idx = jnp.arange(C, dtype=F32)
    rel = idx[:, None] - idx[None, :]
    decay_in = jnp.where(rel >= 0, jnp.exp(log_gamma[:, None, None] * jnp.maximum(rel, 0.0)), 0.0)
    q_decay = jnp.exp(log_gamma[:, None] * (idx + 1.0))[..., None]
    k_decay = jnp.exp(log_gamma[:, None] * (C - 1.0 - idx))[..., None]
    chunk_decay = jnp.exp(log_gamma * C)[:, None, None]
    hb = RET_HB
    HP = H // hb
    qb0 = q_blk // hb
    return pl.pallas_call(
        _retention_kernel,
        grid=(B, HP, N),
        in_specs=[
            pl.BlockSpec((1, C, hb * dk), lambda b, p, n: (b, n, qb0 + p)),
            pl.BlockSpec((1, C, hb * dk), lambda b, p, n: (b, n, qb0 + HP + p)),
            pl.BlockSpec((1, C, hb * dv), lambda b, p, n: (b, n, qb0 + 2 * HP + p)),
            pl.BlockSpec((1, C, hb * dv), lambda b, p, n: (b, n, qb0 + 3 * HP + p)),
            pl.BlockSpec((C, half), lambda b, p, n: (n, 0)),
            pl.BlockSpec((C, half), lambda b, p, n: (n, 0)),
            pl.BlockSpec((hb, C, C), lambda b, p, n: (p, 0, 0)),
            pl.BlockSpec((hb, C, 1), lambda b, p, n: (p, 0, 0)),
            pl.BlockSpec((hb, C, 1), lambda b, p, n: (p, 0, 0)),
            pl.BlockSpec((hb, 1, 1), lambda b, p, n: (p, 0, 0)),
            pl.BlockSpec((1, hb * dv), lambda b, p, n: (0, p)),
        ],
        out_specs=pl.BlockSpec((1, C, hb * dv), lambda b, p, n: (b, n, p)),
        out_shape=jax.ShapeDtypeStruct((B, S, H * dv), BF16),
        scratch_shapes=[pltpu.VMEM((hb, dk, dv), F32)],
        compiler_params=_cparams("parallel", "parallel", "arbitrary"),
        name="retention",
    )(proj, proj, proj, proj, cos, sin, decay_in, q_decay, k_decay, chunk_decay, gn_gain.reshape(1, H * dv))


GDN_HB = 32
GDN_HALO = 8


def _conv_silu(x_ref, halo_ref, keep, w, stage_ref):
    c = x_ref.shape[1]
    stage_ref[0:GDN_HALO, :] = halo_ref[0] * keep
    stage_ref[GDN_HALO:GDN_HALO + c, :] = x_ref[0]
    y = w[GDN_CONV - 1:GDN_CONV] * x_ref[0]
    for j in range(GDN_CONV - 1):
        off = GDN_HALO - (GDN_CONV - 1) + j
        y = y + w[j:j + 1] * stage_ref[off:off + c, :]
    return y * jax.nn.sigmoid(y)


def _l2_rows(x):
    return x * lax.rsqrt(jnp.sum(x * x, axis=-1, keepdims=True) + NORM_EPS)


def _gdn_kernel(q_ref, k_ref, v_ref, qh_ref, kh_ref, vh_ref, wq_ref, wk_ref, wv_ref, z_ref,
                ba_ref, alog_ref, dtb_ref, ng_ref, o_ref, s_scr, qst_scr, kst_scr, vst_scr):
    C, dh, hb = GDN_CHUNK, GDN_HEAD_DIM, GDN_HB
    rep = GDN_V_HEADS // GDN_QK_HEADS
    first = pl.program_id(2) == 0

    @pl.when(first)
    def _():
        s_scr[...] = jnp.zeros_like(s_scr)

    keep = jnp.where(first, 0.0, 1.0)
    qc = _conv_silu(q_ref, qh_ref, keep, wq_ref[...], qst_scr)
    kc = _conv_silu(k_ref, kh_ref, keep, wk_ref[...], kst_scr)
    vc = _conv_silu(v_ref, vh_ref, keep, wv_ref[...], vst_scr)

    ri = lax.broadcasted_iota(jnp.int32, (C, C), 0)
    ci = lax.broadcasted_iota(jnp.int32, (C, C), 1)
    causal = ri >= ci
    strict = ri > ci
    ba = ba_ref[0]
    bcol = jax.nn.sigmoid(ba[:, :hb])
    g = -jnp.exp(alog_ref[0]) * jax.nn.softplus(ba[:, hb:2 * hb] + dtb_ref[0])
    gcol = jnp.dot(jnp.where(causal, 1.0, 0.0), g, preferred_element_type=F32, precision=lax.Precision.HIGHEST)
    grow = lax.dot_general(g, jnp.where(ri <= ci, 1.0, 0.0), (((0,), (0,)), ((), ())),
                           preferred_element_type=F32, precision=lax.Precision.HIGHEST)
    heads = range(hb)
    qs, ks, grams = [], [], []
    for hq in range(hb // rep):
        qh = _l2_rows(qc[:, hq * dh:(hq + 1) * dh]) * (dh ** -0.5)
        kh = _l2_rows(kc[:, hq * dh:(hq + 1) * dh])
        k16 = kh.astype(BF16)
        qs.append(qh)
        ks.append(kh)
        grams.append(_dot_nt(jnp.concatenate([qh.astype(BF16), k16], axis=0), k16))
    beta = [bcol[:, h:h + 1] for h in heads]
    gc = [gcol[:, h:h + 1] for h in heads]
    gr = [grow[h:h + 1, :] for h in heads]
    g_last = [gr[h][:, C - 1:C] for h in heads]
    eg = [jnp.exp(gc[h]) for h in heads]
    decay = [jnp.where(causal, jnp.exp(jnp.minimum(gc[h] - gr[h], 0.0)), 0.0) for h in heads]
    attn = [(grams[h // rep][:C] * decay[h]).astype(BF16) for h in heads]
    nm = [jnp.where(strict, grams[h // rep][C:] * decay[h] * (-beta[h]), 0.0) for h in heads]
    m = [nm[h].astype(BF16) for h in heads]
    for _ in range(int(math.log2(C)) - 1):
        mf = [jnp.dot(m[h], m[h], preferred_element_type=F32) for h in heads]
        m = [mf[h].astype(BF16) for h in heads]
        nm = [nm[h] + mf[h] + jnp.dot(m[h], nm[h].astype(BF16), preferred_element_type=F32) for h in heads]
    x = [jnp.concatenate([vc[:, h * dh:(h + 1) * dh] * beta[h], ks[h // rep] * (beta[h] * eg[h])], axis=1)
         for h in heads]
    x = [x[h] + jnp.dot(nm[h].astype(BF16), x[h].astype(BF16), preferred_element_type=F32) for h in heads]
    s = [s_scr[h] for h in heads]
    ws = [jnp.dot(jnp.concatenate([x[h][:, dh:].astype(BF16), (qs[h // rep] * eg[h]).astype(BF16)], axis=0),
                  s[h].astype(BF16), preferred_element_type=F32) for h in heads]
    vn = [(x[h][:, :dh] - ws[h][:C]).astype(BF16) for h in heads]
    o = [ws[h][C:] + jnp.dot(attn[h], vn[h], preferred_element_type=F32) for h in heads]
    for h in heads:
        k_dec = (ks[h // rep] * jnp.exp(g_last[h] - gc[h])).astype(BF16)
        s_scr[h] = s[h] * jnp.exp(g_last[h]) + _dot_tn(k_dec, vn[h])
    outs = []
    for h in heads:
        zh = z_ref[0, :, h * dh:(h + 1) * dh]
        outs.append((_rms_rows(o[h], ng_ref[...]) * (zh * jax.nn.sigmoid(zh))).astype(o_ref.dtype))
    o_ref[0] = jnp.concatenate(outs, axis=1)


_GDN_GATE_COL0 = GDN_CONV_CH + GDN_V_WIDTH
_GDN_TN = 1792
_GDN_COLS = _round_up(_GDN_GATE_COL0 + GDN_V_HEADS // GDN_HB * V7X_LANES, _GDN_TN)


def _gdn_w_in_layout(w_in):
    b0 = _GDN_GATE_COL0
    a0 = b0 + GDN_V_HEADS
    parts = [w_in[:, :b0]]
    for p in range(GDN_V_HEADS // GDN_HB):
        parts.append(w_in[:, b0 + p * GDN_HB:b0 + (p + 1) * GDN_HB])
        parts.append(jnp.pad(w_in[:, a0 + p * GDN_HB:a0 + (p + 1) * GDN_HB], ((0, 0), (0, V7X_LANES - 2 * GDN_HB))))
    w = jnp.concatenate(parts, axis=1)
    return jnp.pad(w, ((0, 0), (0, _GDN_COLS - w.shape[1]))).astype(BF16)


def gdn_delta_rule(proj, conv_w, a_log, dt_bias, norm_gain):
    B, S, _ = proj.shape
    C, dh, hb, H = GDN_CHUNK, GDN_HEAD_DIM, GDN_HB, GDN_V_HEADS
    rep = GDN_V_HEADS // GDN_QK_HEADS
    N, HP = S // C, H // hb
    wqk, wv = hb // rep * dh, hb * dh
    kb0, vb0, zb0 = GDN_QK_WIDTH // wqk, 2 * GDN_QK_WIDTH // wv, GDN_CONV_CH // wv
    gb0 = _GDN_GATE_COL0 // V7X_LANES
    hr = C // GDN_HALO
    halo = lambda n: jnp.maximum(n * hr - 1, 0)
    return pl.pallas_call(
        _gdn_kernel,
        grid=(B, HP, N),
        in_specs=[
            pl.BlockSpec((1, C, wqk), lambda b, p, n: (b, n, p)),
            pl.BlockSpec((1, C, wqk), lambda b, p, n: (b, n, kb0 + p)),
            pl.BlockSpec((1, C, wv), lambda b, p, n: (b, n, vb0 + p)),
            pl.BlockSpec((1, GDN_HALO, wqk), lambda b, p, n: (b, halo(n), p)),
            pl.BlockSpec((1, GDN_HALO, wqk), lambda b, p, n: (b, halo(n), kb0 + p)),
            pl.BlockSpec((1, GDN_HALO, wv), lambda b, p, n: (b, halo(n), vb0 + p)),
            pl.BlockSpec((GDN_CONV, wqk), lambda b, p, n: (0, p)),
            pl.BlockSpec((GDN_CONV, wqk), lambda b, p, n: (0, kb0 + p)),
            pl.BlockSpec((GDN_CONV, wv), lambda b, p, n: (0, vb0 + p)),
            pl.BlockSpec((1, C, wv), lambda b, p, n: (b, n, zb0 + p)),
            pl.BlockSpec((1, C, V7X_LANES), lambda b, p, n: (b, n, gb0 + p)),
            pl.BlockSpec((1, 1, hb), lambda b, p, n: (p, 0, 0)),
            pl.BlockSpec((1, 1, hb), lambda b, p, n: (p, 0, 0)),
            pl.BlockSpec((1, dh), lambda b, p, n: (0, 0)),
        ],
        out_specs=pl.BlockSpec((1, C, wv), lambda b, p, n: (b, n, p)),
        out_shape=jax.ShapeDtypeStruct((B, S, H * dh), BF16),
        scratch_shapes=[pltpu.VMEM((hb, dh, dh), F32), pltpu.VMEM((GDN_HALO + C, wqk), F32),
                        pltpu.VMEM((GDN_HALO + C, wqk), F32), pltpu.VMEM((GDN_HALO + C, wv), F32)],
        compiler_params=_cparams("parallel", "parallel", "arbitrary"),
        name="gdn_delta_rule",
    )(proj, proj, proj, proj, proj, proj, conv_w, conv_w, conv_w, proj, proj,
      a_log.astype(F32).reshape(HP, 1, hb), dt_bias.astype(F32).reshape(HP, 1, hb), norm_gain.reshape(1, dh))


_HYB_NSA_COLS = NSA_WIDTH + 6 * NSA_KV_WIDTH
_HYB_RET_COLS = 2 * RET_HEADS * RET_DK + 2 * RET_HEADS * RET_DV
_HYB_Q_COL0 = _HYB_RET_COLS
_HYB_KV_COL0 = _HYB_Q_COL0 + NSA_WIDTH
_HYB_GATE_COL0 = _HYB_NSA_COLS + _HYB_RET_COLS
_HYB_TN = 768
_HYB_COLS = _round_up(_HYB_GATE_COL0 + NSA_KV_GROUPS * V7X_LANES, _HYB_TN)


def _hybrid_w_in_layout(w_in):
    gate0 = _HYB_NSA_COLS
    ret0 = gate0 + 3 * NSA_HEADS
    per_group = 3 * NSA_GROUP_HEADS
    parts = [w_in[:, ret0:ret0 + _HYB_RET_COLS], w_in[:, :gate0]]
    for g in range(NSA_KV_GROUPS):
        parts.append(jnp.pad(w_in[:, gate0 + g * per_group:gate0 + (g + 1) * per_group],
                             ((0, 0), (0, V7X_LANES - per_group))))
    w = jnp.concatenate(parts, axis=1)
    return jnp.pad(w, ((0, 0), (0, _HYB_COLS - w.shape[1]))).astype(BF16)


def hybrid_mixer(h, ln_gain, w_in, w_out, q_gain, k_gain, cmp_pos, cmp_w1, cmp_w2, gn_gain, rel_bias):
    B, S, D = h.shape
    proj = norm_matmul(h.reshape(B * S, D), ln_gain, _hybrid_w_in_layout(w_in), tn=_HYB_TN).reshape(B, S, _HYB_COLS)
    cmp = nsa_compress(proj, _HYB_KV_COL0 // NSA_HEAD_DIM, cmp_pos, cmp_w1, cmp_w2, k_gain[0])
    ksn, vsb, kwn, vwb = nsa_kv_prep(proj, (_HYB_KV_COL0 + 2 * NSA_KV_WIDTH) // NSA_KV_WIDTH, k_gain)
    a_out = nsa_attention_pallas(proj, _HYB_Q_COL0 // (NSA_GROUP_HEADS * NSA_HEAD_DIM), _HYB_GATE_COL0 // V7X_LANES,
                                 cmp[0], cmp[1], ksn, vsb, kwn, vwb, q_gain, rel_bias)
    b_out = retention_pallas(proj, 0, gn_gain)
    w_out = w_out.astype(BF16)
    return matmul2_residual(a_out.reshape(B * S, -1), b_out.reshape(B * S, -1), w_out,
                            h.reshape(B * S, D)).reshape(B, S, D)


def gdn_mixer(h, ln_gain, w_in, conv_w, a_log, dt_bias, norm_gain, w_out):
    B, S, D = h.shape
    proj = norm_matmul(h.reshape(B * S, D), ln_gain, _gdn_w_in_layout(w_in), tn=_GDN_TN).reshape(B, S, _GDN_COLS)
    o = gdn_delta_rule(proj, conv_w, a_log, dt_bias, norm_gain)
    return matmul_residual(o.reshape(B * S, GDN_V_WIDTH), w_out.astype(BF16), h.reshape(B * S, D)).reshape(B, S, D)


def memory_kv(mem, mem_gain, wkv, k_gain):
    B, M, D = mem.shape
    kv = norm_matmul(mem.reshape(B * M, D), mem_gain, wkv.astype(BF16))
    k, v = jnp.split(kv.reshape(B, M, 2 * XA_WIDTH), 2, axis=-1)
    k = rms_norm(k.reshape(B, M, XA_HEADS, XA_HEAD_DIM), k_gain).reshape(B, M, XA_WIDTH)
    return k.astype(BF16), v.astype(BF16)


def kernel(x, mem, rel_bias, ln_mix, ln_mem, ln_ffn, hyb_w_in, hyb_w_out, nsa_q_gain, nsa_k_gain,
           nsa_cmp_pos, nsa_cmp_w1, nsa_cmp_w2, ret_gn_gain, gdn_w_in, gdn_conv_w, gdn_a_log,
           gdn_dt_bias, gdn_norm_gain, gdn_w_out, xa_wq, xa_wkv, xa_q_gain, xa_k_gain, xa_mem_gain,
           xa_wo, ffn_w_in, ffn_w_out):
    B, S, D = x.shape
    h = x
    ffn_w_in, ffn_w_out, gdn_w_out = cast_bf16(ffn_w_in), cast_bf16(ffn_w_out), cast_bf16(gdn_w_out)
    for layer in range(DEPTH):
        if layer % 2 == 0:
            e = layer // 2
            h = hybrid_mixer(h, ln_mix[layer], hyb_w_in[e], hyb_w_out[e], nsa_q_gain[e], nsa_k_gain[e],
                             nsa_cmp_pos[e], nsa_cmp_w1[e], nsa_cmp_w2[e], ret_gn_gain[e], rel_bias)
        else:
            o = layer // 2
            h = gdn_mixer(h, ln_mix[layer], gdn_w_in[o], gdn_conv_w[o], gdn_a_log[o], gdn_dt_bias[o],
                          gdn_norm_gain[o], gdn_w_out[o])
        k_mem, v_mem = memory_kv(mem, xa_mem_gain[layer], xa_wkv[layer], xa_k_gain[layer])
        h = xattn_residual(h, ln_mem[layer], xa_wq[layer].astype(BF16), xa_q_gain[layer], k_mem, v_mem,
                           xa_wo[layer].astype(BF16))
        h = ffn_residual(h.reshape(B * S, D), ln_ffn[layer], ffn_w_in[layer].astype(BF16),
                         ffn_w_out[layer].astype(BF16)).reshape(B, S, D)
    return h
```

```python
import functools
import math

import jax
import jax.numpy as jnp
import numpy as np
from jax import lax
from jax.experimental import pallas as pl
from jax.experimental.pallas import tpu as pltpu

F32 = jnp.float32
BF16 = jnp.bfloat16

D_MODEL = 2048
DEPTH = 4
NORM_EPS = 1e-6
NEG_INF = -1e30

NSA_HEADS = 8
NSA_KV_GROUPS = 2
NSA_GROUP_HEADS = NSA_HEADS // NSA_KV_GROUPS
NSA_HEAD_DIM = 128
NSA_CMP_LEN = 32
NSA_CMP_STRIDE = 16
NSA_SEL_BLOCK = 64
NSA_N_SELECT = 16
NSA_WINDOW = 512
NSA_Q_BLOCK = 64
NSA_FORCE_SCORE = 1e6
NSA_WIDTH = NSA_HEADS * NSA_HEAD_DIM
NSA_KV_WIDTH = NSA_KV_GROUPS * NSA_HEAD_DIM

RET_HEADS = 4
RET_DK = 256
RET_DV = 256
RET_CHUNK = 128
RET_ROPE_BASE = 10000.0
RET_GN_EPS = 1e-5

GDN_QK_HEADS = 16
GDN_V_HEADS = 32
GDN_HEAD_DIM = 128
GDN_CONV = 4
GDN_CHUNK = 64
GDN_QK_WIDTH = GDN_QK_HEADS * GDN_HEAD_DIM
GDN_V_WIDTH = GDN_V_HEADS * GDN_HEAD_DIM
GDN_CONV_CH = 2 * GDN_QK_WIDTH + GDN_V_WIDTH

REL_BUCKETS = 32
REL_MAX_DIST = 1024

XA_HEADS = 4
XA_HEAD_DIM = 128
XA_WIDTH = XA_HEADS * XA_HEAD_DIM

V7X_VMEM_LIMIT_BYTES = 56 * 1024 * 1024
V7X_LANES = 128


def _cparams(*sem):
    return pltpu.CompilerParams(dimension_semantics=sem, vmem_limit_bytes=V7X_VMEM_LIMIT_BYTES)


def _round_up(n, m):
    return -(-n // m) * m


def _rms_rows(x, gain):
    return x * lax.rsqrt(jnp.mean(x * x, axis=-1, keepdims=True) + NORM_EPS) * gain


def _cast_kernel(x_ref, o_ref):
    o_ref[...] = x_ref[...].astype(o_ref.dtype)


def cast_bf16(w, *, block_bytes=8 * 1024 * 1024):
    n = w.shape[-1]
    x = w.reshape(-1, n)
    rows = x.shape[0]
    tr = rows
    while tr * n * 4 > block_bytes and tr % 32 == 0:
        tr //= 2
    out = pl.pallas_call(
        _cast_kernel,
        grid=(rows // tr,),
        in_specs=[pl.BlockSpec((tr, n), lambda i: (i, 0))],
        out_specs=pl.BlockSpec((tr, n), lambda i: (i, 0)),
        out_shape=jax.ShapeDtypeStruct((rows, n), BF16),
        compiler_params=_cparams("parallel"),
        name="cast_bf16",
    )(x)
    return out.reshape(w.shape)


def _norm_matmul_kernel(x_ref, g_ref, w_ref, o_ref, xn_ref):
    @pl.when(pl.program_id(1) == 0)
    def _():
        xn_ref[...] = _rms_rows(x_ref[...], g_ref[...]).astype(BF16)

    o_ref[...] = jnp.dot(xn_ref[...], w_ref[...], preferred_element_type=F32).astype(o_ref.dtype)


def norm_matmul(x, gain, w, *, tm=1024, tn=512, out_dtype=F32):
    m, k = x.shape
    n = w.shape[1]
    tm = min(tm, m)
    return pl.pallas_call(
        _norm_matmul_kernel,
        grid=(m // tm, n // tn),
        in_specs=[pl.BlockSpec((tm, k), lambda i, j: (i, 0)),
                  pl.BlockSpec((1, k), lambda i, j: (0, 0)),
                  pl.BlockSpec((k, tn), lambda i, j: (0, j))],
        out_specs=pl.BlockSpec((tm, tn), lambda i, j: (i, j)),
        out_shape=jax.ShapeDtypeStruct((m, n), out_dtype),
        scratch_shapes=[pltpu.VMEM((tm, k), BF16)],
        compiler_params=_cparams("parallel", "arbitrary"),
        name="norm_matmul",
    )(x, gain.reshape(1, k), w)


def _matmul_res_kernel(x_ref, w_ref, r_ref, o_ref):
    o_ref[...] = r_ref[...] + jnp.dot(x_ref[...], w_ref[...], preferred_element_type=F32)


def matmul_residual(x, w, res, *, tm=1024, tn=1024):
    m, k = x.shape
    n = w.shape[1]
    return pl.pallas_call(
        _matmul_res_kernel,
        grid=(m // tm, n // tn),
        in_specs=[pl.BlockSpec((tm, k), lambda i, j: (i, 0)),
                  pl.BlockSpec((k, tn), lambda i, j: (0, j)),
                  pl.BlockSpec((tm, tn), lambda i, j: (i, j))],
        out_specs=pl.BlockSpec((tm, tn), lambda i, j: (i, j)),
        out_shape=jax.ShapeDtypeStruct((m, n), F32),
        compiler_params=_cparams("parallel", "arbitrary"),
        name="matmul_residual",
    )(x, w, res)


def _matmul2_res_kernel(x1_ref, x2_ref, w1_ref, w2_ref, r_ref, o_ref):
    o_ref[...] = (r_ref[...] + jnp.dot(x1_ref[...], w1_ref[...], preferred_element_type=F32)
                  + jnp.dot(x2_ref[...], w2_ref[...], preferred_element_type=F32))


def matmul2_residual(x1, x2, w, res, *, tm=1024, tn=1024):
    m, k1 = x1.shape
    n = w.shape[1]
    return pl.pallas_call(
        _matmul2_res_kernel,
        grid=(m // tm, n // tn),
        in_specs=[pl.BlockSpec((tm, k1), lambda i, j: (i, 0)),
                  pl.BlockSpec((tm, k1), lambda i, j: (i, 0)),
                  pl.BlockSpec((k1, tn), lambda i, j: (0, j)),
                  pl.BlockSpec((k1, tn), lambda i, j: (1, j)),
                  pl.BlockSpec((tm, tn), lambda i, j: (i, j))],
        out_specs=pl.BlockSpec((tm, tn), lambda i, j: (i, j)),
        out_shape=jax.ShapeDtypeStruct((m, n), F32),
        compiler_params=_cparams("parallel", "arbitrary"),
        name="matmul2_residual",
    )(x1, x2, w, w, res)


def _ffn_kernel(x_ref, g_ref, wg_ref, wu_ref, wo_ref, o_ref, xn_ref, acc_ref):
    j = pl.program_id(1)

    @pl.when(j == 0)
    def _():
        x = x_ref[...]
        xn_ref[...] = _rms_rows(x, g_ref[...]).astype(BF16)
        acc_ref[...] = x

    xn = xn_ref[...]
    gate = jnp.dot(xn, wg_ref[...], preferred_element_type=F32)
    up = jnp.dot(xn, wu_ref[...], preferred_element_type=F32)
    act = (gate * jax.nn.sigmoid(gate) * up).astype(BF16)
    acc_ref[...] += jnp.dot(act, wo_ref[...], preferred_element_type=F32)

    @pl.when(j == pl.num_programs(1) - 1)
    def _():
        o_ref[...] = acc_ref[...]


def ffn_residual(h, gain, w_in, w_out, *, tm=512, th=512):
    m, d = h.shape
    hidden = w_out.shape[0]
    nh = hidden // th
    return pl.pallas_call(
        _ffn_kernel,
        grid=(m // tm, nh),
        in_specs=[pl.BlockSpec((tm, d), lambda i, j: (i, 0)),
                  pl.BlockSpec((1, d), lambda i, j: (0, 0)),
                  pl.BlockSpec((d, th), lambda i, j: (0, j)),
                  pl.BlockSpec((d, th), lambda i, j: (0, j + nh)),
                  pl.BlockSpec((th, d), lambda i, j: (j, 0))],
        out_specs=pl.BlockSpec((tm, d), lambda i, j: (i, 0)),
        out_shape=jax.ShapeDtypeStruct((m, d), F32),
        scratch_shapes=[pltpu.VMEM((tm, d), BF16), pltpu.VMEM((tm, d), F32)],
        compiler_params=_cparams("parallel", "arbitrary"),
        name="ffn_residual",
    )(h, gain.reshape(1, d), w_in, w_in, w_out)


def _xattn_kernel(h_ref, g_ref, wq_ref, qg_ref, k_ref, v_ref, wo_ref, o_ref):
    x = h_ref[0]
    xn = _rms_rows(x, g_ref[...]).astype(BF16)
    q = jnp.dot(xn, wq_ref[...], preferred_element_type=F32)
    k = k_ref[0]
    v = v_ref[0]
    outs = []
    for hh in range(XA_HEADS):
        sl = slice(hh * XA_HEAD_DIM, (hh + 1) * XA_HEAD_DIM)
        qh = _rms_rows(q[:, sl], qg_ref[...]).astype(BF16)
        s = lax.dot_general(qh, k[:, sl], (((1,), (1,)), ((), ())), preferred_element_type=F32)
        s = s * (XA_HEAD_DIM ** -0.5)
        s = s - jnp.max(s, axis=-1, keepdims=True)
        p = jnp.exp(s)
        p = p / jnp.sum(p, axis=-1, keepdims=True)
        outs.append(jnp.dot(p.astype(BF16), v[:, sl], preferred_element_type=F32))
    o = jnp.concatenate(outs, axis=-1).astype(BF16)
    o_ref[0] = x + jnp.dot(o, wo_ref[...], preferred_element_type=F32)


def xattn_residual(h, gain, wq, q_gain, k, v, wo, *, tm=512):
    b, s, d = h.shape
    mt = k.shape[1]
    return pl.pallas_call(
        _xattn_kernel,
        grid=(b, s // tm),
        in_specs=[pl.BlockSpec((1, tm, d), lambda bi, i: (bi, i, 0)),
                  pl.BlockSpec((1, d), lambda bi, i: (0, 0)),
                  pl.BlockSpec((d, XA_WIDTH), lambda bi, i: (0, 0)),
                  pl.BlockSpec((1, XA_HEAD_DIM), lambda bi, i: (0, 0)),
                  pl.BlockSpec((1, mt, XA_WIDTH), lambda bi, i: (bi, 0, 0)),
                  pl.BlockSpec((1, mt, XA_WIDTH), lambda bi, i: (bi, 0, 0)),
                  pl.BlockSpec((XA_WIDTH, d), lambda bi, i: (0, 0))],
        out_specs=pl.BlockSpec((1, tm, d), lambda bi, i: (bi, i, 0)),
        out_shape=jax.ShapeDtypeStruct((b, s, d), F32),
        compiler_params=_cparams("parallel", "parallel"),
        name="xattn_residual",
    )(h, gain.reshape(1, d), wq, q_gain.reshape(1, XA_HEAD_DIM), k, v, wo)


def rms_norm(x, gain, eps=NORM_EPS):
    x32 = x.astype(F32)
    y = x32 * lax.rsqrt(jnp.mean(x32 * x32, axis=-1, keepdims=True) + eps)
    return (y * gain.astype(F32)).astype(x.dtype)


def rel_bucket(dist):
    dist = jnp.maximum(dist, 0)
    max_exact = REL_BUCKETS // 2
    scaled = (jnp.log(jnp.maximum(dist, max_exact).astype(F32) / max_exact)
              / math.log(REL_MAX_DIST / max_exact) * (REL_BUCKETS - max_exact))
    large = jnp.minimum(max_exact + scaled.astype(jnp.int32), REL_BUCKETS - 1)
    return jnp.where(dist < max_exact, dist, large)


NSA_QT = 128
NSA_KT = 512
NSA_WT = NSA_WINDOW + NSA_QT


def _dot_nt(a, b):
    return lax.dot_general(a, b, (((1,), (1,)), ((), ())), preferred_element_type=F32)


def _dot_tn(a, b):
    return lax.dot_general(a, b, (((0,), (0,)), ((), ())), preferred_element_type=F32)


def _bias_tile(tb_ref, h, first_slab, n_slabs):
    return jnp.concatenate([tb_ref[h, jnp.maximum(first_slab - j, 0)] for j in range(n_slabs)], axis=1)


def _nsa_kernel(q_ref, gl_ref, kc_ref, vc_ref, bc_ref, covt_ref, ks_ref, vs_ref, kw_ref, vw_ref,
                tb_ref, e_ref, qg_ref, o_ref, a_scr, *, n_top):
    QT, KT, WT, HG, dh = NSA_QT, NSA_KT, NSA_WT, NSA_GROUP_HEADS, NSA_HEAD_DIM
    nsb, nc = covt_ref.shape
    q0 = pl.program_id(2) * QT

    x = q_ref[0]
    qs = []
    for h in range(HG):
        xh = _rms_rows(x[:, h * dh:(h + 1) * dh], qg_ref[...]) * (dh ** -0.5)
        qs.append(xh.astype(BF16))
    qn = jnp.concatenate(qs, axis=0)

    sc = _dot_nt(qn, kc_ref[0, 0]) + bc_ref[...].reshape(HG * QT, nc)
    row = lax.broadcasted_iota(jnp.int32, (HG * QT, nc), 0)
    col = lax.broadcasted_iota(jnp.int32, (HG * QT, nc), 1)
    tq = q0 + (row & (QT - 1))
    mask_c = tq >= col * NSA_CMP_STRIDE + (NSA_CMP_LEN - 1)
    sc = jnp.where(mask_c, sc, NEG_INF)
    pc = jnp.where(mask_c, jnp.exp(sc - jnp.max(sc, axis=-1, keepdims=True)), 0.0)
    den = jnp.sum(pc, axis=-1, keepdims=True)
    pcb = (pc / jnp.maximum(den, 1e-30)).astype(BF16)
    o_c = jnp.dot(pcb, vc_ref[0, 0], preferred_element_type=F32)

    imp_all = _dot_nt(covt_ref[...], pcb)
    imp = imp_all[:, 0:QT]
    for h in range(1, HG):
        imp = imp + imp_all[:, h * QT:(h + 1) * QT]
    jj = lax.broadcasted_iota(jnp.int32, (nsb, QT), 0)
    cur = (q0 + lax.broadcasted_iota(jnp.int32, (nsb, QT), 1)) // NSA_SEL_BLOCK
    forced = (jj == 0) | (jj == cur) | (jj == cur - 1)
    a = lax.bitcast_convert_type(jnp.where(forced, NSA_FORCE_SCORE, imp), jnp.int32)
    a = jnp.where(jj > cur, -1, a)
    a1 = a + 1
    a_scr[...] = a

    def rank_body(i, rank):
        r = a_scr[pl.ds(i, 1), :]
        return rank + jnp.where(r >= jnp.where(jj > i, a, a1), 1.0, 0.0)

    n_live = jnp.minimum((q0 + QT - 1) // NSA_SEL_BLOCK + 1, nsb)
    rank = lax.fori_loop(0, n_live, rank_body, jnp.zeros((nsb, QT), F32))
    sel_t = jnp.where(rank < n_top, 1.0, 0.0)
    if nsb < V7X_LANES:
        sel_t = jnp.concatenate([sel_t, jnp.zeros((V7X_LANES - nsb, QT), F32)], axis=0)
    sel = sel_t.T.astype(BF16)

    cmr = (lax.broadcasted_iota(jnp.int32, (QT, KT), 1) - lax.broadcasted_iota(jnp.int32, (QT, KT), 0))

    def sel_body(kt, carry):
        m_i, l_i, acc = carry
        k0 = pl.multiple_of(kt * KT, KT)
        k = ks_ref[0, pl.ds(k0, KT), :]
        v = vs_ref[0, pl.ds(k0, KT), :]
        s = _dot_nt(qn, k)
        delta = q0 - k0
        visible = (jnp.dot(sel, e_ref[kt], preferred_element_type=F32) > 0.5) & (cmr <= delta)
        seg = delta // QT
        parts = []
        for h in range(HG):
            bias = _bias_tile(tb_ref, h, seg, KT // QT)
            parts.append(jnp.where(visible, s[h * QT:(h + 1) * QT] + bias, NEG_INF))
        s = jnp.concatenate(parts, axis=0)
        m_new = jnp.maximum(m_i, jnp.max(s, axis=-1, keepdims=True))
        alpha = jnp.exp(m_i - m_new)
        p = jnp.exp(s - m_new)
        l_new = alpha * l_i + jnp.sum(p, axis=-1, keepdims=True)
        acc = alpha * acc + jnp.dot(p.astype(BF16), v, preferred_element_type=F32)
        return m_new, l_new, acc

    init = (jnp.full((HG * QT, 1), NEG_INF, F32), jnp.zeros((HG * QT, 1), F32), jnp.zeros((HG * QT, dh), F32))
    _, l_s, acc_s = lax.fori_loop(0, q0 // KT + 1, sel_body, init)
    o_s = acc_s / l_s

    w0 = pl.multiple_of(q0, QT)
    kwin = kw_ref[0, pl.ds(w0, WT), :]
    vwin = vw_ref[0, pl.ds(w0, WT), :]
    sw = _dot_nt(qn, kwin)
    cw = lax.broadcasted_iota(jnp.int32, (QT, WT), 1)
    dist = NSA_WINDOW + lax.broadcasted_iota(jnp.int32, (QT, WT), 0) - cw
    vis_w = (dist >= 0) & (dist < NSA_WINDOW) & (cw >= NSA_WINDOW - q0)
    parts = []
    for h in range(HG):
        bias = _bias_tile(tb_ref, h, NSA_WINDOW // QT, WT // QT)
        parts.append(jnp.where(vis_w, sw[h * QT:(h + 1) * QT] + bias, NEG_INF))
    sw = jnp.concatenate(parts, axis=0)
    pw = jnp.exp(sw - jnp.max(sw, axis=-1, keepdims=True))
    o_w = jnp.dot(pw.astype(BF16), vwin, preferred_element_type=F32) / jnp.sum(pw, axis=-1, keepdims=True)

    gates = jax.nn.sigmoid(gl_ref[0][:, :3 * HG])
    outs = []
    for h in range(HG):
        rs = slice(h * QT, (h + 1) * QT)
        outs.append(gates[:, 3 * h:3 * h + 1] * o_c[rs] + gates[:, 3 * h + 1:3 * h + 2] * o_s[rs]
                    + gates[:, 3 * h + 2:3 * h + 3] * o_w[rs])
    o_ref[0] = jnp.concatenate(outs, axis=-1).astype(o_ref.dtype)


def nsa_attention_pallas(proj, q_blk, gl_blk, kc, vc, ks, vs, kw, vw, q_gain, rel_bias):
    B, S, _ = proj.shape
    G, HG, dh = NSA_KV_GROUPS, NSA_GROUP_HEADS, NSA_HEAD_DIM
    QT, KT, WT, W, SB = NSA_QT, NSA_KT, NSA_WT, NSA_WINDOW, NSA_SEL_BLOCK
    nsb, nc, nkt, nseg = S // SB, S // NSA_CMP_STRIDE, S // KT, max(S // QT, W // QT + 1)
    n_top = min(NSA_N_SELECT, nsb)

    def bias_of(dist):
        onehot = (rel_bucket(dist)[..., None] == jnp.arange(REL_BUCKETS)) & (dist >= 0)[..., None]
        return jnp.einsum('...b,bh->h...', onehot.astype(F32), rel_bias.astype(F32),
                          precision=lax.Precision.HIGHEST)

    bc = bias_of(jnp.arange(S)[:, None] - (jnp.arange(nc) * NSA_CMP_STRIDE + NSA_CMP_LEN - 1)[None, :])
    tb = bias_of(QT * jnp.arange(nseg)[:, None, None] + jnp.arange(QT)[None, :, None] - jnp.arange(QT)[None, None, :])
    cmp_start = np.arange(nc) * NSA_CMP_STRIDE
    cmp_end = cmp_start + NSA_CMP_LEN - 1
    sel_start = np.arange(nsb) * SB
    cover_t = ((cmp_start[None, :] < sel_start[:, None] + SB) & (cmp_end[None, :] >= sel_start[:, None])
               & (np.arange(nc)[None, :] < nc - 1))
    cover_t = jnp.asarray(cover_t, BF16)
    e = (np.arange(V7X_LANES)[None, :, None]
         == (np.arange(nkt)[:, None, None] * (KT // SB) + np.arange(KT)[None, None, :] // SB))
    e = jnp.asarray(e, BF16)

    kern = functools.partial(_nsa_kernel, n_top=n_top)
    return pl.pallas_call(
        kern,
        grid=(B, G, S // QT),
        in_specs=[
            pl.BlockSpec((1, QT, HG * dh), lambda b, g, i: (b, i, q_blk + g)),
            pl.BlockSpec((1, QT, V7X_LANES), lambda b, g, i: (b, i, gl_blk + g)),
            pl.BlockSpec((1, 1, nc, dh), lambda b, g, i: (b, g, 0, 0)),
            pl.BlockSpec((1, 1, nc, dh), lambda b, g, i: (b, g, 0, 0)),
            pl.BlockSpec((HG, QT, nc), lambda b, g, i: (g, i, 0)),
            pl.BlockSpec((nsb, nc), lambda b, g, i: (0, 0)),
            pl.BlockSpec((1, S, dh), lambda b, g, i: (b, 0, g)),
            pl.BlockSpec((1, S, dh), lambda b, g, i: (b, 0, g)),
            pl.BlockSpec((1, S + W, dh), lambda b, g, i: (b, 0, g)),
            pl.BlockSpec((1, S + W, dh), lambda b, g, i: (b, 0, g)),
            pl.BlockSpec((HG, nseg, QT, QT), lambda b, g, i: (g, 0, 0, 0)),
            pl.BlockSpec((nkt, V7X_LANES, KT), lambda b, g, i: (0, 0, 0)),
            pl.BlockSpec((1, dh), lambda b, g, i: (0, 0)),
        ],
        out_specs=pl.BlockSpec((1, QT, HG * dh), lambda b, g, i: (b, i, g)),
        out_shape=jax.ShapeDtypeStruct((B, S, NSA_WIDTH), BF16),
        scratch_shapes=[pltpu.VMEM((nsb, QT), jnp.int32)],
        compiler_params=_cparams("parallel", "parallel", "arbitrary"),
        name="nsa_attention",
    )(proj, proj, kc, vc, bc, cover_t, ks, vs, kw, vw, tb, e, q_gain.reshape(1, dh))


def _nsa_compress_kernel(x_ref, pos_ref, w1_ref, w2_ref, kg_ref, o_ref):
    stride, dh = NSA_CMP_STRIDE, NSA_HEAD_DIM
    n_chunks = x_ref.shape[1] // stride
    hid = w1_ref.shape[2]
    h_lo = jnp.zeros((n_chunks, hid), F32)
    h_hi = jnp.zeros((n_chunks, hid), F32)
    for p in range(stride):
        xp = x_ref[0, pl.ds(p, n_chunks, stride=stride), :]
        h_lo = h_lo + jnp.dot((xp + pos_ref[0, p:p + 1, :]).astype(BF16), w1_ref[0, p * dh:(p + 1) * dh, :],
                              preferred_element_type=F32)
        h_hi = h_hi + jnp.dot((xp + pos_ref[0, stride + p:stride + p + 1, :]).astype(BF16),
                              w1_ref[0, (stride + p) * dh:(stride + p + 1) * dh, :], preferred_element_type=F32)
    hidden = h_lo + jnp.concatenate([h_hi[1:], jnp.zeros((1, hid), F32)], axis=0)
    out = jnp.dot((hidden * jax.nn.sigmoid(hidden)).astype(BF16), w2_ref[0], preferred_element_type=F32)
    out = jnp.where(pl.program_id(2) == 0, _rms_rows(out, kg_ref[...]), out)
    row = lax.broadcasted_iota(jnp.int32, out.shape, 0)
    o_ref[0, 0, 0] = jnp.where(row < n_chunks - 1, out, 0.0).astype(o_ref.dtype)


def nsa_compress(proj, col_blk, cmp_pos, cmp_w1, cmp_w2, k_gain):
    B, S, _ = proj.shape
    G, dh = NSA_KV_GROUPS, NSA_HEAD_DIM
    nc = S // NSA_CMP_STRIDE
    hid = cmp_w1.shape[-1]
    return pl.pallas_call(
        _nsa_compress_kernel,
        grid=(B, G, 2),
        in_specs=[
            pl.BlockSpec((1, S, dh), lambda b, g, w: (b, 0, col_blk + w * G + g)),
            pl.BlockSpec((1, NSA_CMP_LEN, dh), lambda b, g, w: (w, 0, 0)),
            pl.BlockSpec((1, NSA_CMP_LEN * dh, hid), lambda b, g, w: (w, 0, 0)),
            pl.BlockSpec((1, hid, dh), lambda b, g, w: (w, 0, 0)),
            pl.BlockSpec((1, dh), lambda b, g, w: (0, 0)),
        ],
        out_specs=pl.BlockSpec((1, 1, 1, nc, dh), lambda b, g, w: (w, b, g, 0, 0)),
        out_shape=jax.ShapeDtypeStruct((2, B, G, nc, dh), BF16),
        compiler_params=_cparams("parallel", "parallel", "arbitrary"),
        name="nsa_compress",
    )(proj, cmp_pos, cmp_w1.astype(BF16), cmp_w2.astype(BF16), k_gain.reshape(1, dh))


NSA_PREP_TM = NSA_WINDOW


def _nsa_kv_kernel(ks_ref, vs_ref, kw_ref, vw_ref, gs_ref, gw_ref, kso_ref, vso_ref, kwo_ref, vwo_ref):
    G, dh = NSA_KV_GROUPS, NSA_HEAD_DIM
    i = pl.program_id(1)
    n_in = pl.num_programs(1) - 1

    def normed(ref, gain_ref):
        x = ref[0]
        return jnp.concatenate([_rms_rows(x[:, g * dh:(g + 1) * dh], gain_ref[...]) for g in range(G)], axis=1)

    @pl.when(i < n_in)
    def _():
        kso_ref[0] = normed(ks_ref, gs_ref).astype(kso_ref.dtype)
        vso_ref[0] = vs_ref[0].astype(vso_ref.dtype)

    @pl.when(i == 0)
    def _():
        kwo_ref[0] = jnp.zeros(kwo_ref.shape[1:], kwo_ref.dtype)
        vwo_ref[0] = jnp.zeros(vwo_ref.shape[1:], vwo_ref.dtype)

    @pl.when(i > 0)
    def _():
        kwo_ref[0] = normed(kw_ref, gw_ref).astype(kwo_ref.dtype)
        vwo_ref[0] = vw_ref[0].astype(vwo_ref.dtype)


def nsa_kv_prep(proj, col_blk, k_gain):
    B, S, _ = proj.shape
    tm, kvw, W = NSA_PREP_TM, NSA_KV_WIDTH, NSA_WINDOW
    n_in = S // tm
    cur = lambda i: jnp.minimum(i, n_in - 1)
    prev = lambda i: jnp.maximum(i - 1, 0)
    return pl.pallas_call(
        _nsa_kv_kernel,
        grid=(B, n_in + 1),
        in_specs=[
            pl.BlockSpec((1, tm, kvw), lambda b, i: (b, cur(i), col_blk)),
            pl.BlockSpec((1, tm, kvw), lambda b, i: (b, cur(i), col_blk + 1)),
            pl.BlockSpec((1, tm, kvw), lambda b, i: (b, prev(i), col_blk + 2)),
            pl.BlockSpec((1, tm, kvw), lambda b, i: (b, prev(i), col_blk + 3)),
            pl.BlockSpec((1, NSA_HEAD_DIM), lambda b, i: (0, 0)),
            pl.BlockSpec((1, NSA_HEAD_DIM), lambda b, i: (0, 0)),
        ],
        out_specs=[
            pl.BlockSpec((1, tm, kvw), lambda b, i: (b, cur(i), 0)),
            pl.BlockSpec((1, tm, kvw), lambda b, i: (b, cur(i), 0)),
            pl.BlockSpec((1, tm, kvw), lambda b, i: (b, i, 0)),
            pl.BlockSpec((1, tm, kvw), lambda b, i: (b, i, 0)),
        ],
        out_shape=[jax.ShapeDtypeStruct((B, S, kvw), BF16), jax.ShapeDtypeStruct((B, S, kvw), BF16),
                   jax.ShapeDtypeStruct((B, S + W, kvw), BF16), jax.ShapeDtypeStruct((B, S + W, kvw), BF16)],
        compiler_params=_cparams("parallel", "arbitrary"),
        name="nsa_kv_prep",
    )(proj, proj, proj, proj, k_gain[1:2], k_gain[2:3])


RET_HB = 4


def _rotate_half(x, cos, sin):
    half = x.shape[-1] // 2
    x1, x2 = x[:, :half], x[:, half:]
    return jnp.concatenate([x1 * cos - x2 * sin, x1 * sin + x2 * cos], axis=-1)


def _retention_kernel(q_ref, k_ref, v_ref, g_ref, cos_ref, sin_ref, din_ref, qd_ref, kd_ref, cd_ref, gn_ref,
                      o_ref, s_scr):
    @pl.when(pl.program_id(2) == 0)
    def _():
        s_scr[...] = jnp.zeros_like(s_scr)

    cos, sin = cos_ref[...], sin_ref[...]
    heads = range(RET_HB)
    sl = [slice(h * RET_DK, (h + 1) * RET_DK) for h in heads]
    q = [_rotate_half(q_ref[0, :, sl[h]], cos, sin) * (RET_DK ** -0.5) for h in heads]
    k = [_rotate_half(k_ref[0, :, sl[h]], cos, sin) for h in heads]
    v16 = [v_ref[0, :, sl[h]].astype(BF16) for h in heads]
    inner = [_dot_nt(q[h].astype(BF16), k[h].astype(BF16)) * din_ref[h] for h in heads]
    s = [s_scr[h] for h in heads]
    o = [jnp.dot(inner[h].astype(BF16), v16[h], preferred_element_type=F32)
         + jnp.dot((q[h] * qd_ref[h]).astype(BF16), s[h].astype(BF16), preferred_element_type=F32) for h in heads]
    for h in heads:
        s_scr[h] = s[h] * cd_ref[h] + _dot_tn((k[h] * kd_ref[h]).astype(BF16), v16[h])
    outs = []
    for h in heads:
        mu = jnp.mean(o[h], axis=-1, keepdims=True)
        d = o[h] - mu
        var = jnp.mean(d * d, axis=-1, keepdims=True)
        gate = g_ref[0, :, sl[h]]
        outs.append((gate * jax.nn.sigmoid(gate)
                     * (d * lax.rsqrt(var + RET_GN_EPS) * gn_ref[:, sl[h]])).astype(o_ref.dtype))
    o_ref[0] = jnp.concatenate(outs, axis=-1)


def retention_pallas(proj, q_blk, gn_gain):
    B, S, _ = proj.shape
    H, dk, dv, C = RET_HEADS, RET_DK, RET_DV, RET_CHUNK
    N = S // C
    half = dk // 2
    inv = RET_ROPE_BASE ** (-jnp.arange(half, dtype=F32) / half)
    ang = jnp.arange(S).astype(F32)[:, None] * inv[None, :]
    cos, sin = jnp.cos(ang), jnp.sin(ang)
    log_gamma = jnp.log(1.0 - 2.0 ** (-5.0 - jnp.arange(H, dtype=F32)))
    idx = jnp.arange(C, dtype=F32)
    rel = idx[:, None] - idx[None, :]
    decay_in = jnp.where(rel >= 0, jnp.exp(log_gamma[:, None, None] * jnp.maximum(rel, 0.0)), 0.0)
    q_decay = jnp.exp(log_gamma[:, None] * (idx + 1.0))[..., None]
    k_decay = jnp.exp(log_gamma[:, None] * (C - 1.0 - idx))[..., None]
    chunk_decay = jnp.exp(log_gamma * C)[:, None, None]
    hb = RET_HB
    HP = H // hb
    qb0 = q_blk // hb
    return pl.pallas_call(
        _retention_kernel,
        grid=(B, HP, N),
        in_specs=[
            pl.BlockSpec((1, C, hb * dk), lambda b, p, n: (b, n, qb0 + p)),
            pl.BlockSpec((1, C, hb * dk), lambda b, p, n: (b, n, qb0 + HP + p)),
            pl.BlockSpec((1, C, hb * dv), lambda b, p, n: (b, n, qb0 + 2 * HP + p)),
            pl.BlockSpec((1, C, hb * dv), lambda b, p, n: (b, n, qb0 + 3 * HP + p)),
            pl.BlockSpec((C, half), lambda b, p, n: (n, 0)),
            pl.BlockSpec((C, half), lambda b, p, n: (n, 0)),
            pl.BlockSpec((hb, C, C), lambda b, p, n: (p, 0, 0)),
            pl.BlockSpec((hb, C, 1), lambda b, p, n: (p, 0, 0)),
            pl.BlockSpec((hb, C, 1), lambda b, p, n: (p, 0, 0)),
            pl.BlockSpec((hb, 1, 1), lambda b, p, n: (p, 0, 0)),
            pl.BlockSpec((1, hb * dv), lambda b, p, n: (0, p)),
        ],
        out_specs=pl.BlockSpec((1, C, hb * dv), lambda b, p, n: (b, n, p)),
        out_shape=jax.ShapeDtypeStruct((B, S, H * dv), BF16),
        scratch_shapes=[pltpu.VMEM((hb, dk, dv), F32)],
        compiler_params=_cparams("parallel", "parallel", "arbitrary"),
        name="retention",
    )(proj, proj, proj, proj, cos, sin, decay_in, q_decay, k_decay, chunk_decay, gn_gain.reshape(1, H * dv))


GDN_HB = 32
GDN_HALO = 8


def _conv_silu(x_ref, halo_ref, keep, w, stage_ref):
    c = x_ref.shape[1]
    stage_ref[0:GDN_HALO, :] = halo_ref[0] * keep
    stage_ref[GDN_HALO:GDN_HALO + c, :] = x_ref[0]
    y = w[GDN_CONV - 1:GDN_CONV] * x_ref[0]
    for j in range(GDN_CONV - 1):
        off = GDN_HALO - (GDN_CONV - 1) + j
        y = y + w[j:j + 1] * stage_ref[off:off + c, :]
    return y * jax.nn.sigmoid(y)


def _l2_rows(x):
    return x * lax.rsqrt(jnp.sum(x * x, axis=-1, keepdims=True) + NORM_EPS)


def _gdn_kernel(q_ref, k_ref, v_ref, qh_ref, kh_ref, vh_ref, wq_ref, wk_ref, wv_ref, z_ref,
                ba_ref, alog_ref, dtb_ref, ng_ref, o_ref, s_scr, qst_scr, kst_scr, vst_scr):
    C, dh, hb = GDN_CHUNK, GDN_HEAD_DIM, GDN_HB
    rep = GDN_V_HEADS // GDN_QK_HEADS
    first = pl.program_id(2) == 0

    @pl.when(first)
    def _():
        s_scr[...] = jnp.zeros_like(s_scr)

    keep = jnp.where(first, 0.0, 1.0)
    qc = _conv_silu(q_ref, qh_ref, keep, wq_ref[...], qst_scr)
    kc = _conv_silu(k_ref, kh_ref, keep, wk_ref[...], kst_scr)
    vc = _conv_silu(v_ref, vh_ref, keep, wv_ref[...], vst_scr)

    ri = lax.broadcasted_iota(jnp.int32, (C, C), 0)
    ci = lax.broadcasted_iota(jnp.int32, (C, C), 1)
    causal = ri >= ci
    strict = ri > ci
    ba = ba_ref[0]
    bcol = jax.nn.sigmoid(ba[:, :hb])
    g = -jnp.exp(alog_ref[0]) * jax.nn.softplus(ba[:, hb:2 * hb] + dtb_ref[0])
    gcol = jnp.dot(jnp.where(causal, 1.0, 0.0), g, preferred_element_type=F32, precision=lax.Precision.HIGHEST)
    grow = lax.dot_general(g, jnp.where(ri <= ci, 1.0, 0.0), (((0,), (0,)), ((), ())),
                           preferred_element_type=F32, precision=lax.Precision.HIGHEST)
    heads = range(hb)
    qs, ks, grams = [], [], []
    for hq in range(hb // rep):
        qh = _l2_rows(qc[:, hq * dh:(hq + 1) * dh]) * (dh ** -0.5)
        kh = _l2_rows(kc[:, hq * dh:(hq + 1) * dh])
        k16 = kh.astype(BF16)
        qs.append(qh)
        ks.append(kh)
        grams.append(_dot_nt(jnp.concatenate([qh.astype(BF16), k16], axis=0), k16))
    beta = [bcol[:, h:h + 1] for h in heads]
    gc = [gcol[:, h:h + 1] for h in heads]
    gr = [grow[h:h + 1, :] for h in heads]
    g_last = [gr[h][:, C - 1:C] for h in heads]
    eg = [jnp.exp(gc[h]) for h in heads]
    decay = [jnp.where(causal, jnp.exp(jnp.minimum(gc[h] - gr[h], 0.0)), 0.0) for h in heads]
    attn = [(grams[h // rep][:C] * decay[h]).astype(BF16) for h in heads]
    nm = [jnp.where(strict, grams[h // rep][C:] * decay[h] * (-beta[h]), 0.0) for h in heads]
    m = [nm[h].astype(BF16) for h in heads]
    for _ in range(int(math.log2(C)) - 1):
        mf = [jnp.dot(m[h], m[h], preferred_element_type=F32) for h in heads]
        m = [mf[h].astype(BF16) for h in heads]
        nm = [nm[h] + mf[h] + jnp.dot(m[h], nm[h].astype(BF16), preferred_element_type=F32) for h in heads]
    x = [jnp.concatenate([vc[:, h * dh:(h + 1) * dh] * beta[h], ks[h // rep] * (beta[h] * eg[h])], axis=1)
         for h in heads]
    x = [x[h] + jnp.dot(nm[h].astype(BF16), x[h].astype(BF16), preferred_element_type=F32) for h in heads]
    s = [s_scr[h] for h in heads]
    ws = [jnp.dot(jnp.concatenate([x[h][:, dh:].astype(BF16), (qs[h // rep] * eg[h]).astype(BF16)], axis=0),
                  s[h].astype(BF16), preferred_element_type=F32) for h in heads]
    vn = [(x[h][:, :dh] - ws[h][:C]).astype(BF16) for h in heads]
    o = [ws[h][C:] + jnp.dot(attn[h], vn[h], preferred_element_type=F32) for h in heads]
    for h in heads:
        k_dec = (ks[h // rep] * jnp.exp(g_last[h] - gc[h])).astype(BF16)
        s_scr[h] = s[h] * jnp.exp(g_last[h]) + _dot_tn(k_dec, vn[h])
    outs = []
    for h in heads:
        zh = z_ref[0, :, h * dh:(h + 1) * dh]
        outs.append((_rms_rows(o[h], ng_ref[...]) * (zh * jax.nn.sigmoid(zh))).astype(o_ref.dtype))
    o_ref[0] = jnp.concatenate(outs, axis=1)


_GDN_GATE_COL0 = GDN_CONV_CH + GDN_V_WIDTH
_GDN_TN = 1792
_GDN_COLS = _round_up(_GDN_GATE_COL0 + GDN_V_HEADS // GDN_HB * V7X_LANES, _GDN_TN)


def _gdn_w_in_layout(w_in):
    b0 = _GDN_GATE_COL0
    a0 = b0 + GDN_V_HEADS
    parts = [w_in[:, :b0]]
    for p in range(GDN_V_HEADS // GDN_HB):
        parts.append(w_in[:, b0 + p * GDN_HB:b0 + (p + 1) * GDN_HB])
        parts.append(jnp.pad(w_in[:, a0 + p * GDN_HB:a0 + (p + 1) * GDN_HB], ((0, 0), (0, V7X_LANES - 2 * GDN_HB))))
    w = jnp.concatenate(parts, axis=1)
    return jnp.pad(w, ((0, 0), (0, _GDN_COLS - w.shape[1]))).astype(BF16)


def gdn_delta_rule(proj, conv_w, a_log, dt_bias, norm_gain):
    B, S, _ = proj.shape
    C, dh, hb, H = GDN_CHUNK, GDN_HEAD_DIM, GDN_HB, GDN_V_HEADS
    rep = GDN_V_HEADS // GDN_QK_HEADS
    N, HP = S // C, H // hb
    wqk, wv = hb // rep * dh, hb * dh
    kb0, vb0, zb0 = GDN_QK_WIDTH // wqk, 2 * GDN_QK_WIDTH // wv, GDN_CONV_CH // wv
    gb0 = _GDN_GATE_COL0 // V7X_LANES
    hr = C // GDN_HALO
    halo = lambda n: jnp.maximum(n * hr - 1, 0)
    return pl.pallas_call(
        _gdn_kernel,
        grid=(B, HP, N),
        in_specs=[
            pl.BlockSpec((1, C, wqk), lambda b, p, n: (b, n, p)),
            pl.BlockSpec((1, C, wqk), lambda b, p, n: (b, n, kb0 + p)),
            pl.BlockSpec((1, C, wv), lambda b, p, n: (b, n, vb0 + p)),
            pl.BlockSpec((1, GDN_HALO, wqk), lambda b, p, n: (b, halo(n), p)),
            pl.BlockSpec((1, GDN_HALO, wqk), lambda b, p, n: (b, halo(n), kb0 + p)),
            pl.BlockSpec((1, GDN_HALO, wv), lambda b, p, n: (b, halo(n), vb0 + p)),
            pl.BlockSpec((GDN_CONV, wqk), lambda b, p, n: (0, p)),
            pl.BlockSpec((GDN_CONV, wqk), lambda b, p, n: (0, kb0 + p)),
            pl.BlockSpec((GDN_CONV, wv), lambda b, p, n: (0, vb0 + p)),
            pl.BlockSpec((1, C, wv), lambda b, p, n: (b, n, zb0 + p)),
            pl.BlockSpec((1, C, V7X_LANES), lambda b, p, n: (b, n, gb0 + p)),
            pl.BlockSpec((1, 1, hb), lambda b, p, n: (p, 0, 0)),
            pl.BlockSpec((1, 1, hb), lambda b, p, n: (p, 0, 0)),
            pl.BlockSpec((1, dh), lambda b, p, n: (0, 0)),
        ],
        out_specs=pl.BlockSpec((1, C, wv), lambda b, p, n: (b, n, p)),
        out_shape=jax.ShapeDtypeStruct((B, S, H * dh), BF16),
        scratch_shapes=[pltpu.VMEM((hb, dh, dh), F32), pltpu.VMEM((GDN_HALO + C, wqk), F32),
                        pltpu.VMEM((GDN_HALO + C, wqk), F32), pltpu.VMEM((GDN_HALO + C, wv), F32)],
        compiler_params=_cparams("parallel", "parallel", "arbitrary"),
        name="gdn_delta_rule",
    )(proj, proj, proj, proj, proj, proj, conv_w, conv_w, conv_w, proj, proj,
      a_log.astype(F32).reshape(HP, 1, hb), dt_bias.astype(F32).reshape(HP, 1, hb), norm_gain.reshape(1, dh))


_HYB_NSA_COLS = NSA_WIDTH + 6 * NSA_KV_WIDTH
_HYB_RET_COLS = 2 * RET_HEADS * RET_DK + 2 * RET_HEADS * RET_DV
_HYB_Q_COL0 = _HYB_RET_COLS
_HYB_KV_COL0 = _HYB_Q_COL0 + NSA_WIDTH
_HYB_GATE_COL0 = _HYB_NSA_COLS + _HYB_RET_COLS
_HYB_TN = 768
_HYB_COLS = _round_up(_HYB_GATE_COL0 + NSA_KV_GROUPS * V7X_LANES, _HYB_TN)


def _hybrid_w_in_layout(w_in):
    gate0 = _HYB_NSA_COLS
    ret0 = gate0 + 3 * NSA_HEADS
    per_group = 3 * NSA_GROUP_HEADS
    parts = [w_in[:, ret0:ret0 + _HYB_RET_COLS], w_in[:, :gate0]]
    for g in range(NSA_KV_GROUPS):
        parts.append(jnp.pad(w_in[:, gate0 + g * per_group:gate0 + (g + 1) * per_group],
                             ((0, 0), (0, V7X_LANES - per_group))))
    w = jnp.concatenate(parts, axis=1)
    return jnp.pad(w, ((0, 0), (0, _HYB_COLS - w.shape[1]))).astype(BF16)


def hybrid_mixer(h, ln_gain, w_in, w_out, q_gain, k_gain, cmp_pos, cmp_w1, cmp_w2, gn_gain, rel_bias):
    B, S, D = h.shape
    proj = norm_matmul(h.reshape(B * S, D), ln_gain, _hybrid_w_in_layout(w_in), tn=_HYB_TN).reshape(B, S, _HYB_COLS)
    cmp = nsa_compress(proj, _HYB_KV_COL0 // NSA_HEAD_DIM, cmp_pos, cmp_w1, cmp_w2, k_gain[0])
    ksn, vsb, kwn, vwb = nsa_kv_prep(proj, (_HYB_KV_COL0 + 2 * NSA_KV_WIDTH) // NSA_KV_WIDTH, k_gain)
    a_out = nsa_attention_pallas(proj, _HYB_Q_COL0 // (NSA_GROUP_HEADS * NSA_HEAD_DIM), _HYB_GATE_COL0 // V7X_LANES,
                                 cmp[0], cmp[1], ksn, vsb, kwn, vwb, q_gain, rel_bias)
    b_out = retention_pallas(proj, 0, gn_gain)
    w_out = w_out.astype(BF16)
    return matmul2_residual(a_out.reshape(B * S, -1), b_out.reshape(B * S, -1), w_out,
                            h.reshape(B * S, D)).reshape(B, S, D)


def gdn_mixer(h, ln_gain, w_in, conv_w, a_log, dt_bias, norm_gain, w_out):
    B, S, D = h.shape
    proj = norm_matmul(h.reshape(B * S, D), ln_gain, _gdn_w_in_layout(w_in), tn=_GDN_TN).reshape(B, S, _GDN_COLS)
    o = gdn_delta_rule(proj, conv_w, a_log, dt_bias, norm_gain)
    return matmul_residual(o.reshape(B * S, GDN_V_WIDTH), w_out.astype(BF16), h.reshape(B * S, D)).reshape(B, S, D)


def memory_kv(mem, mem_gain, wkv, k_gain):
    B, M, D = mem.shape
    kv = norm_matmul(mem.reshape(B * M, D), mem_gain, wkv.astype(BF16))
    k, v = jnp.split(kv.reshape(B, M, 2 * XA_WIDTH), 2, axis=-1)
    k = rms_norm(k.reshape(B, M, XA_HEADS, XA_HEAD_DIM), k_gain).reshape(B, M, XA_WIDTH)
    return k.astype(BF16), v.astype(BF16)


def kernel(x, mem, rel_bias, ln_mix, ln_mem, ln_ffn, hyb_w_in, hyb_w_out, nsa_q_gain, nsa_k_gain,
           nsa_cmp_pos, nsa_cmp_w1, nsa_cmp_w2, ret_gn_gain, gdn_w_in, gdn_conv_w, gdn_a_log,
           gdn_dt_bias, gdn_norm_gain, gdn_w_out, xa_wq, xa_wkv, xa_q_gain, xa_k_gain, xa_mem_gain,
           xa_wo, ffn_w_in, ffn_w_out):
    B, S, D = x.shape
    h = x
    ffn_w_in, ffn_w_out, gdn_w_out = cast_bf16(ffn_w_in), cast_bf16(ffn_w_out), cast_bf16(gdn_w_out)
    for layer in range(DEPTH):
        if layer % 2 == 0:
            e = layer // 2
            h = hybrid_mixer(h, ln_mix[layer], hyb_w_in[e], hyb_w_out[e], nsa_q_gain[e], nsa_k_gain[e],
                             nsa_cmp_pos[e], nsa_cmp_w1[e], nsa_cmp_w2[e], ret_gn_gain[e], rel_bias)
        else:
            o = layer // 2
            h = gdn_mixer(h, ln_mix[layer], gdn_w_in[o], gdn_conv_w[o], gdn_a_log[o], gdn_dt_bias[o],
                          gdn_norm_gain[o], gdn_w_out[o])
        k_mem, v_mem = memory_kv(mem, xa_mem_gain[layer], xa_wkv[layer], xa_k_gain[layer])
        h = xattn_residual(h, ln_mem[layer], xa_wq[layer].astype(BF16), xa_q_gain[layer], k_mem, v_mem,
                           xa_wo[layer].astype(BF16))
        h = ffn_residual(h.reshape(B * S, D), ln_ffn[layer], ffn_w_in[layer].astype(BF16),
                         ffn_w_out[layer].astype(BF16)).reshape(B, S, D)
    return h
```

```python
import functools
import math

import jax
import jax.numpy as jnp
import numpy as np
from jax import lax
from jax.experimental import pallas as pl
from jax.experimental.pallas import tpu as pltpu

F32 = jnp.float32
BF16 = jnp.bfloat16

D_MODEL = 2048
DEPTH = 4
NORM_EPS = 1e-6
NEG_INF = -1e30

NSA_HEADS = 8
NSA_KV_GROUPS = 2
NSA_GROUP_HEADS = NSA_HEADS // NSA_KV_GROUPS
NSA_HEAD_DIM = 128
NSA_CMP_LEN = 32
NSA_CMP_STRIDE = 16
NSA_SEL_BLOCK = 64
NSA_N_SELECT = 16
NSA_WINDOW = 512
NSA_Q_BLOCK = 64
NSA_FORCE_SCORE = 1e6
NSA_WIDTH = NSA_HEADS * NSA_HEAD_DIM
NSA_KV_WIDTH = NSA_KV_GROUPS * NSA_HEAD_DIM

RET_HEADS = 4
RET_DK = 256
RET_DV = 256
RET_CHUNK = 128
RET_ROPE_BASE = 10000.0
RET_GN_EPS = 1e-5

GDN_QK_HEADS = 16
GDN_V_HEADS = 32
GDN_HEAD_DIM = 128
GDN_CONV = 4
GDN_CHUNK = 64
GDN_QK_WIDTH = GDN_QK_HEADS * GDN_HEAD_DIM
GDN_V_WIDTH = GDN_V_HEADS * GDN_HEAD_DIM
GDN_CONV_CH = 2 * GDN_QK_WIDTH + GDN_V_WIDTH

REL_BUCKETS = 32
REL_MAX_DIST = 1024

XA_HEADS = 4
XA_HEAD_DIM = 128
XA_WIDTH = XA_HEADS * XA_HEAD_DIM

V7X_VMEM_LIMIT_BYTES = 56 * 1024 * 1024
V7X_LANES = 128


def _cparams(*sem):
    return pltpu.CompilerParams(dimension_semantics=sem, vmem_limit_bytes=V7X_VMEM_LIMIT_BYTES)


def _round_up(n, m):
    return -(-n // m) * m


def _rms_rows(x, gain):
    return x * lax.rsqrt(jnp.mean(x * x, axis=-1, keepdims=True) + NORM_EPS) * gain


def _cast_kernel(x_ref, o_ref):
    o_ref[...] = x_ref[...].astype(o_ref.dtype)


def cast_bf16(w, *, block_bytes=8 * 1024 * 1024):
    n = w.shape[-1]
    x = w.reshape(-1, n)
    rows = x.shape[0]
    tr = rows
    while tr * n * 4 > block_bytes and tr % 32 == 0:
        tr //= 2
    out = pl.pallas_call(
        _cast_kernel,
        grid=(rows // tr,),
        in_specs=[pl.BlockSpec((tr, n), lambda i: (i, 0))],
        out_specs=pl.BlockSpec((tr, n), lambda i: (i, 0)),
        out_shape=jax.ShapeDtypeStruct((rows, n), BF16),
        compiler_params=_cparams("parallel"),
        name="cast_bf16",
    )(x)
    return out.reshape(w.shape)


def _norm_matmul_kernel(x_ref, g_ref, w_ref, o_ref, xn_ref):
    @pl.when(pl.program_id(1) == 0)
    def _():
        xn_ref[...] = _rms_rows(x_ref[...], g_ref[...]).astype(BF16)

    o_ref[...] = jnp.dot(xn_ref[...], w_ref[...], preferred_element_type=F32).astype(o_ref.dtype)


def norm_matmul(x, gain, w, *, tm=1024, tn=512, out_dtype=F32):
    m, k = x.shape
    n = w.shape[1]
    tm = min(tm, m)
    return pl.pallas_call(
        _norm_matmul_kernel,
        grid=(m // tm, n // tn),
        in_specs=[pl.BlockSpec((tm, k), lambda i, j: (i, 0)),
                  pl.BlockSpec((1, k), lambda i, j: (0, 0)),
                  pl.BlockSpec((k, tn), lambda i, j: (0, j))],
        out_specs=pl.BlockSpec((tm, tn), lambda i, j: (i, j)),
        out_shape=jax.ShapeDtypeStruct((m, n), out_dtype),
        scratch_shapes=[pltpu.VMEM((tm, k), BF16)],
        compiler_params=_cparams("parallel", "arbitrary"),
        name="norm_matmul",
    )(x, gain.reshape(1, k), w)


def _matmul_res_kernel(x_ref, w_ref, r_ref, o_ref):
    o_ref[...] = r_ref[...] + jnp.dot(x_ref[...], w_ref[...], preferred_element_type=F32)


def matmul_residual(x, w, res, *, tm=1024, tn=1024):
    m, k = x.shape
    n = w.shape[1]
    return pl.pallas_call(
        _matmul_res_kernel,
        grid=(m // tm, n // tn),
        in_specs=[pl.BlockSpec((tm, k), lambda i, j: (i, 0)),
                  pl.BlockSpec((k, tn), lambda i, j: (0, j)),
                  pl.BlockSpec((tm, tn), lambda i, j: (i, j))],
        out_specs=pl.BlockSpec((tm, tn), lambda i, j: (i, j)),
        out_shape=jax.ShapeDtypeStruct((m, n), F32),
        compiler_params=_cparams("parallel", "arbitrary"),
        name="matmul_residual",
    )(x, w, res)


def _matmul2_res_kernel(x1_ref, x2_ref, w1_ref, w2_ref, r_ref, o_ref):
    o_ref[...] = (r_ref[...] + jnp.dot(x1_ref[...], w1_ref[...], preferred_element_type=F32)
                  + jnp.dot(x2_ref[...], w2_ref[...], preferred_element_type=F32))


def matmul2_residual(x1, x2, w, res, *, tm=1024, tn=1024):
    m, k1 = x1.shape
    n = w.shape[1]
    return pl.pallas_call(
        _matmul2_res_kernel,
        grid=(m // tm, n // tn),
        in_specs=[pl.BlockSpec((tm, k1), lambda i, j: (i, 0)),
                  pl.BlockSpec((tm, k1), lambda i, j: (i, 0)),
                  pl.BlockSpec((k1, tn), lambda i, j: (0, j)),
                  pl.BlockSpec((k1, tn), lambda i, j: (1, j)),
                  pl.BlockSpec((tm, tn), lambda i, j: (i, j))],
        out_specs=pl.BlockSpec((tm, tn), lambda i, j: (i, j)),
        out_shape=jax.ShapeDtypeStruct((m, n), F32),
        compiler_params=_cparams("parallel", "arbitrary"),
        name="matmul2_residual",
    )(x1, x2, w, w, res)


def _ffn_kernel(x_ref, g_ref, wg_ref, wu_ref, wo_ref, o_ref, xn_ref, acc_ref):
    j = pl.program_id(1)

    @pl.when(j == 0)
    def _():
        x = x_ref[...]
        xn_ref[...] = _rms_rows(x, g_ref[...]).astype(BF16)
        acc_ref[...] = x

    xn = xn_ref[...]
    gate = jnp.dot(xn, wg_ref[...], preferred_element_type=F32)
    up = jnp.dot(xn, wu_ref[...], preferred_element_type=F32)
    act = (gate * jax.nn.sigmoid(gate) * up).astype(BF16)
    acc_ref[...] += jnp.dot(act, wo_ref[...], preferred_element_type=F32)

    @pl.when(j == pl.num_programs(1) - 1)
    def _():
        o_ref[...] = acc_ref[...]


def ffn_residual(h, gain, w_in, w_out, *, tm=512, th=512):
    m, d = h.shape
    hidden = w_out.shape[0]
    nh = hidden // th
    return pl.pallas_call(
        _ffn_kernel,
        grid=(m // tm, nh),
        in_specs=[pl.BlockSpec((tm, d), lambda i, j: (i, 0)),
                  pl.BlockSpec((1, d), lambda i, j: (0, 0)),
                  pl.BlockSpec((d, th), lambda i, j: (0, j)),
                  pl.BlockSpec((d, th), lambda i, j: (0, j + nh)),
                  pl.BlockSpec((th, d), lambda i, j: (j, 0))],
        out_specs=pl.BlockSpec((tm, d), lambda i, j: (i, 0)),
        out_shape=jax.ShapeDtypeStruct((m, d), F32),
        scratch_shapes=[pltpu.VMEM((tm, d), BF16), pltpu.VMEM((tm, d), F32)],
        compiler_params=_cparams("parallel", "arbitrary"),
        name="ffn_residual",
    )(h, gain.reshape(1, d), w_in, w_in, w_out)


def _xattn_kernel(h_ref, g_ref, wq_ref, qg_ref, k_ref, v_ref, wo_ref, o_ref):
    x = h_ref[0]
    xn = _rms_rows(x, g_ref[...]).astype(BF16)
    q = jnp.dot(xn, wq_ref[...], preferred_element_type=F32)
    k = k_ref[0]
    v = v_ref[0]
    outs = []
    for hh in range(XA_HEADS):
        sl = slice(hh * XA_HEAD_DIM, (hh + 1) * XA_HEAD_DIM)
        qh = _rms_rows(q[:, sl], qg_ref[...]).astype(BF16)
        s = lax.dot_general(qh, k[:, sl], (((1,), (1,)), ((), ())), preferred_element_type=F32)
        s = s * (XA_HEAD_DIM ** -0.5)
        s = s - jnp.max(s, axis=-1, keepdims=True)
        p = jnp.exp(s)
        p = p / jnp.sum(p, axis=-1, keepdims=True)
        outs.append(jnp.dot(p.astype(BF16), v[:, sl], preferred_element_type=F32))
    o = jnp.concatenate(outs, axis=-1).astype(BF16)
    o_ref[0] = x + jnp.dot(o, wo_ref[...], preferred_element_type=F32)


def xattn_residual(h, gain, wq, q_gain, k, v, wo, *, tm=512):
    b, s, d = h.shape
    mt = k.shape[1]
    return pl.pallas_call(
        _xattn_kernel,
        grid=(b, s // tm),
        in_specs=[pl.BlockSpec((1, tm, d), lambda bi, i: (bi, i, 0)),
                  pl.BlockSpec((1, d), lambda bi, i: (0, 0)),
                  pl.BlockSpec((d, XA_WIDTH), lambda bi, i: (0, 0)),
                  pl.BlockSpec((1, XA_HEAD_DIM), lambda bi, i: (0, 0)),
                  pl.BlockSpec((1, mt, XA_WIDTH), lambda bi, i: (bi, 0, 0)),
                  pl.BlockSpec((1, mt, XA_WIDTH), lambda bi, i: (bi, 0, 0)),
                  pl.BlockSpec((XA_WIDTH, d), lambda bi, i: (0, 0))],
        out_specs=pl.BlockSpec((1, tm, d), lambda bi, i: (bi, i, 0)),
        out_shape=jax.ShapeDtypeStruct((b, s, d), F32),
        compiler_params=_cparams("parallel", "parallel"),
        name="xattn_residual",
    )(h, gain.reshape(1, d), wq, q_gain.reshape(1, XA_HEAD_DIM), k, v, wo)


def rms_norm(x, gain, eps=NORM_EPS):
    x32 = x.astype(F32)
    y = x32 * lax.rsqrt(jnp.mean(x32 * x32, axis=-1, keepdims=True) + eps)
    return (y * gain.astype(F32)).astype(x.dtype)


def rel_bucket(dist):
    dist = jnp.maximum(dist, 0)
    max_exact = REL_BUCKETS // 2
    scaled = (jnp.log(jnp.maximum(dist, max_exact).astype(F32) / max_exact)
              / math.log(REL_MAX_DIST / max_exact) * (REL_BUCKETS - max_exact))
    large = jnp.minimum(max_exact + scaled.astype(jnp.int32), REL_BUCKETS - 1)
    return jnp.where(dist < max_exact, dist, large)


NSA_QT = 128
NSA_KT = 512
NSA_WT = NSA_WINDOW + NSA_QT


def _dot_nt(a, b):
    return lax.dot_general(a, b, (((1,), (1,)), ((), ())), preferred_element_type=F32)


def _dot_tn(a, b):
    return lax.dot_general(a, b, (((0,), (0,)), ((), ())), preferred_element_type=F32)


def _bias_tile(tb_ref, h, first_slab, n_slabs):
    return jnp.concatenate([tb_ref[h, jnp.maximum(first_slab - j, 0)] for j in range(n_slabs)], axis=1)


def _nsa_kernel(q_ref, gl_ref, kc_ref, vc_ref, bc_ref, covt_ref, ks_ref, vs_ref, kw_ref, vw_ref,
                tb_ref, e_ref, qg_ref, o_ref, a_scr, *, n_top):
    QT, KT, WT, HG, dh = NSA_QT, NSA_KT, NSA_WT, NSA_GROUP_HEADS, NSA_HEAD_DIM
    nsb, nc = covt_ref.shape
    q0 = pl.program_id(2) * QT

    x = q_ref[0]
    qs = []
    for h in range(HG):
        xh = _rms_rows(x[:, h * dh:(h + 1) * dh], qg_ref[...]) * (dh ** -0.5)
        qs.append(xh.astype(BF16))
    qn = jnp.concatenate(qs, axis=0)

    w0 = pl.multiple_of(q0, QT)
    kwin = kw_ref[0, pl.ds(w0, WT), :]
    vwin = vw_ref[0, pl.ds(w0, WT), :]
    sw = _dot_nt(qn, kwin)
    cw = lax.broadcasted_iota(jnp.int32, (QT, WT), 1)
    dist = NSA_WINDOW + lax.broadcasted_iota(jnp.int32, (QT, WT), 0) - cw
    vis_w = (dist >= 0) & (dist < NSA_WINDOW) & (cw >= NSA_WINDOW - q0)
    parts = []
    for h in range(HG):
        bias = _bias_tile(tb_ref, h, NSA_WINDOW // QT, WT // QT)
        parts.append(jnp.where(vis_w, sw[h * QT:(h + 1) * QT] + bias, NEG_INF))
    sw = jnp.concatenate(parts, axis=0)
    pw = jnp.exp(sw - jnp.max(sw, axis=-1, keepdims=True))
    o_w = jnp.dot(pw.astype(BF16), vwin, preferred_element_type=F32) / jnp.sum(pw, axis=-1, keepdims=True)

    sc = _dot_nt(qn, kc_ref[0, 0]) + bc_ref[...].reshape(HG * QT, nc)
    row = lax.broadcasted_iota(jnp.int32, (HG * QT, nc), 0)
    col = lax.broadcasted_iota(jnp.int32, (HG * QT, nc), 1)
    tq = q0 + (row & (QT - 1))
    mask_c = tq >= col * NSA_CMP_STRIDE + (NSA_CMP_LEN - 1)
    sc = jnp.where(mask_c, sc, NEG_INF)
    pc = jnp.where(mask_c, jnp.exp(sc - jnp.max(sc, axis=-1, keepdims=True)), 0.0)
    den = jnp.sum(pc, axis=-1, keepdims=True)
    pcb = (pc / jnp.maximum(den, 1e-30)).astype(BF16)
    o_c = jnp.dot(pcb, vc_ref[0, 0], preferred_element_type=F32)

    imp_all = _dot_nt(covt_ref[...], pcb)
    imp = imp_all[:, 0:QT]
    for h in range(1, HG):
        imp = imp + imp_all[:, h * QT:(h + 1) * QT]
    jj = lax.broadcasted_iota(jnp.int32, (nsb, QT), 0)
    cur = (q0 + lax.broadcasted_iota(jnp.int32, (nsb, QT), 1)) // NSA_SEL_BLOCK
    forced = (jj == 0) | (jj == cur) | (jj == cur - 1)
    a = lax.bitcast_convert_type(jnp.where(forced, NSA_FORCE_SCORE, imp), jnp.int32)
    a = jnp.where(jj > cur, -1, a)
    a1 = a + 1
    a_scr[...] = a

    def rank_body(i, rank):
        r = a_scr[pl.ds(i, 1), :]
        return rank + jnp.where(r >= jnp.where(jj > i, a, a1), 1.0, 0.0)

    n_live = jnp.minimum((q0 + QT - 1) // NSA_SEL_BLOCK + 1, nsb)
    rank = lax.fori_loop(0, n_live, rank_body, jnp.zeros((nsb, QT), F32))
    sel_t = jnp.where(rank < n_top, 1.0, 0.0)
    if nsb < V7X_LANES:
        sel_t = jnp.concatenate([sel_t, jnp.zeros((V7X_LANES - nsb, QT), F32)], axis=0)
    sel = sel_t.T.astype(BF16)

    cmr = (lax.broadcasted_iota(jnp.int32, (QT, KT), 1) - lax.broadcasted_iota(jnp.int32, (QT, KT), 0))

    def sel_body(kt, carry):
        m_i, l_i, acc = carry
        k0 = pl.multiple_of(kt * KT, KT)
        k = ks_ref[0, pl.ds(k0, KT), :]
        v = vs_ref[0, pl.ds(k0, KT), :]
        s = _dot_nt(qn, k)
        delta = q0 - k0
        visible = (jnp.dot(sel, e_ref[kt], preferred_element_type=F32) > 0.5) & (cmr <= delta)
        seg = delta // QT
        parts = []
        for h in range(HG):
            bias = _bias_tile(tb_ref, h, seg, KT // QT)
            parts.append(jnp.where(visible, s[h * QT:(h + 1) * QT] + bias, NEG_INF))
        s = jnp.concatenate(parts, axis=0)
        m_new = jnp.maximum(m_i, jnp.max(s, axis=-1, keepdims=True))
        alpha = jnp.exp(m_i - m_new)
        p = jnp.exp(s - m_new)
        l_new = alpha * l_i + jnp.sum(p, axis=-1, keepdims=True)
        acc = alpha * acc + jnp.dot(p.astype(BF16), v, preferred_element_type=F32)
        return m_new, l_new, acc

    init = (jnp.full((HG * QT, 1), NEG_INF, F32), jnp.zeros((HG * QT, 1), F32), jnp.zeros((HG * QT, dh), F32))
    _, l_s, acc_s = lax.fori_loop(0, q0 // KT + 1, sel_body, init)
    o_s = acc_s / l_s

    gates = jax.nn.sigmoid(gl_ref[0][:, :3 * HG])
    outs = []
    for h in range(HG):
        rs = slice(h * QT, (h + 1) * QT)
        outs.append(gates[:, 3 * h:3 * h + 1] * o_c[rs] + gates[:, 3 * h + 1:3 * h + 2] * o_s[rs]
                    + gates[:, 3 * h + 2:3 * h + 3] * o_w[rs])
    o_ref[0] = jnp.concatenate(outs, axis=-1).astype(o_ref.dtype)


def nsa_attention_pallas(proj, q_blk, gl_blk, kc, vc, ks, vs, kw, vw, q_gain, rel_bias):
    B, S, _ = proj.shape
    G, HG, dh = NSA_KV_GROUPS, NSA_GROUP_HEADS, NSA_HEAD_DIM
    QT, KT, WT, W, SB = NSA_QT, NSA_KT, NSA_WT, NSA_WINDOW, NSA_SEL_BLOCK
    nsb, nc, nkt, nseg = S // SB, S // NSA_CMP_STRIDE, S // KT, max(S // QT, W // QT + 1)
    n_top = min(NSA_N_SELECT, nsb)

    def bias_of(dist):
        onehot = (rel_bucket(dist)[..., None] == jnp.arange(REL_BUCKETS)) & (dist >= 0)[..., None]
        return jnp.einsum('...b,bh->h...', onehot.astype(F32), rel_bias.astype(F32),
                          precision=lax.Precision.HIGHEST)

    bc = bias_of(jnp.arange(S)[:, None] - (jnp.arange(nc) * NSA_CMP_STRIDE + NSA_CMP_LEN - 1)[None, :])
    tb = bias_of(QT * jnp.arange(nseg)[:, None, None] + jnp.arange(QT)[None, :, None] - jnp.arange(QT)[None, None, :])
    cmp_start = np.arange(nc) * NSA_CMP_STRIDE
    cmp_end = cmp_start + NSA_CMP_LEN - 1
    sel_start = np.arange(nsb) * SB
    cover_t = ((cmp_start[None, :] < sel_start[:, None] + SB) & (cmp_end[None, :] >= sel_start[:, None])
               & (np.arange(nc)[None, :] < nc - 1))
    cover_t = jnp.asarray(cover_t, BF16)
    e = (np.arange(V7X_LANES)[None, :, None]
         == (np.arange(nkt)[:, None, None] * (KT // SB) + np.arange(KT)[None, None, :] // SB))
    e = jnp.asarray(e, BF16)

    kern = functools.partial(_nsa_kernel, n_top=n_top)
    return pl.pallas_call(
        kern,
        grid=(B, G, S // QT),
        in_specs=[
            pl.BlockSpec((1, QT, HG * dh), lambda b, g, i: (b, i, q_blk + g)),
            pl.BlockSpec((1, QT, V7X_LANES), lambda b, g, i: (b, i, gl_blk + g)),
            pl.BlockSpec((1, 1, nc, dh), lambda b, g, i: (b, g, 0, 0)),
            pl.BlockSpec((1, 1, nc, dh), lambda b, g, i: (b, g, 0, 0)),
            pl.BlockSpec((HG, QT, nc), lambda b, g, i: (g, i, 0)),
            pl.BlockSpec((nsb, nc), lambda b, g, i: (0, 0)),
            pl.BlockSpec((1, S, dh), lambda b, g, i: (b, 0, g)),
            pl.BlockSpec((1, S, dh), lambda b, g, i: (b, 0, g)),
            pl.BlockSpec((1, S + W, dh), lambda b, g, i: (b, 0, g)),
            pl.BlockSpec((1, S + W, dh), lambda b, g, i: (b, 0, g)),
            pl.BlockSpec((HG, nseg, QT, QT), lambda b, g, i: (g, 0, 0, 0)),
            pl.BlockSpec((nkt, V7X_LANES, KT), lambda b, g, i: (0, 0, 0)),
            pl.BlockSpec((1, dh), lambda b, g, i: (0, 0)),
        ],
        out_specs=pl.BlockSpec((1, QT, HG * dh), lambda b, g, i: (b, i, g)),
        out_shape=jax.ShapeDtypeStruct((B, S, NSA_WIDTH), BF16),
        scratch_shapes=[pltpu.VMEM((nsb, QT), jnp.int32)],
        compiler_params=_cparams("parallel", "parallel", "arbitrary"),
        name="nsa_attention",
    )(proj, proj, kc, vc, bc, cover_t, ks, vs, kw, vw, tb, e, q_gain.reshape(1, dh))


def _nsa_compress_kernel(x_ref, pos_ref, w1_ref, w2_ref, kg_ref, o_ref):
    stride, dh = NSA_CMP_STRIDE, NSA_HEAD_DIM
    n_chunks = x_ref.shape[1] // stride
    hid = w1_ref.shape[2]
    h_lo = jnp.zeros((n_chunks, hid), F32)
    h_hi = jnp.zeros((n_chunks, hid), F32)
    for p in range(stride):
        xp = x_ref[0, pl.ds(p, n_chunks, stride=stride), :]
        h_lo = h_lo + jnp.dot((xp + pos_ref[0, p:p + 1, :]).astype(BF16), w1_ref[0, p * dh:(p + 1) * dh, :],
                              preferred_element_type=F32)
        h_hi = h_hi + jnp.dot((xp + pos_ref[0, stride + p:stride + p + 1, :]).astype(BF16),
                              w1_ref[0, (stride + p) * dh:(stride + p + 1) * dh, :], preferred_element_type=F32)
    hidden = h_lo + jnp.concatenate([h_hi[1:], jnp.zeros((1, hid), F32)], axis=0)
    out = jnp.dot((hidden * jax.nn.sigmoid(hidden)).astype(BF16), w2_ref[0], preferred_element_type=F32)
    out = jnp.where(pl.program_id(2) == 0, _rms_rows(out, kg_ref[...]), out)
    row = lax.broadcasted_iota(jnp.int32, out.shape, 0)
    o_ref[0, 0, 0] = jnp.where(row < n_chunks - 1, out, 0.0).astype(o_ref.dtype)


def nsa_compress(proj, col_blk, cmp_pos, cmp_w1, cmp_w2, k_gain):
    B, S, _ = proj.shape
    G, dh = NSA_KV_GROUPS, NSA_HEAD_DIM
    nc = S // NSA_CMP_STRIDE
    hid = cmp_w1.shape[-1]
    return pl.pallas_call(
        _nsa_compress_kernel,
        grid=(B, G, 2),
        in_specs=[
            pl.BlockSpec((1, S, dh), lambda b, g, w: (b, 0, col_blk + w * G + g)),
            pl.BlockSpec((1, NSA_CMP_LEN, dh), lambda b, g, w: (w, 0, 0)),
            pl.BlockSpec((1, NSA_CMP_LEN * dh, hid), lambda b, g, w: (w, 0, 0)),
            pl.BlockSpec((1, hid, dh), lambda b, g, w: (w, 0, 0)),
            pl.BlockSpec((1, dh), lambda b, g, w: (0, 0)),
        ],
        out_specs=pl.BlockSpec((1, 1, 1, nc, dh), lambda b, g, w: (w, b, g, 0, 0)),
        out_shape=jax.ShapeDtypeStruct((2, B, G, nc, dh), BF16),
        compiler_params=_cparams("parallel", "parallel", "arbitrary"),
        name="nsa_compress",
    )(proj, cmp_pos, cmp_w1.astype(BF16), cmp_w2.astype(BF16), k_gain.reshape(1, dh))


NSA_PREP_TM = NSA_WINDOW


def _nsa_kv_kernel(ks_ref, vs_ref, kw_ref, vw_ref, gs_ref, gw_ref, kso_ref, vso_ref, kwo_ref, vwo_ref):
    G, dh = NSA_KV_GROUPS, NSA_HEAD_DIM
    i = pl.program_id(1)
    n_in = pl.num_programs(1) - 1

    def normed(ref, gain_ref):
        x = ref[0]
        return jnp.concatenate([_rms_rows(x[:, g * dh:(g + 1) * dh], gain_ref[...]) for g in range(G)], axis=1)

    @pl.when(i < n_in)
    def _():
        kso_ref[0] = normed(ks_ref, gs_ref).astype(kso_ref.dtype)
        vso_ref[0] = vs_ref[0].astype(vso_ref.dtype)

    @pl.when(i == 0)
    def _():
        kwo_ref[0] = jnp.zeros(kwo_ref.shape[1:], kwo_ref.dtype)
        vwo_ref[0] = jnp.zeros(vwo_ref.shape[1:], vwo_ref.dtype)

    @pl.when(i > 0)
    def _():
        kwo_ref[0] = normed(kw_ref, gw_ref).astype(kwo_ref.dtype)
        vwo_ref[0] = vw_ref[0].astype(vwo_ref.dtype)


def nsa_kv_prep(proj, col_blk, k_gain):
    B, S, _ = proj.shape
    tm, kvw, W = NSA_PREP_TM, NSA_KV_WIDTH, NSA_WINDOW
    n_in = S // tm
    cur = lambda i: jnp.minimum(i, n_in - 1)
    prev = lambda i: jnp.maximum(i - 1, 0)
    return pl.pallas_call(
        _nsa_kv_kernel,
        grid=(B, n_in + 1),
        in_specs=[
            pl.BlockSpec((1, tm, kvw), lambda b, i: (b, cur(i), col_blk)),
            pl.BlockSpec((1, tm, kvw), lambda b, i: (b, cur(i), col_blk + 1)),
            pl.BlockSpec((1, tm, kvw), lambda b, i: (b, prev(i), col_blk + 2)),
            pl.BlockSpec((1, tm, kvw), lambda b, i: (b, prev(i), col_blk + 3)),
            pl.BlockSpec((1, NSA_HEAD_DIM), lambda b, i: (0, 0)),
            pl.BlockSpec((1, NSA_HEAD_DIM), lambda b, i: (0, 0)),
        ],
        out_specs=[
            pl.BlockSpec((1, tm, kvw), lambda b, i: (b, cur(i), 0)),
            pl.BlockSpec((1, tm, kvw), lambda b, i: (b, cur(i), 0)),
            pl.BlockSpec((1, tm, kvw), lambda b, i: (b, i, 0)),
            pl.BlockSpec((1, tm, kvw), lambda b, i: (b, i, 0)),
        ],
        out_shape=[jax.ShapeDtypeStruct((B, S, kvw), BF16), jax.ShapeDtypeStruct((B, S, kvw), BF16),
                   jax.ShapeDtypeStruct((B, S + W, kvw), BF16), jax.ShapeDtypeStruct((B, S + W, kvw), BF16)],
        compiler_params=_cparams("parallel", "arbitrary"),
        name="nsa_kv_prep",
    )(proj, proj, proj, proj, k_gain[1:2], k_gain[2:3])


RET_HB = 4


def _rotate_half(x, cos, sin):
    half = x.shape[-1] // 2
    x1, x2 = x[:, :half], x[:, half:]
    return jnp.concatenate([x1 * cos - x2 * sin, x1 * sin + x2 * cos], axis=-1)


def _retention_kernel(q_ref, k_ref, v_ref, g_ref, cos_ref, sin_ref, din_ref, qd_ref, kd_ref, cd_ref, gn_ref,
                      o_ref, s_scr):
    @pl.when(pl.program_id(2) == 0)
    def _():
        s_scr[...] = jnp.zeros_like(s_scr)

    cos, sin = cos_ref[...], sin_ref[...]
    heads = range(RET_HB)
    sl = [slice(h * RET_DK, (h + 1) * RET_DK) for h in heads]
    q = [_rotate_half(q_ref[0, :, sl[h]], cos, sin) * (RET_DK ** -0.5) for h in heads]
    k = [_rotate_half(k_ref[0, :, sl[h]], cos, sin) for h in heads]
    v16 = [v_ref[0, :, sl[h]].astype(BF16) for h in heads]
    inner = [_dot_nt(q[h].astype(BF16), k[h].astype(BF16)) * din_ref[h] for h in heads]
    s = [s_scr[h] for h in heads]
    o = [jnp.dot(inner[h].astype(BF16), v16[h], preferred_element_type=F32)
         + jnp.dot((q[h] * qd_ref[h]).astype(BF16), s[h].astype(BF16), preferred_element_type=F32) for h in heads]
    for h in heads:
        s_scr[h] = s[h] * cd_ref[h] + _dot_tn((k[h] * kd_ref[h]).astype(BF16), v16[h])
    outs = []
    for h in heads:
        mu = jnp.mean(o[h], axis=-1, keepdims=True)
        d = o[h] - mu
        var = jnp.mean(d * d, axis=-1, keepdims=True)
        gate = g_ref[0, :, sl[h]]
        outs.append((gate * jax.nn.sigmoid(gate)
                     * (d * lax.rsqrt(var + RET_GN_EPS) * gn_ref[:, sl[h]])).astype(o_ref.dtype))
    o_ref[0] = jnp.concatenate(outs, axis=-1)


def retention_pallas(proj, q_blk, gn_gain):
    B, S, _ = proj.shape
    H, dk, dv, C = RET_HEADS, RET_DK, RET_DV, RET_CHUNK
    N = S // C
    half = dk // 2
    inv = RET_ROPE_BASE ** (-jnp.arange(half, dtype=F32) / half)
    ang = jnp.arange(S).astype(F32)[:, None] * inv[None, :]
    cos, sin = jnp.cos(ang), jnp.sin(ang)
    log_gamma = jnp.log(1.0 - 2.0 ** (-5.0 - jnp.arange(H, dtype=F32)))
    idx = jnp.arange(C, dtype=F32)
    rel = idx[:, None] - idx[None, :]
    decay_in = jnp.where(rel >= 0, jnp.exp(log_gamma[:, None, None] * jnp.maximum(rel, 0.0)), 0.0)
    q_decay = jnp.exp(log_gamma[:, None] * (idx + 1.0))[..., None]
    k_decay = jnp.exp(log_gamma[:, None] * (C - 1.0 - idx))[..., None]
    chunk_decay = jnp.exp(log_gamma * C)[:, None, None]
    hb = RET_HB
    HP = H // hb
    qb0 = q_blk // hb
    return pl.pallas_call(
        _retention_kernel,
        grid=(B, HP, N),
        in_specs=[
            pl.BlockSpec((1, C, hb * dk), lambda b, p, n: (b, n, qb0 + p)),
            pl.BlockSpec((1, C, hb * dk), lambda b, p, n: (b, n, qb0 + HP + p)),
            pl.BlockSpec((1, C, hb * dv), lambda b, p, n: (b, n, qb0 + 2 * HP + p)),
            pl.BlockSpec((1, C, hb * dv), lambda b, p, n: (b, n, qb0 + 3 * HP + p)),
            pl.BlockSpec((C, half), lambda b, p, n: (n, 0)),
            pl.BlockSpec((C, half), lambda b, p, n: (n, 0)),
            pl.BlockSpec((hb, C, C), lambda b, p, n: (p, 0, 0)),
            pl.BlockSpec((hb, C, 1), lambda b, p, n: (p, 0, 0)),
            pl.BlockSpec((hb, C, 1), lambda b, p, n: (p, 0, 0)),
            pl.BlockSpec((hb, 1, 1), lambda b, p, n: (p, 0, 0)),
            pl.BlockSpec((1, hb * dv), lambda b, p, n: (0, p)),
        ],
        out_specs=pl.BlockSpec((1, C, hb * dv), lambda b, p, n: (b, n, p)),
        out_shape=jax.ShapeDtypeStruct((B, S, H * dv), BF16),
        scratch_shapes=[pltpu.VMEM((hb, dk, dv), F32)],
        compiler_params=_cparams("parallel", "parallel", "arbitrary"),
        name="retention",
    )(proj, proj, proj, proj, cos, sin, decay_in, q_decay, k_decay, chunk_decay, gn_gain.reshape(1, H * dv))


GDN_HB = 32
GDN_HALO = 8


def _conv_silu(x_ref, halo_ref, keep, w, stage_ref):
    c = x_ref.shape[1]
    stage_ref[0:GDN_HALO, :] = halo_ref[0] * keep
    stage_ref[GDN_HALO:GDN_HALO + c, :] = x_ref[0]
    y = w[GDN_CONV - 1:GDN_CONV] * x_ref[0]
    for j in range(GDN_CONV - 1):
        off = GDN_HALO - (GDN_CONV - 1) + j
        y = y + w[j:j + 1] * stage_ref[off:off + c, :]
    return y * jax.nn.sigmoid(y)


def _l2_rows(x):
    return x * lax.rsqrt(jnp.sum(x * x, axis=-1, keepdims=True) + NORM_EPS)


def _gdn_kernel(q_ref, k_ref, v_ref, qh_ref, kh_ref, vh_ref, wq_ref, wk_ref, wv_ref, z_ref,
                ba_ref, alog_ref, dtb_ref, ng_ref, o_ref, s_scr, qst_scr, kst_scr, vst_scr):
    C, dh, hb = GDN_CHUNK, GDN_HEAD_DIM, GDN_HB
    rep = GDN_V_HEADS // GDN_QK_HEADS
    first = pl.program_id(2) == 0

    @pl.when(first)
    def _():
        s_scr[...] = jnp.zeros_like(s_scr)

    keep = jnp.where(first, 0.0, 1.0)
    qc = _conv_silu(q_ref, qh_ref, keep, wq_ref[...], qst_scr)
    kc = _conv_silu(k_ref, kh_ref, keep, wk_ref[...], kst_scr)
    vc = _conv_silu(v_ref, vh_ref, keep, wv_ref[...], vst_scr)

    ri = lax.broadcasted_iota(jnp.int32, (C, C), 0)
    ci = lax.broadcasted_iota(jnp.int32, (C, C), 1)
    causal = ri >= ci
    strict = ri > ci
    ba = ba_ref[0]
    bcol = jax.nn.sigmoid(ba[:, :hb])
    g = -jnp.exp(alog_ref[0]) * jax.nn.softplus(ba[:, hb:2 * hb] + dtb_ref[0])
    gcol = jnp.dot(jnp.where(causal, 1.0, 0.0), g, preferred_element_type=F32, precision=lax.Precision.HIGHEST)
    grow = lax.dot_general(g, jnp.where(ri <= ci, 1.0, 0.0), (((0,), (0,)), ((), ())),
                           preferred_element_type=F32, precision=lax.Precision.HIGHEST)
    heads = range(hb)
    qs, ks, grams = [], [], []
    for hq in range(hb // rep):
        qh = _l2_rows(qc[:, hq * dh:(hq + 1) * dh]) * (dh ** -0.5)
        kh = _l2_rows(kc[:, hq * dh:(hq + 1) * dh])
        k16 = kh.astype(BF16)
        qs.append(qh)
        ks.append(kh)
        grams.append(_dot_nt(jnp.concatenate([qh.astype(BF16), k16], axis=0), k16))
    beta = [bcol[:, h:h + 1] for h in heads]
    gc = [gcol[:, h:h + 1] for h in heads]
    gr = [grow[h:h + 1, :] for h in heads]
    g_last = [gr[h][:, C - 1:C] for h in heads]
    eg = [jnp.exp(gc[h]) for h in heads]
    decay = [jnp.where(causal, jnp.exp(jnp.minimum(gc[h] - gr[h], 0.0)), 0.0) for h in heads]
    attn = [(grams[h // rep][:C] * decay[h]).astype(BF16) for h in heads]
    nm = [jnp.where(strict, grams[h // rep][C:] * decay[h] * (-beta[h]), 0.0) for h in heads]
    m = [nm[h].astype(BF16) for h in heads]
    for _ in range(int(math.log2(C)) - 1):
        mf = [jnp.dot(m[h], m[h], preferred_element_type=F32) for h in heads]
        m = [mf[h].astype(BF16) for h in heads]
        nm = [nm[h] + mf[h] + jnp.dot(m[h], nm[h].astype(BF16), preferred_element_type=F32) for h in heads]
    x = [jnp.concatenate([vc[:, h * dh:(h + 1) * dh] * beta[h], ks[h // rep] * (beta[h] * eg[h])], axis=1)
         for h in heads]
    x = [x[h] + jnp.dot(nm[h].astype(BF16), x[h].astype(BF16), preferred_element_type=F32) for h in heads]
    s = [s_scr[h] for h in heads]
    ws = [jnp.dot(jnp.concatenate([x[h][:, dh:].astype(BF16), (qs[h // rep] * eg[h]).astype(BF16)], axis=0),
                  s[h].astype(BF16), preferred_element_type=F32) for h in heads]
    vn = [(x[h][:, :dh] - ws[h][:C]).astype(BF16) for h in heads]
    o = [ws[h][C:] + jnp.dot(attn[h], vn[h], preferred_element_type=F32) for h in heads]
    for h in heads:
        k_dec = (ks[h // rep] * jnp.exp(g_last[h] - gc[h])).astype(BF16)
        s_scr[h] = s[h] * jnp.exp(g_last[h]) + _dot_tn(k_dec, vn[h])
    outs = []
    for h in heads:
        zh = z_ref[0, :, h * dh:(h + 1) * dh]
        outs.append((_rms_rows(o[h], ng_ref[...]) * (zh * jax.nn.sigmoid(zh))).astype(o_ref.dtype))
    o_ref[0] = jnp.concatenate(outs, axis=1)


_GDN_GATE_COL0 = GDN_CONV_CH + GDN_V_WIDTH
_GDN_TN = 1792
_GDN_COLS = _round_up(_GDN_GATE_COL0 + GDN_V_HEADS // GDN_HB * V7X_LANES, _GDN_TN)


def _gdn_w_in_layout(w_in):
    b0 = _GDN_GATE_COL0
    a0 = b0 + GDN_V_HEADS
    parts = [w_in[:, :b0]]
    for p in range(GDN_V_HEADS // GDN_HB):
        parts.append(w_in[:, b0 + p * GDN_HB:b0 + (p + 1) * GDN_HB])
        parts.append(jnp.pad(w_in[:, a0 + p * GDN_HB:a0 + (p + 1) * GDN_HB], ((0, 0), (0, V7X_LANES - 2 * GDN_HB))))
    w = jnp.concatenate(parts, axis=1)
    return jnp.pad(w, ((0, 0), (0, _GDN_COLS - w.shape[1]))).astype(BF16)


def gdn_delta_rule(proj, conv_w, a_log, dt_bias, norm_gain):
    B, S, _ = proj.shape
    C, dh, hb, H = GDN_CHUNK, GDN_HEAD_DIM, GDN_HB, GDN_V_HEADS
    rep = GDN_V_HEADS // GDN_QK_HEADS
    N, HP = S // C, H // hb
    wqk, wv = hb // rep * dh, hb * dh
    kb0, vb0, zb0 = GDN_QK_WIDTH // wqk, 2 * GDN_QK_WIDTH // wv, GDN_CONV_CH // wv
    gb0 = _GDN_GATE_COL0 // V7X_LANES
    hr = C // GDN_HALO
    halo = lambda n: jnp.maximum(n * hr - 1, 0)
    return pl.pallas_call(
        _gdn_kernel,
        grid=(B, HP, N),
        in_specs=[
            pl.BlockSpec((1, C, wqk), lambda b, p, n: (b, n, p)),
            pl.BlockSpec((1, C, wqk), lambda b, p, n: (b, n, kb0 + p)),
            pl.BlockSpec((1, C, wv), lambda b, p, n: (b, n, vb0 + p)),
            pl.BlockSpec((1, GDN_HALO, wqk), lambda b, p, n: (b, halo(n), p)),
            pl.BlockSpec((1, GDN_HALO, wqk), lambda b, p, n: (b, halo(n), kb0 + p)),
            pl.BlockSpec((1, GDN_HALO, wv), lambda b, p, n: (b, halo(n), vb0 + p)),
            pl.BlockSpec((GDN_CONV, wqk), lambda b, p, n: (0, p)),
            pl.BlockSpec((GDN_CONV, wqk), lambda b, p, n: (0, kb0 + p)),
            pl.BlockSpec((GDN_CONV, wv), lambda b, p, n: (0, vb0 + p)),
            pl.BlockSpec((1, C, wv), lambda b, p, n: (b, n, zb0 + p)),
            pl.BlockSpec((1, C, V7X_LANES), lambda b, p, n: (b, n, gb0 + p)),
            pl.BlockSpec((1, 1, hb), lambda b, p, n: (p, 0, 0)),
            pl.BlockSpec((1, 1, hb), lambda b, p, n: (p, 0, 0)),
            pl.BlockSpec((1, dh), lambda b, p, n: (0, 0)),
        ],
        out_specs=pl.BlockSpec((1, C, wv), lambda b, p, n: (b, n, p)),
        out_shape=jax.ShapeDtypeStruct((B, S, H * dh), BF16),
        scratch_shapes=[pltpu.VMEM((hb, dh, dh), F32), pltpu.VMEM((GDN_HALO + C, wqk), F32),
                        pltpu.VMEM((GDN_HALO + C, wqk), F32), pltpu.VMEM((GDN_HALO + C, wv), F32)],
        compiler_params=_cparams("parallel", "parallel", "arbitrary"),
        name="gdn_delta_rule",
    )(proj, proj, proj, proj, proj, proj, conv_w, conv_w, conv_w, proj, proj,
      a_log.astype(F32).reshape(HP, 1, hb), dt_bias.astype(F32).reshape(HP, 1, hb), norm_gain.reshape(1, dh))


_HYB_NSA_COLS = NSA_WIDTH + 6 * NSA_KV_WIDTH
_HYB_RET_COLS = 2 * RET_HEADS * RET_DK + 2 * RET_HEADS * RET_DV
_HYB_Q_COL0 = _HYB_RET_COLS
_HYB_KV_COL0 = _HYB_Q_COL0 + NSA_WIDTH
_HYB_GATE_COL0 = _HYB_NSA_COLS + _HYB_RET_COLS
_HYB_TN = 768
_HYB_COLS = _round_up(_HYB_GATE_COL0 + NSA_KV_GROUPS * V7X_LANES, _HYB_TN)


def _hybrid_w_in_layout(w_in):
    gate0 = _HYB_NSA_COLS
    ret0 = gate0 + 3 * NSA_HEADS
    per_group = 3 * NSA_GROUP_HEADS
    parts = [w_in[:, ret0:ret0 + _HYB_RET_COLS], w_in[:, :gate0]]
    for g in range(NSA_KV_GROUPS):
        parts.append(jnp.pad(w_in[:, gate0 + g * per_group:gate0 + (g + 1) * per_group],
                             ((0, 0), (0, V7X_LANES - per_group))))
    w = jnp.concatenate(parts, axis=1)
    return jnp.pad(w, ((0, 0), (0, _HYB_COLS - w.shape[1]))).astype(BF16)


def hybrid_mixer(h, ln_gain, w_in, w_out, q_gain, k_gain, cmp_pos, cmp_w1, cmp_w2, gn_gain, rel_bias):
    B, S, D = h.shape
    proj = norm_matmul(h.reshape(B * S, D), ln_gain, _hybrid_w_in_layout(w_in), tn=_HYB_TN).reshape(B, S, _HYB_COLS)
    cmp = nsa_compress(proj, _HYB_KV_COL0 // NSA_HEAD_DIM, cmp_pos, cmp_w1, cmp_w2, k_gain[0])
    ksn, vsb, kwn, vwb = nsa_kv_prep(proj, (_HYB_KV_COL0 + 2 * NSA_KV_WIDTH) // NSA_KV_WIDTH, k_gain)
    a_out = nsa_attention_pallas(proj, _HYB_Q_COL0 // (NSA_GROUP_HEADS * NSA_HEAD_DIM), _HYB_GATE_COL0 // V7X_LANES,
                                 cmp[0], cmp[1], ksn, vsb, kwn, vwb, q_gain, rel_bias)
    b_out = retention_pallas(proj, 0, gn_gain)
    w_out = w_out.astype(BF16)
    return matmul2_residual(a_out.reshape(B * S, -1), b_out.reshape(B * S, -1), w_out,
                            h.reshape(B * S, D)).reshape(B, S, D)


def gdn_mixer(h, ln_gain, w_in, conv_w, a_log, dt_bias, norm_gain, w_out):
    B, S, D = h.shape
    proj = norm_matmul(h.reshape(B * S, D), ln_gain, _gdn_w_in_layout(w_in), tn=_GDN_TN).reshape(B, S, _GDN_COLS)
    o = gdn_delta_rule(proj, conv_w, a_log, dt_bias, norm_gain)
    return matmul_residual(o.reshape(B * S, GDN_V_WIDTH), w_out.astype(BF16), h.reshape(B * S, D)).reshape(B, S, D)


def memory_kv(mem, mem_gain, wkv, k_gain):
    B, M, D = mem.shape
    kv = norm_matmul(mem.reshape(B * M, D), mem_gain, wkv.astype(BF16))
    k, v = jnp.split(kv.reshape(B, M, 2 * XA_WIDTH), 2, axis=-1)
    k = rms_norm(k.reshape(B, M, XA_HEADS, XA_HEAD_DIM), k_gain).reshape(B, M, XA_WIDTH)
    return k.astype(BF16), v.astype(BF16)


def kernel(x, mem, rel_bias, ln_mix, ln_mem, ln_ffn, hyb_w_in, hyb_w_out, nsa_q_gain, nsa_k_gain,
           nsa_cmp_pos, nsa_cmp_w1, nsa_cmp_w2, ret_gn_gain, gdn_w_in, gdn_conv_w, gdn_a_log,
           gdn_dt_bias, gdn_norm_gain, gdn_w_out, xa_wq, xa_wkv, xa_q_gain, xa_k_gain, xa_mem_gain,
           xa_wo, ffn_w_in, ffn_w_out):
    B, S, D = x.shape
    h = x
    ffn_w_in, ffn_w_out, gdn_w_out = cast_bf16(ffn_w_in), cast_bf16(ffn_w_out), cast_bf16(gdn_w_out)
    for layer in range(DEPTH):
        if layer % 2 == 0:
            e = layer // 2
            h = hybrid_mixer(h, ln_mix[layer], hyb_w_in[e], hyb_w_out[e], nsa_q_gain[e], nsa_k_gain[e],
                             nsa_cmp_pos[e], nsa_cmp_w1[e], nsa_cmp_w2[e], ret_gn_gain[e], rel_bias)
        else:
            o = layer // 2
            h = gdn_mixer(h, ln_mix[layer], gdn_w_in[o], gdn_conv_w[o], gdn_a_log[o], gdn_dt_bias[o],
                          gdn_norm_gain[o], gdn_w_out[o])
        k_mem, v_mem = memory_kv(mem, xa_mem_gain[layer], xa_wkv[layer], xa_k_gain[layer])
        h = xattn_residual(h, ln_mem[layer], xa_wq[layer].astype(BF16), xa_q_gain[layer], k_mem, v_mem,
                           xa_wo[layer].astype(BF16))
        h = ffn_residual(h.reshape(B * S, D), ln_ffn[layer], ffn_w_in[layer].astype(BF16),
                         ffn_w_out[layer].astype(BF16)).reshape(B, S, D)
    return h
```

```python
import functools
import math

import jax
import jax.numpy as jnp
import numpy as np
from jax import lax
from jax.experimental import pallas as pl
from jax.experimental.pallas import tpu as pltpu

F32 = jnp.float32
BF16 = jnp.bfloat16

D_MODEL = 2048
DEPTH = 4
NORM_EPS = 1e-6
NEG_INF = -1e30

NSA_HEADS = 8
NSA_KV_GROUPS = 2
NSA_GROUP_HEADS = NSA_HEADS // NSA_KV_GROUPS
NSA_HEAD_DIM = 128
NSA_CMP_LEN = 32
NSA_CMP_STRIDE = 16
NSA_SEL_BLOCK = 64
NSA_N_SELECT = 16
NSA_WINDOW = 512
NSA_Q_BLOCK = 64
NSA_FORCE_SCORE = 1e6
NSA_WIDTH = NSA_HEADS * NSA_HEAD_DIM
NSA_KV_WIDTH = NSA_KV_GROUPS * NSA_HEAD_DIM

RET_HEADS = 4
RET_DK = 256
RET_DV = 256
RET_CHUNK = 128
RET_ROPE_BASE = 10000.0
RET_GN_EPS = 1e-5

GDN_QK_HEADS = 16
GDN_V_HEADS = 32
GDN_HEAD_DIM = 128
GDN_CONV = 4
GDN_CHUNK = 64
GDN_QK_WIDTH = GDN_QK_HEADS * GDN_HEAD_DIM
GDN_V_WIDTH = GDN_V_HEADS * GDN_HEAD_DIM
GDN_CONV_CH = 2 * GDN_QK_WIDTH + GDN_V_WIDTH

REL_BUCKETS = 32
REL_MAX_DIST = 1024

XA_HEADS = 4
XA_HEAD_DIM = 128
XA_WIDTH = XA_HEADS * XA_HEAD_DIM

V7X_VMEM_LIMIT_BYTES = 56 * 1024 * 1024
V7X_LANES = 128


def _cparams(*sem):
    return pltpu.CompilerParams(dimension_semantics=sem, vmem_limit_bytes=V7X_VMEM_LIMIT_BYTES)


def _round_up(n, m):
    return -(-n // m) * m


def _rms_rows(x, gain):
    return x * lax.rsqrt(jnp.mean(x * x, axis=-1, keepdims=True) + NORM_EPS) * gain


def _cast_kernel(x_ref, o_ref):
    o_ref[...] = x_ref[...].astype(o_ref.dtype)


def cast_bf16(w, *, block_bytes=8 * 1024 * 1024):
    n = w.shape[-1]
    x = w.reshape(-1, n)
    rows = x.shape[0]
    tr = rows
    while tr * n * 4 > block_bytes and tr % 32 == 0:
        tr //= 2
    out = pl.pallas_call(
        _cast_kernel,
        grid=(rows // tr,),
        in_specs=[pl.BlockSpec((tr, n), lambda i: (i, 0))],
        out_specs=pl.BlockSpec((tr, n), lambda i: (i, 0)),
        out_shape=jax.ShapeDtypeStruct((rows, n), BF16),
        compiler_params=_cparams("parallel"),
        name="cast_bf16",
    )(x)
    return out.reshape(w.shape)


def _norm_matmul_kernel(x_ref, g_ref, w_ref, o_ref, xn_ref):
    @pl.when(pl.program_id(1) == 0)
    def _():
        xn_ref[...] = _rms_rows(x_ref[...], g_ref[...]).astype(BF16)

    o_ref[...] = jnp.dot(xn_ref[...], w_ref[...], preferred_element_type=F32).astype(o_ref.dtype)


def norm_matmul(x, gain, w, *, tm=1024, tn=512, out_dtype=F32):
    m, k = x.shape
    n = w.shape[1]
    tm = min(tm, m)
    return pl.pallas_call(
        _norm_matmul_kernel,
        grid=(m // tm, n // tn),
        in_specs=[pl.BlockSpec((tm, k), lambda i, j: (i, 0)),
                  pl.BlockSpec((1, k), lambda i, j: (0, 0)),
                  pl.BlockSpec((k, tn), lambda i, j: (0, j))],
        out_specs=pl.BlockSpec((tm, tn), lambda i, j: (i, j)),
        out_shape=jax.ShapeDtypeStruct((m, n), out_dtype),
        scratch_shapes=[pltpu.VMEM((tm, k), BF16)],
        compiler_params=_cparams("parallel", "arbitrary"),
        name="norm_matmul",
    )(x, gain.reshape(1, k), w)


def _matmul_res_kernel(x_ref, w_ref, r_ref, o_ref):
    o_ref[...] = r_ref[...] + jnp.dot(x_ref[...], w_ref[...], preferred_element_type=F32)


def matmul_residual(x, w, res, *, tm=1024, tn=1024):
    m, k = x.shape
    n = w.shape[1]
    return pl.pallas_call(
        _matmul_res_kernel,
        grid=(m // tm, n // tn),
        in_specs=[pl.BlockSpec((tm, k), lambda i, j: (i, 0)),
                  pl.BlockSpec((k, tn), lambda i, j: (0, j)),
                  pl.BlockSpec((tm, tn), lambda i, j: (i, j))],
        out_specs=pl.BlockSpec((tm, tn), lambda i, j: (i, j)),
        out_shape=jax.ShapeDtypeStruct((m, n), F32),
        compiler_params=_cparams("parallel", "arbitrary"),
        name="matmul_residual",
    )(x, w, res)


def _matmul2_res_kernel(x1_ref, x2_ref, w1_ref, w2_ref, r_ref, o_ref):
    o_ref[...] = (r_ref[...] + jnp.dot(x1_ref[...], w1_ref[...], preferred_element_type=F32)
                  + jnp.dot(x2_ref[...], w2_ref[...], preferred_element_type=F32))


def matmul2_residual(x1, x2, w, res, *, tm=1024, tn=1024):
    m, k1 = x1.shape
    n = w.shape[1]
    return pl.pallas_call(
        _matmul2_res_kernel,
        grid=(m // tm, n // tn),
        in_specs=[pl.BlockSpec((tm, k1), lambda i, j: (i, 0)),
                  pl.BlockSpec((tm, k1), lambda i, j: (i, 0)),
                  pl.BlockSpec((k1, tn), lambda i, j: (0, j)),
                  pl.BlockSpec((k1, tn), lambda i, j: (1, j)),
                  pl.BlockSpec((tm, tn), lambda i, j: (i, j))],
        out_specs=pl.BlockSpec((tm, tn), lambda i, j: (i, j)),
        out_shape=jax.ShapeDtypeStruct((m, n), F32),
        compiler_params=_cparams("parallel", "arbitrary"),
        name="matmul2_residual",
    )(x1, x2, w, w, res)


def _ffn_kernel(x_ref, g_ref, wg_ref, wu_ref, wo_ref, o_ref, xn_ref, acc_ref):
    j = pl.program_id(1)

    @pl.when(j == 0)
    def _():
        x = x_ref[...]
        xn_ref[...] = _rms_rows(x, g_ref[...]).astype(BF16)
        acc_ref[...] = x

    xn = xn_ref[...]
    gate = jnp.dot(xn, wg_ref[...], preferred_element_type=F32)
    up = jnp.dot(xn, wu_ref[...], preferred_element_type=F32)
    act = (gate * jax.nn.sigmoid(gate) * up).astype(BF16)
    acc_ref[...] += jnp.dot(act, wo_ref[...], preferred_element_type=F32)

    @pl.when(j == pl.num_programs(1) - 1)
    def _():
        o_ref[...] = acc_ref[...]


def ffn_residual(h, gain, w_in, w_out, *, tm=512, th=512):
    m, d = h.shape
    hidden = w_out.shape[0]
    nh = hidden // th
    return pl.pallas_call(
        _ffn_kernel,
        grid=(m // tm, nh),
        in_specs=[pl.BlockSpec((tm, d), lambda i, j: (i, 0)),
                  pl.BlockSpec((1, d), lambda i, j: (0, 0)),
                  pl.BlockSpec((d, th), lambda i, j: (0, j)),
                  pl.BlockSpec((d, th), lambda i, j: (0, j + nh)),
                  pl.BlockSpec((th, d), lambda i, j: (j, 0))],
        out_specs=pl.BlockSpec((tm, d), lambda i, j: (i, 0)),
        out_shape=jax.ShapeDtypeStruct((m, d), F32),
        scratch_shapes=[pltpu.VMEM((tm, d), BF16), pltpu.VMEM((tm, d), F32)],
        compiler_params=_cparams("parallel", "arbitrary"),
        name="ffn_residual",
    )(h, gain.reshape(1, d), w_in, w_in, w_out)


def _xattn_kernel(h_ref, g_ref, wq_ref, qg_ref, k_ref, v_ref, wo_ref, o_ref):
    x = h_ref[0]
    xn = _rms_rows(x, g_ref[...]).astype(BF16)
    q = jnp.dot(xn, wq_ref[...], preferred_element_type=F32)
    k = k_ref[0]
    v = v_ref[0]
    outs = []
    for hh in range(XA_HEADS):
        sl = slice(hh * XA_HEAD_DIM, (hh + 1) * XA_HEAD_DIM)
        qh = _rms_rows(q[:, sl], qg_ref[...]).astype(BF16)
        s = lax.dot_general(qh, k[:, sl], (((1,), (1,)), ((), ())), preferred_element_type=F32)
        s = s * (XA_HEAD_DIM ** -0.5)
        s = s - jnp.max(s, axis=-1, keepdims=True)
        p = jnp.exp(s)
        p = p / jnp.sum(p, axis=-1, keepdims=True)
        outs.append(jnp.dot(p.astype(BF16), v[:, sl], preferred_element_type=F32))
    o = jnp.concatenate(outs, axis=-1).astype(BF16)
    o_ref[0] = x + jnp.dot(o, wo_ref[...], preferred_element_type=F32)


def xattn_residual(h, gain, wq, q_gain, k, v, wo, *, tm=1024):
    b, s, d = h.shape
    mt = k.shape[1]
    return pl.pallas_call(
        _xattn_kernel,
        grid=(b, s // tm),
        in_specs=[pl.BlockSpec((1, tm, d), lambda bi, i: (bi, i, 0)),
                  pl.BlockSpec((1, d), lambda bi, i: (0, 0)),
                  pl.BlockSpec((d, XA_WIDTH), lambda bi, i: (0, 0)),
                  pl.BlockSpec((1, XA_HEAD_DIM), lambda bi, i: (0, 0)),
                  pl.BlockSpec((1, mt, XA_WIDTH), lambda bi, i: (bi, 0, 0)),
                  pl.BlockSpec((1, mt, XA_WIDTH), lambda bi, i: (bi, 0, 0)),
                  pl.BlockSpec((XA_WIDTH, d), lambda bi, i: (0, 0))],
        out_specs=pl.BlockSpec((1, tm, d), lambda bi, i: (bi, i, 0)),
        out_shape=jax.ShapeDtypeStruct((b, s, d), F32),
        compiler_params=_cparams("parallel", "parallel"),
        name="xattn_residual",
    )(h, gain.reshape(1, d), wq, q_gain.reshape(1, XA_HEAD_DIM), k, v, wo)


def rms_norm(x, gain, eps=NORM_EPS):
    x32 = x.astype(F32)
    y = x32 * lax.rsqrt(jnp.mean(x32 * x32, axis=-1, keepdims=True) + eps)
    return (y * gain.astype(F32)).astype(x.dtype)


def rel_bucket(dist):
    dist = jnp.maximum(dist, 0)
    max_exact = REL_BUCKETS // 2
    scaled = (jnp.log(jnp.maximum(dist, max_exact).astype(F32) / max_exact)
              / math.log(REL_MAX_DIST / max_exact) * (REL_BUCKETS - max_exact))
    large = jnp.minimum(max_exact + scaled.astype(jnp.int32), REL_BUCKETS - 1)
    return jnp.where(dist < max_exact, dist, large)


NSA_QT = 128
NSA_KT = 512
NSA_WT = NSA_WINDOW + NSA_QT


def _dot_nt(a, b):
    return lax.dot_general(a, b, (((1,), (1,)), ((), ())), preferred_element_type=F32)


def _dot_tn(a, b):
    return lax.dot_general(a, b, (((0,), (0,)), ((), ())), preferred_element_type=F32)


def _bias_tile(tb_ref, h, first_slab, n_slabs):
    return jnp.concatenate([tb_ref[h, jnp.maximum(first_slab - j, 0)] for j in range(n_slabs)], axis=1)


def _nsa_kernel(q_ref, gl_ref, kc_ref, vc_ref, bc_ref, covt_ref, ks_ref, vs_ref, kw_ref, vw_ref,
                tb_ref, e_ref, qg_ref, o_ref, a_scr, *, n_top):
    QT, KT, WT, HG, dh = NSA_QT, NSA_KT, NSA_WT, NSA_GROUP_HEADS, NSA_HEAD_DIM
    nsb, nc = covt_ref.shape
    q0 = pl.program_id(2) * QT

    x = q_ref[0]
    qs = []
    for h in range(HG):
        xh = _rms_rows(x[:, h * dh:(h + 1) * dh], qg_ref[...]) * (dh ** -0.5)
        qs.append(xh.astype(BF16))
    qn = jnp.concatenate(qs, axis=0)

    sc = _dot_nt(qn, kc_ref[0, 0]) + bc_ref[...].reshape(HG * QT, nc)
    row = lax.broadcasted_iota(jnp.int32, (HG * QT, nc), 0)
    col = lax.broadcasted_iota(jnp.int32, (HG * QT, nc), 1)
    tq = q0 + (row & (QT - 1))
    mask_c = tq >= col * NSA_CMP_STRIDE + (NSA_CMP_LEN - 1)
    sc = jnp.where(mask_c, sc, NEG_INF)
    pc = jnp.where(mask_c, jnp.exp(sc - jnp.max(sc, axis=-1, keepdims=True)), 0.0)
    den = jnp.sum(pc, axis=-1, keepdims=True)
    pcb = (pc / jnp.maximum(den, 1e-30)).astype(BF16)
    o_c = jnp.dot(pcb, vc_ref[0, 0], preferred_element_type=F32)

    imp_all = _dot_nt(covt_ref[...], pcb)
    imp = imp_all[:, 0:QT]
    for h in range(1, HG):
        imp = imp + imp_all[:, h * QT:(h + 1) * QT]
    jj = lax.broadcasted_iota(jnp.int32, (nsb, QT), 0)
    cur = (q0 + lax.broadcasted_iota(jnp.int32, (nsb, QT), 1)) // NSA_SEL_BLOCK
    forced = (jj == 0) | (jj == cur) | (jj == cur - 1)
    a = lax.bitcast_convert_type(jnp.where(forced, NSA_FORCE_SCORE, imp), jnp.int32)
    a = jnp.where(jj > cur, -1, a)
    a1 = a + 1
    a_scr[...] = a

    def rank_body(i, rank):
        r = a_scr[pl.ds(i, 1), :]
        return rank + jnp.where(r >= jnp.where(jj > i, a, a1), 1.0, 0.0)

    n_live = jnp.minimum((q0 + QT - 1) // NSA_SEL_BLOCK + 1, nsb)
    rank = lax.fori_loop(0, n_live, rank_body, jnp.zeros((nsb, QT), F32))
    sel_t = jnp.where(rank < n_top, 1.0, 0.0)
    if nsb < V7X_LANES:
        sel_t = jnp.concatenate([sel_t, jnp.zeros((V7X_LANES - nsb, QT), F32)], axis=0)
    sel = sel_t.T.astype(BF16)

    cmr = (lax.broadcasted_iota(jnp.int32, (QT, KT), 1) - lax.broadcasted_iota(jnp.int32, (QT, KT), 0))

    def sel_body(kt, carry, diagonal=False):
        m_i, l_i, acc = carry
        k0 = pl.multiple_of(kt * KT, KT)
        k = ks_ref[0, pl.ds(k0, KT), :]
        v = vs_ref[0, pl.ds(k0, KT), :]
        s = _dot_nt(qn, k)
        delta = q0 - k0
        visible = jnp.dot(sel, e_ref[kt], preferred_element_type=F32) > 0.5
        if diagonal:
            visible = visible & (cmr <= delta)
        seg = delta // QT
        parts = []
        for h in range(HG):
            bias = _bias_tile(tb_ref, h, seg, KT // QT)
            parts.append(jnp.where(visible, s[h * QT:(h + 1) * QT] + bias, NEG_INF))
        s = jnp.concatenate(parts, axis=0)
        m_new = jnp.maximum(m_i, jnp.max(s, axis=-1, keepdims=True))
        alpha = jnp.exp(m_i - m_new)
        p = jnp.exp(s - m_new)
        l_new = alpha * l_i + jnp.sum(p, axis=-1, keepdims=True)
        acc = alpha * acc + jnp.dot(p.astype(BF16), v, preferred_element_type=F32)
        return m_new, l_new, acc

    init = (jnp.full((HG * QT, 1), NEG_INF, F32), jnp.zeros((HG * QT, 1), F32), jnp.zeros((HG * QT, dh), F32))
    n_full = q0 // KT
    _, l_s, acc_s = sel_body(n_full, lax.fori_loop(0, n_full, sel_body, init), diagonal=True)
    o_s = acc_s / l_s

    w0 = pl.multiple_of(q0, QT)
    kwin = kw_ref[0, pl.ds(w0, WT), :]
    vwin = vw_ref[0, pl.ds(w0, WT), :]
    sw = _dot_nt(qn, kwin)
    cw = lax.broadcasted_iota(jnp.int32, (QT, WT), 1)
    dist = NSA_WINDOW + lax.broadcasted_iota(jnp.int32, (QT, WT), 0) - cw
    vis_w = (dist >= 0) & (dist < NSA_WINDOW) & (cw >= NSA_WINDOW - q0)
    parts = []
    for h in range(HG):
        bias = _bias_tile(tb_ref, h, NSA_WINDOW // QT, WT // QT)
        parts.append(jnp.where(vis_w, sw[h * QT:(h + 1) * QT] + bias, NEG_INF))
    sw = jnp.concatenate(parts, axis=0)
    pw = jnp.exp(sw - jnp.max(sw, axis=-1, keepdims=True))
    o_w = jnp.dot(pw.astype(BF16), vwin, preferred_element_type=F32) / jnp.sum(pw, axis=-1, keepdims=True)

    gates = jax.nn.sigmoid(gl_ref[0][:, :3 * HG])
    outs = []
    for h in range(HG):
        rs = slice(h * QT, (h + 1) * QT)
        outs.append(gates[:, 3 * h:3 * h + 1] * o_c[rs] + gates[:, 3 * h + 1:3 * h + 2] * o_s[rs]
                    + gates[:, 3 * h + 2:3 * h + 3] * o_w[rs])
    o_ref[0] = jnp.concatenate(outs, axis=-1).astype(o_ref.dtype)


def nsa_attention_pallas(proj, q_blk, gl_blk, kc, vc, ks, vs, kw, vw, q_gain, rel_bias):
    B, S, _ = proj.shape
    G, HG, dh = NSA_KV_GROUPS, NSA_GROUP_HEADS, NSA_HEAD_DIM
    QT, KT, WT, W, SB = NSA_QT, NSA_KT, NSA_WT, NSA_WINDOW, NSA_SEL_BLOCK
    nsb, nc, nkt, nseg = S // SB, S // NSA_CMP_STRIDE, S // KT, max(S // QT, W // QT + 1)
    n_top = min(NSA_N_SELECT, nsb)

    def bias_of(dist):
        onehot = (rel_bucket(dist)[..., None] == jnp.arange(REL_BUCKETS)) & (dist >= 0)[..., None]
        return jnp.einsum('...b,bh->h...', onehot.astype(F32), rel_bias.astype(F32),
                          precision=lax.Precision.HIGHEST)

    bc = bias_of(jnp.arange(S)[:, None] - (jnp.arange(nc) * NSA_CMP_STRIDE + NSA_CMP_LEN - 1)[None, :])
    tb = bias_of(QT * jnp.arange(nseg)[:, None, None] + jnp.arange(QT)[None, :, None] - jnp.arange(QT)[None, None, :])
    cmp_start = np.arange(nc) * NSA_CMP_STRIDE
    cmp_end = cmp_start + NSA_CMP_LEN - 1
    sel_start = np.arange(nsb) * SB
    cover_t = ((cmp_start[None, :] < sel_start[:, None] + SB) & (cmp_end[None, :] >= sel_start[:, None])
               & (np.arange(nc)[None, :] < nc - 1))
    cover_t = jnp.asarray(cover_t, BF16)
    e = (np.arange(V7X_LANES)[None, :, None]
         == (np.arange(nkt)[:, None, None] * (KT // SB) + np.arange(KT)[None, None, :] // SB))
    e = jnp.asarray(e, BF16)

    kern = functools.partial(_nsa_kernel, n_top=n_top)
    return pl.pallas_call(
        kern,
        grid=(B, G, S // QT),
        in_specs=[
            pl.BlockSpec((1, QT, HG * dh), lambda b, g, i: (b, i, q_blk + g)),
            pl.BlockSpec((1, QT, V7X_LANES), lambda b, g, i: (b, i, gl_blk + g)),
            pl.BlockSpec((1, 1, nc, dh), lambda b, g, i: (b, g, 0, 0)),
            pl.BlockSpec((1, 1, nc, dh), lambda b, g, i: (b, g, 0, 0)),
            pl.BlockSpec((HG, QT, nc), lambda b, g, i: (g, i, 0)),
            pl.BlockSpec((nsb, nc), lambda b, g, i: (0, 0)),
            pl.BlockSpec((1, S, dh), lambda b, g, i: (b, 0, g)),
            pl.BlockSpec((1, S, dh), lambda b, g, i: (b, 0, g)),
            pl.BlockSpec((1, S + W, dh), lambda b, g, i: (b, 0, g)),
            pl.BlockSpec((1, S + W, dh), lambda b, g, i: (b, 0, g)),
            pl.BlockSpec((HG, nseg, QT, QT), lambda b, g, i: (g, 0, 0, 0)),
            pl.BlockSpec((nkt, V7X_LANES, KT), lambda b, g, i: (0, 0, 0)),
            pl.BlockSpec((1, dh), lambda b, g, i: (0, 0)),
        ],
        out_specs=pl.BlockSpec((1, QT, HG * dh), lambda b, g, i: (b, i, g)),
        out_shape=jax.ShapeDtypeStruct((B, S, NSA_WIDTH), BF16),
        scratch_shapes=[pltpu.VMEM((nsb, QT), jnp.int32)],
        compiler_params=_cparams("parallel", "parallel", "arbitrary"),
        name="nsa_attention",
    )(proj, proj, kc, vc, bc, cover_t, ks, vs, kw, vw, tb, e, q_gain.reshape(1, dh))


def _nsa_compress_kernel(x_ref, pos_ref, w1_ref, w2_ref, kg_ref, o_ref):
    stride, dh = NSA_CMP_STRIDE, NSA_HEAD_DIM
    n_chunks = x_ref.shape[1] // stride
    hid = w1_ref.shape[2]
    h_lo = jnp.zeros((n_chunks, hid), F32)
    h_hi = jnp.zeros((n_chunks, hid), F32)
    for p in range(stride):
        xp = x_ref[0, pl.ds(p, n_chunks, stride=stride), :]
        h_lo = h_lo + jnp.dot((xp + pos_ref[0, p:p + 1, :]).astype(BF16), w1_ref[0, p * dh:(p + 1) * dh, :],
                              preferred_element_type=F32)
        h_hi = h_hi + jnp.dot((xp + pos_ref[0, stride + p:stride + p + 1, :]).astype(BF16),
                              w1_ref[0, (stride + p) * dh:(stride + p + 1) * dh, :], preferred_element_type=F32)
    hidden = h_lo + jnp.concatenate([h_hi[1:], jnp.zeros((1, hid), F32)], axis=0)
    out = jnp.dot((hidden * jax.nn.sigmoid(hidden)).astype(BF16), w2_ref[0], preferred_element_type=F32)
    out = jnp.where(pl.program_id(2) == 0, _rms_rows(out, kg_ref[...]), out)
    row = lax.broadcasted_iota(jnp.int32, out.shape, 0)
    o_ref[0, 0, 0] = jnp.where(row < n_chunks - 1, out, 0.0).astype(o_ref.dtype)


def nsa_compress(proj, col_blk, cmp_pos, cmp_w1, cmp_w2, k_gain):
    B, S, _ = proj.shape
    G, dh = NSA_KV_GROUPS, NSA_HEAD_DIM
    nc = S // NSA_CMP_STRIDE
    hid = cmp_w1.shape[-1]
    return pl.pallas_call(
        _nsa_compress_kernel,
        grid=(B, G, 2),
        in_specs=[
            pl.BlockSpec((1, S, dh), lambda b, g, w: (b, 0, col_blk + w * G + g)),
            pl.BlockSpec((1, NSA_CMP_LEN, dh), lambda b, g, w: (w, 0, 0)),
            pl.BlockSpec((1, NSA_CMP_LEN * dh, hid), lambda b, g, w: (w, 0, 0)),
            pl.BlockSpec((1, hid, dh), lambda b, g, w: (w, 0, 0)),
            pl.BlockSpec((1, dh), lambda b, g, w: (0, 0)),
        ],
        out_specs=pl.BlockSpec((1, 1, 1, nc, dh), lambda b, g, w: (w, b, g, 0, 0)),
        out_shape=jax.ShapeDtypeStruct((2, B, G, nc, dh), BF16),
        compiler_params=_cparams("parallel", "parallel", "arbitrary"),
        name="nsa_compress",
    )(proj, cmp_pos, cmp_w1.astype(BF16), cmp_w2.astype(BF16), k_gain.reshape(1, dh))


NSA_PREP_TM = NSA_WINDOW


def _nsa_kv_kernel(ks_ref, vs_ref, kw_ref, vw_ref, gs_ref, gw_ref, kso_ref, vso_ref, kwo_ref, vwo_ref):
    G, dh = NSA_KV_GROUPS, NSA_HEAD_DIM
    i = pl.program_id(1)
    n_in = pl.num_programs(1) - 1

    def normed(ref, gain_ref):
        x = ref[0]
        return jnp.concatenate([_rms_rows(x[:, g * dh:(g + 1) * dh], gain_ref[...]) for g in range(G)], axis=1)

    @pl.when(i < n_in)
    def _():
        kso_ref[0] = normed(ks_ref, gs_ref).astype(kso_ref.dtype)
        vso_ref[0] = vs_ref[0].astype(vso_ref.dtype)

    @pl.when(i == 0)
    def _():
        kwo_ref[0] = jnp.zeros(kwo_ref.shape[1:], kwo_ref.dtype)
        vwo_ref[0] = jnp.zeros(vwo_ref.shape[1:], vwo_ref.dtype)

    @pl.when(i > 0)
    def _():
        kwo_ref[0] = normed(kw_ref, gw_ref).astype(kwo_ref.dtype)
        vwo_ref[0] = vw_ref[0].astype(vwo_ref.dtype)


def nsa_kv_prep(proj, col_blk, k_gain):
    B, S, _ = proj.shape
    tm, kvw, W = NSA_PREP_TM, NSA_KV_WIDTH, NSA_WINDOW
    n_in = S // tm
    cur = lambda i: jnp.minimum(i, n_in - 1)
    prev = lambda i: jnp.maximum(i - 1, 0)
    return pl.pallas_call(
        _nsa_kv_kernel,
        grid=(B, n_in + 1),
        in_specs=[
            pl.BlockSpec((1, tm, kvw), lambda b, i: (b, cur(i), col_blk)),
            pl.BlockSpec((1, tm, kvw), lambda b, i: (b, cur(i), col_blk + 1)),
            pl.BlockSpec((1, tm, kvw), lambda b, i: (b, prev(i), col_blk + 2)),
            pl.BlockSpec((1, tm, kvw), lambda b, i: (b, prev(i), col_blk + 3)),
            pl.BlockSpec((1, NSA_HEAD_DIM), lambda b, i: (0, 0)),
            pl.BlockSpec((1, NSA_HEAD_DIM), lambda b, i: (0, 0)),
        ],
        out_specs=[
            pl.BlockSpec((1, tm, kvw), lambda b, i: (b, cur(i), 0)),
            pl.BlockSpec((1, tm, kvw), lambda b, i: (b, cur(i), 0)),
            pl.BlockSpec((1, tm, kvw), lambda b, i: (b, i, 0)),
            pl.BlockSpec((1, tm, kvw), lambda b, i: (b, i, 0)),
        ],
        out_shape=[jax.ShapeDtypeStruct((B, S, kvw), BF16), jax.ShapeDtypeStruct((B, S, kvw), BF16),
                   jax.ShapeDtypeStruct((B, S + W, kvw), BF16), jax.ShapeDtypeStruct((B, S + W, kvw), BF16)],
        compiler_params=_cparams("parallel", "arbitrary"),
        name="nsa_kv_prep",
    )(proj, proj, proj, proj, k_gain[1:2], k_gain[2:3])


RET_HB = 4


def _rotate_half(x, cos, sin):
    half = x.shape[-1] // 2
    x1, x2 = x[:, :half], x[:, half:]
    return jnp.concatenate([x1 * cos - x2 * sin, x1 * sin + x2 * cos], axis=-1)


def _retention_kernel(q_ref, k_ref, v_ref, g_ref, cos_ref, sin_ref, din_ref, qd_ref, kd_ref, cd_ref, gn_ref,
                      o_ref, s_scr):
    @pl.when(pl.program_id(2) == 0)
    def _():
        s_scr[...] = jnp.zeros_like(s_scr)

    cos, sin = cos_ref[...], sin_ref[...]
    heads = range(RET_HB)
    sl = [slice(h * RET_DK, (h + 1) * RET_DK) for h in heads]
    q = [_rotate_half(q_ref[0, :, sl[h]], cos, sin) * (RET_DK ** -0.5) for h in heads]
    k = [_rotate_half(k_ref[0, :, sl[h]], cos, sin) for h in heads]
    v16 = [v_ref[0, :, sl[h]].astype(BF16) for h in heads]
    inner = [_dot_nt(q[h].astype(BF16), k[h].astype(BF16)) * din_ref[h] for h in heads]
    s = [s_scr[h] for h in heads]
    o = [jnp.dot(inner[h].astype(BF16), v16[h], preferred_element_type=F32)
         + jnp.dot((q[h] * qd_ref[h]).astype(BF16), s[h].astype(BF16), preferred_element_type=F32) for h in heads]
    for h in heads:
        s_scr[h] = s[h] * cd_ref[h] + _dot_tn((k[h] * kd_ref[h]).astype(BF16), v16[h])
    outs = []
    for h in heads:
        mu = jnp.mean(o[h], axis=-1, keepdims=True)
        d = o[h] - mu
        var = jnp.mean(d * d, axis=-1, keepdims=True)
        gate = g_ref[0, :, sl[h]]
        outs.append((gate * jax.nn.sigmoid(gate)
                     * (d * lax.rsqrt(var + RET_GN_EPS) * gn_ref[:, sl[h]])).astype(o_ref.dtype))
    o_ref[0] = jnp.concatenate(outs, axis=-1)


def retention_pallas(proj, q_blk, gn_gain):
    B, S, _ = proj.shape
    H, dk, dv, C = RET_HEADS, RET_DK, RET_DV, RET_CHUNK
    N = S // C
    half = dk // 2
    inv = RET_ROPE_BASE ** (-jnp.arange(half, dtype=F32) / half)
    ang = jnp.arange(S).astype(F32)[:, None] * inv[None, :]
    cos, sin = jnp.cos(ang), jnp.sin(ang)
    log_gamma = jnp.log(1.0 - 2.0 ** (-5.0 - jnp.arange(H, dtype=F32)))
    idx = jnp.arange(C, dtype=F32)
    rel = idx[:, None] - idx[None, :]
    decay_in = jnp.where(rel >= 0, jnp.exp(log_gamma[:, None, None] * jnp.maximum(rel, 0.0)), 0.0)
    q_decay = jnp.exp(log_gamma[:, None] * (idx + 1.0))[..., None]
    k_decay = jnp.exp(log_gamma[:, None] * (C - 1.0 - idx))[..., None]
    chunk_decay = jnp.exp(log_gamma * C)[:, None, None]
    hb = RET_HB
    HP = H // hb
    qb0 = q_blk // hb
    return pl.pallas_call(
        _retention_kernel,
        grid=(B, HP, N),
        in_specs=[
            pl.BlockSpec((1, C, hb * dk), lambda b, p, n: (b, n, qb0 + p)),
            pl.BlockSpec((1, C, hb * dk), lambda b, p, n: (b, n, qb0 + HP + p)),
            pl.BlockSpec((1, C, hb * dv), lambda b, p, n: (b, n, qb0 + 2 * HP + p)),
            pl.BlockSpec((1, C, hb * dv), lambda b, p, n: (b, n, qb0 + 3 * HP + p)),
            pl.BlockSpec((C, half), lambda b, p, n: (n, 0)),
            pl.BlockSpec((C, half), lambda b, p, n: (n, 0)),
            pl.BlockSpec((hb, C, C), lambda b, p, n: (p, 0, 0)),
            pl.BlockSpec((hb, C, 1), lambda b, p, n: (p, 0, 0)),
            pl.BlockSpec((hb, C, 1), lambda b, p, n: (p, 0, 0)),
            pl.BlockSpec((hb, 1, 1), lambda b, p, n: (p, 0, 0)),
            pl.BlockSpec((1, hb * dv), lambda b, p, n: (0, p)),
        ],
        out_specs=pl.BlockSpec((1, C, hb * dv), lambda b, p, n: (b, n, p)),
        out_shape=jax.ShapeDtypeStruct((B, S, H * dv), BF16),
        scratch_shapes=[pltpu.VMEM((hb, dk, dv), F32)],
        compiler_params=_cparams("parallel", "parallel", "arbitrary"),
        name="retention",
    )(proj, proj, proj, proj, cos, sin, decay_in, q_decay, k_decay, chunk_decay, gn_gain.reshape(1, H * dv))


GDN_HB = 32
GDN_HALO = 8


def _conv_silu(x_ref, halo_ref, keep, w, stage_ref):
    c = x_ref.shape[1]
    stage_ref[0:GDN_HALO, :] = halo_ref[0] * keep
    stage_ref[GDN_HALO:GDN_HALO + c, :] = x_ref[0]
    y = w[GDN_CONV - 1:GDN_CONV] * x_ref[0]
    for j in range(GDN_CONV - 1):
        off = GDN_HALO - (GDN_CONV - 1) + j
        y = y + w[j:j + 1] * stage_ref[off:off + c, :]
    return y * jax.nn.sigmoid(y)


def _l2_rows(x):
    return x * lax.rsqrt(jnp.sum(x * x, axis=-1, keepdims=True) + NORM_EPS)


def _gdn_kernel(q_ref, k_ref, v_ref, qh_ref, kh_ref, vh_ref, wq_ref, wk_ref, wv_ref, z_ref,
                ba_ref, alog_ref, dtb_ref, ng_ref, o_ref, s_scr, qst_scr, kst_scr, vst_scr):
    C, dh, hb = GDN_CHUNK, GDN_HEAD_DIM, GDN_HB
    rep = GDN_V_HEADS // GDN_QK_HEADS
    first = pl.program_id(2) == 0

    @pl.when(first)
    def _():
        s_scr[...] = jnp.zeros_like(s_scr)

    keep = jnp.where(first, 0.0, 1.0)
    qc = _conv_silu(q_ref, qh_ref, keep, wq_ref[...], qst_scr)
    kc = _conv_silu(k_ref, kh_ref, keep, wk_ref[...], kst_scr)
    vc = _conv_silu(v_ref, vh_ref, keep, wv_ref[...], vst_scr)

    ri = lax.broadcasted_iota(jnp.int32, (C, C), 0)
    ci = lax.broadcasted_iota(jnp.int32, (C, C), 1)
    causal = ri >= ci
    strict = ri > ci
    ba = ba_ref[0]
    bcol = jax.nn.sigmoid(ba[:, :hb])
    g = -jnp.exp(alog_ref[0]) * jax.nn.softplus(ba[:, hb:2 * hb] + dtb_ref[0])
    gcol = jnp.dot(jnp.where(causal, 1.0, 0.0), g, preferred_element_type=F32, precision=lax.Precision.HIGHEST)
    grow = lax.dot_general(g, jnp.where(ri <= ci, 1.0, 0.0), (((0,), (0,)), ((), ())),
                           preferred_element_type=F32, precision=lax.Precision.HIGHEST)
    heads = range(hb)
    qs, ks, grams = [], [], []
    for hq in range(hb // rep):
        qh = _l2_rows(qc[:, hq * dh:(hq + 1) * dh]) * (dh ** -0.5)
        kh = _l2_rows(kc[:, hq * dh:(hq + 1) * dh])
        k16 = kh.astype(BF16)
        qs.append(qh)
        ks.append(kh)
        grams.append(_dot_nt(jnp.concatenate([qh.astype(BF16), k16], axis=0), k16))
    beta = [bcol[:, h:h + 1] for h in heads]
    gc = [gcol[:, h:h + 1] for h in heads]
    gr = [grow[h:h + 1, :] for h in heads]
    g_last = [gr[h][:, C - 1:C] for h in heads]
    eg = [jnp.exp(gc[h]) for h in heads]
    decay = [jnp.where(causal, jnp.exp(jnp.minimum(gc[h] - gr[h], 0.0)), 0.0) for h in heads]
    attn = [(grams[h // rep][:C] * decay[h]).astype(BF16) for h in heads]
    nm = [jnp.where(strict, grams[h // rep][C:] * decay[h] * (-beta[h]), 0.0) for h in heads]
    m = [nm[h].astype(BF16) for h in heads]
    for _ in range(int(math.log2(C)) - 1):
        mf = [jnp.dot(m[h], m[h], preferred_element_type=F32) for h in heads]
        m = [mf[h].astype(BF16) for h in heads]
        nm = [nm[h] + mf[h] + jnp.dot(m[h], nm[h].astype(BF16), preferred_element_type=F32) for h in heads]
    x = [jnp.concatenate([vc[:, h * dh:(h + 1) * dh] * beta[h], ks[h // rep] * (beta[h] * eg[h])], axis=1)
         for h in heads]
    x = [x[h] + jnp.dot(nm[h].astype(BF16), x[h].astype(BF16), preferred_element_type=F32) for h in heads]
    s = [s_scr[h] for h in heads]
    ws = [jnp.dot(jnp.concatenate([x[h][:, dh:].astype(BF16), (qs[h // rep] * eg[h]).astype(BF16)], axis=0),
                  s[h].astype(BF16), preferred_element_type=F32) for h in heads]
    vn = [(x[h][:, :dh] - ws[h][:C]).astype(BF16) for h in heads]
    o = [ws[h][C:] + jnp.dot(attn[h], vn[h], preferred_element_type=F32) for h in heads]
    for h in heads:
        k_dec = (ks[h // rep] * jnp.exp(g_last[h] - gc[h])).astype(BF16)
        s_scr[h] = s[h] * jnp.exp(g_last[h]) + _dot_tn(k_dec, vn[h])
    outs = []
    for h in heads:
        zh = z_ref[0, :, h * dh:(h + 1) * dh]
        outs.append((_rms_rows(o[h], ng_ref[...]) * (zh * jax.nn.sigmoid(zh))).astype(o_ref.dtype))
    o_ref[0] = jnp.concatenate(outs, axis=1)


_GDN_GATE_COL0 = GDN_CONV_CH + GDN_V_WIDTH
_GDN_TN = 1792
_GDN_COLS = _round_up(_GDN_GATE_COL0 + GDN_V_HEADS // GDN_HB * V7X_LANES, _GDN_TN)


def _gdn_w_in_layout(w_in):
    b0 = _GDN_GATE_COL0
    a0 = b0 + GDN_V_HEADS
    parts = [w_in[:, :b0]]
    for p in range(GDN_V_HEADS // GDN_HB):
        parts.append(w_in[:, b0 + p * GDN_HB:b0 + (p + 1) * GDN_HB])
        parts.append(jnp.pad(w_in[:, a0 + p * GDN_HB:a0 + (p + 1) * GDN_HB], ((0, 0), (0, V7X_LANES - 2 * GDN_HB))))
    w = jnp.concatenate(parts, axis=1)
    return jnp.pad(w, ((0, 0), (0, _GDN_COLS - w.shape[1]))).astype(BF16)


def gdn_delta_rule(proj, conv_w, a_log, dt_bias, norm_gain):
    B, S, _ = proj.shape
    C, dh, hb, H = GDN_CHUNK, GDN_HEAD_DIM, GDN_HB, GDN_V_HEADS
    rep = GDN_V_HEADS // GDN_QK_HEADS
    N, HP = S // C, H // hb
    wqk, wv = hb // rep * dh, hb * dh
    kb0, vb0, zb0 = GDN_QK_WIDTH // wqk, 2 * GDN_QK_WIDTH // wv, GDN_CONV_CH // wv
    gb0 = _GDN_GATE_COL0 // V7X_LANES
    hr = C // GDN_HALO
    halo = lambda n: jnp.maximum(n * hr - 1, 0)
    return pl.pallas_call(
        _gdn_kernel,
        grid=(B, HP, N),
        in_specs=[
            pl.BlockSpec((1, C, wqk), lambda b, p, n: (b, n, p)),
            pl.BlockSpec((1, C, wqk), lambda b, p, n: (b, n, kb0 + p)),
            pl.BlockSpec((1, C, wv), lambda b, p, n: (b, n, vb0 + p)),
            pl.BlockSpec((1, GDN_HALO, wqk), lambda b, p, n: (b, halo(n), p)),
            pl.BlockSpec((1, GDN_HALO, wqk), lambda b, p, n: (b, halo(n), kb0 + p)),
            pl.BlockSpec((1, GDN_HALO, wv), lambda b, p, n: (b, halo(n), vb0 + p)),
            pl.BlockSpec((GDN_CONV, wqk), lambda b, p, n: (0, p)),
            pl.BlockSpec((GDN_CONV, wqk), lambda b, p, n: (0, kb0 + p)),
            pl.BlockSpec((GDN_CONV, wv), lambda b, p, n: (0, vb0 + p)),
            pl.BlockSpec((1, C, wv), lambda b, p, n: (b, n, zb0 + p)),
            pl.BlockSpec((1, C, V7X_LANES), lambda b, p, n: (b, n, gb0 + p)),
            pl.BlockSpec((1, 1, hb), lambda b, p, n: (p, 0, 0)),
            pl.BlockSpec((1, 1, hb), lambda b, p, n: (p, 0, 0)),
            pl.BlockSpec((1, dh), lambda b, p, n: (0, 0)),
        ],
        out_specs=pl.BlockSpec((1, C, wv), lambda b, p, n: (b, n, p)),
        out_shape=jax.ShapeDtypeStruct((B, S, H * dh), BF16),
        scratch_shapes=[pltpu.VMEM((hb, dh, dh), F32), pltpu.VMEM((GDN_HALO + C, wqk), F32),
                        pltpu.VMEM((GDN_HALO + C, wqk), F32), pltpu.VMEM((GDN_HALO + C, wv), F32)],
        compiler_params=_cparams("parallel", "parallel", "arbitrary"),
        name="gdn_delta_rule",
    )(proj, proj, proj, proj, proj, proj, conv_w, conv_w, conv_w, proj, proj,
      a_log.astype(F32).reshape(HP, 1, hb), dt_bias.astype(F32).reshape(HP, 1, hb), norm_gain.reshape(1, dh))


_HYB_NSA_COLS = NSA_WIDTH + 6 * NSA_KV_WIDTH
_HYB_RET_COLS = 2 * RET_HEADS * RET_DK + 2 * RET_HEADS * RET_DV
_HYB_Q_COL0 = _HYB_RET_COLS
_HYB_KV_COL0 = _HYB_Q_COL0 + NSA_WIDTH
_HYB_GATE_COL0 = _HYB_NSA_COLS + _HYB_RET_COLS
_HYB_TN = 768
_HYB_COLS = _round_up(_HYB_GATE_COL0 + NSA_KV_GROUPS * V7X_LANES, _HYB_TN)


def _hybrid_w_in_layout(w_in):
    gate0 = _HYB_NSA_COLS
    ret0 = gate0 + 3 * NSA_HEADS
    per_group = 3 * NSA_GROUP_HEADS
    parts = [w_in[:, ret0:ret0 + _HYB_RET_COLS], w_in[:, :gate0]]
    for g in range(NSA_KV_GROUPS):
        parts.append(jnp.pad(w_in[:, gate0 + g * per_group:gate0 + (g + 1) * per_group],
                             ((0, 0), (0, V7X_LANES - per_group))))
    w = jnp.concatenate(parts, axis=1)
    return jnp.pad(w, ((0, 0), (0, _HYB_COLS - w.shape[1]))).astype(BF16)


def hybrid_mixer(h, ln_gain, w_in, w_out, q_gain, k_gain, cmp_pos, cmp_w1, cmp_w2, gn_gain, rel_bias):
    B, S, D = h.shape
    proj = norm_matmul(h.reshape(B * S, D), ln_gain, _hybrid_w_in_layout(w_in), tn=_HYB_TN).reshape(B, S, _HYB_COLS)
    cmp = nsa_compress(proj, _HYB_KV_COL0 // NSA_HEAD_DIM, cmp_pos, cmp_w1, cmp_w2, k_gain[0])
    ksn, vsb, kwn, vwb = nsa_kv_prep(proj, (_HYB_KV_COL0 + 2 * NSA_KV_WIDTH) // NSA_KV_WIDTH, k_gain)
    a_out = nsa_attention_pallas(proj, _HYB_Q_COL0 // (NSA_GROUP_HEADS * NSA_HEAD_DIM), _HYB_GATE_COL0 // V7X_LANES,
                                 cmp[0], cmp[1], ksn, vsb, kwn, vwb, q_gain, rel_bias)
    b_out = retention_pallas(proj, 0, gn_gain)
    w_out = w_out.astype(BF16)
    return matmul2_residual(a_out.reshape(B * S, -1), b_out.reshape(B * S, -1), w_out,
                            h.reshape(B * S, D)).reshape(B, S, D)


def gdn_mixer(h, ln_gain, w_in, conv_w, a_log, dt_bias, norm_gain, w_out):
    B, S, D = h.shape
    proj = norm_matmul(h.reshape(B * S, D), ln_gain, _gdn_w_in_layout(w_in), tn=_GDN_TN).reshape(B, S, _GDN_COLS)
    o = gdn_delta_rule(proj, conv_w, a_log, dt_bias, norm_gain)
    return matmul_residual(o.reshape(B * S, GDN_V_WIDTH), w_out.astype(BF16), h.reshape(B * S, D)).reshape(B, S, D)


def memory_kv(mem, mem_gain, wkv, k_gain):
    B, M, D = mem.shape
    kv = norm_matmul(mem.reshape(B * M, D), mem_gain, wkv.astype(BF16))
    k, v = jnp.split(kv.reshape(B, M, 2 * XA_WIDTH), 2, axis=-1)
    k = rms_norm(k.reshape(B, M, XA_HEADS, XA_HEAD_DIM), k_gain).reshape(B, M, XA_WIDTH)
    return k.astype(BF16), v.astype(BF16)


def kernel(x, mem, rel_bias, ln_mix, ln_mem, ln_ffn, hyb_w_in, hyb_w_out, nsa_q_gain, nsa_k_gain,
           nsa_cmp_pos, nsa_cmp_w1, nsa_cmp_w2, ret_gn_gain, gdn_w_in, gdn_conv_w, gdn_a_log,
           gdn_dt_bias, gdn_norm_gain, gdn_w_out, xa_wq, xa_wkv, xa_q_gain, xa_k_gain, xa_mem_gain,
           xa_wo, ffn_w_in, ffn_w_out):
    B, S, D = x.shape
    h = x
    ffn_w_in, ffn_w_out, gdn_w_out = cast_bf16(ffn_w_in), cast_bf16(ffn_w_out), cast_bf16(gdn_w_out)
    for layer in range(DEPTH):
        if layer % 2 == 0:
            e = layer // 2
            h = hybrid_mixer(h, ln_mix[layer], hyb_w_in[e], hyb_w_out[e], nsa_q_gain[e], nsa_k_gain[e],
                             nsa_cmp_pos[e], nsa_cmp_w1[e], nsa_cmp_w2[e], ret_gn_gain[e], rel_bias)
        else:
            o = layer // 2
            h = gdn_mixer(h, ln_mix[layer], gdn_w_in[o], gdn_conv_w[o], gdn_a_log[o], gdn_dt_bias[o],
                          gdn_norm_gain[o], gdn_w_out[o])
        k_mem, v_mem = memory_kv(mem, xa_mem_gain[layer], xa_wkv[layer], xa_k_gain[layer])
        h = xattn_residual(h, ln_mem[layer], xa_wq[layer].astype(BF16), xa_q_gain[layer], k_mem, v_mem,
                           xa_wo[layer].astype(BF16))
        h = ffn_residual(h.reshape(B * S, D), ln_ffn[layer], ffn_w_in[layer].astype(BF16),
                         ffn_w_out[layer].astype(BF16)).reshape(B, S, D)
    return h
```
